```python
import jax, jax.numpy as jnp
from jax import lax
import numpy as np

D_MODEL = 1024
BATCH = 8
SEQ = 2048
DEPTH = 2

HEAD_DIM = 64
ROT_DIM = HEAD_DIM // 4
ROPE_THETA = 500000.0
EPS = 1e-6
NSA_HEADS = 8
NSA_KV_HEADS = 2
CMP_LEN = 32
CMP_STRIDE = 16
CMP_HIDDEN = 128
SEL_LEN = 64
N_SEL = 16
WIN = 512
SEL_Q_BLOCK = 64
DIL_PATTERNS = ((128, 1), (512, 4), (2048, 16))
N_DIL_GROUPS = 3
DIL_HEADS = 4
ATTN_BLOCK = 128
N_BRANCHES = 2
D_FF = 2816
N_EXPERTS = 8
TOP_K = 2
N_DENSE = (DEPTH + 1) // 2
N_MOE = DEPTH // 2

Q_A_COLS = NSA_HEADS * HEAD_DIM
KV_A_COLS = 3 * 2 * NSA_KV_HEADS * HEAD_DIM
GATE_A_COLS = NSA_HEADS * 3
QKV_B_COLS = 3 * N_DIL_GROUPS * DIL_HEADS * HEAD_DIM
MERGE_COLS = N_BRANCHES * D_MODEL
IN_COLS = Q_A_COLS + KV_A_COLS + GATE_A_COLS + QKV_B_COLS + MERGE_COLS
SPLITS = (Q_A_COLS, Q_A_COLS + KV_A_COLS, Q_A_COLS + KV_A_COLS + GATE_A_COLS,
          Q_A_COLS + KV_A_COLS + GATE_A_COLS + QKV_B_COLS)
OUT_A = NSA_HEADS * HEAD_DIM
OUT_B = DIL_HEADS * HEAD_DIM

kernel_name = "nsa_dilated_gated_hybrid_moe"

F32 = jnp.float32


def rms_norm(x, g):
    x32 = x.astype(F32)
    y = x32 * lax.rsqrt(jnp.mean(x32 * x32, axis=-1, keepdims=True) + EPS)
    return (y * g.astype(F32)).astype(x.dtype)


def rope_tables(positions):
    inv = ROPE_THETA ** (-jnp.arange(0, ROT_DIM, 2, dtype=F32) / ROT_DIM)
    ang = positions.astype(F32)[..., None] * inv
    return jnp.cos(ang)[:, :, None, :], jnp.sin(ang)[:, :, None, :]


def apply_rope(x, cos, sin):
    half = ROT_DIM // 2
    xf = x.astype(F32)
    x1, x2 = xf[..., :half], xf[..., half:ROT_DIM]
    out = jnp.concatenate([x1 * cos - x2 * sin, x2 * cos + x1 * sin, xf[..., ROT_DIM:]], axis=-1)
    return out.astype(x.dtype)


def masked_softmax(s, mask):
    s = jnp.where(mask, s, -jnp.inf)
    m = jnp.max(s, axis=-1, keepdims=True)
    m = jnp.where(jnp.isfinite(m), m, 0.0)
    e = jnp.where(mask, jnp.exp(s - m), 0.0)
    den = jnp.sum(e, axis=-1, keepdims=True)
    return e / jnp.where(den > 0, den, 1.0)


def banded_attention(q, k, v, n_back):
    B, L, H, dh = q.shape
    G = k.shape[2]
    hpg = H // G
    C = ATTN_BLOCK
    nb = -(-L // C)
    tail = nb * C - L
    npv = -(-n_back // C)
    q = jnp.pad(q, ((0, 0), (0, tail), (0, 0), (0, 0)))
    pad_kv = ((0, 0), (npv * C, tail), (0, 0), (0, 0))
    kb = jnp.pad(k, pad_kv).reshape(B, nb + npv, C, G, dh)
    vb = jnp.pad(v, pad_kv).reshape(B, nb + npv, C, G, dh)
    win = jnp.arange(nb)[:, None] + jnp.arange(npv + 1)[None, :]
    K = (npv + 1) * C
    kw = kb[:, win].reshape(B, nb, K, G, dh)
    vw = vb[:, win].reshape(B, nb, K, G, dh)
    qb = q.reshape(B, nb, C, G, hpg, dh)
    s = jnp.einsum('bnqghd,bnkgd->bnghqk', qb.astype(F32), kw.astype(F32)) * (dh ** -0.5)
    t = jnp.arange(nb)[:, None] * C + jnp.arange(C)[None, :]
    kpos = (jnp.arange(nb)[:, None] - npv) * C + jnp.arange(K)[None, :]
    diff = t[:, :, None] - kpos[:, None, :]
    mask = (diff >= 0) & (diff <= n_back) & (kpos[:, None, :] >= 0)
    s = jnp.where(mask[None, :, None, None], s, -jnp.inf)
    lse = jax.nn.logsumexp(s, axis=-1)
    p = jnp.exp(s - lse[..., None])
    o = jnp.einsum('bnghqk,bnkgd->bnqghd', p, vw.astype(F32))
    o = o.reshape(B, nb * C, H, dh)[:, :L]
    lse = lse.transpose(0, 1, 4, 2, 3).reshape(B, nb * C, H)[:, :L]
    return o, lse


def dilated_attention(q, k, v, window, dilation):
    B, S, H, dh = q.shape
    L = S // dilation

    def fold(a):
        return a.reshape(B, L, dilation, a.shape[2], dh).transpose(0, 2, 1, 3, 4).reshape(B * dilation, L, a.shape[2], dh)

    o, lse = banded_attention(fold(q), fold(k), fold(v), window // dilation)
    o = o.reshape(B, dilation, L, H, dh).transpose(0, 2, 1, 3, 4).reshape(B, S, H, dh)
    lse = lse.reshape(B, dilation, L, H).transpose(0, 2, 1, 3).reshape(B, S, H)
    return o, lse


def dilated_mixer(qkv_b, cos, sin):
    B, S, _ = qkv_b.shape
    qkv = qkv_b.reshape(B, S, 3, N_DIL_GROUPS, DIL_HEADS, HEAD_DIM)
    outs, lses = [], []
    for g, (w, d) in enumerate(DIL_PATTERNS):
        q = apply_rope(qkv[:, :, 0, g], cos, sin)
        k = apply_rope(qkv[:, :, 1, g], cos, sin)
        o, l = dilated_attention(q, k, qkv[:, :, 2, g], w, d)
        outs.append(o)
        lses.append(l)
    alpha = jax.nn.softmax(jnp.stack(lses, axis=0), axis=0)
    o = jnp.sum(alpha[..., None] * jnp.stack(outs, axis=0), axis=0)
    return o.reshape(B, S, OUT_B).astype(qkv_b.dtype)


def nsa_mixer(q, kv, gate_logits, cos, sin, pos_k, pos_v, wk1, wk2, wv1, wv2):
    B, S, _ = q.shape
    H, G, dh = NSA_HEADS, NSA_KV_HEADS, HEAD_DIM
    hpg = H // G
    scale = dh ** -0.5
    q = apply_rope(q.reshape(B, S, H, dh), cos, sin)
    kv = kv.reshape(B, S, 3, 2, G, dh)
    k_cmp, v_cmp = apply_rope(kv[:, :, 0, 0], cos, sin), kv[:, :, 0, 1]
    k_sel, v_sel = apply_rope(kv[:, :, 1, 0], cos, sin), kv[:, :, 1, 1]
    k_win, v_win = apply_rope(kv[:, :, 2, 0], cos, sin), kv[:, :, 2, 1]
    qg = q.reshape(B, S, G, hpg, dh).astype(F32)
    t = jnp.arange(S)

    n_c = (S - CMP_LEN) // CMP_STRIDE + 1
    starts = jnp.arange(n_c) * CMP_STRIDE
    idx = starts[:, None] + jnp.arange(CMP_LEN)[None, :]

    def compress(a, pos, w1, w2):
        blk = a[:, idx] + pos[None, None, :, None, :]
        blk = blk.transpose(0, 1, 3, 2, 4).reshape(B, n_c, G, CMP_LEN * dh)
        return jax.nn.silu(blk @ w1) @ w2

    kc = compress(k_cmp, pos_k, wk1, wk2)
    vc = compress(v_cmp, pos_v, wv1, wv2)
    s = jnp.einsum('bsghd,bcgd->bghsc', qg, kc.astype(F32)) * scale
    cmask = (starts + CMP_LEN - 1)[None, :] <= t[:, None]
    p_cmp = masked_softmax(s, cmask)
    o_cmp = jnp.einsum('bghsc,bcgd->bsghd', p_cmp, vc.astype(F32))

    nbs = S // SEL_LEN
    bstart = jnp.arange(nbs) * SEL_LEN
    overlap = jnp.clip(jnp.minimum(starts[:, None] + CMP_LEN, bstart[None, :] + SEL_LEN)
                       - jnp.maximum(starts[:, None], bstart[None, :]), 0)
    M = overlap.astype(F32) / CMP_LEN
    imp = jnp.einsum('bghsc,cj->bgsj', p_cmp, M)
    cur = (t // SEL_LEN)[:, None]
    j = jnp.arange(nbs)[None, :]
    forced = (j == 0) | (j == cur) | (j == cur - 1)
    imp = jnp.where(forced, jnp.inf, imp)
    imp = jnp.where(j > cur, -jnp.inf, imp)
    n_pick = min(N_SEL, nbs)
    _, sel_idx = lax.top_k(imp, n_pick)

    Cq = SEL_Q_BLOCK
    nq = S // Cq
    ksb = k_sel.reshape(B, nbs, SEL_LEN, G, dh).transpose(0, 3, 1, 2, 4)
    vsb = v_sel.reshape(B, nbs, SEL_LEN, G, dh).transpose(0, 3, 1, 2, 4)
    q_ch = qg.reshape(B, nq, Cq, G, hpg, dh).transpose(1, 0, 2, 3, 4, 5)
    idx_ch = sel_idx.reshape(B, G, nq, Cq, n_pick).transpose(2, 0, 1, 3, 4)
    t_ch = t.reshape(nq, Cq)
    b_ix = jnp.arange(B)[:, None, None, None]
    g_ix = jnp.arange(G)[None, :, None, None]
    Ksel = n_pick * SEL_LEN

    def sel_block(args):
        qc, ic, tc = args
        kg = ksb[b_ix, g_ix, ic].reshape(B, G, Cq, Ksel, dh)
        vg = vsb[b_ix, g_ix, ic].reshape(B, G, Cq, Ksel, dh)
        kpos = (ic[..., None] * SEL_LEN + jnp.arange(SEL_LEN)).reshape(B, G, Cq, Ksel)
        sc = jnp.einsum('bqghd,bgqkd->bghqk', qc, kg.astype(F32)) * scale
        m = (kpos <= tc[None, None, :, None])[:, :, None]
        p = jax.nn.softmax(jnp.where(m, sc, -jnp.inf), axis=-1)
        return jnp.einsum('bghqk,bgqkd->bqghd', p, vg.astype(F32))

    o_sel = lax.map(sel_block, (q_ch, idx_ch, t_ch))
    o_sel = o_sel.transpose(1, 0, 2, 3, 4, 5).reshape(B, S, H, dh)

    o_win, _ = banded_attention(q, k_win, v_win, WIN - 1)

    gates = jax.nn.sigmoid(gate_logits.astype(F32)).reshape(B, S, H, 3)
    o = (gates[..., 0:1] * o_cmp.reshape(B, S, H, dh)
         + gates[..., 1:2] * o_sel
         + gates[..., 2:3] * o_win)
    return o.reshape(B, S, OUT_A).astype(q.dtype)


def hybrid_mixer(h, cos, sin, w_in, pos_k, pos_v, wk1, wk2, wv1, wv2, p_a, p_b, w_o):
    B, S, D = h.shape
    proj = h @ w_in
    q_a, kv_a, g_a, qkv_b, merge = jnp.split(proj, SPLITS, axis=-1)
    y_a = nsa_mixer(q_a, kv_a, g_a, cos, sin, pos_k, pos_v, wk1, wk2, wv1, wv2) @ p_a
    y_b = dilated_mixer(qkv_b, cos, sin) @ p_b
    g = jax.nn.sigmoid(merge.astype(F32)).reshape(B, S, N_BRANCHES, D)
    y = (g[:, :, 0] * y_a.astype(F32) + g[:, :, 1] * y_b.astype(F32)).astype(h.dtype)
    return y @ w_o


def swiglu(h, w1, w3, w2):
    return (jax.nn.silu(h @ w1) * (h @ w3)) @ w2


def moe_swiglu(h, router, w1, w3, w2):
    B, S, D = h.shape
    x = h.reshape(B * S, D)
    logits = (x @ router).astype(F32)
    top_v, top_i = lax.top_k(logits, TOP_K)
    wts = jax.nn.softmax(top_v, axis=-1)
    combine = jnp.sum(jax.nn.one_hot(top_i, N_EXPERTS, dtype=F32) * wts[..., None], axis=1)
    out = jnp.zeros((B * S, D), F32)
    for e in range(N_EXPERTS):
        out = out + combine[:, e:e + 1] * swiglu(x, w1[e], w3[e], w2[e]).astype(F32)
    return out.reshape(B, S, D).astype(h.dtype)


def setup_inputs(seed: int = 0) -> dict:
    key = jax.random.key(seed)
    ks = iter(jax.random.split(key, 32))

    def nrm(shape, scale):
        return jax.random.normal(next(ks), shape, F32) * scale

    x = nrm((BATCH, SEQ, D_MODEL), 1.0)
    offset = jax.random.randint(next(ks), (BATCH, 1), 0, 4096, dtype=jnp.int32)
    positions = offset + jnp.arange(SEQ, dtype=jnp.int32)[None, :]
    return {
        "x": x,
        "positions": positions,
        "norm_mix": 1.0 + nrm((DEPTH, D_MODEL), 0.02),
        "w_in": nrm((DEPTH, D_MODEL, IN_COLS), D_MODEL ** -0.5),
        "cmp_pos_k": nrm((DEPTH, CMP_LEN, HEAD_DIM), 0.02),
        "cmp_pos_v": nrm((DEPTH, CMP_LEN, HEAD_DIM), 0.02),
        "cmp_k_w1": nrm((DEPTH, CMP_LEN * HEAD_DIM, CMP_HIDDEN), (CMP_LEN * HEAD_DIM) ** -0.5),
        "cmp_k_w2": nrm((DEPTH, CMP_HIDDEN, HEAD_DIM), CMP_HIDDEN ** -0.5),
        "cmp_v_w1": nrm((DEPTH, CMP_LEN * HEAD_DIM, CMP_HIDDEN), (CMP_LEN * HEAD_DIM) ** -0.5),
        "cmp_v_w2": nrm((DEPTH, CMP_HIDDEN, HEAD_DIM), CMP_HIDDEN ** -0.5),
        "w_branch_a": nrm((DEPTH, OUT_A, D_MODEL), OUT_A ** -0.5),
        "w_branch_b": nrm((DEPTH, OUT_B, D_MODEL), OUT_B ** -0.5),
        "w_out": nrm((DEPTH, D_MODEL, D_MODEL), D_MODEL ** -0.5),
        "norm_ffn": 1.0 + nrm((DEPTH, D_MODEL), 0.02),
        "ffn_w1": nrm((N_DENSE, D_MODEL, D_FF), D_MODEL ** -0.5),
        "ffn_w3": nrm((N_DENSE, D_MODEL, D_FF), D_MODEL ** -0.5),
        "ffn_w2": nrm((N_DENSE, D_FF, D_MODEL), D_FF ** -0.5),
        "router": nrm((N_MOE, D_MODEL, N_EXPERTS), D_MODEL ** -0.5),
        "moe_w1": nrm((N_MOE, N_EXPERTS, D_MODEL, D_FF), D_MODEL ** -0.5),
        "moe_w3": nrm((N_MOE, N_EXPERTS, D_MODEL, D_FF), D_MODEL ** -0.5),
        "moe_w2": nrm((N_MOE, N_EXPERTS, D_FF, D_MODEL), D_FF ** -0.5),
        "final_norm": 1.0 + nrm((D_MODEL,), 0.02),
    }


def reference(x, positions, norm_mix, w_in, cmp_pos_k, cmp_pos_v, cmp_k_w1, cmp_k_w2,
              cmp_v_w1, cmp_v_w2, w_branch_a, w_branch_b, w_out, norm_ffn, ffn_w1, ffn_w3,
              ffn_w2, router, moe_w1, moe_w3, moe_w2, final_norm):
    cos, sin = rope_tables(positions)
    for layer in range(DEPTH):
        h = rms_norm(x, norm_mix[layer])
        x = x + hybrid_mixer(h, cos, sin, w_in[layer], cmp_pos_k[layer], cmp_pos_v[layer],
                             cmp_k_w1[layer], cmp_k_w2[layer], cmp_v_w1[layer], cmp_v_w2[layer],
                             w_branch_a[layer], w_branch_b[layer], w_out[layer])
        h = rms_norm(x, norm_ffn[layer])
        i = layer // 2
        if layer % 2 == 0:
            x = x + swiglu(h, ffn_w1[i], ffn_w3[i], ffn_w2[i])
        else:
            x = x + moe_swiglu(h, router[i], moe_w1[i], moe_w3[i], moe_w2[i])
    return rms_norm(x, final_norm)
```

```python
import functools

import numpy as np
import jax
import jax.numpy as jnp
from jax import lax
from jax.experimental import pallas as pl
from jax.experimental.pallas import tpu as pltpu

F32 = jnp.float32
BF16 = jnp.bfloat16

D_MODEL = 1024
HEAD_DIM = 64
ROT_DIM = HEAD_DIM // 4
ROPE_THETA = 500000.0
EPS = 1e-6
NSA_HEADS = 8
NSA_KV_HEADS = 2
HEADS_PER_KV = NSA_HEADS // NSA_KV_HEADS
CMP_LEN = 32
CMP_STRIDE = 16
CMP_HIDDEN = 128
SEL_LEN = 64
N_SEL = 16
WIN = 512
DIL_PATTERNS = ((128, 1), (512, 4), (2048, 16))
N_DIL_GROUPS = 3
DIL_HEADS = 4
D_FF = 2816
N_EXPERTS = 8

LANES = 128
VMEM_LIMIT = 56 * 1024 * 1024
NEG = -1e30
BIG = 1e30

COL_QA = 0
COL_QB = 512
COL_KB = 1280
COL_MERGE = 2048
COL_VB = 4096
COL_KCMP = 4864
COL_KSEL = 4992
COL_KWIN = 5120
COL_VCMP = 5248
COL_VSEL = 5376
COL_VWIN = 5504
COL_GATE = 5632
N_PROJ = 5888
IN_CHUNKS = (
    (0, 512, True), (512, 512, True), (1024, 256, True), (1280, 512, True), (1792, 256, True),
    (2048, 512, False), (2560, 512, False), (3072, 512, False), (3584, 512, False),
    (4096, 512, False), (4608, 256, False),
    (4864, 384, True), (5248, 384, False), (5632, 256, False),
)


def _dot(a, b, precision=None):
    return jnp.dot(a, b, preferred_element_type=F32, precision=precision)


def _dot_nt(a, b, precision=None):
    return lax.dot_general(a, b, (((1,), (1,)), ((), ())), preferred_element_type=F32,
                           precision=precision)


def _rms(x, g):
    ms = jnp.mean(x * x, axis=-1, keepdims=True)
    return x * lax.rsqrt(ms + EPS) * g


def _silu(x):
    return x * jax.nn.sigmoid(x)


def _params(**kw):
    return pltpu.CompilerParams(vmem_limit_bytes=VMEM_LIMIT, **kw)


def _const_spec(shape):
    nd = len(shape)
    return pl.BlockSpec(shape, lambda *_: (0,) * nd)


def _inproj_body(x_ref, g_ref, w_ref, c_ref, sa_ref, sb_ref, o_ref):
    h = _rms(x_ref[...], g_ref[...]).astype(BF16)
    c = c_ref[...]
    sa = sa_ref[...]
    sb = sb_ref[...]
    for start, size, rope in IN_CHUNKS:
        acc = _dot(h, w_ref[:, start:start + size])
        if rope:
            for j in range(size // LANES):
                a = acc[:, j * LANES:(j + 1) * LANES]
                r = a * c + pltpu.roll(a, LANES - 8, 1) * sa + pltpu.roll(a, 8, 1) * sb
                o_ref[:, start + j * LANES:start + (j + 1) * LANES] = r.astype(BF16)
        else:
            o_ref[:, start:start + size] = acc.astype(BF16)


def _inproj(x2, g, w, rc, rsa, rsb, tm=512):
    T = x2.shape[0]
    return pl.pallas_call(
        _inproj_body,
        grid=(T // tm,),
        in_specs=[
            pl.BlockSpec((tm, D_MODEL), lambda i: (i, 0)),
            _const_spec((1, D_MODEL)),
            _const_spec((D_MODEL, N_PROJ)),
            pl.BlockSpec((tm, LANES), lambda i: (i, 0)),
            pl.BlockSpec((tm, LANES), lambda i: (i, 0)),
            pl.BlockSpec((tm, LANES), lambda i: (i, 0)),
        ],
        out_specs=pl.BlockSpec((tm, N_PROJ), lambda i: (i, 0)),
        out_shape=jax.ShapeDtypeStruct((T, N_PROJ), BF16),
        compiler_params=_params(),
        name="inproj",
    )(x2, g, w, rc, rsa, rsb)


def _compress_body(a_ref, w1_ref, w2_ref, pos_ref, o_ref):
    a = a_ref[0]
    w1 = w1_ref[0]
    half = CMP_STRIDE * HEAD_DIM
    top = _dot(a, w1[:half])
    bot = _dot(a, w1[half:])
    pc = _dot(pos_ref[0], w1)
    rows = a.shape[0]
    hid = top + pltpu.roll(bot, rows - 1, 0) + pc[0:1]
    o_ref[0] = _dot(_silu(hid).astype(BF16), w2_ref[0]).astype(BF16)


def _compress(a, w1, w2, pos):
    n, rows, _ = a.shape
    return pl.pallas_call(
        _compress_body,
        grid=(n,),
        in_specs=[
            pl.BlockSpec((1, rows, CMP_STRIDE * HEAD_DIM), lambda i: (i, 0, 0)),
            pl.BlockSpec((1, CMP_LEN * HEAD_DIM, CMP_HIDDEN), lambda i: (i, 0, 0)),
            pl.BlockSpec((1, CMP_HIDDEN, HEAD_DIM), lambda i: (i, 0, 0)),
            pl.BlockSpec((1, 8, CMP_LEN * HEAD_DIM), lambda i: (i, 0, 0)),
        ],
        out_specs=pl.BlockSpec((1, rows, HEAD_DIM), lambda i: (i, 0, 0)),
        out_shape=jax.ShapeDtypeStruct((n, rows, HEAD_DIM), BF16),
        compiler_params=_params(),
        name="compress",
    )(a, w1, w2, pos)


def _softmax_av(s, mask, v):
    nh, tq, nk = s.shape
    s = jnp.where(mask[None], s, NEG)
    m = jnp.max(s, axis=-1, keepdims=True)
    e = jnp.exp(s - m)
    l = jnp.sum(e, axis=-1, keepdims=True)
    o = _dot(e.astype(BF16).reshape(nh * tq, nk), v)
    return o / l.reshape(nh * tq, 1)


def _nsa_body(q_ref, kc_ref, vc_ref, ksel_ref, vsel_ref, kwin_ref, vwin_ref, gate_ref, mt_ref,
              o_ref, *, tq, seq):
    nblk = seq // SEL_LEN
    ncp = kc_ref.shape[1]
    n_cmp = (seq - CMP_LEN) // CMP_STRIDE + 1
    q0 = pl.program_id(1) * tq
    q = q_ref[0]
    t_col = q0 + lax.broadcasted_iota(jnp.int32, (tq, 1), 0)
    t_row = q0 + lax.broadcasted_iota(jnp.int32, (1, tq), 1)
    gates = jax.nn.sigmoid(gate_ref[0].astype(F32))

    zeros64 = jnp.zeros((tq, HEAD_DIM), BF16)

    def stacked_q(g):
        parts = []
        for hh in range(HEADS_PER_KV):
            h = g * HEADS_PER_KV + hh
            qh = q[:, h * HEAD_DIM:(h + 1) * HEAD_DIM]
            parts.append(jnp.concatenate([qh, zeros64] if g == 0 else [zeros64, qh], axis=1))
        return jnp.concatenate(parts, axis=0)

    qs = [stacked_q(g) for g in range(NSA_KV_HEADS)]

    cidx = lax.broadcasted_iota(jnp.int32, (tq, ncp), 1)
    cmask = ((cidx * CMP_STRIDE + (CMP_LEN - 1)) <= t_col) & (cidx < n_cmp)
    jidx = lax.broadcasted_iota(jnp.int32, (nblk, tq), 0)
    cur = lax.shift_right_logical(t_row, 6)
    forced = (jidx == 0) | (jidx == cur) | (jidx == cur - 1)
    future = jidx > cur
    o_cmp = []
    sel_rows = []
    for g in range(NSA_KV_HEADS):
        s = _dot_nt(qs[g], kc_ref[0]).reshape(HEADS_PER_KV, tq, ncp)
        s = jnp.where(cmask[None], s, NEG)
        m = jnp.max(s, axis=-1, keepdims=True)
        e = jnp.where(cmask[None], jnp.exp(s - m), 0.0)
        den = jnp.sum(e, axis=-1, keepdims=True)
        p = e / jnp.where(den > 0, den, 1.0)
        o_cmp.append(_dot(p.astype(BF16).reshape(HEADS_PER_KV * tq, ncp), vc_ref[0]))
        psum = p[0] + p[1] + p[2] + p[3]
        imp = _dot_nt(mt_ref[...], psum, precision=lax.Precision.HIGHEST)
        imp = jnp.where(forced, BIG, imp)
        imp = jnp.where(future, -BIG, imp)
        rank = jnp.zeros((nblk, tq), jnp.int32)
        for i in range(nblk):
            row = imp[i:i + 1, :]
            beats = (row > imp) | ((row == imp) & (jidx > i))
            rank = rank + beats.astype(jnp.int32)
        sel_rows.append((rank < N_SEL).astype(F32))
    pad_rows = LANES - NSA_KV_HEADS * nblk
    sel_t = jnp.concatenate(sel_rows + [jnp.zeros((pad_rows, tq), F32)], axis=0)
    sel = sel_t.T.astype(BF16)

    kpos = lax.broadcasted_iota(jnp.int32, (1, seq), 1)
    causal = kpos <= t_col
    erow = lax.broadcasted_iota(jnp.int32, (LANES, seq), 0)
    ecol = lax.shift_right_logical(lax.broadcasted_iota(jnp.int32, (LANES, seq), 1), 6)
    o_sel = []
    for g in range(NSA_KV_HEADS):
        expand = jnp.where(erow == ecol + g * nblk, 1.0, 0.0).astype(BF16)
        picked = _dot(sel, expand) > 0.5
        s = _dot_nt(qs[g], ksel_ref[0]).reshape(HEADS_PER_KV, tq, seq)
        o_sel.append(_softmax_av(s, picked & causal, vsel_ref[0]))

    span = WIN + tq
    ks = pl.multiple_of(jnp.maximum(q0 - WIN, 0), tq)
    kw = kwin_ref[0, pl.ds(ks, span), :]
    vw = vwin_ref[0, pl.ds(ks, span), :]
    wpos = ks + lax.broadcasted_iota(jnp.int32, (1, span), 1)
    wmask = (wpos <= t_col) & (t_col - wpos <= WIN - 1)
    o_win = []
    for g in range(NSA_KV_HEADS):
        s = _dot_nt(qs[g], kw).reshape(HEADS_PER_KV, tq, span)
        o_win.append(_softmax_av(s, wmask, vw))

    outs = []
    for g in range(NSA_KV_HEADS):
        for hh in range(HEADS_PER_KV):
            h = g * HEADS_PER_KV + hh
            acc = jnp.zeros((tq, HEAD_DIM), F32)
            for br, o in enumerate((o_cmp[g], o_sel[g], o_win[g])):
                oh = o[hh * tq:(hh + 1) * tq, g * HEAD_DIM:(g + 1) * HEAD_DIM]
                acc = acc + gates[:, 3 * h + br:3 * h + br + 1] * oh
            outs.append(acc)
    o_ref[0] = jnp.concatenate(outs, axis=1).astype(BF16)


def _nsa(proj3, kc, vc, mt, tq=128):
    B, S, _ = proj3.shape
    blk = lambda c: c // LANES
    seq_spec = lambda c: pl.BlockSpec((1, S, LANES), lambda b, i: (b, 0, blk(c)))
    ncp = kc.shape[1]
    return pl.pallas_call(
        functools.partial(_nsa_body, tq=tq, seq=S),
        grid=(B, S // tq),
        in_specs=[
            pl.BlockSpec((1, tq, NSA_HEADS * HEAD_DIM), lambda b, i: (b, i, 0)),
            pl.BlockSpec((1, ncp, LANES), lambda b, i: (b, 0, 0)),
            pl.BlockSpec((1, ncp, LANES), lambda b, i: (b, 0, 0)),
            seq_spec(COL_KSEL), seq_spec(COL_VSEL), seq_spec(COL_KWIN), seq_spec(COL_VWIN),
            pl.BlockSpec((1, tq, LANES), lambda b, i: (b, i, blk(COL_GATE))),
            _const_spec(mt.shape),
        ],
        out_specs=pl.BlockSpec((1, tq, NSA_HEADS * HEAD_DIM), lambda b, i: (b, i, 0)),
        out_shape=jax.ShapeDtypeStruct((B, S, NSA_HEADS * HEAD_DIM), BF16),
        compiler_params=_params(),
        name="nsa",
    )(proj3, kc, vc, proj3, proj3, proj3, proj3, proj3, mt)


def _dil_body(q_ref, kp_ref, kc_ref, vp_ref, vc_ref, o_ref, lse_ref, *, tq, n_back):
    i = pl.program_id(2)
    q = q_ref[0]
    kk = jnp.concatenate([kp_ref[0], kc_ref[0]], axis=0)
    vv = jnp.concatenate([vp_ref[0], vc_ref[0]], axis=0)
    tpos = i * tq + lax.broadcasted_iota(jnp.int32, (tq, 1), 0)
    kpos = (i - 1) * tq + lax.broadcasted_iota(jnp.int32, (1, 2 * tq), 1)
    diff = tpos - kpos
    mask = (kpos >= 0) & (diff >= 0) & (diff <= n_back)
    lane = lax.broadcasted_iota(jnp.int32, (tq, LANES), 1)
    lse_out = jnp.zeros((tq, LANES), F32)
    outs = []
    for h in range(DIL_HEADS):
        sl = slice(h * HEAD_DIM, (h + 1) * HEAD_DIM)
        s = jnp.where(mask, _dot_nt(q[:, sl], kk[:, sl]), NEG)
        m = jnp.max(s, axis=-1, keepdims=True)
        e = jnp.exp(s - m)
        l = jnp.sum(e, axis=-1, keepdims=True)
        outs.append(_dot(e.astype(BF16), vv[:, sl]) / l)
        lse_out = jnp.where(lane == h, m + jnp.log(l), lse_out)
    o_ref[0] = jnp.concatenate(outs, axis=1)
    lse_ref[0] = lse_out


def _dilated(proj3, group, window, dilation, tq=128):
    B, S, _ = proj3.shape
    L = S // dilation
    n_back = window // dilation
    assert n_back <= tq and L % tq == 0
    width = DIL_HEADS * HEAD_DIM
    view = proj3.reshape(B, L, dilation * N_PROJ)
    per_res = N_PROJ // width
    qb, kb, vb = (c // width + group for c in (COL_QB, COL_KB, COL_VB))
    cur = lambda c: pl.BlockSpec((1, tq, width), lambda b, r, i: (b, i, r * per_res + c))
    prev = lambda c: pl.BlockSpec((1, tq, width),
                                  lambda b, r, i: (b, jnp.maximum(i - 1, 0), r * per_res + c))
    o, lse = pl.pallas_call(
        functools.partial(_dil_body, tq=tq, n_back=n_back),
        grid=(B, dilation, L // tq),
        in_specs=[cur(qb), prev(kb), cur(kb), prev(vb), cur(vb)],
        out_specs=[pl.BlockSpec((1, tq, width), lambda b, r, i: (b, i, r)),
                   pl.BlockSpec((1, tq, LANES), lambda b, r, i: (b, i, r))],
        out_shape=[jax.ShapeDtypeStruct((B, L, dilation * width), F32),
                   jax.ShapeDtypeStruct((B, L, dilation * LANES), F32)],
        compiler_params=_params(),
        name=f"dilated{group}",
    )(view, view, view, view, view)
    return o.reshape(B * S, width), lse.reshape(B * S, LANES)


def _mixout_body(oa_ref, ob0_ref, ob1_ref, ob2_ref, l0_ref, l1_ref, l2_ref, mg_ref, x_ref,
                 pa_ref, pb_ref, wo_ref, out_ref):
    tm = x_ref.shape[0]
    lses = [l0_ref[...], l1_ref[...], l2_ref[...]]
    mx = jnp.maximum(jnp.maximum(lses[0], lses[1]), lses[2])
    ws = [jnp.exp(l - mx) for l in lses]
    den = ws[0] + ws[1] + ws[2]
    ob = jnp.zeros((tm, DIL_HEADS * HEAD_DIM), F32)
    for w, o_ref in zip(ws, (ob0_ref, ob1_ref, ob2_ref)):
        alpha = w / den
        wide = jnp.concatenate(
            [jnp.broadcast_to(alpha[:, h:h + 1], (tm, HEAD_DIM)) for h in range(DIL_HEADS)], axis=1)
        ob = ob + wide * o_ref[...]
    ya = _dot(oa_ref[...], pa_ref[...])
    yb = _dot(ob.astype(BF16), pb_ref[...])
    gm = jax.nn.sigmoid(mg_ref[...].astype(F32))
    y = gm[:, :D_MODEL] * ya + gm[:, D_MODEL:] * yb
    out_ref[...] = x_ref[...] + _dot(y.astype(BF16), wo_ref[...])


def _mixout(oa, obs, lses, proj, x2, pa, pb, wo, tm=512):
    T = x2.shape[0]
    wa, wb = NSA_HEADS * HEAD_DIM, DIL_HEADS * HEAD_DIM
    row = lambda w: pl.BlockSpec((tm, w), lambda i: (i, 0))
    return pl.pallas_call(
        _mixout_body,
        grid=(T // tm,),
        in_specs=[row(wa), row(wb), row(wb), row(wb), row(LANES), row(LANES), row(LANES),
                  pl.BlockSpec((tm, 2 * D_MODEL), lambda i: (i, COL_MERGE // (2 * D_MODEL))),
                  row(D_MODEL),
                  _const_spec(pa.shape), _const_spec(pb.shape), _const_spec(wo.shape)],
        out_specs=row(D_MODEL),
        out_shape=jax.ShapeDtypeStruct((T, D_MODEL), F32),
        compiler_params=_params(),
        name="mixout",
    )(oa, *obs, *lses, proj, x2, pa, pb, wo)


FF_CHUNK = 512


def _ffn_body(x_ref, g_ref, w1_ref, w3_ref, w2_ref, o_ref):
    x = x_ref[...]
    h = _rms(x, g_ref[...]).astype(BF16)
    acc = jnp.zeros(x.shape, F32)
    for c0 in range(0, D_FF, FF_CHUNK):
        c1 = min(c0 + FF_CHUNK, D_FF)
        act = _silu(_dot(h, w1_ref[:, c0:c1])) * _dot(h, w3_ref[:, c0:c1])
        acc = acc + _dot(act.astype(BF16), w2_ref[c0:c1, :])
    o_ref[...] = x + acc


def _ffn(x2, g, w1, w3, w2, tm=512):
    T = x2.shape[0]
    return pl.pallas_call(
        _ffn_body,
        grid=(T // tm,),
        in_specs=[pl.BlockSpec((tm, D_MODEL), lambda i: (i, 0)), _const_spec(g.shape),
                  _const_spec(w1.shape), _const_spec(w3.shape), _const_spec(w2.shape)],
        out_specs=pl.BlockSpec((tm, D_MODEL), lambda i: (i, 0)),
        out_shape=jax.ShapeDtypeStruct((T, D_MODEL), F32),
        compiler_params=_params(),
        name="ffn",
    )(x2, g, w1, w3, w2)


def _router_body(x_ref, g_ref, r_ref, comb_ref):
    h = _rms(x_ref[...], g_ref[...])
    logits = _dot(h, r_ref[...], precision=lax.Precision.HIGHEST)
    lane = lax.broadcasted_iota(jnp.int32, logits.shape, 1)
    lg = jnp.where(lane < N_EXPERTS, logits, NEG)
    m1 = jnp.max(lg, axis=-1, keepdims=True)
    i1 = jnp.min(jnp.where(lg == m1, lane, LANES), axis=-1, keepdims=True)
    lg2 = jnp.where(lane == i1, NEG, lg)
    m2 = jnp.max(lg2, axis=-1, keepdims=True)
    i2 = jnp.min(jnp.where(lg2 == m2, lane, LANES), axis=-1, keepdims=True)
    e2 = jnp.exp(m2 - m1)
    den = 1.0 + e2
    comb_ref[...] = jnp.where(lane == i1, 1.0 / den, 0.0) + jnp.where(lane == i2, e2 / den, 0.0)


def _router(x2, g, r, tm=512):
    T = x2.shape[0]
    return pl.pallas_call(
        _router_body,
        grid=(T // tm,),
        in_specs=[pl.BlockSpec((tm, D_MODEL), lambda i: (i, 0)), _const_spec(g.shape),
                  _const_spec(r.shape)],
        out_specs=pl.BlockSpec((tm, LANES), lambda i: (i, 0)),
        out_shape=jax.ShapeDtypeStruct((T, LANES), F32),
        compiler_params=_params(),
        name="router",
    )(x2, g, r)


def _moe_body(x_ref, g_ref, comb_ref, w1_ref, w3_ref, w2_ref, fn_ref, o_ref, h_scr, acc_scr):
    e = pl.program_id(1)
    f = pl.program_id(2)

    @pl.when((e == 0) & (f == 0))
    def _():
        h_scr[...] = _rms(x_ref[...], g_ref[...]).astype(BF16)
        acc_scr[...] = jnp.zeros_like(acc_scr)

    h = h_scr[...]
    act = _silu(_dot(h, w1_ref[0])) * _dot(h, w3_ref[0])
    y = _dot(act.astype(BF16), w2_ref[0])
    comb = comb_ref[...]
    lane = lax.broadcasted_iota(jnp.int32, comb.shape, 1)
    ce = jnp.sum(jnp.where(lane == e, comb, 0.0), axis=-1, keepdims=True)
    acc_scr[...] += ce * y

    @pl.when((e == pl.num_programs(1) - 1) & (f == pl.num_programs(2) - 1))
    def _():
        o_ref[...] = _rms(x_ref[...] + acc_scr[...], fn_ref[...])


def _moe(x2, g, comb, w1, w3, w2, fn, tm=512, nf=2):
    T = x2.shape[0]
    fc = D_FF // nf
    return pl.pallas_call(
        _moe_body,
        grid=(T // tm, N_EXPERTS, nf),
        in_specs=[pl.BlockSpec((tm, D_MODEL), lambda i, e, f: (i, 0)),
                  _const_spec(g.shape),
                  pl.BlockSpec((tm, LANES), lambda i, e, f: (i, 0)),
                  pl.BlockSpec((1, D_MODEL, fc), lambda i, e, f: (e, 0, f)),
                  pl.BlockSpec((1, D_MODEL, fc), lambda i, e, f: (e, 0, f)),
                  pl.BlockSpec((1, fc, D_MODEL), lambda i, e, f: (e, f, 0)),
                  _const_spec(fn.shape)],
        out_specs=pl.BlockSpec((tm, D_MODEL), lambda i, e, f: (i, 0)),
        out_shape=jax.ShapeDtypeStruct((T, D_MODEL), F32),
        scratch_shapes=[pltpu.VMEM((tm, D_MODEL), BF16), pltpu.VMEM((tm, D_MODEL), F32)],
        compiler_params=_params(),
        name="moe",
    )(x2, g, comb, w1, w3, w2, fn)


def _rope_lane_tables(positions):
    half = ROT_DIM // 2
    inv = ROPE_THETA ** (-jnp.arange(0, ROT_DIM, 2, dtype=F32) / ROT_DIM)
    ang = positions.astype(F32).reshape(-1, 1) * inv
    cos, sin = jnp.cos(ang), jnp.sin(ang)
    lane = np.arange(LANES)
    within = lane % HEAD_DIM
    pick = lane % half
    cos_l, sin_l = cos[:, pick], sin[:, pick]
    c = jnp.where(within < ROT_DIM, cos_l, 1.0)
    sa = jnp.where(within < half, -sin_l, 0.0)
    sb = jnp.where((within >= half) & (within < ROT_DIM), sin_l, 0.0)
    return c, sa, sb


def _permute_w_in(w):
    scale = HEAD_DIM ** -0.5
    qa, kv, gate, qkv_b, merge = 0, 512, 1280, 1304, 3608
    nb = N_DIL_GROUPS * DIL_HEADS * HEAD_DIM
    kv_piece = lambda j: w[:, kv + 128 * j:kv + 128 * (j + 1)]
    pieces = [
        w[:, qa:qa + 512] * scale,
        w[:, qkv_b:qkv_b + nb] * scale,
        w[:, qkv_b + nb:qkv_b + 2 * nb],
        w[:, merge:merge + 2 * D_MODEL],
        w[:, qkv_b + 2 * nb:qkv_b + 3 * nb],
        kv_piece(0), kv_piece(2), kv_piece(4),
        kv_piece(1), kv_piece(3), kv_piece(5),
        w[:, gate:gate + 3 * NSA_HEADS],
    ]
    out = jnp.concatenate(pieces, axis=1)
    return jnp.pad(out, ((0, 0), (0, N_PROJ - out.shape[1]))).astype(BF16)


def _importance_matrix_t(seq, ncp):
    n_c = (seq - CMP_LEN) // CMP_STRIDE + 1
    starts = np.arange(n_c) * CMP_STRIDE
    bstart = np.arange(seq // SEL_LEN) * SEL_LEN
    overlap = np.clip(np.minimum(starts[:, None] + CMP_LEN, bstart[None, :] + SEL_LEN)
                      - np.maximum(starts[:, None], bstart[None, :]), 0, None)
    m = np.zeros((ncp, seq // SEL_LEN), np.float32)
    m[:n_c] = overlap.astype(np.float32) / CMP_LEN
    return jnp.asarray(m.T)


def _cmp_rows(proj3, col):
    B, S, _ = proj3.shape
    a = proj3[:, :, col:col + LANES].reshape(B, S // CMP_STRIDE, CMP_STRIDE, NSA_KV_HEADS, HEAD_DIM)
    return a.transpose(0, 3, 1, 2, 4).reshape(B * NSA_KV_HEADS * (S // CMP_STRIDE),
                                              CMP_STRIDE * HEAD_DIM)


def _mixer(x2, B, S, tables, mt, norm_g, w_in, pos_k, pos_v, wk1, wk2, wv1, wv2, p_a, p_b, w_o):
    proj = _inproj(x2, norm_g.reshape(1, -1), _permute_w_in(w_in), *tables)
    proj3 = proj.reshape(B, S, N_PROJ)
    ncp = S // CMP_STRIDE
    a = jnp.stack([_cmp_rows(proj3, COL_KCMP), _cmp_rows(proj3, COL_VCMP)])
    pos = jnp.stack([pos_k, pos_v]).reshape(2, 1, CMP_LEN * HEAD_DIM)
    pos = jnp.broadcast_to(pos, (2, 8, CMP_LEN * HEAD_DIM)).astype(BF16)
    cmp = _compress(a, jnp.stack([wk1, wv1]).astype(BF16), jnp.stack([wk2, wv2]).astype(BF16), pos)
    cmp = cmp.reshape(2, B, NSA_KV_HEADS, ncp, HEAD_DIM).transpose(0, 1, 3, 2, 4)
    cmp = cmp.reshape(2, B, ncp, NSA_KV_HEADS * HEAD_DIM)
    oa = _nsa(proj3, cmp[0], cmp[1], mt).reshape(B * S, NSA_HEADS * HEAD_DIM)
    obs, lses = [], []
    for g, (w, d) in enumerate(DIL_PATTERNS):
        o, lse = _dilated(proj3, g, w, d)
        obs.append(o)
        lses.append(lse)
    return _mixout(oa, obs, lses, proj, x2, p_a.astype(BF16), p_b.astype(BF16), w_o.astype(BF16))


def kernel(x, positions, norm_mix, w_in, cmp_pos_k, cmp_pos_v, cmp_k_w1, cmp_k_w2, cmp_v_w1,
           cmp_v_w2, w_branch_a, w_branch_b, w_out, norm_ffn, ffn_w1, ffn_w3, ffn_w2, router,
           moe_w1, moe_w3, moe_w2, final_norm):
    B, S, D = x.shape
    depth = norm_mix.shape[0]
    assert depth == 2 and D == D_MODEL
    tables = _rope_lane_tables(positions)
    mt = _importance_matrix_t(S, S // CMP_STRIDE)
    x2 = x.reshape(B * S, D)
    x2 = _mixer(x2, B, S, tables, mt, norm_mix[0], w_in[0], cmp_pos_k[0], cmp_pos_v[0],
                cmp_k_w1[0], cmp_k_w2[0], cmp_v_w1[0], cmp_v_w2[0],
                w_branch_a[0], w_branch_b[0], w_out[0])
    x2 = _ffn(x2, norm_ffn[0].reshape(1, -1), ffn_w1[0].astype(BF16), ffn_w3[0].astype(BF16),
              ffn_w2[0].astype(BF16))
    x2 = _mixer(x2, B, S, tables, mt, norm_mix[1], w_in[1], cmp_pos_k[1], cmp_pos_v[1],
                cmp_k_w1[1], cmp_k_w2[1], cmp_v_w1[1], cmp_v_w2[1],
                w_branch_a[1], w_branch_b[1], w_out[1])
    g1 = norm_ffn[1].reshape(1, -1)
    r = jnp.pad(router[0], ((0, 0), (0, LANES - N_EXPERTS)))
    comb = _router(x2, g1, r)
    out = _moe(x2, g1, comb, moe_w1[0].astype(BF16), moe_w3[0].astype(BF16),
               moe_w2[0].astype(BF16), final_norm.reshape(1, -1))
    return out.reshape(B, S, D)
```

```python
import functools

import numpy as np
import jax
import jax.numpy as jnp
from jax import lax
from jax.experimental import pallas as pl
from jax.experimental.pallas import tpu as pltpu

F32 = jnp.float32
BF16 = jnp.bfloat16

D_MODEL = 1024
HEAD_DIM = 64
ROT_DIM = HEAD_DIM // 4
ROPE_THETA = 500000.0
EPS = 1e-6
NSA_HEADS = 8
NSA_KV_HEADS = 2
HEADS_PER_KV = NSA_HEADS // NSA_KV_HEADS
CMP_LEN = 32
CMP_STRIDE = 16
CMP_HIDDEN = 128
SEL_LEN = 64
N_SEL = 16
WIN = 512
DIL_PATTERNS = ((128, 1), (512, 4), (2048, 16))
N_DIL_GROUPS = 3
DIL_HEADS = 4
D_FF = 2816
N_EXPERTS = 8

LANES = 128
VMEM_LIMIT = 56 * 1024 * 1024
NEG = -1e30
BIG = 1e30

COL_QA = 0
COL_QB0 = 512
COL_KB0 = 768
COL_VB0 = 1024
COL_KCMP = 1280
COL_KSEL = 1408
COL_KWIN = 1536
COL_VCMP = 1664
COL_VSEL = 1792
COL_VWIN = 1920
COL_MERGE = 2048
COL_GATE = 4096
N_PROJ = 4224
DIL_WIDTH = DIL_HEADS * HEAD_DIM
N_FOLD = 2 * DIL_WIDTH
N_W_IN = N_PROJ + 3 * N_FOLD
IN_CHUNKS = (
    (0, 512, True, None), (512, 512, True, None), (1024, 256, False, None),
    (1280, 384, True, None), (1664, 384, False, None),
    (2048, 512, False, None), (2560, 512, False, None), (3072, 512, False, None),
    (3584, 512, False, None), (4096, 128, False, None),
    (N_PROJ, N_FOLD, True, 0), (N_PROJ + N_FOLD, N_FOLD, True, 1),
    (N_PROJ + 2 * N_FOLD, N_FOLD, False, 2),
)


def _dot(a, b, precision=None):
    return jnp.dot(a, b, preferred_element_type=F32, precision=precision)


def _dot_nt(a, b, precision=None):
    return lax.dot_general(a, b, (((1,), (1,)), ((), ())), preferred_element_type=F32,
                           precision=precision)


def _rms(x, g):
    ms = jnp.mean(x * x, axis=-1, keepdims=True)
    return x * lax.rsqrt(ms + EPS) * g


def _silu(x):
    return x * jax.nn.sigmoid(x)


def _params(**kw):
    return pltpu.CompilerParams(vmem_limit_bytes=VMEM_LIMIT, **kw)


def _const_spec(shape):
    nd = len(shape)
    return pl.BlockSpec(shape, lambda *_: (0,) * nd)


def _inproj_body(x_ref, g_ref, w_ref, c_ref, sa_ref, sb_ref, o_ref, f1_ref, f2_ref, st_ref):
    tm = x_ref.shape[0]
    h = _rms(x_ref[...], g_ref[...]).astype(BF16)
    c = c_ref[...]
    sa = sa_ref[...]
    sb = sb_ref[...]
    for start, size, rope, piece in IN_CHUNKS:
        acc = _dot(h, w_ref[:, start:start + size])
        for j in range(size // LANES):
            a = acc[:, j * LANES:(j + 1) * LANES]
            if rope:
                a = a * c + pltpu.roll(a, LANES - 8, 1) * sa + pltpu.roll(a, 8, 1) * sb
            if piece is None:
                o_ref[:, start + j * LANES:start + (j + 1) * LANES] = a.astype(BF16)
            else:
                st_ref[j] = a
        if piece is not None:
            slabs = DIL_WIDTH // LANES
            for gi, f_ref in enumerate((f1_ref, f2_ref)):
                d = DIL_PATTERNS[gi + 1][1]
                for r in range(d):
                    for k in range(slabs):
                        rows = st_ref[gi * slabs + k, pl.ds(r, tm // d, stride=d), :]
                        c0 = piece * DIL_WIDTH + k * LANES
                        f_ref[0, r, :, c0:c0 + LANES] = rows.astype(BF16)


def _inproj(x2, g, w, rc, rsa, rsb, B, S, tm=512):
    T = x2.shape[0]
    per_b = S // tm
    d1, d2 = DIL_PATTERNS[1][1], DIL_PATTERNS[2][1]
    fold_spec = lambda d: pl.BlockSpec((1, d, tm // d, 3 * DIL_WIDTH),
                                       lambda i: (i // per_b, 0, i % per_b, 0))
    fold_shape = lambda d: jax.ShapeDtypeStruct((B, d, S // d, 3 * DIL_WIDTH), BF16)
    return pl.pallas_call(
        _inproj_body,
        grid=(T // tm,),
        in_specs=[
            pl.BlockSpec((tm, D_MODEL), lambda i: (i, 0)),
            _const_spec((1, D_MODEL)),
            _const_spec((D_MODEL, N_W_IN)),
            pl.BlockSpec((tm, LANES), lambda i: (i, 0)),
            pl.BlockSpec((tm, LANES), lambda i: (i, 0)),
            pl.BlockSpec((tm, LANES), lambda i: (i, 0)),
        ],
        out_specs=[pl.BlockSpec((tm, N_PROJ), lambda i: (i, 0)), fold_spec(d1), fold_spec(d2)],
        out_shape=[jax.ShapeDtypeStruct((T, N_PROJ), BF16), fold_shape(d1), fold_shape(d2)],
        scratch_shapes=[pltpu.VMEM((N_FOLD // LANES, tm, LANES), F32)],
        compiler_params=_params(),
        name="inproj",
    )(x2, g, w, rc, rsa, rsb)


def _compress_body(a_ref, w1_ref, w2_ref, pos_ref, o_ref):
    a = a_ref[0]
    w1 = w1_ref[0]
    half = CMP_STRIDE * HEAD_DIM
    top = _dot(a, w1[:half])
    bot = _dot(a, w1[half:])
    pc = _dot(pos_ref[0], w1)
    rows = a.shape[0]
    hid = top + pltpu.roll(bot, rows - 1, 0) + pc[0:1]
    o_ref[0] = _dot(_silu(hid).astype(BF16), w2_ref[0]).astype(BF16)


def _compress(a, w1, w2, pos):
    n, rows, _ = a.shape
    return pl.pallas_call(
        _compress_body,
        grid=(n,),
        in_specs=[
            pl.BlockSpec((1, rows, CMP_STRIDE * HEAD_DIM), lambda i: (i, 0, 0)),
            pl.BlockSpec((1, CMP_LEN * HEAD_DIM, CMP_HIDDEN), lambda i: (i, 0, 0)),
            pl.BlockSpec((1, CMP_HIDDEN, HEAD_DIM), lambda i: (i, 0, 0)),
            pl.BlockSpec((1, 8, CMP_LEN * HEAD_DIM), lambda i: (i, 0, 0)),
        ],
        out_specs=pl.BlockSpec((1, rows, HEAD_DIM), lambda i: (i, 0, 0)),
        out_shape=jax.ShapeDtypeStruct((n, rows, HEAD_DIM), BF16),
        compiler_params=_params(),
        name="compress",
    )(a, w1, w2, pos)


def _softmax_av(s, mask, v):
    nh, tq, nk = s.shape
    s = jnp.where(mask[None], s, NEG)
    m = jnp.max(s, axis=-1, keepdims=True)
    e = jnp.exp(s - m)
    l = jnp.sum(e, axis=-1, keepdims=True)
    o = _dot(e.astype(BF16).reshape(nh * tq, nk), v)
    return o / l.reshape(nh * tq, 1)


SEL_PREFIX = 512


def _nsa_body(q_ref, kc_ref, vc_ref, ksel_ref, vsel_ref, kwin_ref, vwin_ref, gate_ref, mt_ref,
              o_ref, osel_ref, *, tq, seq):
    nblk = seq // SEL_LEN
    ncp = kc_ref.shape[1]
    n_cmp = (seq - CMP_LEN) // CMP_STRIDE + 1
    q0 = pl.program_id(1) * tq
    q = q_ref[0]
    t_col = q0 + lax.broadcasted_iota(jnp.int32, (tq, 1), 0)
    t_row = q0 + lax.broadcasted_iota(jnp.int32, (1, tq), 1)
    gates = jax.nn.sigmoid(gate_ref[0].astype(F32))

    zeros64 = jnp.zeros((tq, HEAD_DIM), BF16)

    def stacked_q(g):
        parts = []
        for hh in range(HEADS_PER_KV):
            h = g * HEADS_PER_KV + hh
            qh = q[:, h * HEAD_DIM:(h + 1) * HEAD_DIM]
            parts.append(jnp.concatenate([qh, zeros64] if g == 0 else [zeros64, qh], axis=1))
        return jnp.concatenate(parts, axis=0)

    qs = [stacked_q(g) for g in range(NSA_KV_HEADS)]

    cidx = lax.broadcasted_iota(jnp.int32, (tq, ncp), 1)
    cmask = ((cidx * CMP_STRIDE + (CMP_LEN - 1)) <= t_col) & (cidx < n_cmp)
    jidx = lax.broadcasted_iota(jnp.int32, (nblk, tq), 0)
    cur = lax.shift_right_logical(t_row, 6)
    forced = (jidx == 0) | (jidx == cur) | (jidx == cur - 1)
    future = jidx > cur
    o_cmp = []
    sel_rows = []
    for g in range(NSA_KV_HEADS):
        s = _dot_nt(qs[g], kc_ref[0]).reshape(HEADS_PER_KV, tq, ncp)
        s = jnp.where(cmask[None], s, NEG)
        m = jnp.max(s, axis=-1, keepdims=True)
        e = jnp.where(cmask[None], jnp.exp(s - m), 0.0)
        den = jnp.sum(e, axis=-1, keepdims=True)
        p = e / jnp.where(den > 0, den, 1.0)
        o_cmp.append(_dot(p.astype(BF16).reshape(HEADS_PER_KV * tq, ncp), vc_ref[0]))
        psum = p[0] + p[1] + p[2] + p[3]
        imp = _dot_nt(mt_ref[...], psum, precision=lax.Precision.HIGHEST)
        imp = jnp.where(forced, BIG, imp)
        imp = jnp.where(future, -BIG, imp)
        rank = jnp.zeros((nblk, tq), jnp.int32)
        for i in range(nblk):
            row = imp[i:i + 1, :]
            beats = (row > imp) | ((row == imp) & (jidx > i))
            rank = rank + beats.astype(jnp.int32)
        sel_rows.append((rank < N_SEL).astype(F32))
    pad_rows = LANES - NSA_KV_HEADS * nblk
    sel_t = jnp.concatenate(sel_rows + [jnp.zeros((pad_rows, tq), F32)], axis=0)
    sel = sel_t.T.astype(BF16)

    n_prefix = q0 // SEL_PREFIX + 1
    for n in range(1, seq // SEL_PREFIX + 1):
        klen = n * SEL_PREFIX

        @pl.when(n_prefix == n)
        def _(klen=klen):
            kpos = lax.broadcasted_iota(jnp.int32, (1, klen), 1)
            causal = kpos <= t_col
            erow = lax.broadcasted_iota(jnp.int32, (LANES, klen), 0)
            ecol = lax.shift_right_logical(lax.broadcasted_iota(jnp.int32, (LANES, klen), 1), 6)
            for g in range(NSA_KV_HEADS):
                expand = jnp.where(erow == ecol + g * nblk, 1.0, 0.0).astype(BF16)
                picked = _dot(sel, expand) > 0.5
                s = _dot_nt(qs[g], ksel_ref[0, :klen, :]).reshape(HEADS_PER_KV, tq, klen)
                osel_ref[g] = _softmax_av(s, picked & causal, vsel_ref[0, :klen, :])

    o_sel = [osel_ref[g] for g in range(NSA_KV_HEADS)]

    span = WIN + tq
    ks = pl.multiple_of(jnp.maximum(q0 - WIN, 0), tq)
    kw = kwin_ref[0, pl.ds(ks, span), :]
    vw = vwin_ref[0, pl.ds(ks, span), :]
    wpos = ks + lax.broadcasted_iota(jnp.int32, (1, span), 1)
    wmask = (wpos <= t_col) & (t_col - wpos <= WIN - 1)
    o_win = []
    for g in range(NSA_KV_HEADS):
        s = _dot_nt(qs[g], kw).reshape(HEADS_PER_KV, tq, span)
        o_win.append(_softmax_av(s, wmask, vw))

    outs = []
    for g in range(NSA_KV_HEADS):
        for hh in range(HEADS_PER_KV):
            h = g * HEADS_PER_KV + hh
            acc = jnp.zeros((tq, HEAD_DIM), F32)
            for br, o in enumerate((o_cmp[g], o_sel[g], o_win[g])):
                oh = o[hh * tq:(hh + 1) * tq, g * HEAD_DIM:(g + 1) * HEAD_DIM]
                acc = acc + gates[:, 3 * h + br:3 * h + br + 1] * oh
            outs.append(acc)
    o_ref[0] = jnp.concatenate(outs, axis=1).astype(BF16)


def _nsa(proj3, kc, vc, mt, tq=128):
    B, S, _ = proj3.shape
    blk = lambda c: c // LANES
    seq_spec = lambda c: pl.BlockSpec((1, S, LANES), lambda b, i: (b, 0, blk(c)))
    ncp = kc.shape[1]
    return pl.pallas_call(
        functools.partial(_nsa_body, tq=tq, seq=S),
        grid=(B, S // tq),
        in_specs=[
            pl.BlockSpec((1, tq, NSA_HEADS * HEAD_DIM), lambda b, i: (b, i, 0)),
            pl.BlockSpec((1, ncp, LANES), lambda b, i: (b, 0, 0)),
            pl.BlockSpec((1, ncp, LANES), lambda b, i: (b, 0, 0)),
            seq_spec(COL_KSEL), seq_spec(COL_VSEL), seq_spec(COL_KWIN), seq_spec(COL_VWIN),
            pl.BlockSpec((1, tq, LANES), lambda b, i: (b, i, blk(COL_GATE))),
            _const_spec(mt.shape),
        ],
        out_specs=pl.BlockSpec((1, tq, NSA_HEADS * HEAD_DIM), lambda b, i: (b, i, 0)),
        out_shape=jax.ShapeDtypeStruct((B, S, NSA_HEADS * HEAD_DIM), BF16),
        scratch_shapes=[pltpu.VMEM((NSA_KV_HEADS, HEADS_PER_KV * tq, LANES), F32)],
        compiler_params=_params(),
        name="nsa",
    )(proj3, kc, vc, proj3, proj3, proj3, proj3, proj3, mt)


def _dil_body(q_ref, kp_ref, kc_ref, vp_ref, vc_ref, o_ref, lse_ref, *, tq, n_back):
    i = pl.program_id(2)
    q = q_ref[0, 0]
    kk = jnp.concatenate([kp_ref[0, 0], kc_ref[0, 0]], axis=0)
    vv = jnp.concatenate([vp_ref[0, 0], vc_ref[0, 0]], axis=0)
    tpos = i * tq + lax.broadcasted_iota(jnp.int32, (tq, 1), 0)
    kpos = (i - 1) * tq + lax.broadcasted_iota(jnp.int32, (1, 2 * tq), 1)
    diff = tpos - kpos
    mask = (kpos >= 0) & (diff >= 0) & (diff <= n_back)
    lane = lax.broadcasted_iota(jnp.int32, (tq, LANES), 1)
    lse_out = jnp.zeros((tq, LANES), F32)
    outs = []
    for h in range(DIL_HEADS):
        sl = slice(h * HEAD_DIM, (h + 1) * HEAD_DIM)
        s = jnp.where(mask, _dot_nt(q[:, sl], kk[:, sl]), NEG)
        m = jnp.max(s, axis=-1, keepdims=True)
        e = jnp.exp(s - m)
        l = jnp.sum(e, axis=-1, keepdims=True)
        outs.append(_dot(e.astype(BF16), vv[:, sl]) / l)
        lse_out = jnp.where(lane == h, m + jnp.log(l), lse_out)
    o_ref[0, 0] = jnp.concatenate(outs, axis=1)
    lse_ref[0, 0] = lse_out


def _dilated(arr, cols, window, dilation, name, tq=128):
    B, d, L, _ = arr.shape
    n_back = window // dilation
    assert d == dilation and n_back <= tq and L % tq == 0
    qc, kc, vc = cols
    cur = lambda c: pl.BlockSpec((1, 1, tq, DIL_WIDTH), lambda b, r, i: (b, r, i, c))
    prev = lambda c: pl.BlockSpec((1, 1, tq, DIL_WIDTH),
                                  lambda b, r, i: (b, r, jnp.maximum(i - 1, 0), c))
    return pl.pallas_call(
        functools.partial(_dil_body, tq=tq, n_back=n_back),
        grid=(B, dilation, L // tq),
        in_specs=[cur(qc), prev(kc), cur(kc), prev(vc), cur(vc)],
        out_specs=[pl.BlockSpec((1, 1, tq, DIL_WIDTH), lambda b, r, i: (b, r, i, 0)),
                   pl.BlockSpec((1, 1, tq, LANES), lambda b, r, i: (b, r, i, 0))],
        out_shape=[jax.ShapeDtypeStruct((B, dilation, L, DIL_WIDTH), F32),
                   jax.ShapeDtypeStruct((B, dilation, L, LANES), F32)],
        compiler_params=_params(),
        name=name,
    )(arr, arr, arr, arr, arr)


def _mixout_body(oa_ref, ob0_ref, ob1_ref, ob2_ref, l0_ref, l1_ref, l2_ref, mg_ref, x_ref,
                 pa_ref, pb_ref, wo_ref, out_ref, so_ref, sl_ref):
    tm = x_ref.shape[0]

    def interleaved(src_ref, st_ref):
        d = src_ref.shape[1]
        if d == 1:
            return src_ref[0, 0]
        slabs = src_ref.shape[3] // LANES
        for r in range(d):
            for k in range(slabs):
                st_ref[k, pl.ds(r, tm // d, stride=d), :] = src_ref[0, r, :, k * LANES:(k + 1) * LANES]
        return jnp.concatenate([st_ref[k] for k in range(slabs)], axis=1)

    lses = [interleaved(l, sl_ref) for l in (l0_ref, l1_ref, l2_ref)]
    mx = jnp.maximum(jnp.maximum(lses[0], lses[1]), lses[2])
    ws = [jnp.exp(l - mx) for l in lses]
    den = ws[0] + ws[1] + ws[2]
    ob = jnp.zeros((tm, DIL_WIDTH), F32)
    for w, o_ref in zip(ws, (ob0_ref, ob1_ref, ob2_ref)):
        alpha = w / den
        wide = jnp.concatenate(
            [jnp.broadcast_to(alpha[:, h:h + 1], (tm, HEAD_DIM)) for h in range(DIL_HEADS)], axis=1)
        ob = ob + wide * interleaved(o_ref, so_ref)
    ya = _dot(oa_ref[...], pa_ref[...])
    yb = _dot(ob.astype(BF16), pb_ref[...])
    gm = jax.nn.sigmoid(mg_ref[...].astype(F32))
    y = gm[:, :D_MODEL] * ya + gm[:, D_MODEL:] * yb
    out_ref[...] = x_ref[...] + _dot(y.astype(BF16), wo_ref[...])


def _mixout(oa, obs, lses, proj, x2, pa, pb, wo, S, tm=512):
    T = x2.shape[0]
    per_b = S // tm
    row = lambda w: pl.BlockSpec((tm, w), lambda i: (i, 0))
    folded = lambda a: pl.BlockSpec((1, a.shape[1], tm // a.shape[1], a.shape[3]),
                                    lambda i: (i // per_b, 0, i % per_b, 0))
    return pl.pallas_call(
        _mixout_body,
        grid=(T // tm,),
        in_specs=[row(NSA_HEADS * HEAD_DIM), *[folded(a) for a in obs], *[folded(a) for a in lses],
                  pl.BlockSpec((tm, 2 * D_MODEL), lambda i: (i, COL_MERGE // (2 * D_MODEL))),
                  row(D_MODEL),
                  _const_spec(pa.shape), _const_spec(pb.shape), _const_spec(wo.shape)],
        out_specs=row(D_MODEL),
        out_shape=jax.ShapeDtypeStruct((T, D_MODEL), F32),
        scratch_shapes=[pltpu.VMEM((DIL_WIDTH // LANES, tm, LANES), F32),
                        pltpu.VMEM((1, tm, LANES), F32)],
        compiler_params=_params(),
        name="mixout",
    )(oa, *obs, *lses, proj, x2, pa, pb, wo)


FF_CHUNK = 512


def _ffn_body(x_ref, g_ref, w1_ref, w3_ref, w2_ref, o_ref):
    x = x_ref[...]
    h = _rms(x, g_ref[...]).astype(BF16)
    acc = jnp.zeros(x.shape, F32)
    for c0 in range(0, D_FF, FF_CHUNK):
        c1 = min(c0 + FF_CHUNK, D_FF)
        act = _silu(_dot(h, w1_ref[:, c0:c1])) * _dot(h, w3_ref[:, c0:c1])
        acc = acc + _dot(act.astype(BF16), w2_ref[c0:c1, :])
    o_ref[...] = x + acc


def _ffn(x2, g, w1, w3, w2, tm=512):
    T = x2.shape[0]
    return pl.pallas_call(
        _ffn_body,
        grid=(T // tm,),
        in_specs=[pl.BlockSpec((tm, D_MODEL), lambda i: (i, 0)), _const_spec(g.shape),
                  _const_spec(w1.shape), _const_spec(w3.shape), _const_spec(w2.shape)],
        out_specs=pl.BlockSpec((tm, D_MODEL), lambda i: (i, 0)),
        out_shape=jax.ShapeDtypeStruct((T, D_MODEL), F32),
        compiler_params=_params(),
        name="ffn",
    )(x2, g, w1, w3, w2)


MOE_TM = 1024
MOE_CHUNK = 128


def _router_body(x_ref, g_ref, r_ref, h_ref, comb_ref, pos_ref, post_ref, cnt_ref):
    tm = x_ref.shape[0]
    h = _rms(x_ref[...], g_ref[...])
    h_ref[...] = h.astype(BF16)
    logits = _dot(h, r_ref[...], precision=lax.Precision.HIGHEST)
    lane = lax.broadcasted_iota(jnp.int32, logits.shape, 1)
    lg = jnp.where(lane < N_EXPERTS, logits, NEG)
    m1 = jnp.max(lg, axis=-1, keepdims=True)
    i1 = jnp.min(jnp.where(lg == m1, lane, LANES), axis=-1, keepdims=True)
    lg2 = jnp.where(lane == i1, NEG, lg)
    m2 = jnp.max(lg2, axis=-1, keepdims=True)
    i2 = jnp.min(jnp.where(lg2 == m2, lane, LANES), axis=-1, keepdims=True)
    e2 = jnp.exp(m2 - m1)
    den = 1.0 + e2
    comb_ref[...] = jnp.where(lane == i1, 1.0 / den, 0.0) + jnp.where(lane == i2, e2 / den, 0.0)
    chosen = (lane == i1) | (lane == i2)
    ones = jnp.where(chosen, 1.0, 0.0)
    row = lax.broadcasted_iota(jnp.int32, (tm, tm), 0)
    col = lax.broadcasted_iota(jnp.int32, (tm, tm), 1)
    tri = jnp.where(col < row, 1.0, 0.0).astype(BF16)
    pos = jnp.where(chosen, _dot(tri, ones.astype(BF16)), -1.0)
    pos_ref[...] = pos
    post_ref[0] = pos.T[:N_EXPERTS]
    cnt = jnp.sum(ones, axis=0, keepdims=True)
    cnt_ref[0] = jnp.broadcast_to(cnt, (8, LANES)).astype(jnp.int32)


def _router(x2, g, r, tm=MOE_TM):
    T = x2.shape[0]
    nt = T // tm
    row = lambda w: pl.BlockSpec((tm, w), lambda i: (i, 0))
    return pl.pallas_call(
        _router_body,
        grid=(nt,),
        in_specs=[row(D_MODEL), _const_spec(g.shape), _const_spec(r.shape)],
        out_specs=[row(D_MODEL), row(LANES), row(LANES),
                   pl.BlockSpec((1, N_EXPERTS, tm), lambda i: (i, 0, 0)),
                   pl.BlockSpec((1, 8, LANES), lambda i: (i, 0, 0))],
        out_shape=[jax.ShapeDtypeStruct((T, D_MODEL), BF16),
                   jax.ShapeDtypeStruct((T, LANES), F32),
                   jax.ShapeDtypeStruct((T, LANES), F32),
                   jax.ShapeDtypeStruct((nt, N_EXPERTS, tm), F32),
                   jax.ShapeDtypeStruct((nt, 8, LANES), jnp.int32)],
        compiler_params=_params(),
        name="router",
    )(x2, g, r)


def _moe_body(cnt_ref, h_ref, comb_ref, pos_ref, post_ref, acc_ref, w1_ref, w3_ref, w2_ref, fn_ref,
              o_ref):
    e = pl.program_id(0)
    i = pl.program_id(1)
    tm = h_ref.shape[0]
    n_chunks = (cnt_ref[i * N_EXPERTS + e] + (MOE_CHUNK - 1)) // MOE_CHUNK
    lane = lax.broadcasted_iota(jnp.int32, (tm, LANES), 1)
    comb_e = jnp.sum(jnp.where(lane == e, comb_ref[...], 0.0), axis=-1, keepdims=True)
    pos_e = jnp.sum(jnp.where(lane == e, pos_ref[...], 0.0), axis=-1, keepdims=True)
    post_e = post_ref[0, pl.ds(e, 1), :]
    o_ref[...] = acc_ref[...]

    def chunk(c, carry):
        base = (c * MOE_CHUNK).astype(F32)
        slot_col = base + lax.broadcasted_iota(jnp.int32, (MOE_CHUNK, 1), 0).astype(F32)
        slot_row = base + lax.broadcasted_iota(jnp.int32, (1, MOE_CHUNK), 1).astype(F32)
        gather = jnp.where(post_e == slot_col, 1.0, 0.0).astype(BF16)
        xe = _dot(gather, h_ref[...]).astype(BF16)
        act = _silu(_dot(xe, w1_ref[0])) * _dot(xe, w3_ref[0])
        y = _dot(act.astype(BF16), w2_ref[0]).astype(BF16)
        scatter = jnp.where(pos_e == slot_row, 1.0, 0.0).astype(BF16)
        o_ref[...] += comb_e * _dot(scatter, y)
        return carry

    lax.fori_loop(0, n_chunks, chunk, 0)

    @pl.when(e == pl.num_programs(0) - 1)
    def _():
        o_ref[...] = _rms(o_ref[...], fn_ref[...])


def _moe(x2, h, comb, pos, post, counts, w1, w3, w2, fn, tm=MOE_TM):
    T = x2.shape[0]
    assert T % tm == 0 and T // tm >= 3
    tile = lambda w: pl.BlockSpec((tm, w), lambda e, i, cnt: (i, 0))
    wspec = lambda a: pl.BlockSpec((1,) + a.shape[1:], lambda e, i, cnt: (e, 0, 0),
                                   pipeline_mode=pl.Buffered(1))
    grid_spec = pltpu.PrefetchScalarGridSpec(
        num_scalar_prefetch=1,
        grid=(N_EXPERTS, T // tm),
        in_specs=[tile(D_MODEL), tile(LANES), tile(LANES),
                  pl.BlockSpec((1, N_EXPERTS, tm), lambda e, i, cnt: (i, 0, 0)),
                  tile(D_MODEL), wspec(w1), wspec(w3), wspec(w2),
                  pl.BlockSpec(fn.shape, lambda e, i, cnt: (0, 0))],
        out_specs=tile(D_MODEL),
    )
    return pl.pallas_call(
        _moe_body,
        grid_spec=grid_spec,
        out_shape=jax.ShapeDtypeStruct((T, D_MODEL), F32),
        input_output_aliases={5: 0},
        compiler_params=_params(),
        name="moe",
    )(counts, h, comb, pos, post, x2, w1, w3, w2, fn)


def _rope_lane_tables(positions):
    half = ROT_DIM // 2
    inv = ROPE_THETA ** (-jnp.arange(0, ROT_DIM, 2, dtype=F32) / ROT_DIM)
    ang = positions.astype(F32).reshape(-1, 1) * inv
    cos, sin = jnp.cos(ang), jnp.sin(ang)
    lane = np.arange(LANES)
    within = lane % HEAD_DIM
    pick = lane % half
    cos_l, sin_l = cos[:, pick], sin[:, pick]
    c = jnp.where(within < ROT_DIM, cos_l, 1.0)
    sa = jnp.where(within < half, -sin_l, 0.0)
    sb = jnp.where((within >= half) & (within < ROT_DIM), sin_l, 0.0)
    return c, sa, sb


def _permute_w_in(w):
    scale = HEAD_DIM ** -0.5
    qa, kv, gate, qkv_b, merge = 0, 512, 1280, 1304, 3608
    nb = N_DIL_GROUPS * DIL_WIDTH
    kv_piece = lambda j: w[:, kv + 128 * j:kv + 128 * (j + 1)]
    dil = lambda part, g0, g1: w[:, qkv_b + part * nb + g0 * DIL_WIDTH:
                                 qkv_b + part * nb + g1 * DIL_WIDTH]
    pieces = [
        w[:, qa:qa + 512] * scale,
        dil(0, 0, 1) * scale, dil(1, 0, 1), dil(2, 0, 1),
        kv_piece(0), kv_piece(2), kv_piece(4),
        kv_piece(1), kv_piece(3), kv_piece(5),
        w[:, merge:merge + 2 * D_MODEL],
        w[:, gate:gate + 3 * NSA_HEADS],
        jnp.zeros((w.shape[0], LANES - 3 * NSA_HEADS), w.dtype),
        dil(0, 1, 3) * scale, dil(1, 1, 3), dil(2, 1, 3),
    ]
    out = jnp.concatenate(pieces, axis=1)
    assert out.shape[1] == N_W_IN
    return out.astype(BF16)


def _importance_matrix_t(seq, ncp):
    n_c = (seq - CMP_LEN) // CMP_STRIDE + 1
    starts = np.arange(n_c) * CMP_STRIDE
    bstart = np.arange(seq // SEL_LEN) * SEL_LEN
    overlap = np.clip(np.minimum(starts[:, None] + CMP_LEN, bstart[None, :] + SEL_LEN)
                      - np.maximum(starts[:, None], bstart[None, :]), 0, None)
    m = np.zeros((ncp, seq // SEL_LEN), np.float32)
    m[:n_c] = overlap.astype(np.float32) / CMP_LEN
    return jnp.asarray(m.T)


def _cmp_rows(proj3, col):
    B, S, _ = proj3.shape
    a = proj3[:, :, col:col + LANES].reshape(B, S // CMP_STRIDE, CMP_STRIDE, NSA_KV_HEADS, HEAD_DIM)
    return a.transpose(0, 3, 1, 2, 4).reshape(B * NSA_KV_HEADS * (S // CMP_STRIDE),
                                              CMP_STRIDE * HEAD_DIM)


def _mixer(x2, B, S, tables, mt, norm_g, w_in, pos_k, pos_v, wk1, wk2, wv1, wv2, p_a, p_b, w_o):
    proj, fold1, fold2 = _inproj(x2, norm_g.reshape(1, -1), _permute_w_in(w_in), *tables, B, S)
    proj3 = proj.reshape(B, S, N_PROJ)
    ncp = S // CMP_STRIDE
    a = jnp.stack([_cmp_rows(proj3, COL_KCMP), _cmp_rows(proj3, COL_VCMP)])
    pos = jnp.stack([pos_k, pos_v]).reshape(2, 1, CMP_LEN * HEAD_DIM)
    pos = jnp.broadcast_to(pos, (2, 8, CMP_LEN * HEAD_DIM)).astype(BF16)
    cmp = _compress(a, jnp.stack([wk1, wv1]).astype(BF16), jnp.stack([wk2, wv2]).astype(BF16), pos)
    cmp = cmp.reshape(2, B, NSA_KV_HEADS, ncp, HEAD_DIM).transpose(0, 1, 3, 2, 4)
    cmp = cmp.reshape(2, B, ncp, NSA_KV_HEADS * HEAD_DIM)
    oa = _nsa(proj3, cmp[0], cmp[1], mt).reshape(B * S, NSA_HEADS * HEAD_DIM)
    base_cols = tuple(c // DIL_WIDTH for c in (COL_QB0, COL_KB0, COL_VB0))
    sources = ((proj3.reshape(B, 1, S, N_PROJ), base_cols), (fold1, (0, 1, 2)), (fold2, (0, 1, 2)))
    obs, lses = [], []
    for g, ((w, d), (arr, cols)) in enumerate(zip(DIL_PATTERNS, sources)):
        o, lse = _dilated(arr, cols, w, d, f"dilated{g}")
        obs.append(o)
        lses.append(lse)
    return _mixout(oa, obs, lses, proj, x2, p_a.astype(BF16), p_b.astype(BF16), w_o.astype(BF16), S)


def kernel(x, positions, norm_mix, w_in, cmp_pos_k, cmp_pos_v, cmp_k_w1, cmp_k_w2, cmp_v_w1,
           cmp_v_w2, w_branch_a, w_branch_b, w_out, norm_ffn, ffn_w1, ffn_w3, ffn_w2, router,
           moe_w1, moe_w3, moe_w2, final_norm):
    B, S, D = x.shape
    depth = norm_mix.shape[0]
    assert depth == 2 and D == D_MODEL
    tables = _rope_lane_tables(positions)
    mt = _importance_matrix_t(S, S // CMP_STRIDE)
    x2 = x.reshape(B * S, D)
    x2 = _mixer(x2, B, S, tables, mt, norm_mix[0], w_in[0], cmp_pos_k[0], cmp_pos_v[0],
                cmp_k_w1[0], cmp_k_w2[0], cmp_v_w1[0], cmp_v_w2[0],
                w_branch_a[0], w_branch_b[0], w_out[0])
    x2 = _ffn(x2, norm_ffn[0].reshape(1, -1), ffn_w1[0].astype(BF16), ffn_w3[0].astype(BF16),
              ffn_w2[0].astype(BF16))
    x2 = _mixer(x2, B, S, tables, mt, norm_mix[1], w_in[1], cmp_pos_k[1], cmp_pos_v[1],
                cmp_k_w1[1], cmp_k_w2[1], cmp_v_w1[1], cmp_v_w2[1],
                w_branch_a[1], w_branch_b[1], w_out[1])
    g1 = norm_ffn[1].reshape(1, -1)
    r = jnp.pad(router[0], ((0, 0), (0, LANES - N_EXPERTS)))
    h, comb, pos, post, cnt = _router(x2, g1, r)
    counts = cnt[:, 0, :N_EXPERTS].reshape(-1)
    out = _moe(x2, h, comb, pos, post, counts, moe_w1[0].astype(BF16), moe_w3[0].astype(BF16),
               moe_w2[0].astype(BF16), final_norm.reshape(1, -1))
    return out.reshape(B, S, D)
```

```python
import functools

import numpy as np
import jax
import jax.numpy as jnp
from jax import lax
from jax.experimental import pallas as pl
from jax.experimental.pallas import tpu as pltpu

F32 = jnp.float32
BF16 = jnp.bfloat16

D_MODEL = 1024
HEAD_DIM = 64
ROT_DIM = HEAD_DIM // 4
ROPE_THETA = 500000.0
EPS = 1e-6
NSA_HEADS = 8
NSA_KV_HEADS = 2
HEADS_PER_KV = NSA_HEADS // NSA_KV_HEADS
CMP_LEN = 32
CMP_STRIDE = 16
CMP_HIDDEN = 128
SEL_LEN = 64
N_SEL = 16
WIN = 512
DIL_PATTERNS = ((128, 1), (512, 4), (2048, 16))
N_DIL_GROUPS = 3
DIL_HEADS = 4
D_FF = 2816
N_EXPERTS = 8

LANES = 128
VMEM_LIMIT = 56 * 1024 * 1024
NEG = -1e30
BIG = 1e30

COL_QA = 0
COL_QB0 = 512
COL_KB0 = 768
COL_VB0 = 1024
COL_KCMP = 1280
COL_KSEL = 1408
COL_KWIN = 1536
COL_VCMP = 1664
COL_VSEL = 1792
COL_VWIN = 1920
COL_MERGE = 2048
COL_GATE = 4096
N_PROJ = 4224
DIL_WIDTH = DIL_HEADS * HEAD_DIM
N_FOLD = 2 * DIL_WIDTH
N_W_IN = N_PROJ + 3 * N_FOLD
IN_CHUNKS = (
    (0, 512, True, None), (512, 512, True, None), (1024, 256, False, None),
    (1280, 384, True, None), (1664, 384, False, None),
    (2048, 512, False, None), (2560, 512, False, None), (3072, 512, False, None),
    (3584, 512, False, None), (4096, 128, False, None),
    (N_PROJ, N_FOLD, True, 0), (N_PROJ + N_FOLD, N_FOLD, True, 1),
    (N_PROJ + 2 * N_FOLD, N_FOLD, False, 2),
)


def _dot(a, b, precision=None):
    return jnp.dot(a, b, preferred_element_type=F32, precision=precision)


def _dot_nt(a, b, precision=None):
    return lax.dot_general(a, b, (((1,), (1,)), ((), ())), preferred_element_type=F32,
                           precision=precision)


def _rms(x, g):
    ms = jnp.mean(x * x, axis=-1, keepdims=True)
    return x * lax.rsqrt(ms + EPS) * g


def _silu(x):
    return x * jax.nn.sigmoid(x)


def _params(**kw):
    return pltpu.CompilerParams(vmem_limit_bytes=VMEM_LIMIT, **kw)


def _const_spec(shape):
    nd = len(shape)
    return pl.BlockSpec(shape, lambda *_: (0,) * nd)


def _inproj_body(x_ref, g_ref, w_ref, c_ref, sa_ref, sb_ref, o_ref, f1_ref, f2_ref, st_ref):
    tm = x_ref.shape[0]
    h = _rms(x_ref[...], g_ref[...]).astype(BF16)
    c = c_ref[...]
    sa = sa_ref[...]
    sb = sb_ref[...]
    for start, size, rope, piece in IN_CHUNKS:
        acc = _dot(h, w_ref[:, start:start + size])
        for j in range(size // LANES):
            a = acc[:, j * LANES:(j + 1) * LANES]
            if rope:
                a = a * c + pltpu.roll(a, LANES - 8, 1) * sa + pltpu.roll(a, 8, 1) * sb
            if piece is None:
                o_ref[:, start + j * LANES:start + (j + 1) * LANES] = a.astype(BF16)
            else:
                st_ref[j] = a
        if piece is not None:
            slabs = DIL_WIDTH // LANES
            for gi, f_ref in enumerate((f1_ref, f2_ref)):
                d = DIL_PATTERNS[gi + 1][1]
                for r in range(d):
                    for k in range(slabs):
                        rows = st_ref[gi * slabs + k, pl.ds(r, tm // d, stride=d), :]
                        c0 = piece * DIL_WIDTH + k * LANES
                        f_ref[0, r, :, c0:c0 + LANES] = rows.astype(BF16)


def _inproj(x2, g, w, rc, rsa, rsb, B, S, tm=512):
    T = x2.shape[0]
    per_b = S // tm
    d1, d2 = DIL_PATTERNS[1][1], DIL_PATTERNS[2][1]
    fold_spec = lambda d: pl.BlockSpec((1, d, tm // d, 3 * DIL_WIDTH),
                                       lambda i: (i // per_b, 0, i % per_b, 0))
    fold_shape = lambda d: jax.ShapeDtypeStruct((B, d, S // d, 3 * DIL_WIDTH), BF16)
    return pl.pallas_call(
        _inproj_body,
        grid=(T // tm,),
        in_specs=[
            pl.BlockSpec((tm, D_MODEL), lambda i: (i, 0)),
            _const_spec((1, D_MODEL)),
            _const_spec((D_MODEL, N_W_IN)),
            pl.BlockSpec((tm, LANES), lambda i: (i, 0)),
            pl.BlockSpec((tm, LANES), lambda i: (i, 0)),
            pl.BlockSpec((tm, LANES), lambda i: (i, 0)),
        ],
        out_specs=[pl.BlockSpec((tm, N_PROJ), lambda i: (i, 0)), fold_spec(d1), fold_spec(d2)],
        out_shape=[jax.ShapeDtypeStruct((T, N_PROJ), BF16), fold_shape(d1), fold_shape(d2)],
        scratch_shapes=[pltpu.VMEM((N_FOLD // LANES, tm, LANES), F32)],
        compiler_params=_params(),
        name="inproj",
    )(x2, g, w, rc, rsa, rsb)


def _compress_body(a_ref, w1_ref, w2_ref, pos_ref, o_ref):
    a = a_ref[0]
    w1 = w1_ref[0]
    half = CMP_STRIDE * HEAD_DIM
    top = _dot(a, w1[:half])
    bot = _dot(a, w1[half:])
    pc = _dot(pos_ref[0], w1)
    rows = a.shape[0]
    hid = top + pltpu.roll(bot, rows - 1, 0) + pc[0:1]
    o_ref[0] = _dot(_silu(hid).astype(BF16), w2_ref[0]).astype(BF16)


def _compress(a, w1, w2, pos):
    n, rows, _ = a.shape
    return pl.pallas_call(
        _compress_body,
        grid=(n,),
        in_specs=[
            pl.BlockSpec((1, rows, CMP_STRIDE * HEAD_DIM), lambda i: (i, 0, 0)),
            pl.BlockSpec((1, CMP_LEN * HEAD_DIM, CMP_HIDDEN), lambda i: (i, 0, 0)),
            pl.BlockSpec((1, CMP_HIDDEN, HEAD_DIM), lambda i: (i, 0, 0)),
            pl.BlockSpec((1, 8, CMP_LEN * HEAD_DIM), lambda i: (i, 0, 0)),
        ],
        out_specs=pl.BlockSpec((1, rows, HEAD_DIM), lambda i: (i, 0, 0)),
        out_shape=jax.ShapeDtypeStruct((n, rows, HEAD_DIM), BF16),
        compiler_params=_params(),
        name="compress",
    )(a, w1, w2, pos)


def _softmax_av(s, mask, v):
    nh, tq, nk = s.shape
    s = jnp.where(mask[None], s, NEG)
    m = jnp.max(s, axis=-1, keepdims=True)
    e = jnp.exp(s - m)
    l = jnp.sum(e, axis=-1, keepdims=True)
    o = _dot(e.astype(BF16).reshape(nh * tq, nk), v)
    return o / l.reshape(nh * tq, 1)


SEL_PREFIX = 512


def _nsa_body(q_ref, kc_ref, vc_ref, ksel_ref, vsel_ref, kwin_ref, vwin_ref, gate_ref, mt_ref,
              o_ref, osel_ref, *, tq, seq):
    nblk = seq // SEL_LEN
    ncp = kc_ref.shape[1]
    n_cmp = (seq - CMP_LEN) // CMP_STRIDE + 1
    q0 = pl.program_id(1) * tq
    q = q_ref[0]
    t_col = q0 + lax.broadcasted_iota(jnp.int32, (tq, 1), 0)
    t_row = q0 + lax.broadcasted_iota(jnp.int32, (1, tq), 1)
    gates = jax.nn.sigmoid(gate_ref[0].astype(F32))

    zeros64 = jnp.zeros((tq, HEAD_DIM), BF16)

    def stacked_q(g):
        parts = []
        for hh in range(HEADS_PER_KV):
            h = g * HEADS_PER_KV + hh
            qh = q[:, h * HEAD_DIM:(h + 1) * HEAD_DIM]
            parts.append(jnp.concatenate([qh, zeros64] if g == 0 else [zeros64, qh], axis=1))
        return jnp.concatenate(parts, axis=0)

    qs = [stacked_q(g) for g in range(NSA_KV_HEADS)]

    cidx = lax.broadcasted_iota(jnp.int32, (tq, ncp), 1)
    cmask = ((cidx * CMP_STRIDE + (CMP_LEN - 1)) <= t_col) & (cidx < n_cmp)
    jidx = lax.broadcasted_iota(jnp.int32, (nblk, tq), 0)
    cur = lax.shift_right_logical(t_row, 6)
    forced = (jidx == 0) | (jidx == cur) | (jidx == cur - 1)
    future = jidx > cur
    o_cmp = []
    sel_rows = []
    for g in range(NSA_KV_HEADS):
        s = _dot_nt(qs[g], kc_ref[0]).reshape(HEADS_PER_KV, tq, ncp)
        s = jnp.where(cmask[None], s, NEG)
        m = jnp.max(s, axis=-1, keepdims=True)
        e = jnp.where(cmask[None], jnp.exp(s - m), 0.0)
        den = jnp.sum(e, axis=-1, keepdims=True)
        p = e / jnp.where(den > 0, den, 1.0)
        o_cmp.append(_dot(p.astype(BF16).reshape(HEADS_PER_KV * tq, ncp), vc_ref[0]))
        psum = p[0] + p[1] + p[2] + p[3]
        imp = _dot_nt(mt_ref[...], psum, precision=lax.Precision.HIGHEST)
        imp = jnp.where(forced, BIG, imp)
        imp = jnp.where(future, -BIG, imp)
        rank = jnp.zeros((nblk, tq), jnp.int32)
        for i in range(nblk):
            row = imp[i:i + 1, :]
            beats = (row > imp) | ((row == imp) & (jidx > i))
            rank = rank + beats.astype(jnp.int32)
        sel_rows.append((rank < N_SEL).astype(F32))
    pad_rows = LANES - NSA_KV_HEADS * nblk
    sel_t = jnp.concatenate(sel_rows + [jnp.zeros((pad_rows, tq), F32)], axis=0)
    sel = sel_t.T.astype(BF16)

    n_prefix = q0 // SEL_PREFIX + 1
    for n in range(1, seq // SEL_PREFIX + 1):
        klen = n * SEL_PREFIX

        @pl.when(n_prefix == n)
        def _(klen=klen):
            kpos = lax.broadcasted_iota(jnp.int32, (1, klen), 1)
            causal = kpos <= t_col
            erow = lax.broadcasted_iota(jnp.int32, (LANES, klen), 0)
            ecol = lax.shift_right_logical(lax.broadcasted_iota(jnp.int32, (LANES, klen), 1), 6)
            for g in range(NSA_KV_HEADS):
                expand = jnp.where(erow == ecol + g * nblk, 1.0, 0.0).astype(BF16)
                picked = _dot(sel, expand) > 0.5
                s = _dot_nt(qs[g], ksel_ref[0, :klen, :]).reshape(HEADS_PER_KV, tq, klen)
                osel_ref[g] = _softmax_av(s, picked & causal, vsel_ref[0, :klen, :])

    o_sel = [osel_ref[g] for g in range(NSA_KV_HEADS)]

    span = WIN + tq
    ks = pl.multiple_of(jnp.maximum(q0 - WIN, 0), tq)
    kw = kwin_ref[0, pl.ds(ks, span), :]
    vw = vwin_ref[0, pl.ds(ks, span), :]
    wpos = ks + lax.broadcasted_iota(jnp.int32, (1, span), 1)
    wmask = (wpos <= t_col) & (t_col - wpos <= WIN - 1)
    o_win = []
    for g in range(NSA_KV_HEADS):
        s = _dot_nt(qs[g], kw).reshape(HEADS_PER_KV, tq, span)
        o_win.append(_softmax_av(s, wmask, vw))

    outs = []
    for g in range(NSA_KV_HEADS):
        for hh in range(HEADS_PER_KV):
            h = g * HEADS_PER_KV + hh
            acc = jnp.zeros((tq, HEAD_DIM), F32)
            for br, o in enumerate((o_cmp[g], o_sel[g], o_win[g])):
                oh = o[hh * tq:(hh + 1) * tq, g * HEAD_DIM:(g + 1) * HEAD_DIM]
                acc = acc + gates[:, 3 * h + br:3 * h + br + 1] * oh
            outs.append(acc)
    o_ref[0] = jnp.concatenate(outs, axis=1).astype(BF16)


def _nsa(proj3, kc, vc, mt, tq=128):
    B, S, _ = proj3.shape
    blk = lambda c: c // LANES
    seq_spec = lambda c: pl.BlockSpec((1, S, LANES), lambda b, i: (b, 0, blk(c)))
    ncp = kc.shape[1]
    return pl.pallas_call(
        functools.partial(_nsa_body, tq=tq, seq=S),
        grid=(B, S // tq),
        in_specs=[
            pl.BlockSpec((1, tq, NSA_HEADS * HEAD_DIM), lambda b, i: (b, i, 0)),
            pl.BlockSpec((1, ncp, LANES), lambda b, i: (b, 0, 0)),
            pl.BlockSpec((1, ncp, LANES), lambda b, i: (b, 0, 0)),
            seq_spec(COL_KSEL), seq_spec(COL_VSEL), seq_spec(COL_KWIN), seq_spec(COL_VWIN),
            pl.BlockSpec((1, tq, LANES), lambda b, i: (b, i, blk(COL_GATE))),
            _const_spec(mt.shape),
        ],
        out_specs=pl.BlockSpec((1, tq, NSA_HEADS * HEAD_DIM), lambda b, i: (b, i, 0)),
        out_shape=jax.ShapeDtypeStruct((B, S, NSA_HEADS * HEAD_DIM), BF16),
        scratch_shapes=[pltpu.VMEM((NSA_KV_HEADS, HEADS_PER_KV * tq, LANES), F32)],
        compiler_params=_params(),
        name="nsa",
    )(proj3, kc, vc, proj3, proj3, proj3, proj3, proj3, mt)


DIL_SUB = 128
DIL_ROWS = 512


def _dil_body(q_ref, kp_ref, kc_ref, vp_ref, vc_ref, o_ref, lse_ref, *, n_back):
    sub = DIL_SUB
    rb, tq = q_ref.shape[1], q_ref.shape[2]
    t0 = pl.program_id(2) * tq
    head_of = lax.shift_right_logical(lax.broadcasted_iota(jnp.int32, (sub, DIL_WIDTH), 1), 6)
    lane = lax.broadcasted_iota(jnp.int32, (sub, LANES), 1)
    diff = (sub + lax.broadcasted_iota(jnp.int32, (sub, 1), 0)
            - lax.broadcasted_iota(jnp.int32, (1, 2 * sub), 1))
    band = (diff >= 0) & (diff <= n_back)
    band0 = band & (lax.broadcasted_iota(jnp.int32, (1, 2 * sub), 1) + t0 >= sub)
    tiles = [(r, j) for r in range(rb) for j in range(tq // sub)]
    keys = {r: jnp.concatenate([kp_ref[0, r], kc_ref[0, r]], axis=0) for r in range(rb)}
    vals = {r: jnp.concatenate([vp_ref[0, r], vc_ref[0, r]], axis=0) for r in range(rb)}
    scores = []
    for r, j in tiles:
        q = q_ref[0, r, j * sub:(j + 1) * sub, :]
        qs = jnp.concatenate([jnp.where(head_of == h, q, jnp.zeros_like(q))
                              for h in range(DIL_HEADS)], axis=0)
        s = _dot_nt(qs, keys[r][j * sub:(j + 2) * sub]).reshape(DIL_HEADS, sub, 2 * sub)
        scores.append(jnp.where((band0 if j == 0 else band)[None], s, NEG))
    stats = []
    for s in scores:
        m = jnp.max(s, axis=-1, keepdims=True)
        e = jnp.exp(s - m)
        stats.append((m, e, jnp.sum(e, axis=-1, keepdims=True)))
    for (r, j), (m, e, l) in zip(tiles, stats):
        o = _dot(e.astype(BF16).reshape(DIL_HEADS * sub, 2 * sub), vals[r][j * sub:(j + 2) * sub])
        o = o.reshape(DIL_HEADS, sub, DIL_WIDTH) / l
        lse = m + jnp.log(l)
        o_acc = jnp.zeros((sub, DIL_WIDTH), F32)
        lse_out = jnp.zeros((sub, LANES), F32)
        for h in range(DIL_HEADS):
            o_acc = jnp.where(head_of == h, o[h], o_acc)
            lse_out = jnp.where(lane == h, lse[h], lse_out)
        o_ref[0, r, j * sub:(j + 1) * sub, :] = o_acc
        lse_ref[0, r, j * sub:(j + 1) * sub, :] = lse_out


def _dilated(arr, cols, window, dilation, name):
    B, d, L, _ = arr.shape
    n_back = window // dilation
    tq = min(L, DIL_ROWS)
    rb = DIL_ROWS // tq
    assert d == dilation and n_back <= DIL_SUB and L % tq == 0 and d % rb == 0
    per = tq // DIL_SUB
    qc, kc, vc = cols
    cur = lambda c: pl.BlockSpec((1, rb, tq, DIL_WIDTH), lambda b, r, i: (b, r, i, c))
    prev = lambda c: pl.BlockSpec((1, rb, DIL_SUB, DIL_WIDTH),
                                  lambda b, r, i: (b, r, jnp.maximum(i * per - 1, 0), c))
    return pl.pallas_call(
        functools.partial(_dil_body, n_back=n_back),
        grid=(B, dilation // rb, L // tq),
        in_specs=[cur(qc), prev(kc), cur(kc), prev(vc), cur(vc)],
        out_specs=[pl.BlockSpec((1, rb, tq, DIL_WIDTH), lambda b, r, i: (b, r, i, 0)),
                   pl.BlockSpec((1, rb, tq, LANES), lambda b, r, i: (b, r, i, 0))],
        out_shape=[jax.ShapeDtypeStruct((B, dilation, L, DIL_WIDTH), F32),
                   jax.ShapeDtypeStruct((B, dilation, L, LANES), F32)],
        compiler_params=_params(),
        name=name,
    )(arr, arr, arr, arr, arr)


def _mixout_body(oa_ref, ob0_ref, ob1_ref, ob2_ref, l0_ref, l1_ref, l2_ref, mg_ref, x_ref,
                 pa_ref, pb_ref, wo_ref, out_ref, so_ref, sl_ref):
    tm = x_ref.shape[0]

    def interleaved(src_ref, st_ref):
        d = src_ref.shape[1]
        if d == 1:
            return src_ref[0, 0]
        slabs = src_ref.shape[3] // LANES
        for r in range(d):
            for k in range(slabs):
                st_ref[k, pl.ds(r, tm // d, stride=d), :] = src_ref[0, r, :, k * LANES:(k + 1) * LANES]
        return jnp.concatenate([st_ref[k] for k in range(slabs)], axis=1)

    lses = [interleaved(l, sl_ref) for l in (l0_ref, l1_ref, l2_ref)]
    mx = jnp.maximum(jnp.maximum(lses[0], lses[1]), lses[2])
    ws = [jnp.exp(l - mx) for l in lses]
    den = ws[0] + ws[1] + ws[2]
    ob = jnp.zeros((tm, DIL_WIDTH), F32)
    for w, o_ref in zip(ws, (ob0_ref, ob1_ref, ob2_ref)):
        alpha = w / den
        wide = jnp.concatenate(
            [jnp.broadcast_to(alpha[:, h:h + 1], (tm, HEAD_DIM)) for h in range(DIL_HEADS)], axis=1)
        ob = ob + wide * interleaved(o_ref, so_ref)
    ya = _dot(oa_ref[...], pa_ref[...])
    yb = _dot(ob.astype(BF16), pb_ref[...])
    gm = jax.nn.sigmoid(mg_ref[...].astype(F32))
    y = gm[:, :D_MODEL] * ya + gm[:, D_MODEL:] * yb
    out_ref[...] = x_ref[...] + _dot(y.astype(BF16), wo_ref[...])


def _mixout(oa, obs, lses, proj, x2, pa, pb, wo, S, tm=512):
    T = x2.shape[0]
    per_b = S // tm
    row = lambda w: pl.BlockSpec((tm, w), lambda i: (i, 0))
    folded = lambda a: pl.BlockSpec((1, a.shape[1], tm // a.shape[1], a.shape[3]),
                                    lambda i: (i // per_b, 0, i % per_b, 0))
    return pl.pallas_call(
        _mixout_body,
        grid=(T // tm,),
        in_specs=[row(NSA_HEADS * HEAD_DIM), *[folded(a) for a in obs], *[folded(a) for a in lses],
                  pl.BlockSpec((tm, 2 * D_MODEL), lambda i: (i, COL_MERGE // (2 * D_MODEL))),
                  row(D_MODEL),
                  _const_spec(pa.shape), _const_spec(pb.shape), _const_spec(wo.shape)],
        out_specs=row(D_MODEL),
        out_shape=jax.ShapeDtypeStruct((T, D_MODEL), F32),
        scratch_shapes=[pltpu.VMEM((DIL_WIDTH // LANES, tm, LANES), F32),
                        pltpu.VMEM((1, tm, LANES), F32)],
        compiler_params=_params(),
        name="mixout",
    )(oa, *obs, *lses, proj, x2, pa, pb, wo)


FF_CHUNK = 512


def _ffn_body(x_ref, g_ref, w1_ref, w3_ref, w2_ref, o_ref):
    x = x_ref[...]
    h = _rms(x, g_ref[...]).astype(BF16)
    acc = jnp.zeros(x.shape, F32)
    for c0 in range(0, D_FF, FF_CHUNK):
        c1 = min(c0 + FF_CHUNK, D_FF)
        act = _silu(_dot(h, w1_ref[:, c0:c1])) * _dot(h, w3_ref[:, c0:c1])
        acc = acc + _dot(act.astype(BF16), w2_ref[c0:c1, :])
    o_ref[...] = x + acc


def _ffn(x2, g, w1, w3, w2, tm=512):
    T = x2.shape[0]
    return pl.pallas_call(
        _ffn_body,
        grid=(T // tm,),
        in_specs=[pl.BlockSpec((tm, D_MODEL), lambda i: (i, 0)), _const_spec(g.shape),
                  _const_spec(w1.shape), _const_spec(w3.shape), _const_spec(w2.shape)],
        out_specs=pl.BlockSpec((tm, D_MODEL), lambda i: (i, 0)),
        out_shape=jax.ShapeDtypeStruct((T, D_MODEL), F32),
        compiler_params=_params(),
        name="ffn",
    )(x2, g, w1, w3, w2)


MOE_TM = 1024
MOE_CHUNK = 128


def _router_body(x_ref, g_ref, r_ref, h_ref, comb_ref, pos_ref, post_ref, cnt_ref):
    tm = x_ref.shape[0]
    h = _rms(x_ref[...], g_ref[...])
    h_ref[...] = h.astype(BF16)
    logits = _dot(h, r_ref[...], precision=lax.Precision.HIGHEST)
    lane = lax.broadcasted_iota(jnp.int32, logits.shape, 1)
    lg = jnp.where(lane < N_EXPERTS, logits, NEG)
    m1 = jnp.max(lg, axis=-1, keepdims=True)
    i1 = jnp.min(jnp.where(lg == m1, lane, LANES), axis=-1, keepdims=True)
    lg2 = jnp.where(lane == i1, NEG, lg)
    m2 = jnp.max(lg2, axis=-1, keepdims=True)
    i2 = jnp.min(jnp.where(lg2 == m2, lane, LANES), axis=-1, keepdims=True)
    e2 = jnp.exp(m2 - m1)
    den = 1.0 + e2
    comb_ref[...] = jnp.where(lane == i1, 1.0 / den, 0.0) + jnp.where(lane == i2, e2 / den, 0.0)
    chosen = (lane == i1) | (lane == i2)
    ones = jnp.where(chosen, 1.0, 0.0)
    row = lax.broadcasted_iota(jnp.int32, (tm, tm), 0)
    col = lax.broadcasted_iota(jnp.int32, (tm, tm), 1)
    tri = jnp.where(col < row, 1.0, 0.0).astype(BF16)
    pos = jnp.where(chosen, _dot(tri, ones.astype(BF16)), -1.0)
    pos_ref[...] = pos
    post_ref[0] = pos.T[:N_EXPERTS]
    cnt = jnp.sum(ones, axis=0, keepdims=True)
    cnt_ref[0] = jnp.broadcast_to(cnt, (8, LANES)).astype(jnp.int32)


def _router(x2, g, r, tm=MOE_TM):
    T = x2.shape[0]
    nt = T // tm
    row = lambda w: pl.BlockSpec((tm, w), lambda i: (i, 0))
    return pl.pallas_call(
        _router_body,
        grid=(nt,),
        in_specs=[row(D_MODEL), _const_spec(g.shape), _const_spec(r.shape)],
        out_specs=[row(D_MODEL), row(LANES), row(LANES),
                   pl.BlockSpec((1, N_EXPERTS, tm), lambda i: (i, 0, 0)),
                   pl.BlockSpec((1, 8, LANES), lambda i: (i, 0, 0))],
        out_shape=[jax.ShapeDtypeStruct((T, D_MODEL), BF16),
                   jax.ShapeDtypeStruct((T, LANES), F32),
                   jax.ShapeDtypeStruct((T, LANES), F32),
                   jax.ShapeDtypeStruct((nt, N_EXPERTS, tm), F32),
                   jax.ShapeDtypeStruct((nt, 8, LANES), jnp.int32)],
        compiler_params=_params(),
        name="router",
    )(x2, g, r)


def _moe_body(cnt_ref, x_ref, h_ref, comb_ref, pos_ref, post_ref, w1_ref, w3_ref, w2_ref, fn_ref,
              o_ref, xe_ref, y_ref):
    i = pl.program_id(0)
    e = pl.program_id(1)
    f = pl.program_id(2)
    last_f = pl.num_programs(2) - 1
    tm = h_ref.shape[0]
    n_chunks = (cnt_ref[i * N_EXPERTS + e] + (MOE_CHUNK - 1)) // MOE_CHUNK
    lane = lax.broadcasted_iota(jnp.int32, (tm, LANES), 1)
    comb_e = jnp.sum(jnp.where(lane == e, comb_ref[...], 0.0), axis=-1, keepdims=True)
    pos_e = jnp.sum(jnp.where(lane == e, pos_ref[...], 0.0), axis=-1, keepdims=True)
    post_e = post_ref[0, pl.ds(e, 1), :]

    @pl.when((e == 0) & (f == 0))
    def _():
        o_ref[...] = x_ref[...]

    def chunk(c, carry):
        r0 = pl.multiple_of(c * MOE_CHUNK, MOE_CHUNK)
        rows = pl.ds(r0, MOE_CHUNK)
        base = r0.astype(F32)

        @pl.when(f == 0)
        def _():
            slot_col = base + lax.broadcasted_iota(jnp.int32, (MOE_CHUNK, 1), 0).astype(F32)
            gather = jnp.where(post_e == slot_col, 1.0, 0.0).astype(BF16)
            xe_ref[rows, :] = _dot(gather, h_ref[...]).astype(BF16)

        xe = xe_ref[rows, :]
        act = _silu(_dot(xe, w1_ref[0])) * _dot(xe, w3_ref[0])
        y = _dot(act.astype(BF16), w2_ref[0])

        @pl.when(f == 0)
        def _():
            y_ref[rows, :] = y

        @pl.when(f > 0)
        def _():
            y_ref[rows, :] += y

        @pl.when(f == last_f)
        def _():
            slot_row = base + lax.broadcasted_iota(jnp.int32, (1, MOE_CHUNK), 1).astype(F32)
            scatter = jnp.where(pos_e == slot_row, 1.0, 0.0).astype(BF16)
            o_ref[...] += comb_e * _dot(scatter, y_ref[rows, :].astype(BF16))

        return carry

    lax.fori_loop(0, n_chunks, chunk, 0)

    @pl.when((e == pl.num_programs(1) - 1) & (f == last_f))
    def _():
        o_ref[...] = _rms(o_ref[...], fn_ref[...])


def _moe(x2, h, comb, pos, post, counts, w1, w3, w2, fn, tm=MOE_TM, nf=2):
    T = x2.shape[0]
    fc = D_FF // nf
    once = lambda w: pl.BlockSpec((tm, w), lambda i, e, f, cnt: (i, 0),
                                  pipeline_mode=pl.Buffered(1))
    tile = lambda w: pl.BlockSpec((tm, w), lambda i, e, f, cnt: (i, 0))
    grid_spec = pltpu.PrefetchScalarGridSpec(
        num_scalar_prefetch=1,
        grid=(T // tm, N_EXPERTS, nf),
        in_specs=[once(D_MODEL), once(D_MODEL), tile(LANES), tile(LANES),
                  pl.BlockSpec((1, N_EXPERTS, tm), lambda i, e, f, cnt: (i, 0, 0)),
                  pl.BlockSpec((1, D_MODEL, fc), lambda i, e, f, cnt: (e, 0, f)),
                  pl.BlockSpec((1, D_MODEL, fc), lambda i, e, f, cnt: (e, 0, f)),
                  pl.BlockSpec((1, fc, D_MODEL), lambda i, e, f, cnt: (e, f, 0)),
                  pl.BlockSpec(fn.shape, lambda i, e, f, cnt: (0, 0))],
        out_specs=tile(D_MODEL),
        scratch_shapes=[pltpu.VMEM((tm, D_MODEL), BF16), pltpu.VMEM((tm, D_MODEL), F32)],
    )
    return pl.pallas_call(
        _moe_body,
        grid_spec=grid_spec,
        out_shape=jax.ShapeDtypeStruct((T, D_MODEL), F32),
        compiler_params=_params(),
        name="moe",
    )(counts, x2, h, comb, pos, post, w1, w3, w2, fn)


def _rope_lane_tables(positions):
    half = ROT_DIM // 2
    inv = ROPE_THETA ** (-jnp.arange(0, ROT_DIM, 2, dtype=F32) / ROT_DIM)
    ang = positions.astype(F32).reshape(-1, 1) * inv
    cos, sin = jnp.cos(ang), jnp.sin(ang)
    lane = np.arange(LANES)
    within = lane % HEAD_DIM
    pick = lane % half
    cos_l, sin_l = cos[:, pick], sin[:, pick]
    c = jnp.where(within < ROT_DIM, cos_l, 1.0)
    sa = jnp.where(within < half, -sin_l, 0.0)
    sb = jnp.where((within >= half) & (within < ROT_DIM), sin_l, 0.0)
    return c, sa, sb


def _permute_w_in(w):
    scale = HEAD_DIM ** -0.5
    qa, kv, gate, qkv_b, merge = 0, 512, 1280, 1304, 3608
    nb = N_DIL_GROUPS * DIL_WIDTH
    kv_piece = lambda j: w[:, kv + 128 * j:kv + 128 * (j + 1)]
    dil = lambda part, g0, g1: w[:, qkv_b + part * nb + g0 * DIL_WIDTH:
                                 qkv_b + part * nb + g1 * DIL_WIDTH]
    pieces = [
        w[:, qa:qa + 512] * scale,
        dil(0, 0, 1) * scale, dil(1, 0, 1), dil(2, 0, 1),
        kv_piece(0), kv_piece(2), kv_piece(4),
        kv_piece(1), kv_piece(3), kv_piece(5),
        w[:, merge:merge + 2 * D_MODEL],
        w[:, gate:gate + 3 * NSA_HEADS],
        jnp.zeros((w.shape[0], LANES - 3 * NSA_HEADS), w.dtype),
        dil(0, 1, 3) * scale, dil(1, 1, 3), dil(2, 1, 3),
    ]
    out = jnp.concatenate(pieces, axis=1)
    assert out.shape[1] == N_W_IN
    return out.astype(BF16)


def _importance_matrix_t(seq, ncp):
    n_c = (seq - CMP_LEN) // CMP_STRIDE + 1
    starts = np.arange(n_c) * CMP_STRIDE
    bstart = np.arange(seq // SEL_LEN) * SEL_LEN
    overlap = np.clip(np.minimum(starts[:, None] + CMP_LEN, bstart[None, :] + SEL_LEN)
                      - np.maximum(starts[:, None], bstart[None, :]), 0, None)
    m = np.zeros((ncp, seq // SEL_LEN), np.float32)
    m[:n_c] = overlap.astype(np.float32) / CMP_LEN
    return jnp.asarray(m.T)


def _cmp_rows(proj3, col):
    B, S, _ = proj3.shape
    a = proj3[:, :, col:col + LANES].reshape(B, S // CMP_STRIDE, CMP_STRIDE, NSA_KV_HEADS, HEAD_DIM)
    return a.transpose(0, 3, 1, 2, 4).reshape(B * NSA_KV_HEADS * (S // CMP_STRIDE),
                                              CMP_STRIDE * HEAD_DIM)


def _mixer(x2, B, S, tables, mt, norm_g, w_in, pos_k, pos_v, wk1, wk2, wv1, wv2, p_a, p_b, w_o):
    proj, fold1, fold2 = _inproj(x2, norm_g.reshape(1, -1), _permute_w_in(w_in), *tables, B, S)
    proj3 = proj.reshape(B, S, N_PROJ)
    ncp = S // CMP_STRIDE
    a = jnp.stack([_cmp_rows(proj3, COL_KCMP), _cmp_rows(proj3, COL_VCMP)])
    pos = jnp.stack([pos_k, pos_v]).reshape(2, 1, CMP_LEN * HEAD_DIM)
    pos = jnp.broadcast_to(pos, (2, 8, CMP_LEN * HEAD_DIM)).astype(BF16)
    cmp = _compress(a, jnp.stack([wk1, wv1]).astype(BF16), jnp.stack([wk2, wv2]).astype(BF16), pos)
    cmp = cmp.reshape(2, B, NSA_KV_HEADS, ncp, HEAD_DIM).transpose(0, 1, 3, 2, 4)
    cmp = cmp.reshape(2, B, ncp, NSA_KV_HEADS * HEAD_DIM)
    oa = _nsa(proj3, cmp[0], cmp[1], mt).reshape(B * S, NSA_HEADS * HEAD_DIM)
    base_cols = tuple(c // DIL_WIDTH for c in (COL_QB0, COL_KB0, COL_VB0))
    sources = ((proj3.reshape(B, 1, S, N_PROJ), base_cols), (fold1, (0, 1, 2)), (fold2, (0, 1, 2)))
    obs, lses = [], []
    for g, ((w, d), (arr, cols)) in enumerate(zip(DIL_PATTERNS, sources)):
        o, lse = _dilated(arr, cols, w, d, f"dilated{g}")
        obs.append(o)
        lses.append(lse)
    return _mixout(oa, obs, lses, proj, x2, p_a.astype(BF16), p_b.astype(BF16), w_o.astype(BF16), S)


def kernel(x, positions, norm_mix, w_in, cmp_pos_k, cmp_pos_v, cmp_k_w1, cmp_k_w2, cmp_v_w1,
           cmp_v_w2, w_branch_a, w_branch_b, w_out, norm_ffn, ffn_w1, ffn_w3, ffn_w2, router,
           moe_w1, moe_w3, moe_w2, final_norm):
    B, S, D = x.shape
    depth = norm_mix.shape[0]
    assert depth == 2 and D == D_MODEL
    tables = _rope_lane_tables(positions)
    mt = _importance_matrix_t(S, S // CMP_STRIDE)
    x2 = x.reshape(B * S, D)
    x2 = _mixer(x2, B, S, tables, mt, norm_mix[0], w_in[0], cmp_pos_k[0], cmp_pos_v[0],
                cmp_k_w1[0], cmp_k_w2[0], cmp_v_w1[0], cmp_v_w2[0],
                w_branch_a[0], w_branch_b[0], w_out[0])
    x2 = _ffn(x2, norm_ffn[0].reshape(1, -1), ffn_w1[0].astype(BF16), ffn_w3[0].astype(BF16),
              ffn_w2[0].astype(BF16))
    x2 = _mixer(x2, B, S, tables, mt, norm_mix[1], w_in[1], cmp_pos_k[1], cmp_pos_v[1],
                cmp_k_w1[1], cmp_k_w2[1], cmp_v_w1[1], cmp_v_w2[1],
                w_branch_a[1], w_branch_b[1], w_out[1])
    g1 = norm_ffn[1].reshape(1, -1)
    r = jnp.pad(router[0], ((0, 0), (0, LANES - N_EXPERTS)))
    h, comb, pos, post, cnt = _router(x2, g1, r)
    counts = cnt[:, 0, :N_EXPERTS].reshape(-1)
    out = _moe(x2, h, comb, pos, post, counts, moe_w1[0].astype(BF16), moe_w3[0].astype(BF16),
               moe_w2[0].astype(BF16), final_norm.reshape(1, -1))
    return out.reshape(B, S, D)
```

```python
import functools

import numpy as np
import jax
import jax.numpy as jnp
from jax import lax
from jax.experimental import pallas as pl
from jax.experimental.pallas import tpu as pltpu

F32 = jnp.float32
BF16 = jnp.bfloat16

D_MODEL = 1024
HEAD_DIM = 64
ROT_DIM = HEAD_DIM // 4
ROPE_THETA = 500000.0
EPS = 1e-6
NSA_HEADS = 8
NSA_KV_HEADS = 2
HEADS_PER_KV = NSA_HEADS // NSA_KV_HEADS
CMP_LEN = 32
CMP_STRIDE = 16
CMP_HIDDEN = 128
SEL_LEN = 64
N_SEL = 16
WIN = 512
DIL_PATTERNS = ((128, 1), (512, 4), (2048, 16))
N_DIL_GROUPS = 3
DIL_HEADS = 4
D_FF = 2816
N_EXPERTS = 8

LANES = 128
VMEM_LIMIT = 56 * 1024 * 1024
NEG = -1e30
BIG = 1e30

COL_QA = 0
COL_QB0 = 512
COL_KB0 = 768
COL_VB0 = 1024
COL_KSEL = 1280
COL_BLK = COL_KSEL + 128
COL_KCMP = 1536
COL_KWIN = 1664
COL_VCMP = 1792
COL_VSEL = 1920
COL_MERGE = 2048
COL_VWIN = 4096
COL_GATE = 4224
N_PROJ = 4352
DIL_WIDTH = DIL_HEADS * HEAD_DIM
N_FOLD = 2 * DIL_WIDTH
N_W_IN = N_PROJ + 3 * N_FOLD
IN_CHUNKS = (
    (0, 512, True, None), (512, 512, True, None), (1024, 256, False, None),
    (COL_KSEL, 128, True, None), (COL_KCMP, 256, True, None), (COL_VCMP, 256, False, None),
    (2048, 512, False, None), (2560, 512, False, None), (3072, 512, False, None),
    (3584, 512, False, None), (COL_VWIN, 256, False, None),
    (N_PROJ, N_FOLD, True, 0), (N_PROJ + N_FOLD, N_FOLD, True, 1),
    (N_PROJ + 2 * N_FOLD, N_FOLD, False, 2),
)
LOG2E = 1.4426950408889634
LN2 = 0.6931471805599453
MASK_BIAS = -(2.0 ** 100)
SEL_BLOCKS_MAX = 32


def _dot(a, b, precision=None):
    return jnp.dot(a, b, preferred_element_type=F32, precision=precision)


def _dot_nt(a, b, precision=None):
    return lax.dot_general(a, b, (((1,), (1,)), ((), ())), preferred_element_type=F32,
                           precision=precision)


def _rms(x, g):
    ms = jnp.mean(x * x, axis=-1, keepdims=True)
    return x * lax.rsqrt(ms + EPS) * g


def _silu(x):
    return x * jax.nn.sigmoid(x)


def _params(**kw):
    return pltpu.CompilerParams(vmem_limit_bytes=VMEM_LIMIT, **kw)


def _const_spec(shape):
    nd = len(shape)
    return pl.BlockSpec(shape, lambda *_: (0,) * nd)


def _inproj_body(x_ref, g_ref, w_ref, c_ref, sa_ref, sb_ref, o_ref, f1_ref, f2_ref, st_ref, *,
                 per_b):
    tm = x_ref.shape[0]
    t_seq = (pl.program_id(0) % per_b) * tm + lax.broadcasted_iota(jnp.int32, (tm, LANES), 0)
    lane = lax.broadcasted_iota(jnp.int32, (tm, LANES), 1)
    blk = lax.shift_right_logical(t_seq, 6)
    hot = (lane == blk) | (lane == blk + SEL_BLOCKS_MAX)
    o_ref[:, COL_BLK:COL_BLK + LANES] = jnp.where(hot, 1.0, 0.0).astype(BF16)
    h = _rms(x_ref[...], g_ref[...]).astype(BF16)
    c = c_ref[...]
    sa = sa_ref[...]
    sb = sb_ref[...]
    for start, size, rope, piece in IN_CHUNKS:
        acc = _dot(h, w_ref[:, start:start + size])
        for j in range(size // LANES):
            a = acc[:, j * LANES:(j + 1) * LANES]
            if rope:
                a = a * c + pltpu.roll(a, LANES - 8, 1) * sa + pltpu.roll(a, 8, 1) * sb
            if piece is None:
                o_ref[:, start + j * LANES:start + (j + 1) * LANES] = a.astype(BF16)
            else:
                st_ref[j] = a
        if piece is not None:
            slabs = DIL_WIDTH // LANES
            for gi, f_ref in enumerate((f1_ref, f2_ref)):
                d = DIL_PATTERNS[gi + 1][1]
                for r in range(d):
                    for k in range(slabs):
                        rows = st_ref[gi * slabs + k, pl.ds(r, tm // d, stride=d), :]
                        c0 = piece * DIL_WIDTH + k * LANES
                        f_ref[0, r, :, c0:c0 + LANES] = rows.astype(BF16)


def _inproj(x2, g, w, rc, rsa, rsb, B, S, tm=512):
    T = x2.shape[0]
    per_b = S // tm
    d1, d2 = DIL_PATTERNS[1][1], DIL_PATTERNS[2][1]
    fold_spec = lambda d: pl.BlockSpec((1, d, tm // d, 3 * DIL_WIDTH),
                                       lambda i: (i // per_b, 0, i % per_b, 0))
    fold_shape = lambda d: jax.ShapeDtypeStruct((B, d, S // d, 3 * DIL_WIDTH), BF16)
    assert S // SEL_LEN <= SEL_BLOCKS_MAX
    return pl.pallas_call(
        functools.partial(_inproj_body, per_b=per_b),
        grid=(T // tm,),
        in_specs=[
            pl.BlockSpec((tm, D_MODEL), lambda i: (i, 0)),
            _const_spec((1, D_MODEL)),
            _const_spec((D_MODEL, N_W_IN)),
            pl.BlockSpec((tm, LANES), lambda i: (i, 0)),
            pl.BlockSpec((tm, LANES), lambda i: (i, 0)),
            pl.BlockSpec((tm, LANES), lambda i: (i, 0)),
        ],
        out_specs=[pl.BlockSpec((tm, N_PROJ), lambda i: (i, 0)), fold_spec(d1), fold_spec(d2)],
        out_shape=[jax.ShapeDtypeStruct((T, N_PROJ), BF16), fold_shape(d1), fold_shape(d2)],
        scratch_shapes=[pltpu.VMEM((N_FOLD // LANES, tm, LANES), F32)],
        compiler_params=_params(),
        name="inproj",
    )(x2, g, w, rc, rsa, rsb)


def _compress_body(a_ref, w1_ref, w2_ref, pos_ref, o_ref):
    a = a_ref[0]
    w1 = w1_ref[0]
    half = CMP_STRIDE * HEAD_DIM
    top = _dot(a, w1[:half])
    bot = _dot(a, w1[half:])
    pc = _dot(pos_ref[0], w1)
    rows = a.shape[0]
    hid = top + pltpu.roll(bot, rows - 1, 0) + pc[0:1]
    o_ref[0] = _dot(_silu(hid).astype(BF16), w2_ref[0]).astype(BF16)


def _compress(a, w1, w2, pos):
    n, rows, _ = a.shape
    return pl.pallas_call(
        _compress_body,
        grid=(n,),
        in_specs=[
            pl.BlockSpec((1, rows, CMP_STRIDE * HEAD_DIM), lambda i: (i, 0, 0)),
            pl.BlockSpec((1, CMP_LEN * HEAD_DIM, CMP_HIDDEN), lambda i: (i, 0, 0)),
            pl.BlockSpec((1, CMP_HIDDEN, HEAD_DIM), lambda i: (i, 0, 0)),
            pl.BlockSpec((1, 8, CMP_LEN * HEAD_DIM), lambda i: (i, 0, 0)),
        ],
        out_specs=pl.BlockSpec((1, rows, HEAD_DIM), lambda i: (i, 0, 0)),
        out_shape=jax.ShapeDtypeStruct((n, rows, HEAD_DIM), BF16),
        compiler_params=_params(),
        name="compress",
    )(a, w1, w2, pos)


def _softmax2(s):
    m = jnp.max(s, axis=-1, keepdims=True)
    e = jnp.exp2(s - m)
    return e.astype(BF16), jnp.sum(e, axis=-1, keepdims=True)


def _weighted_values(e, l, v):
    nh, tq, nk = e.shape
    return _dot(e.reshape(nh * tq, nk), v) / l.reshape(nh * tq, 1)


SEL_PREFIX = 512


def _nsa_body(q_ref, kc_ref, vc_ref, ksel_ref, vsel_ref, kwin_ref, vwin_ref, gate_ref, mt_ref,
              o_ref, osel_ref, *, tq, seq):
    nblk = seq // SEL_LEN
    ncp = kc_ref.shape[1]
    n_cmp = (seq - CMP_LEN) // CMP_STRIDE + 1
    q0 = pl.program_id(1) * tq
    q = q_ref[0]
    t_col = q0 + lax.broadcasted_iota(jnp.int32, (tq, 1), 0)
    t_row = q0 + lax.broadcasted_iota(jnp.int32, (1, tq), 1)
    gates = jax.nn.sigmoid(gate_ref[0].astype(F32))

    zeros64 = jnp.zeros((tq, HEAD_DIM), BF16)

    def stacked_q(g):
        parts = []
        for hh in range(HEADS_PER_KV):
            h = g * HEADS_PER_KV + hh
            qh = q[:, h * HEAD_DIM:(h + 1) * HEAD_DIM]
            parts.append(jnp.concatenate([qh, zeros64] if g == 0 else [zeros64, qh], axis=1))
        return jnp.concatenate(parts, axis=0)

    qs = [stacked_q(g) for g in range(NSA_KV_HEADS)]

    cidx = lax.broadcasted_iota(jnp.int32, (tq, ncp), 1)
    cmask = ((cidx * CMP_STRIDE + (CMP_LEN - 1)) <= t_col) & (cidx < n_cmp)
    jidx = lax.broadcasted_iota(jnp.int32, (nblk, tq), 0)
    cur = lax.shift_right_logical(t_row, 6)
    forced = (jidx == 0) | (jidx == cur) | (jidx == cur - 1)
    future = jidx > cur
    groups = range(NSA_KV_HEADS)
    span = WIN + tq
    ks = pl.multiple_of(jnp.maximum(q0 - WIN, 0), tq)
    kw = kwin_ref[0, pl.ds(ks, span), :]
    vw = vwin_ref[0, pl.ds(ks, span), :]
    wpos = ks + lax.broadcasted_iota(jnp.int32, (1, span), 1)
    wmask = (wpos <= t_col) & (t_col - wpos <= WIN - 1)

    s_cmp = [jnp.where(cmask[None], _dot_nt(qs[g], kc_ref[0]).reshape(HEADS_PER_KV, tq, ncp), NEG)
             for g in groups]
    s_win = [jnp.where(wmask[None], _dot_nt(qs[g], kw).reshape(HEADS_PER_KV, tq, span), NEG)
             for g in groups]
    p_cmp = []
    for g in groups:
        m = jnp.max(s_cmp[g], axis=-1, keepdims=True)
        e = jnp.where(cmask[None], jnp.exp2(s_cmp[g] - m), 0.0)
        den = jnp.sum(e, axis=-1, keepdims=True)
        p_cmp.append(e / jnp.where(den > 0, den, 1.0))
    e_win = [_softmax2(s_win[g]) for g in groups]
    o_cmp = [_dot(p_cmp[g].astype(BF16).reshape(HEADS_PER_KV * tq, ncp), vc_ref[0]) for g in groups]
    imps = [_dot_nt(mt_ref[...], p_cmp[g][0] + p_cmp[g][1] + p_cmp[g][2] + p_cmp[g][3],
                    precision=lax.Precision.HIGHEST) for g in groups]
    o_win = [_weighted_values(*e_win[g], vw) for g in groups]

    first_blk = lax.shift_right_logical(q0, 6)
    bias_rows = []
    for g in groups:
        imp = jnp.where(forced, BIG, imps[g])
        imp = jnp.where(future, -BIG, imp)
        rank = jnp.zeros((nblk, tq), jnp.int32)
        for i in range(nblk):
            row = imp[i:i + 1, :]
            beats = (row > imp) | ((row == imp) & (jidx > i))
            rank = rank + beats.astype(jnp.int32)
        bias_rows.append(jnp.where((rank < N_SEL) & (jidx < first_blk), 0.0, MASK_BIAS))
    pad_rows = LANES - NSA_KV_HEADS * SEL_BLOCKS_MAX
    bias_t = jnp.concatenate(bias_rows + [jnp.zeros((pad_rows, tq), F32)], axis=0)
    bias = bias_t.T.astype(BF16)
    lane_group = lax.shift_right_logical(lax.broadcasted_iota(jnp.int32, (tq, LANES), 1), 5)
    qb = []
    for g in groups:
        own = jnp.where(lane_group == g, bias, jnp.zeros_like(bias))
        qb.append(jnp.concatenate([qs[g], jnp.concatenate([own] * HEADS_PER_KV, axis=0)], axis=1))

    kdiag = ksel_ref[0, pl.ds(pl.multiple_of(q0, tq), tq), :LANES]
    vdiag = vsel_ref[0, pl.ds(pl.multiple_of(q0, tq), tq), :]
    tri = (lax.broadcasted_iota(jnp.int32, (tq, tq), 1) <= lax.broadcasted_iota(jnp.int32, (tq, tq), 0))
    s_diag = [jnp.where(tri[None], _dot_nt(qs[g], kdiag).reshape(HEADS_PER_KV, tq, tq), NEG)
              for g in groups]
    n_prefix = q0 // SEL_PREFIX + 1
    for n in range(1, seq // SEL_PREFIX + 1):
        klen = n * SEL_PREFIX

        @pl.when(n_prefix == n)
        def _(klen=klen):
            vall = jnp.concatenate([vsel_ref[0, :klen, :], vdiag], axis=0)
            s = [jnp.concatenate(
                [_dot_nt(qb[g], ksel_ref[0, :klen, :]).reshape(HEADS_PER_KV, tq, klen), s_diag[g]],
                axis=-1) for g in groups]
            ew = [_softmax2(s[g]) for g in groups]
            for g in groups:
                osel_ref[g] = _weighted_values(*ew[g], vall)

    o_sel = [osel_ref[g] for g in groups]

    outs = []
    for g in range(NSA_KV_HEADS):
        for hh in range(HEADS_PER_KV):
            h = g * HEADS_PER_KV + hh
            acc = jnp.zeros((tq, HEAD_DIM), F32)
            for br, o in enumerate((o_cmp[g], o_sel[g], o_win[g])):
                oh = o[hh * tq:(hh + 1) * tq, g * HEAD_DIM:(g + 1) * HEAD_DIM]
                acc = acc + gates[:, 3 * h + br:3 * h + br + 1] * oh
            outs.append(acc)
    o_ref[0] = jnp.concatenate(outs, axis=1).astype(BF16)


def _nsa(proj3, kc, vc, mt, tq=128):
    B, S, _ = proj3.shape
    blk = lambda c: c // LANES
    seq_spec = lambda c: pl.BlockSpec((1, S, LANES), lambda b, i: (b, 0, blk(c)))
    ncp = kc.shape[1]
    return pl.pallas_call(
        functools.partial(_nsa_body, tq=tq, seq=S),
        grid=(B, S // tq),
        in_specs=[
            pl.BlockSpec((1, tq, NSA_HEADS * HEAD_DIM), lambda b, i: (b, i, 0)),
            pl.BlockSpec((1, ncp, LANES), lambda b, i: (b, 0, 0)),
            pl.BlockSpec((1, ncp, LANES), lambda b, i: (b, 0, 0)),
            pl.BlockSpec((1, S, 2 * LANES), lambda b, i: (b, 0, COL_KSEL // (2 * LANES))),
            seq_spec(COL_VSEL), seq_spec(COL_KWIN), seq_spec(COL_VWIN),
            pl.BlockSpec((1, tq, LANES), lambda b, i: (b, i, blk(COL_GATE))),
            _const_spec(mt.shape),
        ],
        out_specs=pl.BlockSpec((1, tq, NSA_HEADS * HEAD_DIM), lambda b, i: (b, i, 0)),
        out_shape=jax.ShapeDtypeStruct((B, S, NSA_HEADS * HEAD_DIM), BF16),
        scratch_shapes=[pltpu.VMEM((NSA_KV_HEADS, HEADS_PER_KV * tq, LANES), F32)],
        compiler_params=_params(),
        name="nsa",
    )(proj3, kc, vc, proj3, proj3, proj3, proj3, proj3, mt)


DIL_SUB = 128
DIL_ROWS = 512


def _dil_body(q_ref, kp_ref, kc_ref, vp_ref, vc_ref, o_ref, lse_ref, *, n_back):
    sub = DIL_SUB
    rb, tq = q_ref.shape[1], q_ref.shape[2]
    t0 = pl.program_id(2) * tq
    head_of = lax.shift_right_logical(lax.broadcasted_iota(jnp.int32, (sub, DIL_WIDTH), 1), 6)
    lane = lax.broadcasted_iota(jnp.int32, (sub, LANES), 1)
    diff = (sub + lax.broadcasted_iota(jnp.int32, (sub, 1), 0)
            - lax.broadcasted_iota(jnp.int32, (1, 2 * sub), 1))
    band = (diff >= 0) & (diff <= n_back)
    band0 = band & (lax.broadcasted_iota(jnp.int32, (1, 2 * sub), 1) + t0 >= sub)
    tiles = [(r, j) for r in range(rb) for j in range(tq // sub)]
    keys = {r: jnp.concatenate([kp_ref[0, r], kc_ref[0, r]], axis=0) for r in range(rb)}
    vals = {r: jnp.concatenate([vp_ref[0, r], vc_ref[0, r]], axis=0) for r in range(rb)}
    scores = []
    for r, j in tiles:
        q = q_ref[0, r, j * sub:(j + 1) * sub, :]
        qs = jnp.concatenate([jnp.where(head_of == h, q, jnp.zeros_like(q))
                              for h in range(DIL_HEADS)], axis=0)
        s = _dot_nt(qs, keys[r][j * sub:(j + 2) * sub]).reshape(DIL_HEADS, sub, 2 * sub)
        scores.append(jnp.where((band0 if j == 0 else band)[None], s, NEG))
    stats = []
    for s in scores:
        m = jnp.max(s, axis=-1, keepdims=True)
        e = jnp.exp2(s - m)
        stats.append((m, e, jnp.sum(e, axis=-1, keepdims=True)))
    for (r, j), (m, e, l) in zip(tiles, stats):
        o = _dot(e.astype(BF16).reshape(DIL_HEADS * sub, 2 * sub), vals[r][j * sub:(j + 2) * sub])
        o = o.reshape(DIL_HEADS, sub, DIL_WIDTH) / l
        lse = m * LN2 + jnp.log(l)
        o_acc = jnp.zeros((sub, DIL_WIDTH), F32)
        lse_out = jnp.zeros((sub, LANES), F32)
        for h in range(DIL_HEADS):
            o_acc = jnp.where(head_of == h, o[h], o_acc)
            lse_out = jnp.where(lane == h, lse[h], lse_out)
        o_ref[0, r, j * sub:(j + 1) * sub, :] = o_acc
        lse_ref[0, r, j * sub:(j + 1) * sub, :] = lse_out


def _dilated(arr, cols, window, dilation, name):
    B, d, L, _ = arr.shape
    n_back = window // dilation
    tq = min(L, DIL_ROWS)
    rb = DIL_ROWS // tq
    assert d == dilation and n_back <= DIL_SUB and L % tq == 0 and d % rb == 0
    per = tq // DIL_SUB
    qc, kc, vc = cols
    cur = lambda c: pl.BlockSpec((1, rb, tq, DIL_WIDTH), lambda b, r, i: (b, r, i, c))
    prev = lambda c: pl.BlockSpec((1, rb, DIL_SUB, DIL_WIDTH),
                                  lambda b, r, i: (b, r, jnp.maximum(i * per - 1, 0), c))
    return pl.pallas_call(
        functools.partial(_dil_body, n_back=n_back),
        grid=(B, dilation // rb, L // tq),
        in_specs=[cur(qc), prev(kc), cur(kc), prev(vc), cur(vc)],
        out_specs=[pl.BlockSpec((1, rb, tq, DIL_WIDTH), lambda b, r, i: (b, r, i, 0)),
                   pl.BlockSpec((1, rb, tq, LANES), lambda b, r, i: (b, r, i, 0))],
        out_shape=[jax.ShapeDtypeStruct((B, dilation, L, DIL_WIDTH), F32),
                   jax.ShapeDtypeStruct((B, dilation, L, LANES), F32)],
        compiler_params=_params(),
        name=name,
    )(arr, arr, arr, arr, arr)


def _mixout_body(oa_ref, ob0_ref, ob1_ref, ob2_ref, l0_ref, l1_ref, l2_ref, mg_ref, x_ref,
                 pa_ref, pb_ref, wo_ref, out_ref, so_ref, sl_ref):
    tm = x_ref.shape[0]

    def interleaved(src_ref, st_ref):
        d = src_ref.shape[1]
        if d == 1:
            return src_ref[0, 0]
        slabs = src_ref.shape[3] // LANES
        for r in range(d):
            for k in range(slabs):
                st_ref[k, pl.ds(r, tm // d, stride=d), :] = src_ref[0, r, :, k * LANES:(k + 1) * LANES]
        return jnp.concatenate([st_ref[k] for k in range(slabs)], axis=1)

    lses = [interleaved(l, sl_ref) for l in (l0_ref, l1_ref, l2_ref)]
    mx = jnp.maximum(jnp.maximum(lses[0], lses[1]), lses[2])
    ws = [jnp.exp(l - mx) for l in lses]
    den = ws[0] + ws[1] + ws[2]
    ob = jnp.zeros((tm, DIL_WIDTH), F32)
    for w, o_ref in zip(ws, (ob0_ref, ob1_ref, ob2_ref)):
        alpha = w / den
        wide = jnp.concatenate(
            [jnp.broadcast_to(alpha[:, h:h + 1], (tm, HEAD_DIM)) for h in range(DIL_HEADS)], axis=1)
        ob = ob + wide * interleaved(o_ref, so_ref)
    ya = _dot(oa_ref[...], pa_ref[...])
    yb = _dot(ob.astype(BF16), pb_ref[...])
    gm = jax.nn.sigmoid(mg_ref[...].astype(F32))
    y = gm[:, :D_MODEL] * ya + gm[:, D_MODEL:] * yb
    out_ref[...] = x_ref[...] + _dot(y.astype(BF16), wo_ref[...])


def _mixout(oa, obs, lses, proj, x2, pa, pb, wo, S, tm=512):
    T = x2.shape[0]
    per_b = S // tm
    row = lambda w: pl.BlockSpec((tm, w), lambda i: (i, 0))
    folded = lambda a: pl.BlockSpec((1, a.shape[1], tm // a.shape[1], a.shape[3]),
                                    lambda i: (i // per_b, 0, i % per_b, 0))
    return pl.pallas_call(
        _mixout_body,
        grid=(T // tm,),
        in_specs=[row(NSA_HEADS * HEAD_DIM), *[folded(a) for a in obs], *[folded(a) for a in lses],
                  pl.BlockSpec((tm, 2 * D_MODEL), lambda i: (i, COL_MERGE // (2 * D_MODEL))),
                  row(D_MODEL),
                  _const_spec(pa.shape), _const_spec(pb.shape), _const_spec(wo.shape)],
        out_specs=row(D_MODEL),
        out_shape=jax.ShapeDtypeStruct((T, D_MODEL), F32),
        scratch_shapes=[pltpu.VMEM((DIL_WIDTH // LANES, tm, LANES), F32),
                        pltpu.VMEM((1, tm, LANES), F32)],
        compiler_params=_params(),
        name="mixout",
    )(oa, *obs, *lses, proj, x2, pa, pb, wo)


FF_CHUNK = 512


def _ffn_body(x_ref, g_ref, w1_ref, w3_ref, w2_ref, o_ref):
    x = x_ref[...]
    h = _rms(x, g_ref[...]).astype(BF16)
    acc = jnp.zeros(x.shape, F32)
    for c0 in range(0, D_FF, FF_CHUNK):
        c1 = min(c0 + FF_CHUNK, D_FF)
        act = _silu(_dot(h, w1_ref[:, c0:c1])) * _dot(h, w3_ref[:, c0:c1])
        acc = acc + _dot(act.astype(BF16), w2_ref[c0:c1, :])
    o_ref[...] = x + acc


def _ffn(x2, g, w1, w3, w2, tm=512):
    T = x2.shape[0]
    return pl.pallas_call(
        _ffn_body,
        grid=(T // tm,),
        in_specs=[pl.BlockSpec((tm, D_MODEL), lambda i: (i, 0)), _const_spec(g.shape),
                  _const_spec(w1.shape), _const_spec(w3.shape), _const_spec(w2.shape)],
        out_specs=pl.BlockSpec((tm, D_MODEL), lambda i: (i, 0)),
        out_shape=jax.ShapeDtypeStruct((T, D_MODEL), F32),
        compiler_params=_params(),
        name="ffn",
    )(x2, g, w1, w3, w2)


MOE_TM = 1024
MOE_CHUNK = 128


def _router_body(x_ref, g_ref, r_ref, h_ref, combw_ref, posw_ref, post_ref, cnt_ref):
    tm = x_ref.shape[0]
    h = _rms(x_ref[...], g_ref[...])
    h_ref[...] = h.astype(BF16)
    logits = _dot(h, r_ref[...], precision=lax.Precision.HIGHEST)
    lane = lax.broadcasted_iota(jnp.int32, logits.shape, 1)
    lg = jnp.where(lane < N_EXPERTS, logits, NEG)
    m1 = jnp.max(lg, axis=-1, keepdims=True)
    i1 = jnp.min(jnp.where(lg == m1, lane, LANES), axis=-1, keepdims=True)
    lg2 = jnp.where(lane == i1, NEG, lg)
    m2 = jnp.max(lg2, axis=-1, keepdims=True)
    i2 = jnp.min(jnp.where(lg2 == m2, lane, LANES), axis=-1, keepdims=True)
    e2 = jnp.exp(m2 - m1)
    den = 1.0 + e2
    comb = jnp.where(lane == i1, 1.0 / den, 0.0) + jnp.where(lane == i2, e2 / den, 0.0)
    chosen = (lane == i1) | (lane == i2)
    ones = jnp.where(chosen, 1.0, 0.0)
    row = lax.broadcasted_iota(jnp.int32, (tm, tm), 0)
    col = lax.broadcasted_iota(jnp.int32, (tm, tm), 1)
    tri = jnp.where(col < row, 1.0, 0.0).astype(BF16)
    pos = jnp.where(chosen, _dot(tri, ones.astype(BF16)), -1.0)
    post_ref[0] = pos.T[:N_EXPERTS]
    for ex in range(N_EXPERTS):
        slab = slice(ex * LANES, (ex + 1) * LANES)
        combw_ref[:, slab] = jnp.broadcast_to(comb[:, ex:ex + 1], (tm, LANES))
        posw_ref[:, slab] = jnp.broadcast_to(pos[:, ex:ex + 1], (tm, LANES))
    cnt = jnp.sum(ones, axis=0, keepdims=True)
    cnt_ref[0] = jnp.broadcast_to(cnt, (8, LANES)).astype(jnp.int32)


def _router(x2, g, r, tm=MOE_TM):
    T = x2.shape[0]
    nt = T // tm
    row = lambda w: pl.BlockSpec((tm, w), lambda i: (i, 0))
    return pl.pallas_call(
        _router_body,
        grid=(nt,),
        in_specs=[row(D_MODEL), _const_spec(g.shape), _const_spec(r.shape)],
        out_specs=[row(D_MODEL), row(N_EXPERTS * LANES), row(N_EXPERTS * LANES),
                   pl.BlockSpec((1, N_EXPERTS, tm), lambda i: (i, 0, 0)),
                   pl.BlockSpec((1, 8, LANES), lambda i: (i, 0, 0))],
        out_shape=[jax.ShapeDtypeStruct((T, D_MODEL), BF16),
                   jax.ShapeDtypeStruct((T, N_EXPERTS * LANES), F32),
                   jax.ShapeDtypeStruct((T, N_EXPERTS * LANES), F32),
                   jax.ShapeDtypeStruct((nt, N_EXPERTS, tm), F32),
                   jax.ShapeDtypeStruct((nt, 8, LANES), jnp.int32)],
        compiler_params=_params(),
        name="router",
    )(x2, g, r)


def _moe_body(cnt_ref, x_ref, h_ref, comb_ref, pos_ref, post_ref, w1_ref, w3_ref, w2_ref, fn_ref,
              o_ref, xe_ref, y_ref):
    i = pl.program_id(0)
    e = pl.program_id(1)
    f = pl.program_id(2)
    last_f = pl.num_programs(2) - 1
    tm = h_ref.shape[0]
    n_chunks = (cnt_ref[i * N_EXPERTS + e] + (MOE_CHUNK - 1)) // MOE_CHUNK
    post_e = post_ref[0, pl.ds(e, 1), :]

    @pl.when((e == 0) & (f == 0))
    def _():
        o_ref[...] = x_ref[...]

    def chunk(c, carry):
        r0 = pl.multiple_of(c * MOE_CHUNK, MOE_CHUNK)
        rows = pl.ds(r0, MOE_CHUNK)
        base = r0.astype(F32)

        @pl.when(f == 0)
        def _():
            slot_col = base + lax.broadcasted_iota(jnp.int32, (MOE_CHUNK, 1), 0).astype(F32)
            gather = jnp.where(post_e == slot_col, 1.0, 0.0).astype(BF16)
            xe_ref[rows, :] = _dot(gather, h_ref[...]).astype(BF16)

        xe = xe_ref[rows, :]
        act = _silu(_dot(xe, w1_ref[0])) * _dot(xe, w3_ref[0])
        y = _dot(act.astype(BF16), w2_ref[0])

        @pl.when(f == 0)
        def _():
            y_ref[rows, :] = y

        @pl.when(f > 0)
        def _():
            y_ref[rows, :] += y

        @pl.when(f == last_f)
        def _():
            slot_row = base + lax.broadcasted_iota(jnp.int32, (1, MOE_CHUNK), 1).astype(F32)
            scatter = jnp.where(pos_ref[...] == slot_row, 1.0, 0.0).astype(BF16)
            weight = jnp.concatenate([comb_ref[...]] * (D_MODEL // LANES), axis=1)
            o_ref[...] += weight * _dot(scatter, y_ref[rows, :].astype(BF16))

        return carry

    lax.fori_loop(0, n_chunks, chunk, 0)

    @pl.when((e == pl.num_programs(1) - 1) & (f == last_f))
    def _():
        o_ref[...] = _rms(o_ref[...], fn_ref[...])


def _moe(x2, h, comb, pos, post, counts, w1, w3, w2, fn, tm=MOE_TM, nf=2):
    T = x2.shape[0]
    fc = D_FF // nf
    once = lambda w: pl.BlockSpec((tm, w), lambda i, e, f, cnt: (i, 0),
                                  pipeline_mode=pl.Buffered(1))
    tile = lambda w: pl.BlockSpec((tm, w), lambda i, e, f, cnt: (i, 0))
    assert MOE_CHUNK == LANES
    slab = pl.BlockSpec((tm, LANES), lambda i, e, f, cnt: (i, e))
    grid_spec = pltpu.PrefetchScalarGridSpec(
        num_scalar_prefetch=1,
        grid=(T // tm, N_EXPERTS, nf),
        in_specs=[once(D_MODEL), once(D_MODEL), slab, slab,
                  pl.BlockSpec((1, N_EXPERTS, tm), lambda i, e, f, cnt: (i, 0, 0)),
                  pl.BlockSpec((1, D_MODEL, fc), lambda i, e, f, cnt: (e, 0, f)),
                  pl.BlockSpec((1, D_MODEL, fc), lambda i, e, f, cnt: (e, 0, f)),
                  pl.BlockSpec((1, fc, D_MODEL), lambda i, e, f, cnt: (e, f, 0)),
                  pl.BlockSpec(fn.shape, lambda i, e, f, cnt: (0, 0))],
        out_specs=tile(D_MODEL),
        scratch_shapes=[pltpu.VMEM((tm, D_MODEL), BF16), pltpu.VMEM((tm, D_MODEL), F32)],
    )
    return pl.pallas_call(
        _moe_body,
        grid_spec=grid_spec,
        out_shape=jax.ShapeDtypeStruct((T, D_MODEL), F32),
        compiler_params=_params(),
        name="moe",
    )(counts, x2, h, comb, pos, post, w1, w3, w2, fn)


def _rope_lane_tables(positions):
    half = ROT_DIM // 2
    inv = ROPE_THETA ** (-jnp.arange(0, ROT_DIM, 2, dtype=F32) / ROT_DIM)
    ang = positions.astype(F32).reshape(-1, 1) * inv
    cos, sin = jnp.cos(ang), jnp.sin(ang)
    lane = np.arange(LANES)
    within = lane % HEAD_DIM
    pick = lane % half
    cos_l, sin_l = cos[:, pick], sin[:, pick]
    c = jnp.where(within < ROT_DIM, cos_l, 1.0)
    sa = jnp.where(within < half, -sin_l, 0.0)
    sb = jnp.where((within >= half) & (within < ROT_DIM), sin_l, 0.0)
    return c, sa, sb


def _permute_w_in(w):
    scale = HEAD_DIM ** -0.5 * LOG2E
    qa, kv, gate, qkv_b, merge = 0, 512, 1280, 1304, 3608
    nb = N_DIL_GROUPS * DIL_WIDTH
    kv_piece = lambda j: w[:, kv + 128 * j:kv + 128 * (j + 1)]
    dil = lambda part, g0, g1: w[:, qkv_b + part * nb + g0 * DIL_WIDTH:
                                 qkv_b + part * nb + g1 * DIL_WIDTH]
    zeros = lambda n: jnp.zeros((w.shape[0], n), w.dtype)
    pieces = [
        w[:, qa:qa + 512] * scale,
        dil(0, 0, 1) * scale, dil(1, 0, 1), dil(2, 0, 1),
        kv_piece(2), zeros(LANES),
        kv_piece(0), kv_piece(4), kv_piece(1), kv_piece(3),
        w[:, merge:merge + 2 * D_MODEL],
        kv_piece(5),
        w[:, gate:gate + 3 * NSA_HEADS], zeros(LANES - 3 * NSA_HEADS),
        dil(0, 1, 3) * scale, dil(1, 1, 3), dil(2, 1, 3),
    ]
    out = jnp.concatenate(pieces, axis=1)
    assert out.shape[1] == N_W_IN
    return out.astype(BF16)


def _importance_matrix_t(seq, ncp):
    n_c = (seq - CMP_LEN) // CMP_STRIDE + 1
    starts = np.arange(n_c) * CMP_STRIDE
    bstart = np.arange(seq // SEL_LEN) * SEL_LEN
    overlap = np.clip(np.minimum(starts[:, None] + CMP_LEN, bstart[None, :] + SEL_LEN)
                      - np.maximum(starts[:, None], bstart[None, :]), 0, None)
    m = np.zeros((ncp, seq // SEL_LEN), np.float32)
    m[:n_c] = overlap.astype(np.float32) / CMP_LEN
    return jnp.asarray(m.T)


def _cmp_rows(proj3, col):
    B, S, _ = proj3.shape
    a = proj3[:, :, col:col + LANES].reshape(B, S // CMP_STRIDE, CMP_STRIDE, NSA_KV_HEADS, HEAD_DIM)
    return a.transpose(0, 3, 1, 2, 4).reshape(B * NSA_KV_HEADS * (S // CMP_STRIDE),
                                              CMP_STRIDE * HEAD_DIM)


def _mixer(x2, B, S, tables, mt, norm_g, w_in, pos_k, pos_v, wk1, wk2, wv1, wv2, p_a, p_b, w_o):
    proj, fold1, fold2 = _inproj(x2, norm_g.reshape(1, -1), _permute_w_in(w_in), *tables, B, S)
    proj3 = proj.reshape(B, S, N_PROJ)
    ncp = S // CMP_STRIDE
    a = jnp.stack([_cmp_rows(proj3, COL_KCMP), _cmp_rows(proj3, COL_VCMP)])
    pos = jnp.stack([pos_k, pos_v]).reshape(2, 1, CMP_LEN * HEAD_DIM)
    pos = jnp.broadcast_to(pos, (2, 8, CMP_LEN * HEAD_DIM)).astype(BF16)
    cmp = _compress(a, jnp.stack([wk1, wv1]).astype(BF16), jnp.stack([wk2, wv2]).astype(BF16), pos)
    cmp = cmp.reshape(2, B, NSA_KV_HEADS, ncp, HEAD_DIM).transpose(0, 1, 3, 2, 4)
    cmp = cmp.reshape(2, B, ncp, NSA_KV_HEADS * HEAD_DIM)
    oa = _nsa(proj3, cmp[0], cmp[1], mt).reshape(B * S, NSA_HEADS * HEAD_DIM)
    base_cols = tuple(c // DIL_WIDTH for c in (COL_QB0, COL_KB0, COL_VB0))
    sources = ((proj3.reshape(B, 1, S, N_PROJ), base_cols), (fold1, (0, 1, 2)), (fold2, (0, 1, 2)))
    obs, lses = [], []
    for g, ((w, d), (arr, cols)) in enumerate(zip(DIL_PATTERNS, sources)):
        o, lse = _dilated(arr, cols, w, d, f"dilated{g}")
        obs.append(o)
        lses.append(lse)
    return _mixout(oa, obs, lses, proj, x2, p_a.astype(BF16), p_b.astype(BF16), w_o.astype(BF16), S)


def kernel(x, positions, norm_mix, w_in, cmp_pos_k, cmp_pos_v, cmp_k_w1, cmp_k_w2, cmp_v_w1,
           cmp_v_w2, w_branch_a, w_branch_b, w_out, norm_ffn, ffn_w1, ffn_w3, ffn_w2, router,
           moe_w1, moe_w3, moe_w2, final_norm):
    B, S, D = x.shape
    depth = norm_mix.shape[0]
    assert depth == 2 and D == D_MODEL
    tables = _rope_lane_tables(positions)
    mt = _importance_matrix_t(S, S // CMP_STRIDE)
    x2 = x.reshape(B * S, D)
    x2 = _mixer(x2, B, S, tables, mt, norm_mix[0], w_in[0], cmp_pos_k[0], cmp_pos_v[0],
                cmp_k_w1[0], cmp_k_w2[0], cmp_v_w1[0], cmp_v_w2[0],
                w_branch_a[0], w_branch_b[0], w_out[0])
    x2 = _ffn(x2, norm_ffn[0].reshape(1, -1), ffn_w1[0].astype(BF16), ffn_w3[0].astype(BF16),
              ffn_w2[0].astype(BF16))
    x2 = _mixer(x2, B, S, tables, mt, norm_mix[1], w_in[1], cmp_pos_k[1], cmp_pos_v[1],
                cmp_k_w1[1], cmp_k_w2[1], cmp_v_w1[1], cmp_v_w2[1],
                w_branch_a[1], w_branch_b[1], w_out[1])
    g1 = norm_ffn[1].reshape(1, -1)
    r = jnp.pad(router[0], ((0, 0), (0, LANES - N_EXPERTS)))
    h, comb, pos, post, cnt = _router(x2, g1, r)
    counts = cnt[:, 0, :N_EXPERTS].reshape(-1)
    out = _moe(x2, h, comb, pos, post, counts, moe_w1[0].astype(BF16), moe_w3[0].astype(BF16),
               moe_w2[0].astype(BF16), final_norm.reshape(1, -1))
    return out.reshape(B, S, D)
```

```python
import functools

import numpy as np
import jax
import jax.numpy as jnp
from jax import lax
from jax.experimental import pallas as pl
from jax.experimental.pallas import tpu as pltpu

F32 = jnp.float32
BF16 = jnp.bfloat16

D_MODEL = 1024
HEAD_DIM = 64
ROT_DIM = HEAD_DIM // 4
ROPE_THETA = 500000.0
EPS = 1e-6
NSA_HEADS = 8
NSA_KV_HEADS = 2
HEADS_PER_KV = NSA_HEADS // NSA_KV_HEADS
CMP_LEN = 32
CMP_STRIDE = 16
CMP_HIDDEN = 128
SEL_LEN = 64
N_SEL = 16
WIN = 512
DIL_PATTERNS = ((128, 1), (512, 4), (2048, 16))
N_DIL_GROUPS = 3
DIL_HEADS = 4
D_FF = 2816
N_EXPERTS = 8

LANES = 128
VMEM_LIMIT = 56 * 1024 * 1024
NEG = -1e30
BIG = 1e30

COL_QA = 0
COL_QB0 = 512
COL_KB0 = 768
COL_VB0 = 1024
COL_KSEL = 1280
COL_BLK = COL_KSEL + 128
COL_KWIN = 1536
COL_VSEL = 1664
COL_VWIN = 1792
COL_GATE = 1920
COL_MERGE = 2048
N_PROJ = 4096
DIL_WIDTH = DIL_HEADS * HEAD_DIM
N_FOLD = 2 * DIL_WIDTH
STAGE = None
IN_CHUNKS = (
    (512, True, (0, 128, 256, 384), None),
    (512, True, (COL_QB0, COL_QB0 + 128, COL_KB0, COL_KB0 + 128), None),
    (384, True, (COL_KSEL, COL_KWIN, STAGE), ("cmp", 0)),
    (256, False, (COL_VB0, COL_VB0 + 128), None),
    (512, False, (COL_VSEL, COL_VWIN, COL_GATE, STAGE), ("cmp", 1)),
    (512, False, tuple(COL_MERGE + 128 * j for j in range(0, 4)), None),
    (512, False, tuple(COL_MERGE + 128 * j for j in range(4, 8)), None),
    (512, False, tuple(COL_MERGE + 128 * j for j in range(8, 12)), None),
    (512, False, tuple(COL_MERGE + 128 * j for j in range(12, 16)), None),
    (N_FOLD, True, (STAGE,) * 4, ("fold", 0)),
    (N_FOLD, True, (STAGE,) * 4, ("fold", 1)),
    (N_FOLD, False, (STAGE,) * 4, ("fold", 2)),
)
N_W_IN = sum(c[0] for c in IN_CHUNKS)
LOG2E = 1.4426950408889634
LN2 = 0.6931471805599453
MASK_BIAS = -(2.0 ** 100)
SEL_BLOCKS_MAX = 32


def _dot(a, b, precision=None):
    return jnp.dot(a, b, preferred_element_type=F32, precision=precision)


def _dot_nt(a, b, precision=None):
    return lax.dot_general(a, b, (((1,), (1,)), ((), ())), preferred_element_type=F32,
                           precision=precision)


def _rms(x, g):
    ms = jnp.mean(x * x, axis=-1, keepdims=True)
    return x * lax.rsqrt(ms + EPS) * g


def _silu(x):
    return x * jax.nn.sigmoid(x)


def _params(**kw):
    return pltpu.CompilerParams(vmem_limit_bytes=VMEM_LIMIT, **kw)


def _const_spec(shape):
    nd = len(shape)
    return pl.BlockSpec(shape, lambda *_: (0,) * nd)


def _inproj_body(x_ref, g_ref, w_ref, c_ref, sa_ref, sb_ref, o_ref, f1_ref, f2_ref, a_ref, st_ref,
                 *, per_b):
    tm = x_ref.shape[0]
    t_seq = (pl.program_id(0) % per_b) * tm + lax.broadcasted_iota(jnp.int32, (tm, LANES), 0)
    lane = lax.broadcasted_iota(jnp.int32, (tm, LANES), 1)
    blk = lax.shift_right_logical(t_seq, 6)
    hot = (lane == blk) | (lane == blk + SEL_BLOCKS_MAX)
    o_ref[:, COL_BLK:COL_BLK + LANES] = jnp.where(hot, 1.0, 0.0).astype(BF16)
    h = _rms(x_ref[...], g_ref[0]).astype(BF16)
    c = c_ref[...]
    sa = sa_ref[...]
    sb = sb_ref[...]
    start = 0
    for size, rope, dests, action in IN_CHUNKS:
        acc = _dot(h, w_ref[0, :, start:start + size])
        start += size
        for j, dest in enumerate(dests):
            a = acc[:, j * LANES:(j + 1) * LANES]
            if rope:
                a = a * c + pltpu.roll(a, LANES - 8, 1) * sa + pltpu.roll(a, 8, 1) * sb
            if dest is STAGE:
                st_ref[j] = a
            else:
                o_ref[:, dest:dest + LANES] = a.astype(BF16)
        if action is None:
            continue
        kind, piece = action
        if kind == "fold":
            slabs = DIL_WIDTH // LANES
            for gi, f_ref in enumerate((f1_ref, f2_ref)):
                d = DIL_PATTERNS[gi + 1][1]
                for r in range(d):
                    for k in range(slabs):
                        rows = st_ref[gi * slabs + k, pl.ds(r, tm // d, stride=d), :]
                        c0 = piece * DIL_WIDTH + k * LANES
                        f_ref[0, r, :, c0:c0 + LANES] = rows.astype(BF16)
        else:
            slab = dests.index(STAGE)
            nrow = tm // CMP_STRIDE
            toks = [st_ref[slab, pl.ds(j, nrow, stride=CMP_STRIDE), :] for j in range(CMP_STRIDE)]
            for g in range(NSA_KV_HEADS):
                head = slice(g * HEAD_DIM, (g + 1) * HEAD_DIM)
                for m in range(CMP_STRIDE // 2):
                    pair = jnp.concatenate([toks[2 * m][:, head], toks[2 * m + 1][:, head]], axis=1)
                    a_ref[piece, 0, g, :, m * LANES:(m + 1) * LANES] = pair.astype(BF16)


def _layer_spec(arr, layer):
    nd = arr.ndim
    return pl.BlockSpec((1,) + arr.shape[1:], lambda *_: (layer,) + (0,) * (nd - 1))


def _inproj(x2, g, w, layer, rc, rsa, rsb, B, S, tm=512):
    T = x2.shape[0]
    per_b = S // tm
    d1, d2 = DIL_PATTERNS[1][1], DIL_PATTERNS[2][1]
    fold_spec = lambda d: pl.BlockSpec((1, d, tm // d, 3 * DIL_WIDTH),
                                       lambda i: (i // per_b, 0, i % per_b, 0))
    fold_shape = lambda d: jax.ShapeDtypeStruct((B, d, S // d, 3 * DIL_WIDTH), BF16)
    cmp_w = CMP_STRIDE * HEAD_DIM
    assert S // SEL_LEN <= SEL_BLOCKS_MAX
    return pl.pallas_call(
        functools.partial(_inproj_body, per_b=per_b),
        grid=(T // tm,),
        in_specs=[
            pl.BlockSpec((tm, D_MODEL), lambda i: (i, 0)),
            _layer_spec(g, layer), _layer_spec(w, layer),
            pl.BlockSpec((tm, LANES), lambda i: (i, 0)),
            pl.BlockSpec((tm, LANES), lambda i: (i, 0)),
            pl.BlockSpec((tm, LANES), lambda i: (i, 0)),
        ],
        out_specs=[pl.BlockSpec((tm, N_PROJ), lambda i: (i, 0)), fold_spec(d1), fold_spec(d2),
                   pl.BlockSpec((2, 1, NSA_KV_HEADS, tm // CMP_STRIDE, cmp_w),
                                lambda i: (0, i // per_b, 0, i % per_b, 0))],
        out_shape=[jax.ShapeDtypeStruct((T, N_PROJ), BF16), fold_shape(d1), fold_shape(d2),
                   jax.ShapeDtypeStruct((2, B, NSA_KV_HEADS, S // CMP_STRIDE, cmp_w), BF16)],
        scratch_shapes=[pltpu.VMEM((N_FOLD // LANES, tm, LANES), F32)],
        compiler_params=_params(),
        name="inproj",
    )(x2, g, w, rc, rsa, rsb)


def _compress_body(a_ref, w1_ref, w2_ref, pos_ref, o_ref):
    nb, ncp = o_ref.shape[1], o_ref.shape[2]
    a = a_ref[0]
    w1 = w1_ref[0, 0]
    half = CMP_STRIDE * HEAD_DIM
    top = _dot(a, w1[:half])
    bot = _dot(a, w1[half:])
    pc = _dot(pos_ref[0, 0], w1)
    rows = a.shape[0]
    hid = top + pltpu.roll(bot, rows - 1, 0) + pc[0:1]
    out = _dot(_silu(hid).astype(BF16), w2_ref[0, 0])
    for b in range(nb):
        heads = [out[(b * NSA_KV_HEADS + g) * ncp:(b * NSA_KV_HEADS + g + 1) * ncp]
                 for g in range(NSA_KV_HEADS)]
        o_ref[0, b] = jnp.concatenate(heads, axis=1).astype(BF16)


def _compress(a, w1, w2, pos, layer, B):
    n, rows, _ = a.shape
    ncp = rows // (B * NSA_KV_HEADS)
    per_kv = lambda arr: pl.BlockSpec((1, 1) + arr.shape[2:], lambda i: (layer, i, 0, 0))
    return pl.pallas_call(
        _compress_body,
        grid=(n,),
        in_specs=[pl.BlockSpec((1, rows, CMP_STRIDE * HEAD_DIM), lambda i: (i, 0, 0)),
                  per_kv(w1), per_kv(w2), per_kv(pos)],
        out_specs=pl.BlockSpec((1, B, ncp, NSA_KV_HEADS * HEAD_DIM), lambda i: (i, 0, 0, 0)),
        out_shape=jax.ShapeDtypeStruct((n, B, ncp, NSA_KV_HEADS * HEAD_DIM), BF16),
        compiler_params=_params(),
        name="compress",
    )(a, w1, w2, pos)


def _softmax2(s):
    m = jnp.max(s, axis=-1, keepdims=True)
    e = jnp.exp2(s - m)
    return e.astype(BF16), jnp.sum(e, axis=-1, keepdims=True)


def _weighted_values(e, l, v):
    nh, tq, nk = e.shape
    return _dot(e.reshape(nh * tq, nk), v) / l.reshape(nh * tq, 1)


SEL_PREFIX = 512


def _nsa_body(q_ref, kc_ref, vc_ref, ksel_ref, vsel_ref, kwin_ref, vwin_ref, gate_ref, mt_ref,
              o_ref, osel_ref, *, tq, seq):
    nblk = seq // SEL_LEN
    ncp = kc_ref.shape[1]
    n_cmp = (seq - CMP_LEN) // CMP_STRIDE + 1
    q0 = pl.program_id(1) * tq
    q = q_ref[0]
    t_col = q0 + lax.broadcasted_iota(jnp.int32, (tq, 1), 0)
    t_row = q0 + lax.broadcasted_iota(jnp.int32, (1, tq), 1)
    gates = jax.nn.sigmoid(gate_ref[0].astype(F32))

    zeros64 = jnp.zeros((tq, HEAD_DIM), BF16)

    def stacked_q(g):
        parts = []
        for hh in range(HEADS_PER_KV):
            h = g * HEADS_PER_KV + hh
            qh = q[:, h * HEAD_DIM:(h + 1) * HEAD_DIM]
            parts.append(jnp.concatenate([qh, zeros64] if g == 0 else [zeros64, qh], axis=1))
        return jnp.concatenate(parts, axis=0)

    qs = [stacked_q(g) for g in range(NSA_KV_HEADS)]

    cidx = lax.broadcasted_iota(jnp.int32, (tq, ncp), 1)
    cmask = ((cidx * CMP_STRIDE + (CMP_LEN - 1)) <= t_col) & (cidx < n_cmp)
    jidx = lax.broadcasted_iota(jnp.int32, (nblk, tq), 0)
    cur = lax.shift_right_logical(t_row, 6)
    forced = (jidx == 0) | (jidx == cur) | (jidx == cur - 1)
    future = jidx > cur
    groups = range(NSA_KV_HEADS)
    span = WIN + tq
    ks = pl.multiple_of(jnp.maximum(q0 - WIN, 0), tq)
    kw = kwin_ref[0, pl.ds(ks, span), :]
    vw = vwin_ref[0, pl.ds(ks, span), :]
    wpos = ks + lax.broadcasted_iota(jnp.int32, (1, span), 1)
    wmask = (wpos <= t_col) & (t_col - wpos <= WIN - 1)

    s_cmp = [jnp.where(cmask[None], _dot_nt(qs[g], kc_ref[0]).reshape(HEADS_PER_KV, tq, ncp), NEG)
             for g in groups]
    s_win = [jnp.where(wmask[None], _dot_nt(qs[g], kw).reshape(HEADS_PER_KV, tq, span), NEG)
             for g in groups]
    p_cmp = []
    for g in groups:
        m = jnp.max(s_cmp[g], axis=-1, keepdims=True)
        e = jnp.where(cmask[None], jnp.exp2(s_cmp[g] - m), 0.0)
        den = jnp.sum(e, axis=-1, keepdims=True)
        p_cmp.append(e / jnp.where(den > 0, den, 1.0))
    e_win = [_softmax2(s_win[g]) for g in groups]
    o_cmp = [_dot(p_cmp[g].astype(BF16).reshape(HEADS_PER_KV * tq, ncp), vc_ref[0]) for g in groups]
    imps = [_dot_nt(mt_ref[...], p_cmp[g][0] + p_cmp[g][1] + p_cmp[g][2] + p_cmp[g][3],
                    precision=lax.Precision.HIGHEST) for g in groups]
    o_win = [_weighted_values(*e_win[g], vw) for g in groups]

    first_blk = lax.shift_right_logical(q0, 6)
    bias_rows = []
    for g in groups:
        imp = jnp.where(forced, BIG, imps[g])
        imp = jnp.where(future, -BIG, imp)
        rank = jnp.zeros((nblk, tq), jnp.int32)
        for i in range(nblk):
            row = imp[i:i + 1, :]
            beats = (row > imp) | ((row == imp) & (jidx > i))
            rank = rank + beats.astype(jnp.int32)
        bias_rows.append(jnp.where((rank < N_SEL) & (jidx < first_blk), 0.0, MASK_BIAS))
    pad_rows = LANES - NSA_KV_HEADS * SEL_BLOCKS_MAX
    bias_t = jnp.concatenate(bias_rows + [jnp.zeros((pad_rows, tq), F32)], axis=0)
    bias = bias_t.T.astype(BF16)
    lane_group = lax.shift_right_logical(lax.broadcasted_iota(jnp.int32, (tq, LANES), 1), 5)
    qb = []
    for g in groups:
        own = jnp.where(lane_group == g, bias, jnp.zeros_like(bias))
        qb.append(jnp.concatenate([qs[g], jnp.concatenate([own] * HEADS_PER_KV, axis=0)], axis=1))

    kdiag = ksel_ref[0, pl.ds(pl.multiple_of(q0, tq), tq), :LANES]
    vdiag = vsel_ref[0, pl.ds(pl.multiple_of(q0, tq), tq), :]
    tri = (lax.broadcasted_iota(jnp.int32, (tq, tq), 1) <= lax.broadcasted_iota(jnp.int32, (tq, tq), 0))
    s_diag = [jnp.where(tri[None], _dot_nt(qs[g], kdiag).reshape(HEADS_PER_KV, tq, tq), NEG)
              for g in groups]
    n_prefix = q0 // SEL_PREFIX + 1
    for n in range(1, seq // SEL_PREFIX + 1):
        klen = n * SEL_PREFIX

        @pl.when(n_prefix == n)
        def _(klen=klen):
            vall = jnp.concatenate([vsel_ref[0, :klen, :], vdiag], axis=0)
            s = [jnp.concatenate(
                [_dot_nt(qb[g], ksel_ref[0, :klen, :]).reshape(HEADS_PER_KV, tq, klen), s_diag[g]],
                axis=-1) for g in groups]
            ew = [_softmax2(s[g]) for g in groups]
            for g in groups:
                osel_ref[g] = _weighted_values(*ew[g], vall)

    o_sel = [osel_ref[g] for g in groups]

    outs = []
    for g in range(NSA_KV_HEADS):
        for hh in range(HEADS_PER_KV):
            h = g * HEADS_PER_KV + hh
            acc = jnp.zeros((tq, HEAD_DIM), F32)
            for br, o in enumerate((o_cmp[g], o_sel[g], o_win[g])):
                oh = o[hh * tq:(hh + 1) * tq, g * HEAD_DIM:(g + 1) * HEAD_DIM]
                acc = acc + gates[:, 3 * h + br:3 * h + br + 1] * oh
            outs.append(acc)
    o_ref[0] = jnp.concatenate(outs, axis=1).astype(BF16)


def _nsa(proj3, kc, vc, mt, tq=128):
    B, S, _ = proj3.shape
    blk = lambda c: c // LANES
    seq_spec = lambda c: pl.BlockSpec((1, S, LANES), lambda b, i: (b, 0, blk(c)))
    ncp = kc.shape[1]
    return pl.pallas_call(
        functools.partial(_nsa_body, tq=tq, seq=S),
        grid=(B, S // tq),
        in_specs=[
            pl.BlockSpec((1, tq, NSA_HEADS * HEAD_DIM), lambda b, i: (b, i, 0)),
            pl.BlockSpec((1, ncp, LANES), lambda b, i: (b, 0, 0)),
            pl.BlockSpec((1, ncp, LANES), lambda b, i: (b, 0, 0)),
            pl.BlockSpec((1, S, 2 * LANES), lambda b, i: (b, 0, COL_KSEL // (2 * LANES))),
            seq_spec(COL_VSEL), seq_spec(COL_KWIN), seq_spec(COL_VWIN),
            pl.BlockSpec((1, tq, LANES), lambda b, i: (b, i, blk(COL_GATE))),
            _const_spec(mt.shape),
        ],
        out_specs=pl.BlockSpec((1, tq, NSA_HEADS * HEAD_DIM), lambda b, i: (b, i, 0)),
        out_shape=jax.ShapeDtypeStruct((B, S, NSA_HEADS * HEAD_DIM), BF16),
        scratch_shapes=[pltpu.VMEM((NSA_KV_HEADS, HEADS_PER_KV * tq, LANES), F32)],
        compiler_params=_params(),
        name="nsa",
    )(proj3, kc, vc, proj3, proj3, proj3, proj3, proj3, mt)


DIL_SUB = 128
DIL_ROWS = 512


def _dil_body(q_ref, kp_ref, kc_ref, vp_ref, vc_ref, o_ref, lse_ref, *, n_back):
    sub = DIL_SUB
    rb, tq = q_ref.shape[1], q_ref.shape[2]
    t0 = pl.program_id(2) * tq
    head_of = lax.shift_right_logical(lax.broadcasted_iota(jnp.int32, (sub, DIL_WIDTH), 1), 6)
    lane = lax.broadcasted_iota(jnp.int32, (sub, LANES), 1)
    diff = (sub + lax.broadcasted_iota(jnp.int32, (sub, 1), 0)
            - lax.broadcasted_iota(jnp.int32, (1, 2 * sub), 1))
    band = (diff >= 0) & (diff <= n_back)
    band0 = band & (lax.broadcasted_iota(jnp.int32, (1, 2 * sub), 1) + t0 >= sub)
    tiles = [(r, j) for r in range(rb) for j in range(tq // sub)]
    keys = {r: jnp.concatenate([kp_ref[0, r], kc_ref[0, r]], axis=0) for r in range(rb)}
    vals = {r: jnp.concatenate([vp_ref[0, r], vc_ref[0, r]], axis=0) for r in range(rb)}
    scores = []
    for r, j in tiles:
        q = q_ref[0, r, j * sub:(j + 1) * sub, :]
        qs = jnp.concatenate([jnp.where(head_of == h, q, jnp.zeros_like(q))
                              for h in range(DIL_HEADS)], axis=0)
        s = _dot_nt(qs, keys[r][j * sub:(j + 2) * sub]).reshape(DIL_HEADS, sub, 2 * sub)
        scores.append(jnp.where((band0 if j == 0 else band)[None], s, NEG))
    stats = []
    for s in scores:
        m = jnp.max(s, axis=-1, keepdims=True)
        e = jnp.exp2(s - m)
        stats.append((m, e, jnp.sum(e, axis=-1, keepdims=True)))
    for (r, j), (m, e, l) in zip(tiles, stats):
        o = _dot(e.astype(BF16).reshape(DIL_HEADS * sub, 2 * sub), vals[r][j * sub:(j + 2) * sub])
        o = o.reshape(DIL_HEADS, sub, DIL_WIDTH) / l
        lse = m * LN2 + jnp.log(l)
        o_acc = jnp.zeros((sub, DIL_WIDTH), F32)
        lse_out = jnp.zeros((sub, LANES), F32)
        for h in range(DIL_HEADS):
            o_acc = jnp.where(head_of == h, o[h], o_acc)
            lse_out = jnp.where(lane == h, lse[h], lse_out)
        o_ref[0, r, j * sub:(j + 1) * sub, :] = o_acc
        lse_ref[0, r, j * sub:(j + 1) * sub, :] = lse_out


def _dilated(arr, cols, window, dilation, name):
    B, d, L, _ = arr.shape
    n_back = window // dilation
    tq = min(L, DIL_ROWS)
    rb = DIL_ROWS // tq
    assert d == dilation and n_back <= DIL_SUB and L % tq == 0 and d % rb == 0
    per = tq // DIL_SUB
    qc, kc, vc = cols
    cur = lambda c: pl.BlockSpec((1, rb, tq, DIL_WIDTH), lambda b, r, i: (b, r, i, c))
    prev = lambda c: pl.BlockSpec((1, rb, DIL_SUB, DIL_WIDTH),
                                  lambda b, r, i: (b, r, jnp.maximum(i * per - 1, 0), c))
    return pl.pallas_call(
        functools.partial(_dil_body, n_back=n_back),
        grid=(B, dilation // rb, L // tq),
        in_specs=[cur(qc), prev(kc), cur(kc), prev(vc), cur(vc)],
        out_specs=[pl.BlockSpec((1, rb, tq, DIL_WIDTH), lambda b, r, i: (b, r, i, 0)),
                   pl.BlockSpec((1, rb, tq, LANES), lambda b, r, i: (b, r, i, 0))],
        out_shape=[jax.ShapeDtypeStruct((B, dilation, L, DIL_WIDTH), F32),
                   jax.ShapeDtypeStruct((B, dilation, L, LANES), F32)],
        compiler_params=_params(),
        name=name,
    )(arr, arr, arr, arr, arr)


def _mixout_body(oa_ref, ob0_ref, ob1_ref, ob2_ref, l0_ref, l1_ref, l2_ref, mg_ref, x_ref,
                 pa_ref, pb_ref, wo_ref, out_ref, so_ref, sl_ref):
    tm = x_ref.shape[0]

    def interleaved(src_ref, st_ref):
        d = src_ref.shape[1]
        if d == 1:
            return src_ref[0, 0]
        slabs = src_ref.shape[3] // LANES
        for r in range(d):
            for k in range(slabs):
                st_ref[k, pl.ds(r, tm // d, stride=d), :] = src_ref[0, r, :, k * LANES:(k + 1) * LANES]
        return jnp.concatenate([st_ref[k] for k in range(slabs)], axis=1)

    lses = [interleaved(l, sl_ref) for l in (l0_ref, l1_ref, l2_ref)]
    mx = jnp.maximum(jnp.maximum(lses[0], lses[1]), lses[2])
    ws = [jnp.exp(l - mx) for l in lses]
    den = ws[0] + ws[1] + ws[2]
    ob = jnp.zeros((tm, DIL_WIDTH), F32)
    for w, o_ref in zip(ws, (ob0_ref, ob1_ref, ob2_ref)):
        alpha = w / den
        wide = jnp.concatenate(
            [jnp.broadcast_to(alpha[:, h:h + 1], (tm, HEAD_DIM)) for h in range(DIL_HEADS)], axis=1)
        ob = ob + wide * interleaved(o_ref, so_ref)
    ya = _dot(oa_ref[...], pa_ref[0])
    yb = _dot(ob.astype(BF16), pb_ref[0])
    gm = jax.nn.sigmoid(mg_ref[...].astype(F32))
    y = gm[:, :D_MODEL] * ya + gm[:, D_MODEL:] * yb
    out_ref[...] = x_ref[...] + _dot(y.astype(BF16), wo_ref[0])


def _mixout(oa, obs, lses, proj, x2, pa, pb, wo, layer, S, tm=512):
    T = x2.shape[0]
    per_b = S // tm
    row = lambda w: pl.BlockSpec((tm, w), lambda i: (i, 0))
    folded = lambda a: pl.BlockSpec((1, a.shape[1], tm // a.shape[1], a.shape[3]),
                                    lambda i: (i // per_b, 0, i % per_b, 0))
    return pl.pallas_call(
        _mixout_body,
        grid=(T // tm,),
        in_specs=[row(NSA_HEADS * HEAD_DIM), *[folded(a) for a in obs], *[folded(a) for a in lses],
                  pl.BlockSpec((tm, 2 * D_MODEL), lambda i: (i, COL_MERGE // (2 * D_MODEL))),
                  row(D_MODEL),
                  _layer_spec(pa, layer), _layer_spec(pb, layer), _layer_spec(wo, layer)],
        out_specs=row(D_MODEL),
        out_shape=jax.ShapeDtypeStruct((T, D_MODEL), F32),
        scratch_shapes=[pltpu.VMEM((DIL_WIDTH // LANES, tm, LANES), F32),
                        pltpu.VMEM((1, tm, LANES), F32)],
        compiler_params=_params(),
        name="mixout",
    )(oa, *obs, *lses, proj, x2, pa, pb, wo)


FF_CHUNK = 512


def _ffn_body(x_ref, g_ref, w1_ref, w3_ref, w2_ref, o_ref):
    x = x_ref[...]
    h = _rms(x, g_ref[...]).astype(BF16)
    acc = jnp.zeros(x.shape, F32)
    for c0 in range(0, D_FF, FF_CHUNK):
        c1 = min(c0 + FF_CHUNK, D_FF)
        act = _silu(_dot(h, w1_ref[:, c0:c1])) * _dot(h, w3_ref[:, c0:c1])
        acc = acc + _dot(act.astype(BF16), w2_ref[c0:c1, :])
    o_ref[...] = x + acc


def _ffn(x2, g, w1, w3, w2, tm=512):
    T = x2.shape[0]
    return pl.pallas_call(
        _ffn_body,
        grid=(T // tm,),
        in_specs=[pl.BlockSpec((tm, D_MODEL), lambda i: (i, 0)), _const_spec(g.shape),
                  _const_spec(w1.shape), _const_spec(w3.shape), _const_spec(w2.shape)],
        out_specs=pl.BlockSpec((tm, D_MODEL), lambda i: (i, 0)),
        out_shape=jax.ShapeDtypeStruct((T, D_MODEL), F32),
        compiler_params=_params(),
        name="ffn",
    )(x2, g, w1, w3, w2)


MOE_TM = 1024
MOE_CHUNK = 128
MOE_MERGED = 3


def _router_body(x_ref, g_ref, r_ref, h_ref, combw_ref, posw_ref, post_ref, cnt_ref):
    tm = x_ref.shape[0]
    h = _rms(x_ref[...], g_ref[...])
    h_ref[...] = h.astype(BF16)
    logits = _dot(h, r_ref[...], precision=lax.Precision.HIGHEST)
    lane = lax.broadcasted_iota(jnp.int32, logits.shape, 1)
    lg = jnp.where(lane < N_EXPERTS, logits, NEG)
    m1 = jnp.max(lg, axis=-1, keepdims=True)
    i1 = jnp.min(jnp.where(lg == m1, lane, LANES), axis=-1, keepdims=True)
    lg2 = jnp.where(lane == i1, NEG, lg)
    m2 = jnp.max(lg2, axis=-1, keepdims=True)
    i2 = jnp.min(jnp.where(lg2 == m2, lane, LANES), axis=-1, keepdims=True)
    e2 = jnp.exp(m2 - m1)
    den = 1.0 + e2
    w_first, w_second = 1.0 / den, e2 / den
    chosen = [jnp.broadcast_to((i1 == ex) | (i2 == ex), (tm, LANES)) for ex in range(N_EXPERTS)]
    ones = jnp.concatenate([jnp.where(c, 1.0, 0.0).astype(BF16) for c in chosen], axis=1)
    row = lax.broadcasted_iota(jnp.int32, (tm, tm), 0)
    col = lax.broadcasted_iota(jnp.int32, (tm, tm), 1)
    tri = jnp.where(col < row, 1.0, 0.0).astype(BF16)
    before = _dot(tri, ones)
    by_lane = jnp.zeros((tm, LANES), F32)
    for ex in range(N_EXPERTS):
        slab = slice(ex * LANES, (ex + 1) * LANES)
        combw_ref[:, slab] = jnp.broadcast_to(
            jnp.where(i1 == ex, w_first, 0.0) + jnp.where(i2 == ex, w_second, 0.0), (tm, LANES))
        pos = jnp.where(chosen[ex], before[:, slab], -1.0)
        posw_ref[:, slab] = pos
        by_lane = jnp.where(lane == ex, pos, by_lane)
    post_ref[0] = by_lane.T[:N_EXPERTS]
    cnt = jnp.sum(jnp.where((lane == i1) | (lane == i2), 1.0, 0.0), axis=0, keepdims=True)
    cnt_ref[0] = jnp.broadcast_to(cnt, (8, LANES)).astype(jnp.int32)


def _router(x2, g, r, tm=MOE_TM):
    T = x2.shape[0]
    nt = T // tm
    row = lambda w: pl.BlockSpec((tm, w), lambda i: (i, 0))
    return pl.pallas_call(
        _router_body,
        grid=(nt,),
        in_specs=[row(D_MODEL), _const_spec(g.shape), _const_spec(r.shape)],
        out_specs=[row(D_MODEL), row(N_EXPERTS * LANES), row(N_EXPERTS * LANES),
                   pl.BlockSpec((1, N_EXPERTS, tm), lambda i: (i, 0, 0)),
                   pl.BlockSpec((1, 8, LANES), lambda i: (i, 0, 0))],
        out_shape=[jax.ShapeDtypeStruct((T, D_MODEL), BF16),
                   jax.ShapeDtypeStruct((T, N_EXPERTS * LANES), F32),
                   jax.ShapeDtypeStruct((T, N_EXPERTS * LANES), F32),
                   jax.ShapeDtypeStruct((nt, N_EXPERTS, tm), F32),
                   jax.ShapeDtypeStruct((nt, 8, LANES), jnp.int32)],
        compiler_params=_params(),
        name="router",
    )(x2, g, r)


def _moe_body(cnt_ref, x_ref, h_ref, comb_ref, pos_ref, post_ref, w1_ref, w3_ref, w2_ref, fn_ref,
              o_ref, xe_ref, y_ref):
    i = pl.program_id(0)
    e = pl.program_id(1)
    f = pl.program_id(2)
    last_f = pl.num_programs(2) - 1
    tm = h_ref.shape[0]
    n_chunks = (cnt_ref[i * N_EXPERTS + e] + (MOE_CHUNK - 1)) // MOE_CHUNK
    post_e = post_ref[0, pl.ds(e, 1), :]

    @pl.when((e == 0) & (f == 0))
    def _():
        o_ref[...] = x_ref[...]

    @pl.when((i == 0) & (e == 0) & (f == 0))
    def _():
        y_ref[...] = jnp.zeros_like(y_ref)

    def scatter_add(first_row, first_slot, n_slabs):
        lane_slot = lax.broadcasted_iota(jnp.int32, (1, LANES), 1).astype(F32)
        hot = [jnp.where(pos_ref[...] == first_slot + (k * LANES) + lane_slot, 1.0, 0.0).astype(BF16)
               for k in range(n_slabs)]
        scatter = hot[0] if n_slabs == 1 else jnp.concatenate(hot, axis=1)
        weight = jnp.concatenate([comb_ref[...]] * (D_MODEL // LANES), axis=1)
        ys = y_ref[pl.ds(first_row, n_slabs * LANES), :].astype(BF16)
        o_ref[...] += weight * _dot(scatter, ys)

    def chunk(c, carry):
        r0 = pl.multiple_of(c * MOE_CHUNK, MOE_CHUNK)
        rows = pl.ds(r0, MOE_CHUNK)
        base = r0.astype(F32)

        @pl.when(f == 0)
        def _():
            slot_col = base + lax.broadcasted_iota(jnp.int32, (MOE_CHUNK, 1), 0).astype(F32)
            gather = jnp.where(post_e == slot_col, 1.0, 0.0).astype(BF16)
            xe_ref[rows, :] = _dot(gather, h_ref[...]).astype(BF16)

        xe = xe_ref[rows, :]
        act = _silu(_dot(xe, w1_ref[0])) * _dot(xe, w3_ref[0])
        y = _dot(act.astype(BF16), w2_ref[0])

        @pl.when(f == 0)
        def _():
            y_ref[rows, :] = y

        @pl.when(f > 0)
        def _():
            y_ref[rows, :] += y

        @pl.when((f == last_f) & (c >= MOE_MERGED))
        def _():
            scatter_add(r0, base, 1)

        return carry

    lax.fori_loop(0, n_chunks, chunk, 0)

    @pl.when((f == last_f) & (n_chunks > 0))
    def _():
        scatter_add(0, 0.0, MOE_MERGED)

    @pl.when((e == pl.num_programs(1) - 1) & (f == last_f))
    def _():
        o_ref[...] = _rms(o_ref[...], fn_ref[...])


def _moe(x2, h, comb, pos, post, counts, w1, w3, w2, fn, tm=MOE_TM, nf=2):
    T = x2.shape[0]
    fc = D_FF // nf
    once = lambda w: pl.BlockSpec((tm, w), lambda i, e, f, cnt: (i, 0),
                                  pipeline_mode=pl.Buffered(1))
    tile = lambda w: pl.BlockSpec((tm, w), lambda i, e, f, cnt: (i, 0))
    assert MOE_CHUNK == LANES
    slab = pl.BlockSpec((tm, LANES), lambda i, e, f, cnt: (i, e))
    grid_spec = pltpu.PrefetchScalarGridSpec(
        num_scalar_prefetch=1,
        grid=(T // tm, N_EXPERTS, nf),
        in_specs=[once(D_MODEL), once(D_MODEL), slab, slab,
                  pl.BlockSpec((1, N_EXPERTS, tm), lambda i, e, f, cnt: (i, 0, 0)),
                  pl.BlockSpec((1, D_MODEL, fc), lambda i, e, f, cnt: (e, 0, f)),
                  pl.BlockSpec((1, D_MODEL, fc), lambda i, e, f, cnt: (e, 0, f)),
                  pl.BlockSpec((1, fc, D_MODEL), lambda i, e, f, cnt: (e, f, 0)),
                  pl.BlockSpec(fn.shape, lambda i, e, f, cnt: (0, 0))],
        out_specs=tile(D_MODEL),
        scratch_shapes=[pltpu.VMEM((tm, D_MODEL), BF16), pltpu.VMEM((tm, D_MODEL), F32)],
    )
    return pl.pallas_call(
        _moe_body,
        grid_spec=grid_spec,
        out_shape=jax.ShapeDtypeStruct((T, D_MODEL), F32),
        compiler_params=_params(),
        name="moe",
    )(counts, x2, h, comb, pos, post, w1, w3, w2, fn)


def _rope_lane_tables(positions):
    half = ROT_DIM // 2
    inv = ROPE_THETA ** (-jnp.arange(0, ROT_DIM, 2, dtype=F32) / ROT_DIM)
    ang = positions.astype(F32).reshape(-1, 1) * inv
    cos, sin = jnp.cos(ang), jnp.sin(ang)
    lane = np.arange(LANES)
    within = lane % HEAD_DIM
    pick = lane % half
    cos_l, sin_l = cos[:, pick], sin[:, pick]
    c = jnp.where(within < ROT_DIM, cos_l, 1.0)
    sa = jnp.where(within < half, -sin_l, 0.0)
    sb = jnp.where((within >= half) & (within < ROT_DIM), sin_l, 0.0)
    return c, sa, sb


def _permute_w_in(w):
    scale = HEAD_DIM ** -0.5 * LOG2E
    qa, kv, gate, qkv_b, merge = 0, 512, 1280, 1304, 3608
    nb = N_DIL_GROUPS * DIL_WIDTH
    kv_piece = lambda j: w[..., kv + 128 * j:kv + 128 * (j + 1)]
    dil = lambda part, g0, g1: w[..., qkv_b + part * nb + g0 * DIL_WIDTH:
                                 qkv_b + part * nb + g1 * DIL_WIDTH]
    pieces = [
        w[..., qa:qa + 512] * scale,
        dil(0, 0, 1) * scale, dil(1, 0, 1),
        kv_piece(2), kv_piece(4), kv_piece(0),
        dil(2, 0, 1),
        kv_piece(3), kv_piece(5),
        w[..., gate:gate + 3 * NSA_HEADS],
        jnp.zeros(w.shape[:-1] + (LANES - 3 * NSA_HEADS,), w.dtype),
        kv_piece(1),
        w[..., merge:merge + 2 * D_MODEL],
        dil(0, 1, 3) * scale, dil(1, 1, 3), dil(2, 1, 3),
    ]
    out = jnp.concatenate(pieces, axis=-1)
    assert out.shape[-1] == N_W_IN
    return out.astype(BF16)


def _importance_matrix_t(seq, ncp):
    n_c = (seq - CMP_LEN) // CMP_STRIDE + 1
    starts = np.arange(n_c) * CMP_STRIDE
    bstart = np.arange(seq // SEL_LEN) * SEL_LEN
    overlap = np.clip(np.minimum(starts[:, None] + CMP_LEN, bstart[None, :] + SEL_LEN)
                      - np.maximum(starts[:, None], bstart[None, :]), 0, None)
    m = np.zeros((ncp, seq // SEL_LEN), np.float32)
    m[:n_c] = overlap.astype(np.float32) / CMP_LEN
    return jnp.asarray(m.T)


def _mixer(x2, layer, B, S, tables, mt, prm):
    proj, fold1, fold2, a = _inproj(x2, prm["norm_mix"], prm["w_in"], layer, *tables, B, S)
    proj3 = proj.reshape(B, S, N_PROJ)
    ncp = S // CMP_STRIDE
    cmp = _compress(a.reshape(2, B * NSA_KV_HEADS * ncp, CMP_STRIDE * HEAD_DIM),
                    prm["cmp_w1"], prm["cmp_w2"], prm["cmp_pos"], layer, B)
    oa = _nsa(proj3, cmp[0], cmp[1], mt).reshape(B * S, NSA_HEADS * HEAD_DIM)
    base_cols = tuple(c // DIL_WIDTH for c in (COL_QB0, COL_KB0, COL_VB0))
    sources = ((proj3.reshape(B, 1, S, N_PROJ), base_cols), (fold1, (0, 1, 2)), (fold2, (0, 1, 2)))
    obs, lses = [], []
    for g, ((w, d), (arr, cols)) in enumerate(zip(DIL_PATTERNS, sources)):
        o, lse = _dilated(arr, cols, w, d, f"dilated{g}")
        obs.append(o)
        lses.append(lse)
    return _mixout(oa, obs, lses, proj, x2, prm["p_a"], prm["p_b"], prm["w_o"], layer, S)


def kernel(x, positions, norm_mix, w_in, cmp_pos_k, cmp_pos_v, cmp_k_w1, cmp_k_w2, cmp_v_w1,
           cmp_v_w2, w_branch_a, w_branch_b, w_out, norm_ffn, ffn_w1, ffn_w3, ffn_w2, router,
           moe_w1, moe_w3, moe_w2, final_norm):
    B, S, D = x.shape
    depth = norm_mix.shape[0]
    assert depth == 2 and D == D_MODEL
    tables = _rope_lane_tables(positions)
    mt = _importance_matrix_t(S, S // CMP_STRIDE)
    cmp_pos = jnp.stack([cmp_pos_k, cmp_pos_v], axis=1).reshape(depth, 2, 1, CMP_LEN * HEAD_DIM)
    prm = {
        "norm_mix": norm_mix.reshape(depth, 1, D),
        "w_in": _permute_w_in(w_in),
        "cmp_w1": jnp.stack([cmp_k_w1, cmp_v_w1], axis=1).astype(BF16),
        "cmp_w2": jnp.stack([cmp_k_w2, cmp_v_w2], axis=1).astype(BF16),
        "cmp_pos": jnp.broadcast_to(cmp_pos, (depth, 2, 8, CMP_LEN * HEAD_DIM)).astype(BF16),
        "p_a": w_branch_a.astype(BF16), "p_b": w_branch_b.astype(BF16), "w_o": w_out.astype(BF16),
    }
    x2 = x.reshape(B * S, D)
    x2 = _mixer(x2, 0, B, S, tables, mt, prm)
    x2 = _ffn(x2, norm_ffn[0].reshape(1, -1), ffn_w1[0].astype(BF16), ffn_w3[0].astype(BF16),
              ffn_w2[0].astype(BF16))
    x2 = _mixer(x2, 1, B, S, tables, mt, prm)
    g1 = norm_ffn[1].reshape(1, -1)
    r = jnp.pad(router[0], ((0, 0), (0, LANES - N_EXPERTS)))
    h, comb, pos, post, cnt = _router(x2, g1, r)
    counts = cnt[:, 0, :N_EXPERTS].reshape(-1)
    out = _moe(x2, h, comb, pos, post, counts, moe_w1[0].astype(BF16), moe_w3[0].astype(BF16),
               moe_w2[0].astype(BF16), final_norm.reshape(1, -1))
    return out.reshape(B, S, D)
```

```python
import functools

import numpy as np
import jax
import jax.numpy as jnp
from jax import lax
from jax.experimental import pallas as pl
from jax.experimental.pallas import tpu as pltpu

F32 = jnp.float32
BF16 = jnp.bfloat16

D_MODEL = 1024
HEAD_DIM = 64
ROT_DIM = HEAD_DIM // 4
ROPE_THETA = 500000.0
EPS = 1e-6
NSA_HEADS = 8
NSA_KV_HEADS = 2
HEADS_PER_KV = NSA_HEADS // NSA_KV_HEADS
CMP_LEN = 32
CMP_STRIDE = 16
CMP_HIDDEN = 128
SEL_LEN = 64
N_SEL = 16
WIN = 512
DIL_PATTERNS = ((128, 1), (512, 4), (2048, 16))
N_DIL_GROUPS = 3
DIL_HEADS = 4
D_FF = 2816
N_EXPERTS = 8

LANES = 128
VMEM_LIMIT = 56 * 1024 * 1024
NEG = -1e30
BIG = 1e30

COL_MERGE = 0
COL_QA = 2048
COL_KSEL = 2560
COL_BLK = COL_KSEL + 128
COL_KWIN = 2816
COL_VSEL = 2944
COL_VWIN = 3072
COL_GATE = 3200
N_PROJ = 3328
DIL_WIDTH = DIL_HEADS * HEAD_DIM
N_FOLD = N_DIL_GROUPS * DIL_WIDTH
STAGE = None
IN_CHUNKS = (
    (512, True, tuple(COL_QA + 128 * j for j in range(4)), None),
    (384, True, (COL_KSEL, COL_KWIN, STAGE), ("cmp", 0)),
    (512, False, (COL_VSEL, COL_VWIN, COL_GATE, STAGE), ("cmp", 1)),
    (512, False, tuple(COL_MERGE + 128 * j for j in range(0, 4)), None),
    (512, False, tuple(COL_MERGE + 128 * j for j in range(4, 8)), None),
    (512, False, tuple(COL_MERGE + 128 * j for j in range(8, 12)), None),
    (512, False, tuple(COL_MERGE + 128 * j for j in range(12, 16)), None),
    (N_FOLD, True, (STAGE,) * 6, ("fold", 0)),
    (N_FOLD, True, (STAGE,) * 6, ("fold", 1)),
    (N_FOLD, False, (STAGE,) * 6, ("fold", 2)),
)
N_W_IN = sum(c[0] for c in IN_CHUNKS)
LOG2E = 1.4426950408889634
LN2 = 0.6931471805599453
MASK_BIAS = -(2.0 ** 100)
SEL_BLOCKS_MAX = 32


def _dot(a, b, precision=None):
    return jnp.dot(a, b, preferred_element_type=F32, precision=precision)


def _dot_nt(a, b, precision=None):
    return lax.dot_general(a, b, (((1,), (1,)), ((), ())), preferred_element_type=F32,
                           precision=precision)


def _rms(x, g):
    ms = jnp.mean(x * x, axis=-1, keepdims=True)
    return x * lax.rsqrt(ms + EPS) * g


def _silu(x):
    return x * jax.nn.sigmoid(x)


def _params(**kw):
    return pltpu.CompilerParams(vmem_limit_bytes=VMEM_LIMIT, **kw)


def _const_spec(shape):
    nd = len(shape)
    return pl.BlockSpec(shape, lambda *_: (0,) * nd)


def _inproj_body(x_ref, g_ref, w_ref, c_ref, sa_ref, sb_ref, o_ref, f0_ref, f1_ref, f2_ref, a_ref,
                 st_ref, *, per_b):
    tm = x_ref.shape[0]
    t_seq = (pl.program_id(0) % per_b) * tm + lax.broadcasted_iota(jnp.int32, (tm, LANES), 0)
    lane = lax.broadcasted_iota(jnp.int32, (tm, LANES), 1)
    blk = lax.shift_right_logical(t_seq, 6)
    hot = (lane == blk) | (lane == blk + SEL_BLOCKS_MAX)
    o_ref[:, COL_BLK:COL_BLK + LANES] = jnp.where(hot, 1.0, 0.0).astype(BF16)
    h = _rms(x_ref[...], g_ref[0]).astype(BF16)
    c = c_ref[...]
    sa = sa_ref[...]
    sb = sb_ref[...]
    start = 0
    for size, rope, dests, action in IN_CHUNKS:
        acc = _dot(h, w_ref[0, :, start:start + size])
        start += size
        for j, dest in enumerate(dests):
            a = acc[:, j * LANES:(j + 1) * LANES]
            if rope:
                a = a * c + pltpu.roll(a, LANES - 8, 1) * sa + pltpu.roll(a, 8, 1) * sb
            if dest is STAGE:
                st_ref[j] = a
            else:
                o_ref[:, dest:dest + LANES] = a.astype(BF16)
        if action is None:
            continue
        kind, piece = action
        if kind == "fold":
            slabs = DIL_WIDTH // LANES
            for gi, f_ref in enumerate((f0_ref, f1_ref, f2_ref)):
                d = DIL_PATTERNS[gi][1]
                for r in range(d):
                    for k in range(slabs):
                        rows = st_ref[gi * slabs + k, pl.ds(r, tm // d, stride=d), :]
                        c0 = piece * DIL_WIDTH + k * LANES
                        f_ref[0, r, :, c0:c0 + LANES] = rows.astype(BF16)
        else:
            slab = dests.index(STAGE)
            nrow = tm // CMP_STRIDE
            toks = [st_ref[slab, pl.ds(j, nrow, stride=CMP_STRIDE), :] for j in range(CMP_STRIDE)]
            for g in range(NSA_KV_HEADS):
                head = slice(g * HEAD_DIM, (g + 1) * HEAD_DIM)
                for m in range(CMP_STRIDE // 2):
                    pair = jnp.concatenate([toks[2 * m][:, head], toks[2 * m + 1][:, head]], axis=1)
                    a_ref[piece, 0, g, :, m * LANES:(m + 1) * LANES] = pair.astype(BF16)


def _layer_spec(arr, layer):
    nd = arr.ndim
    return pl.BlockSpec((1,) + arr.shape[1:], lambda *_: (layer,) + (0,) * (nd - 1))


def _inproj(x2, g, w, layer, rc, rsa, rsb, B, S, tm=512):
    T = x2.shape[0]
    per_b = S // tm
    dils = [d for _, d in DIL_PATTERNS]
    fold_spec = lambda d: pl.BlockSpec((1, d, tm // d, 3 * DIL_WIDTH),
                                       lambda i: (i // per_b, 0, i % per_b, 0))
    fold_shape = lambda d: jax.ShapeDtypeStruct((B, d, S // d, 3 * DIL_WIDTH), BF16)
    cmp_w = CMP_STRIDE * HEAD_DIM
    assert S // SEL_LEN <= SEL_BLOCKS_MAX
    return pl.pallas_call(
        functools.partial(_inproj_body, per_b=per_b),
        grid=(T // tm,),
        in_specs=[
            pl.BlockSpec((tm, D_MODEL), lambda i: (i, 0)),
            _layer_spec(g, layer), _layer_spec(w, layer),
            pl.BlockSpec((tm, LANES), lambda i: (i, 0)),
            pl.BlockSpec((tm, LANES), lambda i: (i, 0)),
            pl.BlockSpec((tm, LANES), lambda i: (i, 0)),
        ],
        out_specs=[pl.BlockSpec((tm, N_PROJ), lambda i: (i, 0)), *[fold_spec(d) for d in dils],
                   pl.BlockSpec((2, 1, NSA_KV_HEADS, tm // CMP_STRIDE, cmp_w),
                                lambda i: (0, i // per_b, 0, i % per_b, 0))],
        out_shape=[jax.ShapeDtypeStruct((T, N_PROJ), BF16), *[fold_shape(d) for d in dils],
                   jax.ShapeDtypeStruct((2, B, NSA_KV_HEADS, S // CMP_STRIDE, cmp_w), BF16)],
        scratch_shapes=[pltpu.VMEM((N_FOLD // LANES, tm, LANES), F32)],
        compiler_params=_params(),
        name="inproj",
    )(x2, g, w, rc, rsa, rsb)


def _compress_body(a_ref, w1_ref, w2_ref, pos_ref, o_ref):
    nb, ncp = o_ref.shape[1], o_ref.shape[2]
    a = a_ref[0]
    w1 = w1_ref[0, 0]
    half = CMP_STRIDE * HEAD_DIM
    top = _dot(a, w1[:half])
    bot = _dot(a, w1[half:])
    pc = _dot(pos_ref[0, 0], w1)
    rows = a.shape[0]
    hid = top + pltpu.roll(bot, rows - 1, 0) + pc[0:1]
    out = _dot(_silu(hid).astype(BF16), w2_ref[0, 0])
    for b in range(nb):
        heads = [out[(b * NSA_KV_HEADS + g) * ncp:(b * NSA_KV_HEADS + g + 1) * ncp]
                 for g in range(NSA_KV_HEADS)]
        o_ref[0, b] = jnp.concatenate(heads, axis=1).astype(BF16)


def _compress(a, w1, w2, pos, layer, B):
    n, rows, _ = a.shape
    ncp = rows // (B * NSA_KV_HEADS)
    per_kv = lambda arr: pl.BlockSpec((1, 1) + arr.shape[2:], lambda i: (layer, i, 0, 0))
    return pl.pallas_call(
        _compress_body,
        grid=(n,),
        in_specs=[pl.BlockSpec((1, rows, CMP_STRIDE * HEAD_DIM), lambda i: (i, 0, 0)),
                  per_kv(w1), per_kv(w2), per_kv(pos)],
        out_specs=pl.BlockSpec((1, B, ncp, NSA_KV_HEADS * HEAD_DIM), lambda i: (i, 0, 0, 0)),
        out_shape=jax.ShapeDtypeStruct((n, B, ncp, NSA_KV_HEADS * HEAD_DIM), BF16),
        compiler_params=_params(),
        name="compress",
    )(a, w1, w2, pos)


def _softmax2(s):
    m = jnp.max(s, axis=-1, keepdims=True)
    e = jnp.exp2(s - m)
    return e.astype(BF16), jnp.sum(e, axis=-1, keepdims=True)


def _weighted_values(e, l, v):
    nh, tq, nk = e.shape
    return _dot(e.reshape(nh * tq, nk), v) / l.reshape(nh * tq, 1)


SEL_PREFIX = 512


def _nsa_body(q_ref, kc_ref, vc_ref, ksel_ref, vsel_ref, kwin_ref, vwin_ref, gate_ref, mt_ref,
              o_ref, osel_ref, *, tq, seq):
    nblk = seq // SEL_LEN
    ncp = kc_ref.shape[1]
    n_cmp = (seq - CMP_LEN) // CMP_STRIDE + 1
    q0 = pl.program_id(1) * tq
    q = q_ref[0]
    t_col = q0 + lax.broadcasted_iota(jnp.int32, (tq, 1), 0)
    t_row = q0 + lax.broadcasted_iota(jnp.int32, (1, tq), 1)
    gates = jax.nn.sigmoid(gate_ref[0].astype(F32))

    zeros64 = jnp.zeros((tq, HEAD_DIM), BF16)

    def stacked_q(g):
        parts = []
        for hh in range(HEADS_PER_KV):
            h = g * HEADS_PER_KV + hh
            qh = q[:, h * HEAD_DIM:(h + 1) * HEAD_DIM]
            parts.append(jnp.concatenate([qh, zeros64] if g == 0 else [zeros64, qh], axis=1))
        return jnp.concatenate(parts, axis=0)

    qs = [stacked_q(g) for g in range(NSA_KV_HEADS)]

    cidx = lax.broadcasted_iota(jnp.int32, (tq, ncp), 1)
    cmask = ((cidx * CMP_STRIDE + (CMP_LEN - 1)) <= t_col) & (cidx < n_cmp)
    jidx = lax.broadcasted_iota(jnp.int32, (nblk, tq), 0)
    cur = lax.shift_right_logical(t_row, 6)
    forced = (jidx == 0) | (jidx == cur) | (jidx == cur - 1)
    future = jidx > cur
    groups = range(NSA_KV_HEADS)
    span = WIN + tq
    ks = pl.multiple_of(jnp.maximum(q0 - WIN, 0), tq)
    kw = kwin_ref[0, pl.ds(ks, span), :]
    vw = vwin_ref[0, pl.ds(ks, span), :]
    wpos = ks + lax.broadcasted_iota(jnp.int32, (1, span), 1)
    wmask = (wpos <= t_col) & (t_col - wpos <= WIN - 1)

    s_cmp = [jnp.where(cmask[None], _dot_nt(qs[g], kc_ref[0]).reshape(HEADS_PER_KV, tq, ncp), NEG)
             for g in groups]
    s_win = [jnp.where(wmask[None], _dot_nt(qs[g], kw).reshape(HEADS_PER_KV, tq, span), NEG)
             for g in groups]
    p_cmp = []
    for g in groups:
        m = jnp.max(s_cmp[g], axis=-1, keepdims=True)
        e = jnp.where(cmask[None], jnp.exp2(s_cmp[g] - m), 0.0)
        den = jnp.sum(e, axis=-1, keepdims=True)
        p_cmp.append(e / jnp.where(den > 0, den, 1.0))
    e_win = [_softmax2(s_win[g]) for g in groups]
    o_cmp = [_dot(p_cmp[g].astype(BF16).reshape(HEADS_PER_KV * tq, ncp), vc_ref[0]) for g in groups]
    imps = [_dot_nt(mt_ref[...], p_cmp[g][0] + p_cmp[g][1] + p_cmp[g][2] + p_cmp[g][3],
                    precision=lax.Precision.HIGHEST) for g in groups]
    o_win = [_weighted_values(*e_win[g], vw) for g in groups]

    first_blk = lax.shift_right_logical(q0, 6)
    picked_rows, before_rows = [], []
    for g in groups:
        imp = jnp.where(forced, BIG, imps[g])
        imp = jnp.where(future, -BIG, imp)
        rank = jnp.zeros((nblk, tq), jnp.int32)
        for i in range(nblk):
            row = imp[i:i + 1, :]
            beats = (row > imp) | ((row == imp) & (jidx > i))
            rank = rank + beats.astype(jnp.int32)
        picked_rows.append(jnp.where(rank < N_SEL, 0.0, MASK_BIAS))
        before_rows.append(jnp.where((rank < N_SEL) & (jidx < first_blk), 0.0, MASK_BIAS))
    assert 2 * NSA_KV_HEADS * SEL_BLOCKS_MAX == LANES
    bias_t = jnp.concatenate(before_rows + picked_rows, axis=0)
    bias_main = bias_t.T
    bias_diag = pltpu.roll(bias_main, LANES // 2, 1)
    lane_group = lax.shift_right_logical(lax.broadcasted_iota(jnp.int32, (tq, LANES), 1), 5)

    def with_bias(g, bias):
        own = jnp.where(lane_group == g, bias, 0.0).astype(BF16)
        return jnp.concatenate([qs[g], jnp.concatenate([own] * HEADS_PER_KV, axis=0)], axis=1)

    qb = [with_bias(g, bias_main) for g in groups]

    kdiag = ksel_ref[0, pl.ds(pl.multiple_of(q0, tq), tq), :]
    vdiag = vsel_ref[0, pl.ds(pl.multiple_of(q0, tq), tq), :]
    tri = (lax.broadcasted_iota(jnp.int32, (tq, tq), 1) <= lax.broadcasted_iota(jnp.int32, (tq, tq), 0))
    s_diag = [jnp.where(tri[None], _dot_nt(with_bias(g, bias_diag), kdiag)
                        .reshape(HEADS_PER_KV, tq, tq), NEG) for g in groups]
    n_prefix = q0 // SEL_PREFIX + 1
    for n in range(1, seq // SEL_PREFIX + 1):
        klen = n * SEL_PREFIX

        @pl.when(n_prefix == n)
        def _(klen=klen):
            vall = jnp.concatenate([vsel_ref[0, :klen, :], vdiag], axis=0)
            s = [jnp.concatenate(
                [_dot_nt(qb[g], ksel_ref[0, :klen, :]).reshape(HEADS_PER_KV, tq, klen), s_diag[g]],
                axis=-1) for g in groups]
            ew = [_softmax2(s[g]) for g in groups]
            for g in groups:
                osel_ref[g] = _weighted_values(*ew[g], vall)

    o_sel = [osel_ref[g] for g in groups]

    outs = []
    for g in range(NSA_KV_HEADS):
        for hh in range(HEADS_PER_KV):
            h = g * HEADS_PER_KV + hh
            acc = jnp.zeros((tq, HEAD_DIM), F32)
            for br, o in enumerate((o_cmp[g], o_sel[g], o_win[g])):
                oh = o[hh * tq:(hh + 1) * tq, g * HEAD_DIM:(g + 1) * HEAD_DIM]
                acc = acc + gates[:, 3 * h + br:3 * h + br + 1] * oh
            outs.append(acc)
    o_ref[0] = jnp.concatenate(outs, axis=1).astype(BF16)


def _nsa(proj3, kc, vc, mt, tq=128):
    B, S, _ = proj3.shape
    blk = lambda c: c // LANES
    seq_spec = lambda c: pl.BlockSpec((1, S, LANES), lambda b, i: (b, 0, blk(c)))
    ncp = kc.shape[1]
    return pl.pallas_call(
        functools.partial(_nsa_body, tq=tq, seq=S),
        grid=(B, S // tq),
        in_specs=[
            pl.BlockSpec((1, tq, NSA_HEADS * HEAD_DIM),
                         lambda b, i: (b, i, COL_QA // (NSA_HEADS * HEAD_DIM))),
            pl.BlockSpec((1, ncp, LANES), lambda b, i: (b, 0, 0)),
            pl.BlockSpec((1, ncp, LANES), lambda b, i: (b, 0, 0)),
            pl.BlockSpec((1, S, 2 * LANES), lambda b, i: (b, 0, COL_KSEL // (2 * LANES))),
            seq_spec(COL_VSEL), seq_spec(COL_KWIN), seq_spec(COL_VWIN),
            pl.BlockSpec((1, tq, LANES), lambda b, i: (b, i, blk(COL_GATE))),
            _const_spec(mt.shape),
        ],
        out_specs=pl.BlockSpec((1, tq, NSA_HEADS * HEAD_DIM), lambda b, i: (b, i, 0)),
        out_shape=jax.ShapeDtypeStruct((B, S, NSA_HEADS * HEAD_DIM), BF16),
        scratch_shapes=[pltpu.VMEM((NSA_KV_HEADS, HEADS_PER_KV * tq, LANES), F32)],
        compiler_params=_params(),
        name="nsa",
    )(proj3, kc, vc, proj3, proj3, proj3, proj3, proj3, mt)


DIL_SUB = 128
DIL_ROWS = 512


def _dil_body(q_ref, kp_ref, kc_ref, vp_ref, vc_ref, o_ref, lse_ref, *, n_back):
    sub = DIL_SUB
    rb, tq = q_ref.shape[1], q_ref.shape[2]
    t0 = pl.program_id(2) * tq
    head_of = lax.shift_right_logical(lax.broadcasted_iota(jnp.int32, (sub, DIL_WIDTH), 1), 6)
    lane = lax.broadcasted_iota(jnp.int32, (sub, LANES), 1)
    diff = (sub + lax.broadcasted_iota(jnp.int32, (sub, 1), 0)
            - lax.broadcasted_iota(jnp.int32, (1, 2 * sub), 1))
    band = (diff >= 0) & (diff <= n_back)
    band0 = band & (lax.broadcasted_iota(jnp.int32, (1, 2 * sub), 1) + t0 >= sub)
    tiles = [(r, j) for r in range(rb) for j in range(tq // sub)]
    keys = {r: jnp.concatenate([kp_ref[0, r], kc_ref[0, r]], axis=0) for r in range(rb)}
    vals = {r: jnp.concatenate([vp_ref[0, r], vc_ref[0, r]], axis=0) for r in range(rb)}
    scores = []
    for r, j in tiles:
        q = q_ref[0, r, j * sub:(j + 1) * sub, :]
        qs = jnp.concatenate([jnp.where(head_of == h, q, jnp.zeros_like(q))
                              for h in range(DIL_HEADS)], axis=0)
        s = _dot_nt(qs, keys[r][j * sub:(j + 2) * sub]).reshape(DIL_HEADS, sub, 2 * sub)
        scores.append(jnp.where((band0 if j == 0 else band)[None], s, NEG))
    stats = []
    for s in scores:
        m = jnp.max(s, axis=-1, keepdims=True)
        e = jnp.exp2(s - m)
        stats.append((m, e, jnp.sum(e, axis=-1, keepdims=True)))
    for (r, j), (m, e, l) in zip(tiles, stats):
        o = _dot(e.astype(BF16).reshape(DIL_HEADS * sub, 2 * sub), vals[r][j * sub:(j + 2) * sub])
        o = o.reshape(DIL_HEADS, sub, DIL_WIDTH) / l
        lse = m * LN2 + jnp.log(l)
        o_acc = jnp.zeros((sub, DIL_WIDTH), F32)
        lse_out = jnp.zeros((sub, LANES), F32)
        for h in range(DIL_HEADS):
            o_acc = jnp.where(head_of == h, o[h], o_acc)
            lse_out = jnp.where(lane == h, lse[h], lse_out)
        o_ref[0, r, j * sub:(j + 1) * sub, :] = o_acc
        lse_ref[0, r, j * sub:(j + 1) * sub, :] = lse_out


def _dilated(arr, cols, window, dilation, name):
    B, d, L, _ = arr.shape
    n_back = window // dilation
    tq = min(L, DIL_ROWS)
    rb = DIL_ROWS // tq
    assert d == dilation and n_back <= DIL_SUB and L % tq == 0 and d % rb == 0
    per = tq // DIL_SUB
    qc, kc, vc = cols
    cur = lambda c: pl.BlockSpec((1, rb, tq, DIL_WIDTH), lambda b, r, i: (b, r, i, c))
    prev = lambda c: pl.BlockSpec((1, rb, DIL_SUB, DIL_WIDTH),
                                  lambda b, r, i: (b, r, jnp.maximum(i * per - 1, 0), c))
    return pl.pallas_call(
        functools.partial(_dil_body, n_back=n_back),
        grid=(B, dilation // rb, L // tq),
        in_specs=[cur(qc), prev(kc), cur(kc), prev(vc), cur(vc)],
        out_specs=[pl.BlockSpec((1, rb, tq, DIL_WIDTH), lambda b, r, i: (b, r, i, 0)),
                   pl.BlockSpec((1, rb, tq, LANES), lambda b, r, i: (b, r, i, 0))],
        out_shape=[jax.ShapeDtypeStruct((B, dilation, L, DIL_WIDTH), F32),
                   jax.ShapeDtypeStruct((B, dilation, L, LANES), F32)],
        compiler_params=_params(),
        name=name,
    )(arr, arr, arr, arr, arr)


def _mixout_body(oa_ref, ob0_ref, ob1_ref, ob2_ref, l0_ref, l1_ref, l2_ref, mg_ref, x_ref,
                 pa_ref, pb_ref, wo_ref, out_ref, so_ref, sl_ref):
    tm = x_ref.shape[0]

    def interleaved(src_ref, st_ref):
        d = src_ref.shape[1]
        if d == 1:
            return src_ref[0, 0]
        slabs = src_ref.shape[3] // LANES
        for r in range(d):
            for k in range(slabs):
                st_ref[k, pl.ds(r, tm // d, stride=d), :] = src_ref[0, r, :, k * LANES:(k + 1) * LANES]
        return jnp.concatenate([st_ref[k] for k in range(slabs)], axis=1)

    lses = [interleaved(l, sl_ref) for l in (l0_ref, l1_ref, l2_ref)]
    mx = jnp.maximum(jnp.maximum(lses[0], lses[1]), lses[2])
    ws = [jnp.exp(l - mx) for l in lses]
    den = ws[0] + ws[1] + ws[2]
    ob = jnp.zeros((tm, DIL_WIDTH), F32)
    for w, o_ref in zip(ws, (ob0_ref, ob1_ref, ob2_ref)):
        alpha = w / den
        wide = jnp.concatenate(
            [jnp.broadcast_to(alpha[:, h:h + 1], (tm, HEAD_DIM)) for h in range(DIL_HEADS)], axis=1)
        ob = ob + wide * interleaved(o_ref, so_ref)
    ya = _dot(oa_ref[...], pa_ref[0])
    yb = _dot(ob.astype(BF16), pb_ref[0])
    gm = jax.nn.sigmoid(mg_ref[...].astype(F32))
    y = gm[:, :D_MODEL] * ya + gm[:, D_MODEL:] * yb
    out_ref[...] = x_ref[...] + _dot(y.astype(BF16), wo_ref[0])


def _mixout(oa, obs, lses, proj, x2, pa, pb, wo, layer, S, tm=512):
    T = x2.shape[0]
    per_b = S // tm
    row = lambda w: pl.BlockSpec((tm, w), lambda i: (i, 0))
    folded = lambda a: pl.BlockSpec((1, a.shape[1], tm // a.shape[1], a.shape[3]),
                                    lambda i: (i // per_b, 0, i % per_b, 0))
    return pl.pallas_call(
        _mixout_body,
        grid=(T // tm,),
        in_specs=[row(NSA_HEADS * HEAD_DIM), *[folded(a) for a in obs], *[folded(a) for a in lses],
                  pl.BlockSpec((tm, 2 * D_MODEL), lambda i: (i, COL_MERGE // (2 * D_MODEL))),
                  row(D_MODEL),
                  _layer_spec(pa, layer), _layer_spec(pb, layer), _layer_spec(wo, layer)],
        out_specs=row(D_MODEL),
        out_shape=jax.ShapeDtypeStruct((T, D_MODEL), F32),
        scratch_shapes=[pltpu.VMEM((DIL_WIDTH // LANES, tm, LANES), F32),
                        pltpu.VMEM((1, tm, LANES), F32)],
        compiler_params=_params(),
        name="mixout",
    )(oa, *obs, *lses, proj, x2, pa, pb, wo)


FF_CHUNK = 512


def _ffn_body(x_ref, g_ref, w1_ref, w3_ref, w2_ref, o_ref):
    x = x_ref[...]
    h = _rms(x, g_ref[...]).astype(BF16)
    acc = jnp.zeros(x.shape, F32)
    for c0 in range(0, D_FF, FF_CHUNK):
        c1 = min(c0 + FF_CHUNK, D_FF)
        act = _silu(_dot(h, w1_ref[:, c0:c1])) * _dot(h, w3_ref[:, c0:c1])
        acc = acc + _dot(act.astype(BF16), w2_ref[c0:c1, :])
    o_ref[...] = x + acc


def _ffn(x2, g, w1, w3, w2, tm=512):
    T = x2.shape[0]
    return pl.pallas_call(
        _ffn_body,
        grid=(T // tm,),
        in_specs=[pl.BlockSpec((tm, D_MODEL), lambda i: (i, 0)), _const_spec(g.shape),
                  _const_spec(w1.shape), _const_spec(w3.shape), _const_spec(w2.shape)],
        out_specs=pl.BlockSpec((tm, D_MODEL), lambda i: (i, 0)),
        out_shape=jax.ShapeDtypeStruct((T, D_MODEL), F32),
        compiler_params=_params(),
        name="ffn",
    )(x2, g, w1, w3, w2)


MOE_TM = 1024
MOE_CHUNK = 128
MOE_BIG = 2 * MOE_CHUNK
MOE_MERGED = 4


def _router_body(x_ref, g_ref, r_ref, h_ref, combw_ref, posw_ref, post_ref, cnt_ref):
    tm = x_ref.shape[0]
    h = _rms(x_ref[...], g_ref[...])
    h_ref[...] = h.astype(BF16)
    r = r_ref[...]
    h_hi, r_hi = h.astype(BF16), r.astype(BF16)
    h_lo = (h - h_hi.astype(F32)).astype(BF16)
    r_lo = (r - r_hi.astype(F32)).astype(BF16)
    logits = _dot(h_hi, r_hi) + (_dot(h_hi, r_lo) + _dot(h_lo, r_hi))
    lane = lax.broadcasted_iota(jnp.int32, logits.shape, 1)
    lg = jnp.where(lane < N_EXPERTS, logits, NEG)
    m1 = jnp.max(lg, axis=-1, keepdims=True)
    i1 = jnp.min(jnp.where(lg == m1, lane, LANES), axis=-1, keepdims=True)
    lg2 = jnp.where(lane == i1, NEG, lg)
    m2 = jnp.max(lg2, axis=-1, keepdims=True)
    i2 = jnp.min(jnp.where(lg2 == m2, lane, LANES), axis=-1, keepdims=True)
    e2 = jnp.exp(m2 - m1)
    den = 1.0 + e2
    w_first, w_second = 1.0 / den, e2 / den
    chosen = [jnp.broadcast_to((i1 == ex) | (i2 == ex), (tm, LANES)) for ex in range(N_EXPERTS)]
    ones = jnp.concatenate([jnp.where(c, 1.0, 0.0).astype(BF16) for c in chosen], axis=1)
    row = lax.broadcasted_iota(jnp.int32, (LANES, LANES), 0)
    col = lax.broadcasted_iota(jnp.int32, (LANES, LANES), 1)
    tri = jnp.where(col < row, 1.0, 0.0).astype(BF16)
    running = jnp.zeros((1, N_EXPERTS * LANES), F32)
    parts = []
    for b in range(tm // LANES):
        blk = ones[b * LANES:(b + 1) * LANES]
        parts.append(_dot(tri, blk) + running)
        running = running + jnp.sum(blk.astype(F32), axis=0, keepdims=True)
    before = jnp.concatenate(parts, axis=0)
    by_lane = jnp.zeros((tm, LANES), F32)
    cnt = jnp.zeros((1, LANES), F32)
    for ex in range(N_EXPERTS):
        slab = slice(ex * LANES, (ex + 1) * LANES)
        combw_ref[:, slab] = jnp.broadcast_to(
            jnp.where(i1 == ex, w_first, 0.0) + jnp.where(i2 == ex, w_second, 0.0), (tm, LANES))
        pos = jnp.where(chosen[ex], before[:, slab], -1.0)
        posw_ref[:, slab] = pos
        by_lane = jnp.where(lane == ex, pos, by_lane)
        cnt = jnp.where(lane[:1] == ex, running[:, slab], cnt)
    post_ref[0] = by_lane.T[:N_EXPERTS]
    cnt_ref[0] = jnp.broadcast_to(cnt, (8, LANES)).astype(jnp.int32)


def _router(x2, g, r, tm=MOE_TM):
    T = x2.shape[0]
    nt = T // tm
    row = lambda w: pl.BlockSpec((tm, w), lambda i: (i, 0))
    return pl.pallas_call(
        _router_body,
        grid=(nt,),
        in_specs=[row(D_MODEL), _const_spec(g.shape), _const_spec(r.shape)],
        out_specs=[row(D_MODEL), row(N_EXPERTS * LANES), row(N_EXPERTS * LANES),
                   pl.BlockSpec((1, N_EXPERTS, tm), lambda i: (i, 0, 0)),
                   pl.BlockSpec((1, 8, LANES), lambda i: (i, 0, 0))],
        out_shape=[jax.ShapeDtypeStruct((T, D_MODEL), BF16),
                   jax.ShapeDtypeStruct((T, N_EXPERTS * LANES), F32),
                   jax.ShapeDtypeStruct((T, N_EXPERTS * LANES), F32),
                   jax.ShapeDtypeStruct((nt, N_EXPERTS, tm), F32),
                   jax.ShapeDtypeStruct((nt, 8, LANES), jnp.int32)],
        compiler_params=_params(),
        name="router",
    )(x2, g, r)


def _moe_body(cnt_ref, x_ref, h_ref, comb_ref, pos_ref, post_ref, w1_ref, w3_ref, w2_ref, fn_ref,
              o_ref, xe_ref, y_ref):
    i = pl.program_id(0)
    e = pl.program_id(1)
    f = pl.program_id(2)
    last_f = pl.num_programs(2) - 1
    tm = h_ref.shape[0]
    n = cnt_ref[i * N_EXPERTS + e]
    n_big = (n + (MOE_BIG - MOE_CHUNK - 1)) // MOE_BIG
    rest_row = pl.multiple_of(n_big * MOE_BIG, MOE_BIG)
    has_rest = n > rest_row
    post_e = post_ref[0, pl.ds(e, 1), :]

    @pl.when((e == 0) & (f == 0))
    def _():
        o_ref[...] = x_ref[...]

    @pl.when((i == 0) & (e == 0) & (f == 0))
    def _():
        y_ref[...] = jnp.zeros_like(y_ref)

    def scatter_add(first_row, first_slot, n_slabs):
        lane_slot = lax.broadcasted_iota(jnp.int32, (1, LANES), 1).astype(F32)
        hot = [jnp.where(pos_ref[...] == first_slot + (k * LANES) + lane_slot, 1.0, 0.0).astype(BF16)
               for k in range(n_slabs)]
        scatter = hot[0] if n_slabs == 1 else jnp.concatenate(hot, axis=1)
        weight = jnp.concatenate([comb_ref[...]] * (D_MODEL // LANES), axis=1)
        ys = y_ref[pl.ds(first_row, n_slabs * LANES), :].astype(BF16)
        o_ref[...] += weight * _dot(scatter, ys)

    def chunk(r0, size):
        rows = pl.ds(r0, size)
        base = r0.astype(F32)

        @pl.when(f == 0)
        def _():
            slot_col = base + lax.broadcasted_iota(jnp.int32, (size, 1), 0).astype(F32)
            gather = jnp.where(post_e == slot_col, 1.0, 0.0).astype(BF16)
            xe_ref[rows, :] = _dot(gather, h_ref[...]).astype(BF16)

        xe = xe_ref[rows, :]
        act = _silu(_dot(xe, w1_ref[0])) * _dot(xe, w3_ref[0])
        y = _dot(act.astype(BF16), w2_ref[0])

        @pl.when(f == 0)
        def _():
            y_ref[rows, :] = y

        @pl.when(f > 0)
        def _():
            y_ref[rows, :] += y

        @pl.when((f == last_f) & (r0 >= MOE_MERGED * LANES))
        def _():
            for k in range(size // LANES):
                scatter_add(r0 + k * LANES, base + float(k * LANES), 1)

    def big_chunk(c, carry):
        chunk(pl.multiple_of(c * MOE_BIG, MOE_BIG), MOE_BIG)
        return carry

    lax.fori_loop(0, n_big, big_chunk, 0)

    @pl.when(has_rest)
    def _():
        chunk(rest_row, MOE_CHUNK)

    @pl.when((f == last_f) & (n > 0))
    def _():
        scatter_add(0, 0.0, MOE_MERGED)

    @pl.when((e == pl.num_programs(1) - 1) & (f == last_f))
    def _():
        o_ref[...] = _rms(o_ref[...], fn_ref[...])


def _moe(x2, h, comb, pos, post, counts, w1, w3, w2, fn, tm=MOE_TM, nf=2):
    T = x2.shape[0]
    fc = D_FF // nf
    once = lambda w: pl.BlockSpec((tm, w), lambda i, e, f, cnt: (i, 0),
                                  pipeline_mode=pl.Buffered(1))
    tile = lambda w: pl.BlockSpec((tm, w), lambda i, e, f, cnt: (i, 0))
    assert MOE_CHUNK == LANES
    slab = pl.BlockSpec((tm, LANES), lambda i, e, f, cnt: (i, e))
    grid_spec = pltpu.PrefetchScalarGridSpec(
        num_scalar_prefetch=1,
        grid=(T // tm, N_EXPERTS, nf),
        in_specs=[once(D_MODEL), once(D_MODEL), slab, slab,
                  pl.BlockSpec((1, N_EXPERTS, tm), lambda i, e, f, cnt: (i, 0, 0)),
                  pl.BlockSpec((1, D_MODEL, fc), lambda i, e, f, cnt: (e, 0, f)),
                  pl.BlockSpec((1, D_MODEL, fc), lambda i, e, f, cnt: (e, 0, f)),
                  pl.BlockSpec((1, fc, D_MODEL), lambda i, e, f, cnt: (e, f, 0)),
                  pl.BlockSpec(fn.shape, lambda i, e, f, cnt: (0, 0))],
        out_specs=tile(D_MODEL),
        scratch_shapes=[pltpu.VMEM((tm, D_MODEL), BF16), pltpu.VMEM((tm, D_MODEL), F32)],
    )
    return pl.pallas_call(
        _moe_body,
        grid_spec=grid_spec,
        out_shape=jax.ShapeDtypeStruct((T, D_MODEL), F32),
        compiler_params=_params(),
        name="moe",
    )(counts, x2, h, comb, pos, post, w1, w3, w2, fn)


def _rope_lane_tables(positions):
    half = ROT_DIM // 2
    inv = ROPE_THETA ** (-jnp.arange(0, ROT_DIM, 2, dtype=F32) / ROT_DIM)
    ang = positions.astype(F32).reshape(-1, 1) * inv
    cos, sin = jnp.cos(ang), jnp.sin(ang)
    lane = np.arange(LANES)
    within = lane % HEAD_DIM
    pick = lane % half
    cos_l, sin_l = cos[:, pick], sin[:, pick]
    c = jnp.where(within < ROT_DIM, cos_l, 1.0)
    sa = jnp.where(within < half, -sin_l, 0.0)
    sb = jnp.where((within >= half) & (within < ROT_DIM), sin_l, 0.0)
    return c, sa, sb


def _permute_w_in(w):
    scale = HEAD_DIM ** -0.5 * LOG2E
    qa, kv, gate, qkv_b, merge = 0, 512, 1280, 1304, 3608
    nb = N_DIL_GROUPS * DIL_WIDTH
    kv_piece = lambda j: w[..., kv + 128 * j:kv + 128 * (j + 1)]
    dil = lambda part, g0, g1: w[..., qkv_b + part * nb + g0 * DIL_WIDTH:
                                 qkv_b + part * nb + g1 * DIL_WIDTH]
    pieces = [
        w[..., qa:qa + 512] * scale,
        kv_piece(2), kv_piece(4), kv_piece(0),
        kv_piece(3), kv_piece(5),
        w[..., gate:gate + 3 * NSA_HEADS],
        jnp.zeros(w.shape[:-1] + (LANES - 3 * NSA_HEADS,), w.dtype),
        kv_piece(1),
        w[..., merge:merge + 2 * D_MODEL],
        dil(0, 0, 3) * scale, dil(1, 0, 3), dil(2, 0, 3),
    ]
    out = jnp.concatenate(pieces, axis=-1)
    assert out.shape[-1] == N_W_IN
    return out.astype(BF16)


def _importance_matrix_t(seq, ncp):
    n_c = (seq - CMP_LEN) // CMP_STRIDE + 1
    starts = np.arange(n_c) * CMP_STRIDE
    bstart = np.arange(seq // SEL_LEN) * SEL_LEN
    overlap = np.clip(np.minimum(starts[:, None] + CMP_LEN, bstart[None, :] + SEL_LEN)
                      - np.maximum(starts[:, None], bstart[None, :]), 0, None)
    m = np.zeros((ncp, seq // SEL_LEN), np.float32)
    m[:n_c] = overlap.astype(np.float32) / CMP_LEN
    return jnp.asarray(m.T)


def _mixer(x2, layer, B, S, tables, mt, prm):
    proj, *folds, a = _inproj(x2, prm["norm_mix"], prm["w_in"], layer, *tables, B, S)
    proj3 = proj.reshape(B, S, N_PROJ)
    ncp = S // CMP_STRIDE
    cmp = _compress(a.reshape(2, B * NSA_KV_HEADS * ncp, CMP_STRIDE * HEAD_DIM),
                    prm["cmp_w1"], prm["cmp_w2"], prm["cmp_pos"], layer, B)
    oa = _nsa(proj3, cmp[0], cmp[1], mt).reshape(B * S, NSA_HEADS * HEAD_DIM)
    obs, lses = [], []
    for g, ((w, d), arr) in enumerate(zip(DIL_PATTERNS, folds)):
        o, lse = _dilated(arr, (0, 1, 2), w, d, f"dilated{g}")
        obs.append(o)
        lses.append(lse)
    return _mixout(oa, obs, lses, proj, x2, prm["p_a"], prm["p_b"], prm["w_o"], layer, S)


def kernel(x, positions, norm_mix, w_in, cmp_pos_k, cmp_pos_v, cmp_k_w1, cmp_k_w2, cmp_v_w1,
           cmp_v_w2, w_branch_a, w_branch_b, w_out, norm_ffn, ffn_w1, ffn_w3, ffn_w2, router,
           moe_w1, moe_w3, moe_w2, final_norm):
    B, S, D = x.shape
    depth = norm_mix.shape[0]
    assert depth == 2 and D == D_MODEL
    tables = _rope_lane_tables(positions)
    mt = _importance_matrix_t(S, S // CMP_STRIDE)
    cmp_pos = jnp.stack([cmp_pos_k, cmp_pos_v], axis=1).reshape(depth, 2, 1, CMP_LEN * HEAD_DIM)
    prm = {
        "norm_mix": norm_mix.reshape(depth, 1, D),
        "w_in": _permute_w_in(w_in),
        "cmp_w1": jnp.stack([cmp_k_w1, cmp_v_w1], axis=1).astype(BF16),
        "cmp_w2": jnp.stack([cmp_k_w2, cmp_v_w2], axis=1).astype(BF16),
        "cmp_pos": jnp.broadcast_to(cmp_pos, (depth, 2, 8, CMP_LEN * HEAD_DIM)).astype(BF16),
        "p_a": w_branch_a.astype(BF16), "p_b": w_branch_b.astype(BF16), "w_o": w_out.astype(BF16),
    }
    x2 = x.reshape(B * S, D)
    x2 = _mixer(x2, 0, B, S, tables, mt, prm)
    x2 = _ffn(x2, norm_ffn[0].reshape(1, -1), ffn_w1[0].astype(BF16), ffn_w3[0].astype(BF16),
              ffn_w2[0].astype(BF16))
    x2 = _mixer(x2, 1, B, S, tables, mt, prm)
    g1 = norm_ffn[1].reshape(1, -1)
    r = jnp.pad(router[0], ((0, 0), (0, LANES - N_EXPERTS)))
    h, comb, pos, post, cnt = _router(x2, g1, r)
    counts = cnt[:, 0, :N_EXPERTS].reshape(-1)
    out = _moe(x2, h, comb, pos, post, counts, moe_w1[0].astype(BF16), moe_w3[0].astype(BF16),
               moe_w2[0].astype(BF16), final_norm.reshape(1, -1))
    return out.reshape(B, S, D)
```

```python
import functools

import numpy as np
import jax
import jax.numpy as jnp
from jax import lax
from jax.experimental import pallas as pl
from jax.experimental.pallas import tpu as pltpu

F32 = jnp.float32
BF16 = jnp.bfloat16

D_MODEL = 1024
HEAD_DIM = 64
ROT_DIM = HEAD_DIM // 4
ROPE_THETA = 500000.0
EPS = 1e-6
NSA_HEADS = 8
NSA_KV_HEADS = 2
HEADS_PER_KV = NSA_HEADS // NSA_KV_HEADS
CMP_LEN = 32
CMP_STRIDE = 16
CMP_HIDDEN = 128
SEL_LEN = 64
N_SEL = 16
WIN = 512
DIL_PATTERNS = ((128, 1), (512, 4), (2048, 16))
N_DIL_GROUPS = 3
DIL_HEADS = 4
D_FF = 2816
N_EXPERTS = 8

LANES = 128
VMEM_LIMIT = 56 * 1024 * 1024
NEG = -1e30
BIG = 1e30

COL_MERGE = 0
COL_QA = 2048
COL_KSEL = 2560
COL_BLK = COL_KSEL + 128
COL_KWIN = 2816
COL_VSEL = 2944
COL_VWIN = 3072
COL_GATE = 3200
N_PROJ = 3328
DIL_WIDTH = DIL_HEADS * HEAD_DIM
N_FOLD = N_DIL_GROUPS * DIL_WIDTH
STAGE = None
IN_CHUNKS = (
    (512, True, tuple(COL_QA + 128 * j for j in range(4)), None),
    (384, True, (COL_KSEL, COL_KWIN, STAGE), ("cmp", 0)),
    (512, False, (COL_VSEL, COL_VWIN, COL_GATE, STAGE), ("cmp", 1)),
    (512, False, tuple(COL_MERGE + 128 * j for j in range(0, 4)), None),
    (512, False, tuple(COL_MERGE + 128 * j for j in range(4, 8)), None),
    (512, False, tuple(COL_MERGE + 128 * j for j in range(8, 12)), None),
    (512, False, tuple(COL_MERGE + 128 * j for j in range(12, 16)), None),
    (N_FOLD, True, (STAGE,) * 6, ("fold", 0)),
    (N_FOLD, True, (STAGE,) * 6, ("fold", 1)),
    (N_FOLD, False, (STAGE,) * 6, ("fold", 2)),
)
N_W_IN = sum(c[0] for c in IN_CHUNKS)
LOG2E = 1.4426950408889634
LN2 = 0.6931471805599453
MASK_BIAS = -(2.0 ** 100)
SEL_BLOCKS_MAX = 32


def _dot(a, b, precision=None):
    return jnp.dot(a, b, preferred_element_type=F32, precision=precision)


def _dot_nt(a, b, precision=None):
    return lax.dot_general(a, b, (((1,), (1,)), ((), ())), preferred_element_type=F32,
                           precision=precision)


def _rms(x, g):
    ms = jnp.mean(x * x, axis=-1, keepdims=True)
    return x * lax.rsqrt(ms + EPS) * g


def _silu(x):
    return x * jax.nn.sigmoid(x)


def _params(**kw):
    return pltpu.CompilerParams(vmem_limit_bytes=VMEM_LIMIT, **kw)


def _const_spec(shape):
    nd = len(shape)
    return pl.BlockSpec(shape, lambda *_: (0,) * nd)


def _inproj_body(x_ref, g_ref, w_ref, c_ref, sa_ref, sb_ref, o_ref, f0_ref, f1_ref, f2_ref, a_ref,
                 st_ref, *, per_b):
    tm = x_ref.shape[0]
    t_seq = (pl.program_id(0) % per_b) * tm + lax.broadcasted_iota(jnp.int32, (tm, LANES), 0)
    lane = lax.broadcasted_iota(jnp.int32, (tm, LANES), 1)
    blk = lax.shift_right_logical(t_seq, 6)
    hot = (lane == blk) | (lane == blk + SEL_BLOCKS_MAX)
    o_ref[:, COL_BLK:COL_BLK + LANES] = jnp.where(hot, 1.0, 0.0).astype(BF16)
    h = _rms(x_ref[...], g_ref[0]).astype(BF16)
    c = c_ref[...]
    sa = sa_ref[...]
    sb = sb_ref[...]
    start = 0
    for size, rope, dests, action in IN_CHUNKS:
        acc = _dot(h, w_ref[0, :, start:start + size])
        start += size
        for j, dest in enumerate(dests):
            a = acc[:, j * LANES:(j + 1) * LANES]
            if rope:
                a = a * c + pltpu.roll(a, LANES - 8, 1) * sa + pltpu.roll(a, 8, 1) * sb
            if dest is STAGE:
                st_ref[j] = a
            else:
                o_ref[:, dest:dest + LANES] = a.astype(BF16)
        if action is None:
            continue
        kind, piece = action
        if kind == "fold":
            slabs = DIL_WIDTH // LANES
            for gi, f_ref in enumerate((f0_ref, f1_ref, f2_ref)):
                d = DIL_PATTERNS[gi][1]
                for r in range(d):
                    for k in range(slabs):
                        rows = st_ref[gi * slabs + k, pl.ds(r, tm // d, stride=d), :]
                        c0 = piece * DIL_WIDTH + k * LANES
                        f_ref[0, r, :, c0:c0 + LANES] = rows.astype(BF16)
        else:
            slab = dests.index(STAGE)
            nrow = tm // CMP_STRIDE
            toks = [st_ref[slab, pl.ds(j, nrow, stride=CMP_STRIDE), :] for j in range(CMP_STRIDE)]
            for g in range(NSA_KV_HEADS):
                head = slice(g * HEAD_DIM, (g + 1) * HEAD_DIM)
                for m in range(CMP_STRIDE // 2):
                    pair = jnp.concatenate([toks[2 * m][:, head], toks[2 * m + 1][:, head]], axis=1)
                    a_ref[piece, 0, g, :, m * LANES:(m + 1) * LANES] = pair.astype(BF16)


def _layer_spec(arr, layer):
    nd = arr.ndim
    return pl.BlockSpec((1,) + arr.shape[1:], lambda *_: (layer,) + (0,) * (nd - 1))


def _inproj(x2, g, w, layer, rc, rsa, rsb, B, S, tm=512):
    T = x2.shape[0]
    per_b = S // tm
    dils = [d for _, d in DIL_PATTERNS]
    fold_spec = lambda d: pl.BlockSpec((1, d, tm // d, 3 * DIL_WIDTH),
                                       lambda i: (i // per_b, 0, i % per_b, 0))
    fold_shape = lambda d: jax.ShapeDtypeStruct((B, d, S // d, 3 * DIL_WIDTH), BF16)
    cmp_w = CMP_STRIDE * HEAD_DIM
    assert S // SEL_LEN <= SEL_BLOCKS_MAX
    return pl.pallas_call(
        functools.partial(_inproj_body, per_b=per_b),
        grid=(T // tm,),
        in_specs=[
            pl.BlockSpec((tm, D_MODEL), lambda i: (i, 0)),
            _layer_spec(g, layer), _layer_spec(w, layer),
            pl.BlockSpec((tm, LANES), lambda i: (i, 0)),
            pl.BlockSpec((tm, LANES), lambda i: (i, 0)),
            pl.BlockSpec((tm, LANES), lambda i: (i, 0)),
        ],
        out_specs=[pl.BlockSpec((tm, N_PROJ), lambda i: (i, 0)), *[fold_spec(d) for d in dils],
                   pl.BlockSpec((2, 1, NSA_KV_HEADS, tm // CMP_STRIDE, cmp_w),
                                lambda i: (0, i // per_b, 0, i % per_b, 0))],
        out_shape=[jax.ShapeDtypeStruct((T, N_PROJ), BF16), *[fold_shape(d) for d in dils],
                   jax.ShapeDtypeStruct((2, B, NSA_KV_HEADS, S // CMP_STRIDE, cmp_w), BF16)],
        scratch_shapes=[pltpu.VMEM((N_FOLD // LANES, tm, LANES), F32)],
        compiler_params=_params(),
        name="inproj",
    )(x2, g, w, rc, rsa, rsb)


def _compress_body(a_ref, w1_ref, w2_ref, pos_ref, o_ref):
    nb, ncp = o_ref.shape[1], o_ref.shape[2]
    a = a_ref[0]
    w1 = w1_ref[0, 0]
    half = CMP_STRIDE * HEAD_DIM
    top = _dot(a, w1[:half])
    bot = _dot(a, w1[half:])
    pc = _dot(pos_ref[0, 0], w1)
    rows = a.shape[0]
    hid = top + pltpu.roll(bot, rows - 1, 0) + pc[0:1]
    out = _dot(_silu(hid).astype(BF16), w2_ref[0, 0])
    for b in range(nb):
        heads = [out[(b * NSA_KV_HEADS + g) * ncp:(b * NSA_KV_HEADS + g + 1) * ncp]
                 for g in range(NSA_KV_HEADS)]
        o_ref[0, b] = jnp.concatenate(heads, axis=1).astype(BF16)


def _compress(a, w1, w2, pos, layer, B):
    n, rows, _ = a.shape
    ncp = rows // (B * NSA_KV_HEADS)
    per_kv = lambda arr: pl.BlockSpec((1, 1) + arr.shape[2:], lambda i: (layer, i, 0, 0))
    return pl.pallas_call(
        _compress_body,
        grid=(n,),
        in_specs=[pl.BlockSpec((1, rows, CMP_STRIDE * HEAD_DIM), lambda i: (i, 0, 0)),
                  per_kv(w1), per_kv(w2), per_kv(pos)],
        out_specs=pl.BlockSpec((1, B, ncp, NSA_KV_HEADS * HEAD_DIM), lambda i: (i, 0, 0, 0)),
        out_shape=jax.ShapeDtypeStruct((n, B, ncp, NSA_KV_HEADS * HEAD_DIM), BF16),
        compiler_params=_params(),
        name="compress",
    )(a, w1, w2, pos)


def _softmax2(s):
    m = jnp.max(s, axis=-1, keepdims=True)
    e = jnp.exp2(s - m)
    return e.astype(BF16), jnp.sum(e, axis=-1, keepdims=True)


def _weighted_values(e, l, v):
    nh, tq, nk = e.shape
    return _dot(e.reshape(nh * tq, nk), v) / l.reshape(nh * tq, 1)


SEL_PREFIX = 512


def _nsa_body(q_ref, kc_ref, vc_ref, ksel_ref, vsel_ref, kwin_ref, vwin_ref, gate_ref, mt_ref,
              o_ref, osel_ref, *, tq, seq):
    nblk = seq // SEL_LEN
    ncp = kc_ref.shape[1]
    n_cmp = (seq - CMP_LEN) // CMP_STRIDE + 1
    q0 = pl.program_id(1) * tq
    q = q_ref[0]
    t_col = q0 + lax.broadcasted_iota(jnp.int32, (tq, 1), 0)
    t_row = q0 + lax.broadcasted_iota(jnp.int32, (1, tq), 1)
    gates = jax.nn.sigmoid(gate_ref[0].astype(F32))

    zeros64 = jnp.zeros((tq, HEAD_DIM), BF16)

    def stacked_q(g):
        parts = []
        for hh in range(HEADS_PER_KV):
            h = g * HEADS_PER_KV + hh
            qh = q[:, h * HEAD_DIM:(h + 1) * HEAD_DIM]
            parts.append(jnp.concatenate([qh, zeros64] if g == 0 else [zeros64, qh], axis=1))
        return jnp.concatenate(parts, axis=0)

    qs = [stacked_q(g) for g in range(NSA_KV_HEADS)]

    cidx = lax.broadcasted_iota(jnp.int32, (tq, ncp), 1)
    cmask = ((cidx * CMP_STRIDE + (CMP_LEN - 1)) <= t_col) & (cidx < n_cmp)
    jidx = lax.broadcasted_iota(jnp.int32, (nblk, tq), 0)
    cur = lax.shift_right_logical(t_row, 6)
    forced = (jidx == 0) | (jidx == cur) | (jidx == cur - 1)
    future = jidx > cur
    groups = range(NSA_KV_HEADS)
    span = WIN + tq
    ks = pl.multiple_of(jnp.maximum(q0 - WIN, 0), tq)
    kw = kwin_ref[0, pl.ds(ks, span), :]
    vw = vwin_ref[0, pl.ds(ks, span), :]
    wpos = ks + lax.broadcasted_iota(jnp.int32, (1, span), 1)
    wmask = (wpos <= t_col) & (t_col - wpos <= WIN - 1)

    s_cmp = [jnp.where(cmask[None], _dot_nt(qs[g], kc_ref[0]).reshape(HEADS_PER_KV, tq, ncp), NEG)
             for g in groups]
    s_win = [jnp.where(wmask[None], _dot_nt(qs[g], kw).reshape(HEADS_PER_KV, tq, span), NEG)
             for g in groups]
    p_cmp = []
    for g in groups:
        m = jnp.max(s_cmp[g], axis=-1, keepdims=True)
        e = jnp.where(cmask[None], jnp.exp2(s_cmp[g] - m), 0.0)
        den = jnp.sum(e, axis=-1, keepdims=True)
        p_cmp.append(e / jnp.where(den > 0, den, 1.0))
    e_win = [_softmax2(s_win[g]) for g in groups]
    o_cmp = [_dot(p_cmp[g].astype(BF16).reshape(HEADS_PER_KV * tq, ncp), vc_ref[0]) for g in groups]
    imps = [_dot_nt(mt_ref[...], p_cmp[g][0] + p_cmp[g][1] + p_cmp[g][2] + p_cmp[g][3],
                    precision=lax.Precision.HIGHEST) for g in groups]
    o_win = [_weighted_values(*e_win[g], vw) for g in groups]

    first_blk = lax.shift_right_logical(q0, 6)
    picked_rows, before_rows = [], []
    for g in groups:
        imp = jnp.where(forced, BIG, imps[g])
        imp = jnp.where(future, -BIG, imp)
        rank = jnp.zeros((nblk, tq), jnp.int32)
        for i in range(nblk):
            row = imp[i:i + 1, :]
            beats = (row > imp) | ((row == imp) & (jidx > i))
            rank = rank + beats.astype(jnp.int32)
        picked_rows.append(jnp.where(rank < N_SEL, 0.0, MASK_BIAS))
        before_rows.append(jnp.where((rank < N_SEL) & (jidx < first_blk), 0.0, MASK_BIAS))
    assert 2 * NSA_KV_HEADS * SEL_BLOCKS_MAX == LANES
    bias_t = jnp.concatenate(before_rows + picked_rows, axis=0)
    bias_main = bias_t.T
    bias_diag = pltpu.roll(bias_main, LANES // 2, 1)
    lane_group = lax.shift_right_logical(lax.broadcasted_iota(jnp.int32, (tq, LANES), 1), 5)

    def with_bias(g, bias):
        own = jnp.where(lane_group == g, bias, 0.0).astype(BF16)
        return jnp.concatenate([qs[g], jnp.concatenate([own] * HEADS_PER_KV, axis=0)], axis=1)

    qb = [with_bias(g, bias_main) for g in groups]

    kdiag = ksel_ref[0, pl.ds(pl.multiple_of(q0, tq), tq), :]
    vdiag = vsel_ref[0, pl.ds(pl.multiple_of(q0, tq), tq), :]
    tri = (lax.broadcasted_iota(jnp.int32, (tq, tq), 1) <= lax.broadcasted_iota(jnp.int32, (tq, tq), 0))
    s_diag = [jnp.where(tri[None], _dot_nt(with_bias(g, bias_diag), kdiag)
                        .reshape(HEADS_PER_KV, tq, tq), NEG) for g in groups]
    n_prefix = q0 // SEL_PREFIX + 1
    for n in range(1, seq // SEL_PREFIX + 1):
        klen = n * SEL_PREFIX

        @pl.when(n_prefix == n)
        def _(klen=klen):
            vall = jnp.concatenate([vsel_ref[0, :klen, :], vdiag], axis=0)
            s = [jnp.concatenate(
                [_dot_nt(qb[g], ksel_ref[0, :klen, :]).reshape(HEADS_PER_KV, tq, klen), s_diag[g]],
                axis=-1) for g in groups]
            ew = [_softmax2(s[g]) for g in groups]
            for g in groups:
                osel_ref[g] = _weighted_values(*ew[g], vall)

    o_sel = [osel_ref[g] for g in groups]

    outs = []
    for g in range(NSA_KV_HEADS):
        for hh in range(HEADS_PER_KV):
            h = g * HEADS_PER_KV + hh
            acc = jnp.zeros((tq, HEAD_DIM), F32)
            for br, o in enumerate((o_cmp[g], o_sel[g], o_win[g])):
                oh = o[hh * tq:(hh + 1) * tq, g * HEAD_DIM:(g + 1) * HEAD_DIM]
                acc = acc + gates[:, 3 * h + br:3 * h + br + 1] * oh
            outs.append(acc)
    o_ref[0] = jnp.concatenate(outs, axis=1).astype(BF16)


def _nsa(proj3, kc, vc, mt, tq=128):
    B, S, _ = proj3.shape
    blk = lambda c: c // LANES
    seq_spec = lambda c: pl.BlockSpec((1, S, LANES), lambda b, i: (b, 0, blk(c)))
    ncp = kc.shape[1]
    return pl.pallas_call(
        functools.partial(_nsa_body, tq=tq, seq=S),
        grid=(B, S // tq),
        in_specs=[
            pl.BlockSpec((1, tq, NSA_HEADS * HEAD_DIM),
                         lambda b, i: (b, i, COL_QA // (NSA_HEADS * HEAD_DIM))),
            pl.BlockSpec((1, ncp, LANES), lambda b, i: (b, 0, 0)),
            pl.BlockSpec((1, ncp, LANES), lambda b, i: (b, 0, 0)),
            pl.BlockSpec((1, S, 2 * LANES), lambda b, i: (b, 0, COL_KSEL // (2 * LANES))),
            seq_spec(COL_VSEL), seq_spec(COL_KWIN), seq_spec(COL_VWIN),
            pl.BlockSpec((1, tq, LANES), lambda b, i: (b, i, blk(COL_GATE))),
            _const_spec(mt.shape),
        ],
        out_specs=pl.BlockSpec((1, tq, NSA_HEADS * HEAD_DIM), lambda b, i: (b, i, 0)),
        out_shape=jax.ShapeDtypeStruct((B, S, NSA_HEADS * HEAD_DIM), BF16),
        scratch_shapes=[pltpu.VMEM((NSA_KV_HEADS, HEADS_PER_KV * tq, LANES), F32)],
        compiler_params=_params(),
        name="nsa",
    )(proj3, kc, vc, proj3, proj3, proj3, proj3, proj3, mt)


DIL_SUB = 128
DIL_ROWS = 512


def _dil_body(q_ref, kp_ref, kc_ref, vp_ref, vc_ref, o_ref, lse_ref, *, n_back):
    sub = DIL_SUB
    rb, tq = q_ref.shape[1], q_ref.shape[2]
    t0 = pl.program_id(2) * tq
    head_of = lax.shift_right_logical(lax.broadcasted_iota(jnp.int32, (sub, DIL_WIDTH), 1), 6)
    lane = lax.broadcasted_iota(jnp.int32, (sub, LANES), 1)
    diff = (sub + lax.broadcasted_iota(jnp.int32, (sub, 1), 0)
            - lax.broadcasted_iota(jnp.int32, (1, 2 * sub), 1))
    band = (diff >= 0) & (diff <= n_back)
    band0 = band & (lax.broadcasted_iota(jnp.int32, (1, 2 * sub), 1) + t0 >= sub)
    tiles = [(r, j) for r in range(rb) for j in range(tq // sub)]
    keys = {r: jnp.concatenate([kp_ref[0, r], kc_ref[0, r]], axis=0) for r in range(rb)}
    vals = {r: jnp.concatenate([vp_ref[0, r], vc_ref[0, r]], axis=0) for r in range(rb)}
    scores = []
    for r, j in tiles:
        q = q_ref[0, r, j * sub:(j + 1) * sub, :]
        qs = jnp.concatenate([jnp.where(head_of == h, q, jnp.zeros_like(q))
                              for h in range(DIL_HEADS)], axis=0)
        s = _dot_nt(qs, keys[r][j * sub:(j + 2) * sub]).reshape(DIL_HEADS, sub, 2 * sub)
        scores.append(jnp.where((band0 if j == 0 else band)[None], s, NEG))
    stats = []
    for s in scores:
        m = jnp.max(s, axis=-1, keepdims=True)
        e = jnp.exp2(s - m)
        stats.append((m, e, jnp.sum(e, axis=-1, keepdims=True)))
    for (r, j), (m, e, l) in zip(tiles, stats):
        o = _dot(e.astype(BF16).reshape(DIL_HEADS * sub, 2 * sub), vals[r][j * sub:(j + 2) * sub])
        o = o.reshape(DIL_HEADS, sub, DIL_WIDTH) / l
        lse = m * LN2 + jnp.log(l)
        o_acc = jnp.zeros((sub, DIL_WIDTH), F32)
        lse_out = jnp.zeros((sub, LANES), F32)
        for h in range(DIL_HEADS):
            o_acc = jnp.where(head_of == h, o[h], o_acc)
            lse_out = jnp.where(lane == h, lse[h], lse_out)
        o_ref[0, r, j * sub:(j + 1) * sub, :] = o_acc
        lse_ref[0, r, j * sub:(j + 1) * sub, :] = lse_out


def _dilated(arr, cols, window, dilation, name):
    B, d, L, _ = arr.shape
    n_back = window // dilation
    tq = min(L, DIL_ROWS)
    rb = DIL_ROWS // tq
    assert d == dilation and n_back <= DIL_SUB and L % tq == 0 and d % rb == 0
    per = tq // DIL_SUB
    qc, kc, vc = cols
    cur = lambda c: pl.BlockSpec((1, rb, tq, DIL_WIDTH), lambda b, r, i: (b, r, i, c))
    prev = lambda c: pl.BlockSpec((1, rb, DIL_SUB, DIL_WIDTH),
                                  lambda b, r, i: (b, r, jnp.maximum(i * per - 1, 0), c))
    return pl.pallas_call(
        functools.partial(_dil_body, n_back=n_back),
        grid=(B, dilation // rb, L // tq),
        in_specs=[cur(qc), prev(kc), cur(kc), prev(vc), cur(vc)],
        out_specs=[pl.BlockSpec((1, rb, tq, DIL_WIDTH), lambda b, r, i: (b, r, i, 0)),
                   pl.BlockSpec((1, rb, tq, LANES), lambda b, r, i: (b, r, i, 0))],
        out_shape=[jax.ShapeDtypeStruct((B, dilation, L, DIL_WIDTH), F32),
                   jax.ShapeDtypeStruct((B, dilation, L, LANES), F32)],
        compiler_params=_params(),
        name=name,
    )(arr, arr, arr, arr, arr)


def _mixout_body(oa_ref, ob0_ref, ob1_ref, ob2_ref, l0_ref, l1_ref, l2_ref, mg_ref, x_ref,
                 pa_ref, pb_ref, wo_ref, out_ref, so_ref, sl_ref):
    tm = x_ref.shape[0]

    def interleaved(src_ref, st_ref):
        d = src_ref.shape[1]
        if d == 1:
            return src_ref[0, 0]
        slabs = src_ref.shape[3] // LANES
        for r in range(d):
            for k in range(slabs):
                st_ref[k, pl.ds(r, tm // d, stride=d), :] = src_ref[0, r, :, k * LANES:(k + 1) * LANES]
        return jnp.concatenate([st_ref[k] for k in range(slabs)], axis=1)

    lses = [interleaved(l, sl_ref) for l in (l0_ref, l1_ref, l2_ref)]
    mx = jnp.maximum(jnp.maximum(lses[0], lses[1]), lses[2])
    ws = [jnp.exp(l - mx) for l in lses]
    den = ws[0] + ws[1] + ws[2]
    ob = jnp.zeros((tm, DIL_WIDTH), F32)
    for w, o_ref in zip(ws, (ob0_ref, ob1_ref, ob2_ref)):
        alpha = w / den
        wide = jnp.concatenate(
            [jnp.broadcast_to(alpha[:, h:h + 1], (tm, HEAD_DIM)) for h in range(DIL_HEADS)], axis=1)
        ob = ob + wide * interleaved(o_ref, so_ref)
    ya = _dot(oa_ref[...], pa_ref[0])
    yb = _dot(ob.astype(BF16), pb_ref[0])
    gm = jax.nn.sigmoid(mg_ref[...].astype(F32))
    y = gm[:, :D_MODEL] * ya + gm[:, D_MODEL:] * yb
    out_ref[...] = x_ref[...] + _dot(y.astype(BF16), wo_ref[0])


def _mixout(oa, obs, lses, proj, x2, pa, pb, wo, layer, S, tm=512):
    T = x2.shape[0]
    per_b = S // tm
    row = lambda w: pl.BlockSpec((tm, w), lambda i: (i, 0))
    folded = lambda a: pl.BlockSpec((1, a.shape[1], tm // a.shape[1], a.shape[3]),
                                    lambda i: (i // per_b, 0, i % per_b, 0))
    return pl.pallas_call(
        _mixout_body,
        grid=(T // tm,),
        in_specs=[row(NSA_HEADS * HEAD_DIM), *[folded(a) for a in obs], *[folded(a) for a in lses],
                  pl.BlockSpec((tm, 2 * D_MODEL), lambda i: (i, COL_MERGE // (2 * D_MODEL))),
                  row(D_MODEL),
                  _layer_spec(pa, layer), _layer_spec(pb, layer), _layer_spec(wo, layer)],
        out_specs=row(D_MODEL),
        out_shape=jax.ShapeDtypeStruct((T, D_MODEL), F32),
        scratch_shapes=[pltpu.VMEM((DIL_WIDTH // LANES, tm, LANES), F32),
                        pltpu.VMEM((1, tm, LANES), F32)],
        compiler_params=_params(),
        name="mixout",
    )(oa, *obs, *lses, proj, x2, pa, pb, wo)


FF_CHUNK = 512


def _ffn_body(x_ref, g_ref, w1_ref, w3_ref, w2_ref, o_ref):
    x = x_ref[...]
    h = _rms(x, g_ref[...]).astype(BF16)
    acc = jnp.zeros(x.shape, F32)
    for c0 in range(0, D_FF, FF_CHUNK):
        c1 = min(c0 + FF_CHUNK, D_FF)
        act = _silu(_dot(h, w1_ref[:, c0:c1])) * _dot(h, w3_ref[:, c0:c1])
        acc = acc + _dot(act.astype(BF16), w2_ref[c0:c1, :])
    o_ref[...] = x + acc


def _ffn(x2, g, w1, w3, w2, tm=512):
    T = x2.shape[0]
    return pl.pallas_call(
        _ffn_body,
        grid=(T // tm,),
        in_specs=[pl.BlockSpec((tm, D_MODEL), lambda i: (i, 0)), _const_spec(g.shape),
                  _const_spec(w1.shape), _const_spec(w3.shape), _const_spec(w2.shape)],
        out_specs=pl.BlockSpec((tm, D_MODEL), lambda i: (i, 0)),
        out_shape=jax.ShapeDtypeStruct((T, D_MODEL), F32),
        compiler_params=_params(),
        name="ffn",
    )(x2, g, w1, w3, w2)


MOE_TM = 1024
MOE_CHUNK = 128
MOE_BIG = 2 * MOE_CHUNK
MOE_MERGED = 4


def _router_body(x_ref, g_ref, r_ref, h_ref, combw_ref, posw_ref, post_ref, cnt_ref):
    tm = x_ref.shape[0]
    h = _rms(x_ref[...], g_ref[...])
    h_ref[...] = h.astype(BF16)
    r = r_ref[...]
    h_hi, r_hi = h.astype(BF16), r.astype(BF16)
    h_lo = (h - h_hi.astype(F32)).astype(BF16)
    r_lo = (r - r_hi.astype(F32)).astype(BF16)
    logits = _dot(h_hi, r_hi) + (_dot(h_hi, r_lo) + _dot(h_lo, r_hi))
    lane = lax.broadcasted_iota(jnp.int32, logits.shape, 1)
    lg = jnp.where(lane < N_EXPERTS, logits, NEG)
    m1 = jnp.max(lg, axis=-1, keepdims=True)
    i1 = jnp.min(jnp.where(lg == m1, lane, LANES), axis=-1, keepdims=True)
    lg2 = jnp.where(lane == i1, NEG, lg)
    m2 = jnp.max(lg2, axis=-1, keepdims=True)
    i2 = jnp.min(jnp.where(lg2 == m2, lane, LANES), axis=-1, keepdims=True)
    e2 = jnp.exp(m2 - m1)
    den = 1.0 + e2
    w_first, w_second = 1.0 / den, e2 / den
    chosen = [jnp.broadcast_to((i1 == ex) | (i2 == ex), (tm, LANES)) for ex in range(N_EXPERTS)]
    ones = jnp.concatenate([jnp.where(c, 1.0, 0.0).astype(BF16) for c in chosen], axis=1)
    row = lax.broadcasted_iota(jnp.int32, (LANES, LANES), 0)
    col = lax.broadcasted_iota(jnp.int32, (LANES, LANES), 1)
    tri = jnp.where(col < row, 1.0, 0.0).astype(BF16)
    running = jnp.zeros((1, N_EXPERTS * LANES), F32)
    parts = []
    for b in range(tm // LANES):
        blk = ones[b * LANES:(b + 1) * LANES]
        parts.append(_dot(tri, blk) + running)
        running = running + jnp.sum(blk.astype(F32), axis=0, keepdims=True)
    before = jnp.concatenate(parts, axis=0)
    by_lane = jnp.zeros((tm, LANES), F32)
    cnt = jnp.zeros((1, LANES), F32)
    for ex in range(N_EXPERTS):
        slab = slice(ex * LANES, (ex + 1) * LANES)
        combw_ref[:, slab] = jnp.broadcast_to(
            jnp.where(i1 == ex, w_first, 0.0) + jnp.where(i2 == ex, w_second, 0.0), (tm, LANES))
        pos = jnp.where(chosen[ex], before[:, slab], -1.0)
        posw_ref[:, slab] = pos
        by_lane = jnp.where(lane == ex, pos, by_lane)
        cnt = jnp.where(lane[:1] == ex, running[:, slab], cnt)
    post_ref[0] = by_lane.T[:N_EXPERTS]
    cnt_ref[0] = jnp.broadcast_to(cnt, (8, LANES)).astype(jnp.int32)


def _router(x2, g, r, tm=MOE_TM):
    T = x2.shape[0]
    nt = T // tm
    row = lambda w: pl.BlockSpec((tm, w), lambda i: (i, 0))
    return pl.pallas_call(
        _router_body,
        grid=(nt,),
        in_specs=[row(D_MODEL), _const_spec(g.shape), _const_spec(r.shape)],
        out_specs=[row(D_MODEL), row(N_EXPERTS * LANES), row(N_EXPERTS * LANES),
                   pl.BlockSpec((1, N_EXPERTS, tm), lambda i: (i, 0, 0)),
                   pl.BlockSpec((1, 8, LANES), lambda i: (i, 0, 0))],
        out_shape=[jax.ShapeDtypeStruct((T, D_MODEL), BF16),
                   jax.ShapeDtypeStruct((T, N_EXPERTS * LANES), F32),
                   jax.ShapeDtypeStruct((T, N_EXPERTS * LANES), F32),
                   jax.ShapeDtypeStruct((nt, N_EXPERTS, tm), F32),
                   jax.ShapeDtypeStruct((nt, 8, LANES), jnp.int32)],
        compiler_params=_params(),
        name="router",
    )(x2, g, r)


def _moe_body(cnt_ref, x_ref, h_ref, comb_ref, pos_ref, post_ref, w1_ref, w3_ref, w2_ref, fn_ref,
              o_ref, y_ref):
    i = pl.program_id(0)
    e = pl.program_id(1)
    merged = MOE_MERGED * LANES
    n = cnt_ref[i * N_EXPERTS + e]
    n_big = (n + (MOE_BIG - MOE_CHUNK - 1)) // MOE_BIG
    rest_row = pl.multiple_of(n_big * MOE_BIG, MOE_BIG)
    has_rest = n > rest_row
    post_e = post_ref[0, pl.ds(e, 1), :]

    @pl.when(e == 0)
    def _():
        o_ref[...] = x_ref[...]

    @pl.when((i == 0) & (e == 0))
    def _():
        y_ref[...] = jnp.zeros_like(y_ref)

    def scatter_add(first_row, first_slot, n_slabs):
        lane_slot = lax.broadcasted_iota(jnp.int32, (1, LANES), 1).astype(F32)
        hot = [jnp.where(pos_ref[...] == first_slot + (k * LANES) + lane_slot, 1.0, 0.0).astype(BF16)
               for k in range(n_slabs)]
        scatter = hot[0] if n_slabs == 1 else jnp.concatenate(hot, axis=1)
        weight = jnp.concatenate([comb_ref[...]] * (D_MODEL // LANES), axis=1)
        ys = y_ref[pl.ds(first_row, n_slabs * LANES), :].astype(BF16)
        o_ref[...] += weight * _dot(scatter, ys)

    def chunk(r0, size):
        base = r0.astype(F32)
        slot_col = base + lax.broadcasted_iota(jnp.int32, (size, 1), 0).astype(F32)
        gather = jnp.where(post_e == slot_col, 1.0, 0.0).astype(BF16)
        xe = _dot(gather, h_ref[...]).astype(BF16)
        act = _silu(_dot(xe, w1_ref[0])) * _dot(xe, w3_ref[0])
        y = _dot(act.astype(BF16), w2_ref[0])

        @pl.when(r0 < merged)
        def _():
            y_ref[pl.ds(r0, size), :] = y

        @pl.when(r0 >= merged)
        def _():
            y_ref[merged:merged + size, :] = y
            for k in range(size // LANES):
                scatter_add(merged + k * LANES, base + float(k * LANES), 1)

    def big_chunk(c, carry):
        chunk(pl.multiple_of(c * MOE_BIG, MOE_BIG), MOE_BIG)
        return carry

    lax.fori_loop(0, n_big, big_chunk, 0)

    @pl.when(has_rest)
    def _():
        chunk(rest_row, MOE_CHUNK)

    @pl.when(n > 0)
    def _():
        scatter_add(0, 0.0, MOE_MERGED)

    @pl.when(e == pl.num_programs(1) - 1)
    def _():
        o_ref[...] = _rms(o_ref[...], fn_ref[...])


def _moe(x2, h, comb, pos, post, counts, w1, w3, w2, fn, tm=MOE_TM):
    T = x2.shape[0]
    once = lambda w: pl.BlockSpec((tm, w), lambda i, e, cnt: (i, 0), pipeline_mode=pl.Buffered(1))
    tile = lambda w: pl.BlockSpec((tm, w), lambda i, e, cnt: (i, 0))
    assert MOE_CHUNK == LANES
    slab = pl.BlockSpec((tm, LANES), lambda i, e, cnt: (i, e))
    expert = lambda a: pl.BlockSpec((1,) + a.shape[1:], lambda i, e, cnt: (e, 0, 0))
    grid_spec = pltpu.PrefetchScalarGridSpec(
        num_scalar_prefetch=1,
        grid=(T // tm, N_EXPERTS),
        in_specs=[once(D_MODEL), once(D_MODEL), slab, slab,
                  pl.BlockSpec((1, N_EXPERTS, tm), lambda i, e, cnt: (i, 0, 0)),
                  expert(w1), expert(w3), expert(w2),
                  pl.BlockSpec(fn.shape, lambda i, e, cnt: (0, 0))],
        out_specs=tile(D_MODEL),
        scratch_shapes=[pltpu.VMEM((MOE_MERGED * LANES + MOE_BIG, D_MODEL), F32)],
    )
    return pl.pallas_call(
        _moe_body,
        grid_spec=grid_spec,
        out_shape=jax.ShapeDtypeStruct((T, D_MODEL), F32),
        compiler_params=_params(),
        name="moe",
    )(counts, x2, h, comb, pos, post, w1, w3, w2, fn)


def _rope_lane_tables(positions):
    half = ROT_DIM // 2
    inv = ROPE_THETA ** (-jnp.arange(0, ROT_DIM, 2, dtype=F32) / ROT_DIM)
    ang = positions.astype(F32).reshape(-1, 1) * inv
    cos, sin = jnp.cos(ang), jnp.sin(ang)
    lane = np.arange(LANES)
    within = lane % HEAD_DIM
    pick = lane % half
    cos_l, sin_l = cos[:, pick], sin[:, pick]
    c = jnp.where(within < ROT_DIM, cos_l, 1.0)
    sa = jnp.where(within < half, -sin_l, 0.0)
    sb = jnp.where((within >= half) & (within < ROT_DIM), sin_l, 0.0)
    return c, sa, sb


def _permute_w_in(w):
    scale = HEAD_DIM ** -0.5 * LOG2E
    qa, kv, gate, qkv_b, merge = 0, 512, 1280, 1304, 3608
    nb = N_DIL_GROUPS * DIL_WIDTH
    kv_piece = lambda j: w[..., kv + 128 * j:kv + 128 * (j + 1)]
    dil = lambda part, g0, g1: w[..., qkv_b + part * nb + g0 * DIL_WIDTH:
                                 qkv_b + part * nb + g1 * DIL_WIDTH]
    pieces = [
        w[..., qa:qa + 512] * scale,
        kv_piece(2), kv_piece(4), kv_piece(0),
        kv_piece(3), kv_piece(5),
        w[..., gate:gate + 3 * NSA_HEADS],
        jnp.zeros(w.shape[:-1] + (LANES - 3 * NSA_HEADS,), w.dtype),
        kv_piece(1),
        w[..., merge:merge + 2 * D_MODEL],
        dil(0, 0, 3) * scale, dil(1, 0, 3), dil(2, 0, 3),
    ]
    out = jnp.concatenate(pieces, axis=-1)
    assert out.shape[-1] == N_W_IN
    return out.astype(BF16)


def _importance_matrix_t(seq, ncp):
    n_c = (seq - CMP_LEN) // CMP_STRIDE + 1
    starts = np.arange(n_c) * CMP_STRIDE
    bstart = np.arange(seq // SEL_LEN) * SEL_LEN
    overlap = np.clip(np.minimum(starts[:, None] + CMP_LEN, bstart[None, :] + SEL_LEN)
                      - np.maximum(starts[:, None], bstart[None, :]), 0, None)
    m = np.zeros((ncp, seq // SEL_LEN), np.float32)
    m[:n_c] = overlap.astype(np.float32) / CMP_LEN
    return jnp.asarray(m.T)


def _mixer(x2, layer, B, S, tables, mt, prm):
    proj, *folds, a = _inproj(x2, prm["norm_mix"], prm["w_in"], layer, *tables, B, S)
    proj3 = proj.reshape(B, S, N_PROJ)
    ncp = S // CMP_STRIDE
    cmp = _compress(a.reshape(2, B * NSA_KV_HEADS * ncp, CMP_STRIDE * HEAD_DIM),
                    prm["cmp_w1"], prm["cmp_w2"], prm["cmp_pos"], layer, B)
    oa = _nsa(proj3, cmp[0], cmp[1], mt).reshape(B * S, NSA_HEADS * HEAD_DIM)
    obs, lses = [], []
    for g, ((w, d), arr) in enumerate(zip(DIL_PATTERNS, folds)):
        o, lse = _dilated(arr, (0, 1, 2), w, d, f"dilated{g}")
        obs.append(o)
        lses.append(lse)
    return _mixout(oa, obs, lses, proj, x2, prm["p_a"], prm["p_b"], prm["w_o"], layer, S)


def kernel(x, positions, norm_mix, w_in, cmp_pos_k, cmp_pos_v, cmp_k_w1, cmp_k_w2, cmp_v_w1,
           cmp_v_w2, w_branch_a, w_branch_b, w_out, norm_ffn, ffn_w1, ffn_w3, ffn_w2, router,
           moe_w1, moe_w3, moe_w2, final_norm):
    B, S, D = x.shape
    depth = norm_mix.shape[0]
    assert depth == 2 and D == D_MODEL
    tables = _rope_lane_tables(positions)
    mt = _importance_matrix_t(S, S // CMP_STRIDE)
    cmp_pos = jnp.stack([cmp_pos_k, cmp_pos_v], axis=1).reshape(depth, 2, 1, CMP_LEN * HEAD_DIM)
    prm = {
        "norm_mix": norm_mix.reshape(depth, 1, D),
        "w_in": _permute_w_in(w_in),
        "cmp_w1": jnp.stack([cmp_k_w1, cmp_v_w1], axis=1).astype(BF16),
        "cmp_w2": jnp.stack([cmp_k_w2, cmp_v_w2], axis=1).astype(BF16),
        "cmp_pos": jnp.broadcast_to(cmp_pos, (depth, 2, 8, CMP_LEN * HEAD_DIM)).astype(BF16),
        "p_a": w_branch_a.astype(BF16), "p_b": w_branch_b.astype(BF16), "w_o": w_out.astype(BF16),
    }
    x2 = x.reshape(B * S, D)
    x2 = _mixer(x2, 0, B, S, tables, mt, prm)
    x2 = _ffn(x2, norm_ffn[0].reshape(1, -1), ffn_w1[0].astype(BF16), ffn_w3[0].astype(BF16),
              ffn_w2[0].astype(BF16))
    x2 = _mixer(x2, 1, B, S, tables, mt, prm)
    g1 = norm_ffn[1].reshape(1, -1)
    r = jnp.pad(router[0], ((0, 0), (0, LANES - N_EXPERTS)))
    h, comb, pos, post, cnt = _router(x2, g1, r)
    counts = cnt[:, 0, :N_EXPERTS].reshape(-1)
    out = _moe(x2, h, comb, pos, post, counts, moe_w1[0].astype(BF16), moe_w3[0].astype(BF16),
               moe_w2[0].astype(BF16), final_norm.reshape(1, -1))
    return out.reshape(B, S, D)
```

```python
import functools

import numpy as np
import jax
import jax.numpy as jnp
from jax import lax
from jax.experimental import pallas as pl
from jax.experimental.pallas import tpu as pltpu

F32 = jnp.float32
BF16 = jnp.bfloat16

D_MODEL = 1024
HEAD_DIM = 64
ROT_DIM = HEAD_DIM // 4
ROPE_THETA = 500000.0
EPS = 1e-6
NSA_HEADS = 8
NSA_KV_HEADS = 2
HEADS_PER_KV = NSA_HEADS // NSA_KV_HEADS
CMP_LEN = 32
CMP_STRIDE = 16
CMP_HIDDEN = 128
SEL_LEN = 64
N_SEL = 16
WIN = 512
DIL_PATTERNS = ((128, 1), (512, 4), (2048, 16))
N_DIL_GROUPS = 3
DIL_HEADS = 4
D_FF = 2816
N_EXPERTS = 8

LANES = 128
VMEM_LIMIT = 56 * 1024 * 1024
NEG = -1e30
BIG = 1e30

COL_MERGE = 0
COL_QA = 2048
COL_KSEL = 2560
COL_BLK = COL_KSEL + 128
COL_KWIN = 2816
COL_VSEL = 2944
COL_VWIN = 3072
COL_GATE = 3200
N_PROJ = 3328
DIL_WIDTH = DIL_HEADS * HEAD_DIM
N_FOLD = N_DIL_GROUPS * DIL_WIDTH
STAGE = None
IN_CHUNKS = (
    (512, True, tuple(COL_QA + 128 * j for j in range(4)), None),
    (384, True, (COL_KSEL, COL_KWIN, STAGE), ("cmp", 0)),
    (512, False, (COL_VSEL, COL_VWIN, COL_GATE, STAGE), ("cmp", 1)),
    (512, False, tuple(COL_MERGE + 128 * j for j in range(0, 4)), None),
    (512, False, tuple(COL_MERGE + 128 * j for j in range(4, 8)), None),
    (512, False, tuple(COL_MERGE + 128 * j for j in range(8, 12)), None),
    (512, False, tuple(COL_MERGE + 128 * j for j in range(12, 16)), None),
    (N_FOLD, True, (STAGE,) * 6, ("fold", 0)),
    (N_FOLD, True, (STAGE,) * 6, ("fold", 1)),
    (N_FOLD, False, (STAGE,) * 6, ("fold", 2)),
)
N_W_IN = sum(c[0] for c in IN_CHUNKS)
LOG2E = 1.4426950408889634
LN2 = 0.6931471805599453
MASK_BIAS = -(2.0 ** 100)
SEL_BLOCKS_MAX = 32


def _dot(a, b, precision=None):
    return jnp.dot(a, b, preferred_element_type=F32, precision=precision)


def _dot_nt(a, b, precision=None):
    return lax.dot_general(a, b, (((1,), (1,)), ((), ())), preferred_element_type=F32,
                           precision=precision)


def _rms(x, g):
    ms = jnp.mean(x * x, axis=-1, keepdims=True)
    return x * lax.rsqrt(ms + EPS) * g


def _silu(x):
    return x * jax.nn.sigmoid(x)


def _params(**kw):
    return pltpu.CompilerParams(vmem_limit_bytes=VMEM_LIMIT, **kw)


def _const_spec(shape):
    nd = len(shape)
    return pl.BlockSpec(shape, lambda *_: (0,) * nd)


def _inproj_body(x_ref, g_ref, w_ref, c_ref, sa_ref, sb_ref, o_ref, f0_ref, f1_ref, f2_ref, a_ref,
                 st_ref, *, per_b):
    tm = x_ref.shape[0]
    t_seq = (pl.program_id(0) % per_b) * tm + lax.broadcasted_iota(jnp.int32, (tm, LANES), 0)
    lane = lax.broadcasted_iota(jnp.int32, (tm, LANES), 1)
    blk = lax.shift_right_logical(t_seq, 6)
    hot = (lane == blk) | (lane == blk + SEL_BLOCKS_MAX)
    o_ref[:, COL_BLK:COL_BLK + LANES] = jnp.where(hot, 1.0, 0.0).astype(BF16)
    h = _rms(x_ref[...], g_ref[0]).astype(BF16)
    c = c_ref[...]
    sa = sa_ref[...]
    sb = sb_ref[...]
    start = 0
    for size, rope, dests, action in IN_CHUNKS:
        acc = _dot(h, w_ref[0, :, start:start + size])
        start += size
        for j, dest in enumerate(dests):
            a = acc[:, j * LANES:(j + 1) * LANES]
            if rope:
                a = a * c + pltpu.roll(a, LANES - 8, 1) * sa + pltpu.roll(a, 8, 1) * sb
            if dest is STAGE:
                st_ref[j] = a
            else:
                o_ref[:, dest:dest + LANES] = a.astype(BF16)
        if action is None:
            continue
        kind, piece = action
        if kind == "fold":
            slabs = DIL_WIDTH // LANES
            for gi, f_ref in enumerate((f0_ref, f1_ref, f2_ref)):
                d = DIL_PATTERNS[gi][1]
                for r in range(d):
                    for k in range(slabs):
                        rows = st_ref[gi * slabs + k, pl.ds(r, tm // d, stride=d), :]
                        c0 = piece * DIL_WIDTH + k * LANES
                        f_ref[0, r, :, c0:c0 + LANES] = rows.astype(BF16)
        else:
            slab = dests.index(STAGE)
            nrow = tm // CMP_STRIDE
            toks = [st_ref[slab, pl.ds(j, nrow, stride=CMP_STRIDE), :] for j in range(CMP_STRIDE)]
            for g in range(NSA_KV_HEADS):
                head = slice(g * HEAD_DIM, (g + 1) * HEAD_DIM)
                for m in range(CMP_STRIDE // 2):
                    pair = jnp.concatenate([toks[2 * m][:, head], toks[2 * m + 1][:, head]], axis=1)
                    a_ref[piece, 0, g, :, m * LANES:(m + 1) * LANES] = pair.astype(BF16)


def _layer_spec(arr, layer):
    nd = arr.ndim
    return pl.BlockSpec((1,) + arr.shape[1:], lambda *_: (layer,) + (0,) * (nd - 1))


def _inproj(x2, g, w, layer, rc, rsa, rsb, B, S, tm=512):
    T = x2.shape[0]
    per_b = S // tm
    dils = [d for _, d in DIL_PATTERNS]
    fold_spec = lambda d: pl.BlockSpec((1, d, tm // d, 3 * DIL_WIDTH),
                                       lambda i: (i // per_b, 0, i % per_b, 0))
    fold_shape = lambda d: jax.ShapeDtypeStruct((B, d, S // d, 3 * DIL_WIDTH), BF16)
    cmp_w = CMP_STRIDE * HEAD_DIM
    assert S // SEL_LEN <= SEL_BLOCKS_MAX
    return pl.pallas_call(
        functools.partial(_inproj_body, per_b=per_b),
        grid=(T // tm,),
        in_specs=[
            pl.BlockSpec((tm, D_MODEL), lambda i: (i, 0)),
            _layer_spec(g, layer), _layer_spec(w, layer),
            pl.BlockSpec((tm, LANES), lambda i: (i, 0)),
            pl.BlockSpec((tm, LANES), lambda i: (i, 0)),
            pl.BlockSpec((tm, LANES), lambda i: (i, 0)),
        ],
        out_specs=[pl.BlockSpec((tm, N_PROJ), lambda i: (i, 0)), *[fold_spec(d) for d in dils],
                   pl.BlockSpec((2, 1, NSA_KV_HEADS, tm // CMP_STRIDE, cmp_w),
                                lambda i: (0, i // per_b, 0, i % per_b, 0))],
        out_shape=[jax.ShapeDtypeStruct((T, N_PROJ), BF16), *[fold_shape(d) for d in dils],
                   jax.ShapeDtypeStruct((2, B, NSA_KV_HEADS, S // CMP_STRIDE, cmp_w), BF16)],
        scratch_shapes=[pltpu.VMEM((N_FOLD // LANES, tm, LANES), F32)],
        compiler_params=_params(),
        name="inproj",
    )(x2, g, w, rc, rsa, rsb)


def _compress_body(a_ref, w1_ref, w2_ref, pos_ref, o_ref):
    nb, ncp = o_ref.shape[1], o_ref.shape[2]
    a = a_ref[0]
    w1 = w1_ref[0, 0]
    half = CMP_STRIDE * HEAD_DIM
    top = _dot(a, w1[:half])
    bot = _dot(a, w1[half:])
    pc = _dot(pos_ref[0, 0], w1)
    rows = a.shape[0]
    hid = top + pltpu.roll(bot, rows - 1, 0) + pc[0:1]
    out = _dot(_silu(hid).astype(BF16), w2_ref[0, 0])
    for b in range(nb):
        heads = [out[(b * NSA_KV_HEADS + g) * ncp:(b * NSA_KV_HEADS + g + 1) * ncp]
                 for g in range(NSA_KV_HEADS)]
        o_ref[0, b] = jnp.concatenate(heads, axis=1).astype(BF16)


def _compress(a, w1, w2, pos, layer, B):
    n, rows, _ = a.shape
    ncp = rows // (B * NSA_KV_HEADS)
    per_kv = lambda arr: pl.BlockSpec((1, 1) + arr.shape[2:], lambda i: (layer, i, 0, 0))
    return pl.pallas_call(
        _compress_body,
        grid=(n,),
        in_specs=[pl.BlockSpec((1, rows, CMP_STRIDE * HEAD_DIM), lambda i: (i, 0, 0)),
                  per_kv(w1), per_kv(w2), per_kv(pos)],
        out_specs=pl.BlockSpec((1, B, ncp, NSA_KV_HEADS * HEAD_DIM), lambda i: (i, 0, 0, 0)),
        out_shape=jax.ShapeDtypeStruct((n, B, ncp, NSA_KV_HEADS * HEAD_DIM), BF16),
        compiler_params=_params(),
        name="compress",
    )(a, w1, w2, pos)


def _softmax2(s):
    m = jnp.max(s, axis=-1, keepdims=True)
    e = jnp.exp2(s - m)
    return e.astype(BF16), jnp.sum(e, axis=-1, keepdims=True)


def _weighted_values(e, l, v):
    nh, tq, nk = e.shape
    return _dot(e.reshape(nh * tq, nk), v) / l.reshape(nh * tq, 1)


SEL_PREFIX = 512


def _nsa_body(q_ref, kc_ref, vc_ref, ksel_ref, vsel_ref, kwin_ref, vwin_ref, gate_ref, mt_ref,
              o_ref, osel_ref, *, tq, seq):
    nblk = seq // SEL_LEN
    ncp = kc_ref.shape[1]
    n_cmp = (seq - CMP_LEN) // CMP_STRIDE + 1
    q0 = pl.program_id(1) * tq
    q = q_ref[0]
    t_col = q0 + lax.broadcasted_iota(jnp.int32, (tq, 1), 0)
    t_row = q0 + lax.broadcasted_iota(jnp.int32, (1, tq), 1)
    gates = jax.nn.sigmoid(gate_ref[0].astype(F32))

    zeros64 = jnp.zeros((tq, HEAD_DIM), BF16)

    def stacked_q(g):
        parts = []
        for hh in range(HEADS_PER_KV):
            h = g * HEADS_PER_KV + hh
            qh = q[:, h * HEAD_DIM:(h + 1) * HEAD_DIM]
            parts.append(jnp.concatenate([qh, zeros64] if g == 0 else [zeros64, qh], axis=1))
        return jnp.concatenate(parts, axis=0)

    qs = [stacked_q(g) for g in range(NSA_KV_HEADS)]

    cidx = lax.broadcasted_iota(jnp.int32, (tq, ncp), 1)
    cmask = ((cidx * CMP_STRIDE + (CMP_LEN - 1)) <= t_col) & (cidx < n_cmp)
    jidx = lax.broadcasted_iota(jnp.int32, (nblk, tq), 0)
    cur = lax.shift_right_logical(t_row, 6)
    forced = (jidx == 0) | (jidx == cur) | (jidx == cur - 1)
    future = jidx > cur
    groups = range(NSA_KV_HEADS)
    span = WIN + tq
    ks = pl.multiple_of(jnp.maximum(q0 - WIN, 0), tq)
    kw = kwin_ref[0, pl.ds(ks, span), :]
    vw = vwin_ref[0, pl.ds(ks, span), :]
    wpos = ks + lax.broadcasted_iota(jnp.int32, (1, span), 1)
    wmask = (wpos <= t_col) & (t_col - wpos <= WIN - 1)

    s_cmp = [jnp.where(cmask[None], _dot_nt(qs[g], kc_ref[0]).reshape(HEADS_PER_KV, tq, ncp), NEG)
             for g in groups]
    s_win = [jnp.where(wmask[None], _dot_nt(qs[g], kw).reshape(HEADS_PER_KV, tq, span), NEG)
             for g in groups]
    p_cmp = []
    for g in groups:
        m = jnp.max(s_cmp[g], axis=-1, keepdims=True)
        e = jnp.where(cmask[None], jnp.exp2(s_cmp[g] - m), 0.0)
        den = jnp.sum(e, axis=-1, keepdims=True)
        p_cmp.append(e / jnp.where(den > 0, den, 1.0))
    e_win = [_softmax2(s_win[g]) for g in groups]
    o_cmp = [_dot(p_cmp[g].astype(BF16).reshape(HEADS_PER_KV * tq, ncp), vc_ref[0]) for g in groups]
    imps = [_dot_nt(mt_ref[...], p_cmp[g][0] + p_cmp[g][1] + p_cmp[g][2] + p_cmp[g][3],
                    precision=lax.Precision.HIGHEST) for g in groups]
    o_win = [_weighted_values(*e_win[g], vw) for g in groups]

    first_blk = lax.shift_right_logical(q0, 6)
    picked_rows, before_rows = [], []
    for g in groups:
        imp = jnp.where(forced, BIG, imps[g])
        imp = jnp.where(future, -BIG, imp)
        rank = jnp.zeros((nblk, tq), jnp.int32)
        for i in range(nblk):
            row = imp[i:i + 1, :]
            beats = (row > imp) | ((row == imp) & (jidx > i))
            rank = rank + beats.astype(jnp.int32)
        picked_rows.append(jnp.where(rank < N_SEL, 0.0, MASK_BIAS))
        before_rows.append(jnp.where((rank < N_SEL) & (jidx < first_blk), 0.0, MASK_BIAS))
    assert 2 * NSA_KV_HEADS * SEL_BLOCKS_MAX == LANES
    bias_t = jnp.concatenate(before_rows + picked_rows, axis=0)
    bias_main = bias_t.T
    bias_diag = pltpu.roll(bias_main, LANES // 2, 1)
    lane_group = lax.shift_right_logical(lax.broadcasted_iota(jnp.int32, (tq, LANES), 1), 5)

    def with_bias(g, bias):
        own = jnp.where(lane_group == g, bias, 0.0).astype(BF16)
        return jnp.concatenate([qs[g], jnp.concatenate([own] * HEADS_PER_KV, axis=0)], axis=1)

    qb = [with_bias(g, bias_main) for g in groups]

    kdiag = ksel_ref[0, pl.ds(pl.multiple_of(q0, tq), tq), :]
    vdiag = vsel_ref[0, pl.ds(pl.multiple_of(q0, tq), tq), :]
    tri = (lax.broadcasted_iota(jnp.int32, (tq, tq), 1) <= lax.broadcasted_iota(jnp.int32, (tq, tq), 0))
    s_diag = [jnp.where(tri[None], _dot_nt(with_bias(g, bias_diag), kdiag)
                        .reshape(HEADS_PER_KV, tq, tq), NEG) for g in groups]
    n_prefix = q0 // SEL_PREFIX + 1
    for n in range(1, seq // SEL_PREFIX + 1):
        klen = n * SEL_PREFIX

        @pl.when(n_prefix == n)
        def _(klen=klen):
            vall = jnp.concatenate([vsel_ref[0, :klen, :], vdiag], axis=0)
            s = [jnp.concatenate(
                [_dot_nt(qb[g], ksel_ref[0, :klen, :]).reshape(HEADS_PER_KV, tq, klen), s_diag[g]],
                axis=-1) for g in groups]
            ew = [_softmax2(s[g]) for g in groups]
            for g in groups:
                osel_ref[g] = _weighted_values(*ew[g], vall)

    o_sel = [osel_ref[g] for g in groups]

    outs = []
    for g in range(NSA_KV_HEADS):
        for hh in range(HEADS_PER_KV):
            h = g * HEADS_PER_KV + hh
            acc = jnp.zeros((tq, HEAD_DIM), F32)
            for br, o in enumerate((o_cmp[g], o_sel[g], o_win[g])):
                oh = o[hh * tq:(hh + 1) * tq, g * HEAD_DIM:(g + 1) * HEAD_DIM]
                acc = acc + gates[:, 3 * h + br:3 * h + br + 1] * oh
            outs.append(acc)
    o_ref[0] = jnp.concatenate(outs, axis=1).astype(BF16)


def _nsa(proj3, kc, vc, mt, tq=128):
    B, S, _ = proj3.shape
    blk = lambda c: c // LANES
    seq_spec = lambda c: pl.BlockSpec((1, S, LANES), lambda b, i: (b, 0, blk(c)))
    ncp = kc.shape[1]
    return pl.pallas_call(
        functools.partial(_nsa_body, tq=tq, seq=S),
        grid=(B, S // tq),
        in_specs=[
            pl.BlockSpec((1, tq, NSA_HEADS * HEAD_DIM),
                         lambda b, i: (b, i, COL_QA // (NSA_HEADS * HEAD_DIM))),
            pl.BlockSpec((1, ncp, LANES), lambda b, i: (b, 0, 0)),
            pl.BlockSpec((1, ncp, LANES), lambda b, i: (b, 0, 0)),
            pl.BlockSpec((1, S, 2 * LANES), lambda b, i: (b, 0, COL_KSEL // (2 * LANES))),
            seq_spec(COL_VSEL), seq_spec(COL_KWIN), seq_spec(COL_VWIN),
            pl.BlockSpec((1, tq, LANES), lambda b, i: (b, i, blk(COL_GATE))),
            _const_spec(mt.shape),
        ],
        out_specs=pl.BlockSpec((1, tq, NSA_HEADS * HEAD_DIM), lambda b, i: (b, i, 0)),
        out_shape=jax.ShapeDtypeStruct((B, S, NSA_HEADS * HEAD_DIM), BF16),
        scratch_shapes=[pltpu.VMEM((NSA_KV_HEADS, HEADS_PER_KV * tq, LANES), F32)],
        compiler_params=_params(),
        name="nsa",
    )(proj3, kc, vc, proj3, proj3, proj3, proj3, proj3, mt)


DIL_SUB = 128
DIL_ROWS = 512


def _dil_body(q_ref, kp_ref, kc_ref, vp_ref, vc_ref, o_ref, lse_ref, *, n_back):
    sub = DIL_SUB
    rb, tq = q_ref.shape[1], q_ref.shape[2]
    t0 = pl.program_id(2) * tq
    head_of = lax.shift_right_logical(lax.broadcasted_iota(jnp.int32, (sub, DIL_WIDTH), 1), 6)
    lane = lax.broadcasted_iota(jnp.int32, (sub, LANES), 1)
    diff = (sub + lax.broadcasted_iota(jnp.int32, (sub, 1), 0)
            - lax.broadcasted_iota(jnp.int32, (1, 2 * sub), 1))
    band = (diff >= 0) & (diff <= n_back)
    band0 = band & (lax.broadcasted_iota(jnp.int32, (1, 2 * sub), 1) + t0 >= sub)
    tiles = [(r, j) for r in range(rb) for j in range(tq // sub)]
    keys = {r: jnp.concatenate([kp_ref[0, r], kc_ref[0, r]], axis=0) for r in range(rb)}
    vals = {r: jnp.concatenate([vp_ref[0, r], vc_ref[0, r]], axis=0) for r in range(rb)}
    scores = []
    for r, j in tiles:
        q = q_ref[0, r, j * sub:(j + 1) * sub, :]
        qs = jnp.concatenate([jnp.where(head_of == h, q, jnp.zeros_like(q))
                              for h in range(DIL_HEADS)], axis=0)
        s = _dot_nt(qs, keys[r][j * sub:(j + 2) * sub]).reshape(DIL_HEADS, sub, 2 * sub)
        scores.append(jnp.where((band0 if j == 0 else band)[None], s, NEG))
    stats = []
    for s in scores:
        m = jnp.max(s, axis=-1, keepdims=True)
        e = jnp.exp2(s - m)
        stats.append((m, e, jnp.sum(e, axis=-1, keepdims=True)))
    for (r, j), (m, e, l) in zip(tiles, stats):
        o = _dot(e.astype(BF16).reshape(DIL_HEADS * sub, 2 * sub), vals[r][j * sub:(j + 2) * sub])
        o = o.reshape(DIL_HEADS, sub, DIL_WIDTH) / l
        lse = m * LN2 + jnp.log(l)
        o_acc = jnp.zeros((sub, DIL_WIDTH), F32)
        lse_out = jnp.zeros((sub, LANES), F32)
        for h in range(DIL_HEADS):
            o_acc = jnp.where(head_of == h, o[h], o_acc)
            lse_out = jnp.where(lane == h, lse[h], lse_out)
        o_ref[0, r, j * sub:(j + 1) * sub, :] = o_acc
        lse_ref[0, r, j * sub:(j + 1) * sub, :] = lse_out


def _dilated(arr, cols, window, dilation, name):
    B, d, L, _ = arr.shape
    n_back = window // dilation
    tq = min(L, DIL_ROWS)
    rb = DIL_ROWS // tq
    assert d == dilation and n_back <= DIL_SUB and L % tq == 0 and d % rb == 0
    per = tq // DIL_SUB
    qc, kc, vc = cols
    cur = lambda c: pl.BlockSpec((1, rb, tq, DIL_WIDTH), lambda b, r, i: (b, r, i, c))
    prev = lambda c: pl.BlockSpec((1, rb, DIL_SUB, DIL_WIDTH),
                                  lambda b, r, i: (b, r, jnp.maximum(i * per - 1, 0), c))
    return pl.pallas_call(
        functools.partial(_dil_body, n_back=n_back),
        grid=(B, dilation // rb, L // tq),
        in_specs=[cur(qc), prev(kc), cur(kc), prev(vc), cur(vc)],
        out_specs=[pl.BlockSpec((1, rb, tq, DIL_WIDTH), lambda b, r, i: (b, r, i, 0)),
                   pl.BlockSpec((1, rb, tq, LANES), lambda b, r, i: (b, r, i, 0))],
        out_shape=[jax.ShapeDtypeStruct((B, dilation, L, DIL_WIDTH), F32),
                   jax.ShapeDtypeStruct((B, dilation, L, LANES), F32)],
        compiler_params=_params(),
        name=name,
    )(arr, arr, arr, arr, arr)


def _mixout_body(oa_ref, ob0_ref, ob1_ref, ob2_ref, l0_ref, l1_ref, l2_ref, mg_ref, x_ref,
                 pa_ref, pb_ref, wo_ref, out_ref, so_ref, sl_ref):
    tm = x_ref.shape[0]

    def interleaved(src_ref, st_ref):
        d = src_ref.shape[1]
        if d == 1:
            return src_ref[0, 0]
        slabs = src_ref.shape[3] // LANES
        for r in range(d):
            for k in range(slabs):
                st_ref[k, pl.ds(r, tm // d, stride=d), :] = src_ref[0, r, :, k * LANES:(k + 1) * LANES]
        return jnp.concatenate([st_ref[k] for k in range(slabs)], axis=1)

    lses = [interleaved(l, sl_ref) for l in (l0_ref, l1_ref, l2_ref)]
    mx = jnp.maximum(jnp.maximum(lses[0], lses[1]), lses[2])
    ws = [jnp.exp(l - mx) for l in lses]
    den = ws[0] + ws[1] + ws[2]
    ob = jnp.zeros((tm, DIL_WIDTH), F32)
    for w, o_ref in zip(ws, (ob0_ref, ob1_ref, ob2_ref)):
        alpha = w / den
        wide = jnp.concatenate(
            [jnp.broadcast_to(alpha[:, h:h + 1], (tm, HEAD_DIM)) for h in range(DIL_HEADS)], axis=1)
        ob = ob + wide * interleaved(o_ref, so_ref)
    ya = _dot(oa_ref[...], pa_ref[0])
    yb = _dot(ob.astype(BF16), pb_ref[0])
    gm = jax.nn.sigmoid(mg_ref[...].astype(F32))
    y = gm[:, :D_MODEL] * ya + gm[:, D_MODEL:] * yb
    out_ref[...] = x_ref[...] + _dot(y.astype(BF16), wo_ref[0])


def _mixout(oa, obs, lses, proj, x2, pa, pb, wo, layer, S, tm=512):
    T = x2.shape[0]
    per_b = S // tm
    row = lambda w: pl.BlockSpec((tm, w), lambda i: (i, 0))
    folded = lambda a: pl.BlockSpec((1, a.shape[1], tm // a.shape[1], a.shape[3]),
                                    lambda i: (i // per_b, 0, i % per_b, 0))
    return pl.pallas_call(
        _mixout_body,
        grid=(T // tm,),
        in_specs=[row(NSA_HEADS * HEAD_DIM), *[folded(a) for a in obs], *[folded(a) for a in lses],
                  pl.BlockSpec((tm, 2 * D_MODEL), lambda i: (i, COL_MERGE // (2 * D_MODEL))),
                  row(D_MODEL),
                  _layer_spec(pa, layer), _layer_spec(pb, layer), _layer_spec(wo, layer)],
        out_specs=row(D_MODEL),
        out_shape=jax.ShapeDtypeStruct((T, D_MODEL), F32),
        scratch_shapes=[pltpu.VMEM((DIL_WIDTH // LANES, tm, LANES), F32),
                        pltpu.VMEM((1, tm, LANES), F32)],
        compiler_params=_params(),
        name="mixout",
    )(oa, *obs, *lses, proj, x2, pa, pb, wo)


FF_CHUNK = 512


def _ffn_body(x_ref, g_ref, w1_ref, w3_ref, w2_ref, o_ref):
    x = x_ref[...]
    h = _rms(x, g_ref[...]).astype(BF16)
    acc = jnp.zeros(x.shape, F32)
    for c0 in range(0, D_FF, FF_CHUNK):
        c1 = min(c0 + FF_CHUNK, D_FF)
        act = _silu(_dot(h, w1_ref[:, c0:c1])) * _dot(h, w3_ref[:, c0:c1])
        acc = acc + _dot(act.astype(BF16), w2_ref[c0:c1, :])
    o_ref[...] = x + acc


def _ffn(x2, g, w1, w3, w2, tm=512):
    T = x2.shape[0]
    return pl.pallas_call(
        _ffn_body,
        grid=(T // tm,),
        in_specs=[pl.BlockSpec((tm, D_MODEL), lambda i: (i, 0)), _const_spec(g.shape),
                  _const_spec(w1.shape), _const_spec(w3.shape), _const_spec(w2.shape)],
        out_specs=pl.BlockSpec((tm, D_MODEL), lambda i: (i, 0)),
        out_shape=jax.ShapeDtypeStruct((T, D_MODEL), F32),
        compiler_params=_params(),
        name="ffn",
    )(x2, g, w1, w3, w2)


MOE_TM = 1024
MOE_CHUNK = 128
MOE_BIG = 2 * MOE_CHUNK
MOE_MERGED = 4


def _router_body(x_ref, g_ref, r_ref, h_ref, combw_ref, posw_ref, post_ref, cnt_ref):
    tm = x_ref.shape[0]
    h = _rms(x_ref[...], g_ref[...])
    h_ref[...] = h.astype(BF16)
    r = r_ref[...]
    h_hi, r_hi = h.astype(BF16), r.astype(BF16)
    h_lo = (h - h_hi.astype(F32)).astype(BF16)
    r_lo = (r - r_hi.astype(F32)).astype(BF16)
    logits = _dot(h_hi, r_hi) + (_dot(h_hi, r_lo) + _dot(h_lo, r_hi))
    lane = lax.broadcasted_iota(jnp.int32, logits.shape, 1)
    lg = jnp.where(lane < N_EXPERTS, logits, NEG)
    m1 = jnp.max(lg, axis=-1, keepdims=True)
    i1 = jnp.min(jnp.where(lg == m1, lane, LANES), axis=-1, keepdims=True)
    lg2 = jnp.where(lane == i1, NEG, lg)
    m2 = jnp.max(lg2, axis=-1, keepdims=True)
    i2 = jnp.min(jnp.where(lg2 == m2, lane, LANES), axis=-1, keepdims=True)
    e2 = jnp.exp(m2 - m1)
    den = 1.0 + e2
    w_first, w_second = 1.0 / den, e2 / den
    chosen = [jnp.broadcast_to((i1 == ex) | (i2 == ex), (tm, LANES)) for ex in range(N_EXPERTS)]
    ones = jnp.concatenate([jnp.where(c, 1.0, 0.0).astype(BF16) for c in chosen], axis=1)
    row = lax.broadcasted_iota(jnp.int32, (LANES, LANES), 0)
    col = lax.broadcasted_iota(jnp.int32, (LANES, LANES), 1)
    tri = jnp.where(col < row, 1.0, 0.0).astype(BF16)
    running = jnp.zeros((1, N_EXPERTS * LANES), F32)
    parts = []
    for b in range(tm // LANES):
        blk = ones[b * LANES:(b + 1) * LANES]
        parts.append(_dot(tri, blk) + running)
        running = running + jnp.sum(blk.astype(F32), axis=0, keepdims=True)
    before = jnp.concatenate(parts, axis=0)
    by_lane = jnp.zeros((tm, LANES), F32)
    cnt = jnp.zeros((1, LANES), F32)
    for ex in range(N_EXPERTS):
        slab = slice(ex * LANES, (ex + 1) * LANES)
        combw_ref[:, slab] = jnp.broadcast_to(
            jnp.where(i1 == ex, w_first, 0.0) + jnp.where(i2 == ex, w_second, 0.0), (tm, LANES))
        pos = jnp.where(chosen[ex], before[:, slab], -1.0)
        posw_ref[:, slab] = pos
        by_lane = jnp.where(lane == ex, pos, by_lane)
        cnt = jnp.where(lane[:1] == ex, running[:, slab], cnt)
    post_ref[0] = by_lane.T[:N_EXPERTS]
    cnt_ref[0] = jnp.broadcast_to(cnt, (8, LANES)).astype(jnp.int32)


def _router(x2, g, r, tm=MOE_TM):
    T = x2.shape[0]
    nt = T // tm
    row = lambda w: pl.BlockSpec((tm, w), lambda i: (i, 0))
    return pl.pallas_call(
        _router_body,
        grid=(nt,),
        in_specs=[row(D_MODEL), _const_spec(g.shape), _const_spec(r.shape)],
        out_specs=[row(D_MODEL), row(N_EXPERTS * LANES), row(N_EXPERTS * LANES),
                   pl.BlockSpec((1, N_EXPERTS, tm), lambda i: (i, 0, 0)),
                   pl.BlockSpec((1, 8, LANES), lambda i: (i, 0, 0))],
        out_shape=[jax.ShapeDtypeStruct((T, D_MODEL), BF16),
                   jax.ShapeDtypeStruct((T, N_EXPERTS * LANES), F32),
                   jax.ShapeDtypeStruct((T, N_EXPERTS * LANES), F32),
                   jax.ShapeDtypeStruct((nt, N_EXPERTS, tm), F32),
                   jax.ShapeDtypeStruct((nt, 8, LANES), jnp.int32)],
        compiler_params=_params(),
        name="router",
    )(x2, g, r)


def _moe_body(cnt_ref, x_ref, h_ref, comb_ref, pos_ref, post_ref, w1_ref, w3_ref, w2_ref, fn_ref,
              o_ref, y_ref):
    i = pl.program_id(0)
    e = pl.program_id(1)
    merged = MOE_MERGED * LANES
    n = cnt_ref[i * N_EXPERTS + e]
    n_big = (n + (MOE_BIG - MOE_CHUNK - 1)) // MOE_BIG
    rest_row = pl.multiple_of(n_big * MOE_BIG, MOE_BIG)
    has_rest = n > rest_row
    post_e = post_ref[0, pl.ds(e, 1), :]

    @pl.when(e == 0)
    def _():
        o_ref[...] = x_ref[...]

    @pl.when((i == 0) & (e == 0))
    def _():
        y_ref[...] = jnp.zeros_like(y_ref)

    def scatter_add(first_row, first_slot, n_slabs):
        lane_slot = lax.broadcasted_iota(jnp.int32, (1, LANES), 1).astype(F32)
        hot = [jnp.where(pos_ref[...] == first_slot + (k * LANES) + lane_slot, 1.0, 0.0).astype(BF16)
               for k in range(n_slabs)]
        scatter = hot[0] if n_slabs == 1 else jnp.concatenate(hot, axis=1)
        weight = jnp.concatenate([comb_ref[...]] * (D_MODEL // LANES), axis=1)
        ys = y_ref[pl.ds(first_row, n_slabs * LANES), :].astype(BF16)
        o_ref[...] += weight * _dot(scatter, ys)

    def chunk(r0, size):
        base = r0.astype(F32)
        slot_col = base + lax.broadcasted_iota(jnp.int32, (size, 1), 0).astype(F32)
        gather = jnp.where(post_e == slot_col, 1.0, 0.0).astype(BF16)
        xe = _dot(gather, h_ref[...]).astype(BF16)
        act = _silu(_dot(xe, w1_ref[0])) * _dot(xe, w3_ref[0])
        y = _dot(act.astype(BF16), w2_ref[0])

        @pl.when(r0 < merged)
        def _():
            y_ref[pl.ds(r0, size), :] = y

        @pl.when(r0 >= merged)
        def _():
            y_ref[merged:merged + size, :] = y
            for k in range(size // LANES):
                scatter_add(merged + k * LANES, base + float(k * LANES), 1)

    def big_chunk(c, carry):
        chunk(pl.multiple_of(c * MOE_BIG, MOE_BIG), MOE_BIG)
        return carry

    lax.fori_loop(0, n_big, big_chunk, 0)

    @pl.when(has_rest)
    def _():
        chunk(rest_row, MOE_CHUNK)

    @pl.when(n > 0)
    def _():
        scatter_add(0, 0.0, MOE_MERGED)

    @pl.when(e == pl.num_programs(1) - 1)
    def _():
        o_ref[...] = _rms(o_ref[...], fn_ref[...])


def _moe(x2, h, comb, pos, post, counts, w1, w3, w2, fn, tm=MOE_TM):
    T = x2.shape[0]
    once = lambda w: pl.BlockSpec((tm, w), lambda i, e, cnt: (i, 0), pipeline_mode=pl.Buffered(1))
    tile = lambda w: pl.BlockSpec((tm, w), lambda i, e, cnt: (i, 0))
    assert MOE_CHUNK == LANES
    slab = pl.BlockSpec((tm, LANES), lambda i, e, cnt: (i, e))
    expert = lambda a: pl.BlockSpec((1,) + a.shape[1:], lambda i, e, cnt: (e, 0, 0))
    grid_spec = pltpu.PrefetchScalarGridSpec(
        num_scalar_prefetch=1,
        grid=(T // tm, N_EXPERTS),
        in_specs=[once(D_MODEL), once(D_MODEL), slab, slab,
                  pl.BlockSpec((1, N_EXPERTS, tm), lambda i, e, cnt: (i, 0, 0)),
                  expert(w1), expert(w3), expert(w2),
                  pl.BlockSpec(fn.shape, lambda i, e, cnt: (0, 0))],
        out_specs=tile(D_MODEL),
        scratch_shapes=[pltpu.VMEM((MOE_MERGED * LANES + MOE_BIG, D_MODEL), F32)],
    )
    return pl.pallas_call(
        _moe_body,
        grid_spec=grid_spec,
        out_shape=jax.ShapeDtypeStruct((T, D_MODEL), F32),
        compiler_params=_params(),
        name="moe",
    )(counts, x2, h, comb, pos, post, w1, w3, w2, fn)


def _rope_lane_tables(positions):
    half = ROT_DIM // 2
    inv = ROPE_THETA ** (-jnp.arange(0, ROT_DIM, 2, dtype=F32) / ROT_DIM)
    ang = positions.astype(F32).reshape(-1, 1) * inv
    cos, sin = jnp.cos(ang), jnp.sin(ang)
    lane = np.arange(LANES)
    within = lane % HEAD_DIM
    pick = lane % half
    cos_l, sin_l = cos[:, pick], sin[:, pick]
    c = jnp.where(within < ROT_DIM, cos_l, 1.0)
    sa = jnp.where(within < half, -sin_l, 0.0)
    sb = jnp.where((within >= half) & (within < ROT_DIM), sin_l, 0.0)
    return c, sa, sb


def _w_in_plan():
    scale = HEAD_DIM ** -0.5 * LOG2E
    qa, kv, gate, qkv_b, merge = 0, 512, 1280, 1304, 3608
    nb = N_DIL_GROUPS * DIL_WIDTH
    kv_piece = lambda j: [(kv + LANES * j, LANES, 1.0)]
    span = lambda start, width, s=1.0: [(start + c, LANES, s) for c in range(0, width, LANES)]
    plan = (span(qa, 512, scale)
            + kv_piece(2) + kv_piece(4) + kv_piece(0)
            + kv_piece(3) + kv_piece(5)
            + [(gate, 3 * NSA_HEADS, 1.0)]
            + kv_piece(1)
            + span(merge, 2 * D_MODEL)
            + span(qkv_b, nb, scale) + span(qkv_b + nb, nb) + span(qkv_b + 2 * nb, nb))
    assert len(plan) * LANES == N_W_IN
    return plan


def _w_in_body(w_ref, o_ref):
    lane = lax.broadcasted_iota(jnp.int32, (w_ref.shape[1], LANES), 1)
    for j, (src, width, scale) in enumerate(_w_in_plan()):
        if width == LANES:
            slab = w_ref[0, :, src:src + LANES]
        else:
            full = w_ref[0, :, src:src + LANES]
            slab = jnp.where(lane < width, full, 0.0)
        if scale != 1.0:
            slab = slab * scale
        o_ref[0, :, j * LANES:(j + 1) * LANES] = slab.astype(BF16)


def _permute_w_in(w, rows=256):
    layers, d, n = w.shape
    return pl.pallas_call(
        _w_in_body,
        grid=(layers, d // rows),
        in_specs=[pl.BlockSpec((1, rows, n), lambda l, i: (l, i, 0))],
        out_specs=pl.BlockSpec((1, rows, N_W_IN), lambda l, i: (l, i, 0)),
        out_shape=jax.ShapeDtypeStruct((layers, d, N_W_IN), BF16),
        compiler_params=_params(),
        name="w_in_layout",
    )(w)


def _importance_matrix_t(seq, ncp):
    n_c = (seq - CMP_LEN) // CMP_STRIDE + 1
    starts = np.arange(n_c) * CMP_STRIDE
    bstart = np.arange(seq // SEL_LEN) * SEL_LEN
    overlap = np.clip(np.minimum(starts[:, None] + CMP_LEN, bstart[None, :] + SEL_LEN)
                      - np.maximum(starts[:, None], bstart[None, :]), 0, None)
    m = np.zeros((ncp, seq // SEL_LEN), np.float32)
    m[:n_c] = overlap.astype(np.float32) / CMP_LEN
    return jnp.asarray(m.T)


def _mixer(x2, layer, B, S, tables, mt, prm):
    proj, *folds, a = _inproj(x2, prm["norm_mix"], prm["w_in"], layer, *tables, B, S)
    proj3 = proj.reshape(B, S, N_PROJ)
    ncp = S // CMP_STRIDE
    cmp = _compress(a.reshape(2, B * NSA_KV_HEADS * ncp, CMP_STRIDE * HEAD_DIM),
                    prm["cmp_w1"], prm["cmp_w2"], prm["cmp_pos"], layer, B)
    oa = _nsa(proj3, cmp[0], cmp[1], mt).reshape(B * S, NSA_HEADS * HEAD_DIM)
    obs, lses = [], []
    for g, ((w, d), arr) in enumerate(zip(DIL_PATTERNS, folds)):
        o, lse = _dilated(arr, (0, 1, 2), w, d, f"dilated{g}")
        obs.append(o)
        lses.append(lse)
    return _mixout(oa, obs, lses, proj, x2, prm["p_a"], prm["p_b"], prm["w_o"], layer, S)


def kernel(x, positions, norm_mix, w_in, cmp_pos_k, cmp_pos_v, cmp_k_w1, cmp_k_w2, cmp_v_w1,
           cmp_v_w2, w_branch_a, w_branch_b, w_out, norm_ffn, ffn_w1, ffn_w3, ffn_w2, router,
           moe_w1, moe_w3, moe_w2, final_norm):
    B, S, D = x.shape
    depth = norm_mix.shape[0]
    assert depth == 2 and D == D_MODEL
    tables = _rope_lane_tables(positions)
    mt = _importance_matrix_t(S, S // CMP_STRIDE)
    cmp_pos = jnp.stack([cmp_pos_k, cmp_pos_v], axis=1).reshape(depth, 2, 1, CMP_LEN * HEAD_DIM)
    prm = {
        "norm_mix": norm_mix.reshape(depth, 1, D),
        "w_in": _permute_w_in(w_in),
        "cmp_w1": jnp.stack([cmp_k_w1, cmp_v_w1], axis=1).astype(BF16),
        "cmp_w2": jnp.stack([cmp_k_w2, cmp_v_w2], axis=1).astype(BF16),
        "cmp_pos": jnp.broadcast_to(cmp_pos, (depth, 2, 8, CMP_LEN * HEAD_DIM)).astype(BF16),
        "p_a": w_branch_a.astype(BF16), "p_b": w_branch_b.astype(BF16), "w_o": w_out.astype(BF16),
    }
    x2 = x.reshape(B * S, D)
    x2 = _mixer(x2, 0, B, S, tables, mt, prm)
    x2 = _ffn(x2, norm_ffn[0].reshape(1, -1), ffn_w1[0].astype(BF16), ffn_w3[0].astype(BF16),
              ffn_w2[0].astype(BF16))
    x2 = _mixer(x2, 1, B, S, tables, mt, prm)
    g1 = norm_ffn[1].reshape(1, -1)
    r = jnp.pad(router[0], ((0, 0), (0, LANES - N_EXPERTS)))
    h, comb, pos, post, cnt = _router(x2, g1, r)
    counts = cnt[:, 0, :N_EXPERTS].reshape(-1)
    out = _moe(x2, h, comb, pos, post, counts, moe_w1[0].astype(BF16), moe_w3[0].astype(BF16),
               moe_w2[0].astype(BF16), final_norm.reshape(1, -1))
    return out.reshape(B, S, D)
```

```python
import functools

import numpy as np
import jax
import jax.numpy as jnp
from jax import lax
from jax.experimental import pallas as pl
from jax.experimental.pallas import tpu as pltpu

F32 = jnp.float32
BF16 = jnp.bfloat16

D_MODEL = 1024
HEAD_DIM = 64
ROT_DIM = HEAD_DIM // 4
ROPE_THETA = 500000.0
EPS = 1e-6
NSA_HEADS = 8
NSA_KV_HEADS = 2
HEADS_PER_KV = NSA_HEADS // NSA_KV_HEADS
CMP_LEN = 32
CMP_STRIDE = 16
CMP_HIDDEN = 128
SEL_LEN = 64
N_SEL = 16
WIN = 512
DIL_PATTERNS = ((128, 1), (512, 4), (2048, 16))
N_DIL_GROUPS = 3
DIL_HEADS = 4
D_FF = 2816
N_EXPERTS = 8

LANES = 128
VMEM_LIMIT = 56 * 1024 * 1024
NEG = -1e30
BIG = 1e30

COL_MERGE = 0
COL_QA = 2048
COL_KSEL = 2560
COL_BLK = COL_KSEL + 128
COL_KWIN = 2816
COL_VSEL = 2944
COL_VWIN = 3072
COL_GATE = 3200
N_PROJ = 3328
DIL_WIDTH = DIL_HEADS * HEAD_DIM
N_FOLD = N_DIL_GROUPS * DIL_WIDTH
STAGE = None
IN_CHUNKS = (
    (512, True, tuple(COL_QA + 128 * j for j in range(4)), None),
    (384, True, (COL_KSEL, COL_KWIN, STAGE), ("cmp", 0)),
    (512, False, (COL_VSEL, COL_VWIN, COL_GATE, STAGE), ("cmp", 1)),
    (512, False, tuple(COL_MERGE + 128 * j for j in range(0, 4)), None),
    (512, False, tuple(COL_MERGE + 128 * j for j in range(4, 8)), None),
    (512, False, tuple(COL_MERGE + 128 * j for j in range(8, 12)), None),
    (512, False, tuple(COL_MERGE + 128 * j for j in range(12, 16)), None),
    (N_FOLD, True, (STAGE,) * 6, ("fold", 0)),
    (N_FOLD, True, (STAGE,) * 6, ("fold", 1)),
    (N_FOLD, False, (STAGE,) * 6, ("fold", 2)),
)
N_W_IN = sum(c[0] for c in IN_CHUNKS)
LOG2E = 1.4426950408889634
LN2 = 0.6931471805599453
MASK_BIAS = -(2.0 ** 100)
SEL_BLOCKS_MAX = 32


def _dot(a, b, precision=None):
    return jnp.dot(a, b, preferred_element_type=F32, precision=precision)


def _dot_nt(a, b, precision=None):
    return lax.dot_general(a, b, (((1,), (1,)), ((), ())), preferred_element_type=F32,
                           precision=precision)


def _rms(x, g):
    ms = jnp.mean(x * x, axis=-1, keepdims=True)
    return x * lax.rsqrt(ms + EPS) * g


def _silu(x):
    return x * jax.nn.sigmoid(x)


def _params(**kw):
    return pltpu.CompilerParams(vmem_limit_bytes=VMEM_LIMIT, **kw)


def _const_spec(shape):
    nd = len(shape)
    return pl.BlockSpec(shape, lambda *_: (0,) * nd)


def _inproj_body(x_ref, g_ref, w_ref, c_ref, sa_ref, sb_ref, o_ref, f0_ref, f1_ref, f2_ref, a_ref,
                 st_ref, *, per_b):
    tm = x_ref.shape[0]
    t_seq = (pl.program_id(0) % per_b) * tm + lax.broadcasted_iota(jnp.int32, (tm, LANES), 0)
    lane = lax.broadcasted_iota(jnp.int32, (tm, LANES), 1)
    blk = lax.shift_right_logical(t_seq, 6)
    hot = (lane == blk) | (lane == blk + SEL_BLOCKS_MAX)
    o_ref[:, COL_BLK:COL_BLK + LANES] = jnp.where(hot, 1.0, 0.0).astype(BF16)
    h = _rms(x_ref[...], g_ref[0]).astype(BF16)
    c = c_ref[...]
    sa = sa_ref[...]
    sb = sb_ref[...]
    start = 0
    for size, rope, dests, action in IN_CHUNKS:
        acc = _dot(h, w_ref[0, :, start:start + size])
        start += size
        for j, dest in enumerate(dests):
            a = acc[:, j * LANES:(j + 1) * LANES]
            if rope:
                a = a * c + pltpu.roll(a, LANES - 8, 1) * sa + pltpu.roll(a, 8, 1) * sb
            if dest is STAGE:
                st_ref[j] = a
            else:
                o_ref[:, dest:dest + LANES] = a.astype(BF16)
        if action is None:
            continue
        kind, piece = action
        if kind == "fold":
            slabs = DIL_WIDTH // LANES
            for gi, f_ref in enumerate((f0_ref, f1_ref, f2_ref)):
                d = DIL_PATTERNS[gi][1]
                for r in range(d):
                    for k in range(slabs):
                        rows = st_ref[gi * slabs + k, pl.ds(r, tm // d, stride=d), :]
                        c0 = piece * DIL_WIDTH + k * LANES
                        f_ref[0, r, :, c0:c0 + LANES] = rows.astype(BF16)
        else:
            slab = dests.index(STAGE)
            nrow = tm // CMP_STRIDE
            toks = [st_ref[slab, pl.ds(j, nrow, stride=CMP_STRIDE), :] for j in range(CMP_STRIDE)]
            for g in range(NSA_KV_HEADS):
                head = slice(g * HEAD_DIM, (g + 1) * HEAD_DIM)
                for m in range(CMP_STRIDE // 2):
                    pair = jnp.concatenate([toks[2 * m][:, head], toks[2 * m + 1][:, head]], axis=1)
                    a_ref[piece, 0, g, :, m * LANES:(m + 1) * LANES] = pair.astype(BF16)


def _layer_spec(arr, layer):
    nd = arr.ndim
    return pl.BlockSpec((1,) + arr.shape[1:], lambda *_: (layer,) + (0,) * (nd - 1))


def _inproj(x2, g, w, layer, rc, rsa, rsb, B, S, tm=512):
    T = x2.shape[0]
    per_b = S // tm
    dils = [d for _, d in DIL_PATTERNS]
    fold_spec = lambda d: pl.BlockSpec((1, d, tm // d, 3 * DIL_WIDTH),
                                       lambda i: (i // per_b, 0, i % per_b, 0))
    fold_shape = lambda d: jax.ShapeDtypeStruct((B, d, S // d, 3 * DIL_WIDTH), BF16)
    cmp_w = CMP_STRIDE * HEAD_DIM
    assert S // SEL_LEN <= SEL_BLOCKS_MAX
    return pl.pallas_call(
        functools.partial(_inproj_body, per_b=per_b),
        grid=(T // tm,),
        in_specs=[
            pl.BlockSpec((tm, D_MODEL), lambda i: (i, 0)),
            _layer_spec(g, layer), _layer_spec(w, layer),
            pl.BlockSpec((tm, LANES), lambda i: (i, 0)),
            pl.BlockSpec((tm, LANES), lambda i: (i, 0)),
            pl.BlockSpec((tm, LANES), lambda i: (i, 0)),
        ],
        out_specs=[pl.BlockSpec((tm, N_PROJ), lambda i: (i, 0)), *[fold_spec(d) for d in dils],
                   pl.BlockSpec((2, 1, NSA_KV_HEADS, tm // CMP_STRIDE, cmp_w),
                                lambda i: (0, i // per_b, 0, i % per_b, 0))],
        out_shape=[jax.ShapeDtypeStruct((T, N_PROJ), BF16), *[fold_shape(d) for d in dils],
                   jax.ShapeDtypeStruct((2, B, NSA_KV_HEADS, S // CMP_STRIDE, cmp_w), BF16)],
        scratch_shapes=[pltpu.VMEM((N_FOLD // LANES, tm, LANES), F32)],
        compiler_params=_params(),
        name="inproj",
    )(x2, g, w, rc, rsa, rsb)


def _compress_body(a_ref, w1_ref, w2_ref, pos_ref, o_ref):
    nb, ncp = o_ref.shape[1], o_ref.shape[2]
    a = a_ref[0]
    w1 = w1_ref[0, 0]
    half = CMP_STRIDE * HEAD_DIM
    top = _dot(a, w1[:half])
    bot = _dot(a, w1[half:])
    pc = _dot(pos_ref[0, 0], w1)
    rows = a.shape[0]
    hid = top + pltpu.roll(bot, rows - 1, 0) + pc[0:1]
    out = _dot(_silu(hid).astype(BF16), w2_ref[0, 0])
    for b in range(nb):
        heads = [out[(b * NSA_KV_HEADS + g) * ncp:(b * NSA_KV_HEADS + g + 1) * ncp]
                 for g in range(NSA_KV_HEADS)]
        o_ref[0, b] = jnp.concatenate(heads, axis=1).astype(BF16)


def _compress(a, w1, w2, pos, layer, B):
    n, rows, _ = a.shape
    ncp = rows // (B * NSA_KV_HEADS)
    per_kv = lambda arr: pl.BlockSpec((1, 1) + arr.shape[2:], lambda i: (layer, i, 0, 0))
    return pl.pallas_call(
        _compress_body,
        grid=(n,),
        in_specs=[pl.BlockSpec((1, rows, CMP_STRIDE * HEAD_DIM), lambda i: (i, 0, 0)),
                  per_kv(w1), per_kv(w2), per_kv(pos)],
        out_specs=pl.BlockSpec((1, B, ncp, NSA_KV_HEADS * HEAD_DIM), lambda i: (i, 0, 0, 0)),
        out_shape=jax.ShapeDtypeStruct((n, B, ncp, NSA_KV_HEADS * HEAD_DIM), BF16),
        compiler_params=_params(),
        name="compress",
    )(a, w1, w2, pos)


def _softmax2(s):
    m = jnp.max(s, axis=-1, keepdims=True)
    e = jnp.exp2(s - m)
    return e.astype(BF16), jnp.sum(e, axis=-1, keepdims=True)


def _weighted_values(e, l, v):
    nh, tq, nk = e.shape
    return _dot(e.reshape(nh * tq, nk), v) / l.reshape(nh * tq, 1)


SEL_PREFIX = 512


def _nsa_body(q_ref, kc_ref, vc_ref, ksel_ref, vsel_ref, kwin_ref, vwin_ref, gate_ref, mt_ref,
              o_ref, osel_ref, *, tq, seq):
    nblk = seq // SEL_LEN
    ncp = kc_ref.shape[1]
    n_cmp = (seq - CMP_LEN) // CMP_STRIDE + 1
    q0 = pl.program_id(1) * tq
    q = q_ref[0]
    t_col = q0 + lax.broadcasted_iota(jnp.int32, (tq, 1), 0)
    t_row = q0 + lax.broadcasted_iota(jnp.int32, (1, tq), 1)
    gates = jax.nn.sigmoid(gate_ref[0].astype(F32))

    zeros64 = jnp.zeros((tq, HEAD_DIM), BF16)

    def stacked_q(g):
        parts = []
        for hh in range(HEADS_PER_KV):
            h = g * HEADS_PER_KV + hh
            qh = q[:, h * HEAD_DIM:(h + 1) * HEAD_DIM]
            parts.append(jnp.concatenate([qh, zeros64] if g == 0 else [zeros64, qh], axis=1))
        return jnp.concatenate(parts, axis=0)

    qs = [stacked_q(g) for g in range(NSA_KV_HEADS)]

    cidx = lax.broadcasted_iota(jnp.int32, (tq, ncp), 1)
    cmask = ((cidx * CMP_STRIDE + (CMP_LEN - 1)) <= t_col) & (cidx < n_cmp)
    jidx = lax.broadcasted_iota(jnp.int32, (nblk, tq), 0)
    cur = lax.shift_right_logical(t_row, 6)
    forced = (jidx == 0) | (jidx == cur) | (jidx == cur - 1)
    future = jidx > cur
    groups = range(NSA_KV_HEADS)
    span = WIN + tq
    ks = pl.multiple_of(jnp.maximum(q0 - WIN, 0), tq)
    kw = kwin_ref[0, pl.ds(ks, span), :]
    vw = vwin_ref[0, pl.ds(ks, span), :]
    wpos = ks + lax.broadcasted_iota(jnp.int32, (1, span), 1)
    wmask = (wpos <= t_col) & (t_col - wpos <= WIN - 1)

    s_cmp = [jnp.where(cmask[None], _dot_nt(qs[g], kc_ref[0]).reshape(HEADS_PER_KV, tq, ncp), NEG)
             for g in groups]
    s_win = [jnp.where(wmask[None], _dot_nt(qs[g], kw).reshape(HEADS_PER_KV, tq, span), NEG)
             for g in groups]
    p_cmp = []
    for g in groups:
        m = jnp.max(s_cmp[g], axis=-1, keepdims=True)
        e = jnp.where(cmask[None], jnp.exp2(s_cmp[g] - m), 0.0)
        den = jnp.sum(e, axis=-1, keepdims=True)
        p_cmp.append(e / jnp.where(den > 0, den, 1.0))
    e_win = [_softmax2(s_win[g]) for g in groups]
    o_cmp = [_dot(p_cmp[g].astype(BF16).reshape(HEADS_PER_KV * tq, ncp), vc_ref[0]) for g in groups]
    imps = [_dot_nt(mt_ref[...], p_cmp[g][0] + p_cmp[g][1] + p_cmp[g][2] + p_cmp[g][3],
                    precision=lax.Precision.HIGHEST) for g in groups]
    o_win = [_weighted_values(*e_win[g], vw) for g in groups]

    first_blk = lax.shift_right_logical(q0, 6)
    picked_rows, before_rows = [], []
    for g in groups:
        imp = jnp.where(forced, BIG, imps[g])
        imp = jnp.where(future, -BIG, imp)
        rank = jnp.zeros((nblk, tq), jnp.int32)
        for i in range(nblk):
            row = imp[i:i + 1, :]
            beats = (row > imp) | ((row == imp) & (jidx > i))
            rank = rank + beats.astype(jnp.int32)
        picked_rows.append(jnp.where(rank < N_SEL, 0.0, MASK_BIAS))
        before_rows.append(jnp.where((rank < N_SEL) & (jidx < first_blk), 0.0, MASK_BIAS))
    assert 2 * NSA_KV_HEADS * SEL_BLOCKS_MAX == LANES
    bias_t = jnp.concatenate(before_rows + picked_rows, axis=0)
    bias_main = bias_t.T
    bias_diag = pltpu.roll(bias_main, LANES // 2, 1)
    lane_group = lax.shift_right_logical(lax.broadcasted_iota(jnp.int32, (tq, LANES), 1), 5)

    def with_bias(g, bias):
        own = jnp.where(lane_group == g, bias, 0.0).astype(BF16)
        return jnp.concatenate([qs[g], jnp.concatenate([own] * HEADS_PER_KV, axis=0)], axis=1)

    qb = [with_bias(g, bias_main) for g in groups]

    kdiag = ksel_ref[0, pl.ds(pl.multiple_of(q0, tq), tq), :]
    vdiag = vsel_ref[0, pl.ds(pl.multiple_of(q0, tq), tq), :]
    tri = (lax.broadcasted_iota(jnp.int32, (tq, tq), 1) <= lax.broadcasted_iota(jnp.int32, (tq, tq), 0))
    s_diag = [jnp.where(tri[None], _dot_nt(with_bias(g, bias_diag), kdiag)
                        .reshape(HEADS_PER_KV, tq, tq), NEG) for g in groups]
    n_prefix = q0 // SEL_PREFIX + 1
    for n in range(1, seq // SEL_PREFIX + 1):
        klen = n * SEL_PREFIX

        @pl.when(n_prefix == n)
        def _(klen=klen):
            vall = jnp.concatenate([vsel_ref[0, :klen, :], vdiag], axis=0)
            s = [jnp.concatenate(
                [_dot_nt(qb[g], ksel_ref[0, :klen, :]).reshape(HEADS_PER_KV, tq, klen), s_diag[g]],
                axis=-1) for g in groups]
            ew = [_softmax2(s[g]) for g in groups]
            for g in groups:
                osel_ref[g] = _weighted_values(*ew[g], vall)

    o_sel = [osel_ref[g] for g in groups]

    outs = []
    for g in range(NSA_KV_HEADS):
        for hh in range(HEADS_PER_KV):
            h = g * HEADS_PER_KV + hh
            acc = jnp.zeros((tq, HEAD_DIM), F32)
            for br, o in enumerate((o_cmp[g], o_sel[g], o_win[g])):
                oh = o[hh * tq:(hh + 1) * tq, g * HEAD_DIM:(g + 1) * HEAD_DIM]
                acc = acc + gates[:, 3 * h + br:3 * h + br + 1] * oh
            outs.append(acc)
    o_ref[0] = jnp.concatenate(outs, axis=1).astype(BF16)


def _nsa(proj3, kc, vc, mt, tq=128):
    B, S, _ = proj3.shape
    blk = lambda c: c // LANES
    seq_spec = lambda c: pl.BlockSpec((1, S, LANES), lambda b, i: (b, 0, blk(c)))
    ncp = kc.shape[1]
    return pl.pallas_call(
        functools.partial(_nsa_body, tq=tq, seq=S),
        grid=(B, S // tq),
        in_specs=[
            pl.BlockSpec((1, tq, NSA_HEADS * HEAD_DIM),
                         lambda b, i: (b, i, COL_QA // (NSA_HEADS * HEAD_DIM))),
            pl.BlockSpec((1, ncp, LANES), lambda b, i: (b, 0, 0)),
            pl.BlockSpec((1, ncp, LANES), lambda b, i: (b, 0, 0)),
            pl.BlockSpec((1, S, 2 * LANES), lambda b, i: (b, 0, COL_KSEL // (2 * LANES))),
            seq_spec(COL_VSEL), seq_spec(COL_KWIN), seq_spec(COL_VWIN),
            pl.BlockSpec((1, tq, LANES), lambda b, i: (b, i, blk(COL_GATE))),
            _const_spec(mt.shape),
        ],
        out_specs=pl.BlockSpec((1, tq, NSA_HEADS * HEAD_DIM), lambda b, i: (b, i, 0)),
        out_shape=jax.ShapeDtypeStruct((B, S, NSA_HEADS * HEAD_DIM), BF16),
        scratch_shapes=[pltpu.VMEM((NSA_KV_HEADS, HEADS_PER_KV * tq, LANES), F32)],
        compiler_params=_params(),
        name="nsa",
    )(proj3, kc, vc, proj3, proj3, proj3, proj3, proj3, mt)


DIL_SUB = 128
DIL_ROWS = 512


def _dil_body(q_ref, kp_ref, kc_ref, vp_ref, vc_ref, o_ref, lse_ref, *, n_back):
    sub = DIL_SUB
    rb, tq = q_ref.shape[1], q_ref.shape[2]
    t0 = pl.program_id(2) * tq
    head_of = lax.shift_right_logical(lax.broadcasted_iota(jnp.int32, (sub, DIL_WIDTH), 1), 6)
    lane = lax.broadcasted_iota(jnp.int32, (sub, LANES), 1)
    diff = (sub + lax.broadcasted_iota(jnp.int32, (sub, 1), 0)
            - lax.broadcasted_iota(jnp.int32, (1, 2 * sub), 1))
    band = (diff >= 0) & (diff <= n_back)
    band0 = band & (lax.broadcasted_iota(jnp.int32, (1, 2 * sub), 1) + t0 >= sub)
    tiles = [(r, j) for r in range(rb) for j in range(tq // sub)]
    keys = {r: jnp.concatenate([kp_ref[0, r], kc_ref[0, r]], axis=0) for r in range(rb)}
    vals = {r: jnp.concatenate([vp_ref[0, r], vc_ref[0, r]], axis=0) for r in range(rb)}
    scores = []
    for r, j in tiles:
        q = q_ref[0, r, j * sub:(j + 1) * sub, :]
        qs = jnp.concatenate([jnp.where(head_of == h, q, jnp.zeros_like(q))
                              for h in range(DIL_HEADS)], axis=0)
        s = _dot_nt(qs, keys[r][j * sub:(j + 2) * sub]).reshape(DIL_HEADS, sub, 2 * sub)
        scores.append(jnp.where((band0 if j == 0 else band)[None], s, NEG))
    stats = []
    for s in scores:
        m = jnp.max(s, axis=-1, keepdims=True)
        e = jnp.exp2(s - m)
        stats.append((m, e, jnp.sum(e, axis=-1, keepdims=True)))
    for (r, j), (m, e, l) in zip(tiles, stats):
        o = _dot(e.astype(BF16).reshape(DIL_HEADS * sub, 2 * sub), vals[r][j * sub:(j + 2) * sub])
        o = o.reshape(DIL_HEADS, sub, DIL_WIDTH) / l
        lse = m * LN2 + jnp.log(l)
        o_acc = jnp.zeros((sub, DIL_WIDTH), F32)
        lse_out = jnp.zeros((sub, LANES), F32)
        for h in range(DIL_HEADS):
            o_acc = jnp.where(head_of == h, o[h], o_acc)
            lse_out = jnp.where(lane == h, lse[h], lse_out)
        o_ref[0, r, j * sub:(j + 1) * sub, :] = o_acc
        lse_ref[0, r, j * sub:(j + 1) * sub, :] = lse_out


def _dilated(arr, cols, window, dilation, name):
    B, d, L, _ = arr.shape
    n_back = window // dilation
    tq = min(L, DIL_ROWS)
    rb = DIL_ROWS // tq
    assert d == dilation and n_back <= DIL_SUB and L % tq == 0 and d % rb == 0
    per = tq // DIL_SUB
    qc, kc, vc = cols
    cur = lambda c: pl.BlockSpec((1, rb, tq, DIL_WIDTH), lambda b, r, i: (b, r, i, c))
    prev = lambda c: pl.BlockSpec((1, rb, DIL_SUB, DIL_WIDTH),
                                  lambda b, r, i: (b, r, jnp.maximum(i * per - 1, 0), c))
    return pl.pallas_call(
        functools.partial(_dil_body, n_back=n_back),
        grid=(B, dilation // rb, L // tq),
        in_specs=[cur(qc), prev(kc), cur(kc), prev(vc), cur(vc)],
        out_specs=[pl.BlockSpec((1, rb, tq, DIL_WIDTH), lambda b, r, i: (b, r, i, 0)),
                   pl.BlockSpec((1, rb, tq, LANES), lambda b, r, i: (b, r, i, 0))],
        out_shape=[jax.ShapeDtypeStruct((B, dilation, L, DIL_WIDTH), F32),
                   jax.ShapeDtypeStruct((B, dilation, L, LANES), F32)],
        compiler_params=_params(),
        name=name,
    )(arr, arr, arr, arr, arr)


def _mixout_body(oa_ref, ob0_ref, ob1_ref, ob2_ref, l0_ref, l1_ref, l2_ref, mg_ref, x_ref,
                 pa_ref, pb_ref, wo_ref, out_ref, so_ref, sl_ref):
    tm = x_ref.shape[0]

    def interleaved(src_ref, st_ref):
        d = src_ref.shape[1]
        if d == 1:
            return src_ref[0, 0]
        slabs = src_ref.shape[3] // LANES
        for r in range(d):
            for k in range(slabs):
                st_ref[k, pl.ds(r, tm // d, stride=d), :] = src_ref[0, r, :, k * LANES:(k + 1) * LANES]
        return jnp.concatenate([st_ref[k] for k in range(slabs)], axis=1)

    lses = [interleaved(l, sl_ref) for l in (l0_ref, l1_ref, l2_ref)]
    mx = jnp.maximum(jnp.maximum(lses[0], lses[1]), lses[2])
    ws = [jnp.exp(l - mx) for l in lses]
    den = ws[0] + ws[1] + ws[2]
    ob = jnp.zeros((tm, DIL_WIDTH), F32)
    for w, o_ref in zip(ws, (ob0_ref, ob1_ref, ob2_ref)):
        alpha = w / den
        wide = jnp.concatenate(
            [jnp.broadcast_to(alpha[:, h:h + 1], (tm, HEAD_DIM)) for h in range(DIL_HEADS)], axis=1)
        ob = ob + wide * interleaved(o_ref, so_ref)
    ya = _dot(oa_ref[...], pa_ref[0])
    yb = _dot(ob.astype(BF16), pb_ref[0])
    gm = jax.nn.sigmoid(mg_ref[...].astype(F32))
    y = gm[:, :D_MODEL] * ya + gm[:, D_MODEL:] * yb
    out_ref[...] = x_ref[...] + _dot(y.astype(BF16), wo_ref[0])


def _mixout(oa, obs, lses, proj, x2, pa, pb, wo, layer, S, tm=512):
    T = x2.shape[0]
    per_b = S // tm
    row = lambda w: pl.BlockSpec((tm, w), lambda i: (i, 0))
    folded = lambda a: pl.BlockSpec((1, a.shape[1], tm // a.shape[1], a.shape[3]),
                                    lambda i: (i // per_b, 0, i % per_b, 0))
    return pl.pallas_call(
        _mixout_body,
        grid=(T // tm,),
        in_specs=[row(NSA_HEADS * HEAD_DIM), *[folded(a) for a in obs], *[folded(a) for a in lses],
                  pl.BlockSpec((tm, 2 * D_MODEL), lambda i: (i, COL_MERGE // (2 * D_MODEL))),
                  row(D_MODEL),
                  _layer_spec(pa, layer), _layer_spec(pb, layer), _layer_spec(wo, layer)],
        out_specs=row(D_MODEL),
        out_shape=jax.ShapeDtypeStruct((T, D_MODEL), F32),
        scratch_shapes=[pltpu.VMEM((DIL_WIDTH // LANES, tm, LANES), F32),
                        pltpu.VMEM((1, tm, LANES), F32)],
        compiler_params=_params(),
        name="mixout",
    )(oa, *obs, *lses, proj, x2, pa, pb, wo)


FF_CHUNK = 512


def _ffn_body(x_ref, g_ref, w1_ref, w3_ref, w2_ref, *rest):
    n_cast = (len(rest) - 1) // 2
    src_refs, o_ref, dst_refs = rest[:n_cast], rest[n_cast], rest[n_cast + 1:]
    x = x_ref[...]
    h = _rms(x, g_ref[...]).astype(BF16)
    acc = jnp.zeros(x.shape, F32)
    for c0 in range(0, D_FF, FF_CHUNK):
        c1 = min(c0 + FF_CHUNK, D_FF)
        act = _silu(_dot(h, w1_ref[:, c0:c1])) * _dot(h, w3_ref[:, c0:c1])
        acc = acc + _dot(act.astype(BF16), w2_ref[c0:c1, :])
    o_ref[...] = x + acc
    for src, dst in zip(src_refs, dst_refs):
        dst[...] = src[...].astype(BF16)


def _ffn(x2, g, w1, w3, w2, to_bf16=(), tm=512):
    T = x2.shape[0]
    steps = T // tm
    held = lambda a: pl.BlockSpec(a.shape, lambda i: (0, 0), pipeline_mode=pl.Buffered(1))
    sliced = lambda a: pl.BlockSpec((a.shape[0] // steps, a.shape[1]), lambda i: (i, 0))
    assert all(a.shape[0] % (16 * steps) == 0 for a in to_bf16)
    out, *copies = pl.pallas_call(
        _ffn_body,
        grid=(steps,),
        in_specs=[pl.BlockSpec((tm, D_MODEL), lambda i: (i, 0)), _const_spec(g.shape),
                  held(w1), held(w3), held(w2), *[sliced(a) for a in to_bf16]],
        out_specs=[pl.BlockSpec((tm, D_MODEL), lambda i: (i, 0)), *[sliced(a) for a in to_bf16]],
        out_shape=[jax.ShapeDtypeStruct((T, D_MODEL), F32),
                   *[jax.ShapeDtypeStruct(a.shape, BF16) for a in to_bf16]],
        compiler_params=_params(),
        name="ffn",
    )(x2, g, w1, w3, w2, *to_bf16)
    return out, copies


MOE_TM = 1024
MOE_CHUNK = 128
MOE_BIG = 2 * MOE_CHUNK
MOE_MERGED = 4


def _router_body(x_ref, g_ref, r_ref, h_ref, combw_ref, posw_ref, post_ref, cnt_ref):
    tm = x_ref.shape[0]
    h = _rms(x_ref[...], g_ref[...])
    h_ref[...] = h.astype(BF16)
    r = r_ref[...]
    h_hi, r_hi = h.astype(BF16), r.astype(BF16)
    h_lo = (h - h_hi.astype(F32)).astype(BF16)
    r_lo = (r - r_hi.astype(F32)).astype(BF16)
    logits = _dot(h_hi, r_hi) + (_dot(h_hi, r_lo) + _dot(h_lo, r_hi))
    lane = lax.broadcasted_iota(jnp.int32, logits.shape, 1)
    lg = jnp.where(lane < N_EXPERTS, logits, NEG)
    m1 = jnp.max(lg, axis=-1, keepdims=True)
    i1 = jnp.min(jnp.where(lg == m1, lane, LANES), axis=-1, keepdims=True)
    lg2 = jnp.where(lane == i1, NEG, lg)
    m2 = jnp.max(lg2, axis=-1, keepdims=True)
    i2 = jnp.min(jnp.where(lg2 == m2, lane, LANES), axis=-1, keepdims=True)
    e2 = jnp.exp(m2 - m1)
    den = 1.0 + e2
    w_first, w_second = 1.0 / den, e2 / den
    chosen = [jnp.broadcast_to((i1 == ex) | (i2 == ex), (tm, LANES)) for ex in range(N_EXPERTS)]
    ones = jnp.concatenate([jnp.where(c, 1.0, 0.0).astype(BF16) for c in chosen], axis=1)
    row = lax.broadcasted_iota(jnp.int32, (LANES, LANES), 0)
    col = lax.broadcasted_iota(jnp.int32, (LANES, LANES), 1)
    tri = jnp.where(col < row, 1.0, 0.0).astype(BF16)
    running = jnp.zeros((1, N_EXPERTS * LANES), F32)
    parts = []
    for b in range(tm // LANES):
        blk = ones[b * LANES:(b + 1) * LANES]
        parts.append(_dot(tri, blk) + running)
        running = running + jnp.sum(blk.astype(F32), axis=0, keepdims=True)
    before = jnp.concatenate(parts, axis=0)
    by_lane = jnp.zeros((tm, LANES), F32)
    cnt = jnp.zeros((1, LANES), F32)
    for ex in range(N_EXPERTS):
        slab = slice(ex * LANES, (ex + 1) * LANES)
        combw_ref[:, slab] = jnp.broadcast_to(
            jnp.where(i1 == ex, w_first, 0.0) + jnp.where(i2 == ex, w_second, 0.0), (tm, LANES))
        pos = jnp.where(chosen[ex], before[:, slab], -1.0)
        posw_ref[:, slab] = pos
        by_lane = jnp.where(lane == ex, pos, by_lane)
        cnt = jnp.where(lane[:1] == ex, running[:, slab], cnt)
    post_ref[0] = by_lane.T[:N_EXPERTS]
    cnt_ref[0] = jnp.broadcast_to(cnt, (8, LANES)).astype(jnp.int32)


def _router(x2, g, r, tm=MOE_TM):
    T = x2.shape[0]
    nt = T // tm
    row = lambda w: pl.BlockSpec((tm, w), lambda i: (i, 0))
    return pl.pallas_call(
        _router_body,
        grid=(nt,),
        in_specs=[row(D_MODEL), _const_spec(g.shape), _const_spec(r.shape)],
        out_specs=[row(D_MODEL), row(N_EXPERTS * LANES), row(N_EXPERTS * LANES),
                   pl.BlockSpec((1, N_EXPERTS, tm), lambda i: (i, 0, 0)),
                   pl.BlockSpec((1, 8, LANES), lambda i: (i, 0, 0))],
        out_shape=[jax.ShapeDtypeStruct((T, D_MODEL), BF16),
                   jax.ShapeDtypeStruct((T, N_EXPERTS * LANES), F32),
                   jax.ShapeDtypeStruct((T, N_EXPERTS * LANES), F32),
                   jax.ShapeDtypeStruct((nt, N_EXPERTS, tm), F32),
                   jax.ShapeDtypeStruct((nt, 8, LANES), jnp.int32)],
        compiler_params=_params(),
        name="router",
    )(x2, g, r)


def _moe_body(cnt_ref, x_ref, h_ref, comb_ref, pos_ref, post_ref, w1_ref, w3_ref, w2_ref, fn_ref,
              o_ref, y_ref):
    i = pl.program_id(0)
    e = pl.program_id(1)
    merged = MOE_MERGED * LANES
    n = cnt_ref[i * N_EXPERTS + e]
    n_big = (n + (MOE_BIG - MOE_CHUNK - 1)) // MOE_BIG
    rest_row = pl.multiple_of(n_big * MOE_BIG, MOE_BIG)
    has_rest = n > rest_row
    post_e = post_ref[0, pl.ds(e, 1), :]

    @pl.when(e == 0)
    def _():
        o_ref[...] = x_ref[...]

    @pl.when((i == 0) & (e == 0))
    def _():
        y_ref[...] = jnp.zeros_like(y_ref)

    def scatter_add(first_row, first_slot, n_slabs):
        lane_slot = lax.broadcasted_iota(jnp.int32, (1, LANES), 1).astype(F32)
        hot = [jnp.where(pos_ref[...] == first_slot + (k * LANES) + lane_slot, 1.0, 0.0).astype(BF16)
               for k in range(n_slabs)]
        scatter = hot[0] if n_slabs == 1 else jnp.concatenate(hot, axis=1)
        weight = jnp.concatenate([comb_ref[...]] * (D_MODEL // LANES), axis=1)
        ys = y_ref[pl.ds(first_row, n_slabs * LANES), :].astype(BF16)
        o_ref[...] += weight * _dot(scatter, ys)

    def chunk(r0, size):
        base = r0.astype(F32)
        slot_col = base + lax.broadcasted_iota(jnp.int32, (size, 1), 0).astype(F32)
        gather = jnp.where(post_e == slot_col, 1.0, 0.0).astype(BF16)
        xe = _dot(gather, h_ref[...]).astype(BF16)
        act = _silu(_dot(xe, w1_ref[0])) * _dot(xe, w3_ref[0])
        y = _dot(act.astype(BF16), w2_ref[0])

        @pl.when(r0 < merged)
        def _():
            y_ref[pl.ds(r0, size), :] = y

        @pl.when(r0 >= merged)
        def _():
            y_ref[merged:merged + size, :] = y
            for k in range(size // LANES):
                scatter_add(merged + k * LANES, base + float(k * LANES), 1)

    def big_chunk(c, carry):
        chunk(pl.multiple_of(c * MOE_BIG, MOE_BIG), MOE_BIG)
        return carry

    lax.fori_loop(0, n_big, big_chunk, 0)

    @pl.when(has_rest)
    def _():
        chunk(rest_row, MOE_CHUNK)

    @pl.when(n > 0)
    def _():
        scatter_add(0, 0.0, MOE_MERGED)

    @pl.when(e == pl.num_programs(1) - 1)
    def _():
        o_ref[...] = _rms(o_ref[...], fn_ref[...])


def _moe(x2, h, comb, pos, post, counts, w1, w3, w2, fn, tm=MOE_TM):
    T = x2.shape[0]
    once = lambda w: pl.BlockSpec((tm, w), lambda i, e, cnt: (i, 0), pipeline_mode=pl.Buffered(1))
    tile = lambda w: pl.BlockSpec((tm, w), lambda i, e, cnt: (i, 0))
    assert MOE_CHUNK == LANES
    slab = pl.BlockSpec((tm, LANES), lambda i, e, cnt: (i, e))
    expert = lambda a: pl.BlockSpec((1,) + a.shape[1:], lambda i, e, cnt: (e, 0, 0))
    grid_spec = pltpu.PrefetchScalarGridSpec(
        num_scalar_prefetch=1,
        grid=(T // tm, N_EXPERTS),
        in_specs=[once(D_MODEL), once(D_MODEL), slab, slab,
                  pl.BlockSpec((1, N_EXPERTS, tm), lambda i, e, cnt: (i, 0, 0)),
                  expert(w1), expert(w3), expert(w2),
                  pl.BlockSpec(fn.shape, lambda i, e, cnt: (0, 0))],
        out_specs=tile(D_MODEL),
        scratch_shapes=[pltpu.VMEM((MOE_MERGED * LANES + MOE_BIG, D_MODEL), F32)],
    )
    return pl.pallas_call(
        _moe_body,
        grid_spec=grid_spec,
        out_shape=jax.ShapeDtypeStruct((T, D_MODEL), F32),
        compiler_params=_params(),
        name="moe",
    )(counts, x2, h, comb, pos, post, w1, w3, w2, fn)


def _rope_lane_tables(positions):
    half = ROT_DIM // 2
    inv = ROPE_THETA ** (-jnp.arange(0, ROT_DIM, 2, dtype=F32) / ROT_DIM)
    ang = positions.astype(F32).reshape(-1, 1) * inv
    cos, sin = jnp.cos(ang), jnp.sin(ang)
    lane = np.arange(LANES)
    within = lane % HEAD_DIM
    pick = lane % half
    cos_l, sin_l = cos[:, pick], sin[:, pick]
    c = jnp.where(within < ROT_DIM, cos_l, 1.0)
    sa = jnp.where(within < half, -sin_l, 0.0)
    sb = jnp.where((within >= half) & (within < ROT_DIM), sin_l, 0.0)
    return c, sa, sb


def _w_in_plan():
    scale = HEAD_DIM ** -0.5 * LOG2E
    qa, kv, gate, qkv_b, merge = 0, 512, 1280, 1304, 3608
    nb = N_DIL_GROUPS * DIL_WIDTH
    kv_piece = lambda j: [(kv + LANES * j, LANES, 1.0)]
    span = lambda start, width, s=1.0: [(start + c, LANES, s) for c in range(0, width, LANES)]
    plan = (span(qa, 512, scale)
            + kv_piece(2) + kv_piece(4) + kv_piece(0)
            + kv_piece(3) + kv_piece(5)
            + [(gate, 3 * NSA_HEADS, 1.0)]
            + kv_piece(1)
            + span(merge, 2 * D_MODEL)
            + span(qkv_b, nb, scale) + span(qkv_b + nb, nb) + span(qkv_b + 2 * nb, nb))
    assert len(plan) * LANES == N_W_IN
    return plan


def _w_in_body(w_ref, o_ref):
    lane = lax.broadcasted_iota(jnp.int32, (w_ref.shape[1], LANES), 1)
    for j, (src, width, scale) in enumerate(_w_in_plan()):
        if width == LANES:
            slab = w_ref[0, :, src:src + LANES]
        else:
            full = w_ref[0, :, src:src + LANES]
            slab = jnp.where(lane < width, full, 0.0)
        if scale != 1.0:
            slab = slab * scale
        o_ref[0, :, j * LANES:(j + 1) * LANES] = slab.astype(BF16)


def _permute_w_in(w, rows=256):
    layers, d, n = w.shape
    return pl.pallas_call(
        _w_in_body,
        grid=(layers, d // rows),
        in_specs=[pl.BlockSpec((1, rows, n), lambda l, i: (l, i, 0))],
        out_specs=pl.BlockSpec((1, rows, N_W_IN), lambda l, i: (l, i, 0)),
        out_shape=jax.ShapeDtypeStruct((layers, d, N_W_IN), BF16),
        compiler_params=_params(),
        name="w_in_layout",
    )(w)


def _importance_matrix_t(seq, ncp):
    n_c = (seq - CMP_LEN) // CMP_STRIDE + 1
    starts = np.arange(n_c) * CMP_STRIDE
    bstart = np.arange(seq // SEL_LEN) * SEL_LEN
    overlap = np.clip(np.minimum(starts[:, None] + CMP_LEN, bstart[None, :] + SEL_LEN)
                      - np.maximum(starts[:, None], bstart[None, :]), 0, None)
    m = np.zeros((ncp, seq // SEL_LEN), np.float32)
    m[:n_c] = overlap.astype(np.float32) / CMP_LEN
    return jnp.asarray(m.T)


def _mixer(x2, layer, B, S, tables, mt, prm):
    proj, *folds, a = _inproj(x2, prm["norm_mix"], prm["w_in"], layer, *tables, B, S)
    proj3 = proj.reshape(B, S, N_PROJ)
    ncp = S // CMP_STRIDE
    cmp = _compress(a.reshape(2, B * NSA_KV_HEADS * ncp, CMP_STRIDE * HEAD_DIM),
                    prm["cmp_w1"], prm["cmp_w2"], prm["cmp_pos"], layer, B)
    oa = _nsa(proj3, cmp[0], cmp[1], mt).reshape(B * S, NSA_HEADS * HEAD_DIM)
    obs, lses = [], []
    for g, ((w, d), arr) in enumerate(zip(DIL_PATTERNS, folds)):
        o, lse = _dilated(arr, (0, 1, 2), w, d, f"dilated{g}")
        obs.append(o)
        lses.append(lse)
    return _mixout(oa, obs, lses, proj, x2, prm["p_a"], prm["p_b"], prm["w_o"], layer, S)


def kernel(x, positions, norm_mix, w_in, cmp_pos_k, cmp_pos_v, cmp_k_w1, cmp_k_w2, cmp_v_w1,
           cmp_v_w2, w_branch_a, w_branch_b, w_out, norm_ffn, ffn_w1, ffn_w3, ffn_w2, router,
           moe_w1, moe_w3, moe_w2, final_norm):
    B, S, D = x.shape
    depth = norm_mix.shape[0]
    assert depth == 2 and D == D_MODEL
    tables = _rope_lane_tables(positions)
    mt = _importance_matrix_t(S, S // CMP_STRIDE)
    cmp_pos = jnp.stack([cmp_pos_k, cmp_pos_v], axis=1).reshape(depth, 2, 1, CMP_LEN * HEAD_DIM)
    prm = {
        "norm_mix": norm_mix.reshape(depth, 1, D),
        "w_in": _permute_w_in(w_in),
        "cmp_w1": jnp.stack([cmp_k_w1, cmp_v_w1], axis=1).astype(BF16),
        "cmp_w2": jnp.stack([cmp_k_w2, cmp_v_w2], axis=1).astype(BF16),
        "cmp_pos": jnp.broadcast_to(cmp_pos, (depth, 2, 8, CMP_LEN * HEAD_DIM)).astype(BF16),
        "p_a": w_branch_a.astype(BF16), "p_b": w_branch_b.astype(BF16), "w_o": w_out.astype(BF16),
    }
    x2 = x.reshape(B * S, D)
    x2 = _mixer(x2, 0, B, S, tables, mt, prm)
    experts = [w[0].reshape(-1, w.shape[-1]) for w in (moe_w1, moe_w3, moe_w2)]
    x2, experts = _ffn(x2, norm_ffn[0].reshape(1, -1), ffn_w1[0].astype(BF16),
                       ffn_w3[0].astype(BF16), ffn_w2[0].astype(BF16), to_bf16=experts)
    ew1, ew3, ew2 = [w.reshape(m.shape[1:]) for w, m in zip(experts, (moe_w1, moe_w3, moe_w2))]
    x2 = _mixer(x2, 1, B, S, tables, mt, prm)
    g1 = norm_ffn[1].reshape(1, -1)
    r = jnp.pad(router[0], ((0, 0), (0, LANES - N_EXPERTS)))
    h, comb, pos, post, cnt = _router(x2, g1, r)
    counts = cnt[:, 0, :N_EXPERTS].reshape(-1)
    out = _moe(x2, h, comb, pos, post, counts, ew1, ew3, ew2, final_norm.reshape(1, -1))
    return out.reshape(B, S, D)
```

```python
import functools

import numpy as np
import jax
import jax.numpy as jnp
from jax import lax
from jax.experimental import pallas as pl
from jax.experimental.pallas import tpu as pltpu

F32 = jnp.float32
BF16 = jnp.bfloat16

D_MODEL = 1024
HEAD_DIM = 64
ROT_DIM = HEAD_DIM // 4
ROPE_THETA = 500000.0
EPS = 1e-6
NSA_HEADS = 8
NSA_KV_HEADS = 2
HEADS_PER_KV = NSA_HEADS // NSA_KV_HEADS
CMP_LEN = 32
CMP_STRIDE = 16
CMP_HIDDEN = 128
SEL_LEN = 64
N_SEL = 16
WIN = 512
DIL_PATTERNS = ((128, 1), (512, 4), (2048, 16))
N_DIL_GROUPS = 3
DIL_HEADS = 4
D_FF = 2816
N_EXPERTS = 8

LANES = 128
VMEM_LIMIT = 56 * 1024 * 1024
NEG = -1e30
BIG = 1e30

COL_MERGE = 0
COL_QA = 2048
COL_KSEL = 2560
COL_BLK = COL_KSEL + 128
COL_KWIN = 2816
COL_VSEL = 2944
COL_VWIN = 3072
COL_GATE = 3200
N_PROJ = 3328
DIL_WIDTH = DIL_HEADS * HEAD_DIM
N_FOLD = N_DIL_GROUPS * DIL_WIDTH
STAGE = None
IN_CHUNKS = (
    (512, True, tuple(COL_QA + 128 * j for j in range(4)), None),
    (384, True, (COL_KSEL, COL_KWIN, STAGE), ("cmp", 0)),
    (512, False, (COL_VSEL, COL_VWIN, COL_GATE, STAGE), ("cmp", 1)),
    (512, False, tuple(COL_MERGE + 128 * j for j in range(0, 4)), None),
    (512, False, tuple(COL_MERGE + 128 * j for j in range(4, 8)), None),
    (512, False, tuple(COL_MERGE + 128 * j for j in range(8, 12)), None),
    (512, False, tuple(COL_MERGE + 128 * j for j in range(12, 16)), None),
    (N_FOLD, True, (STAGE,) * 6, ("fold", 0)),
    (N_FOLD, True, (STAGE,) * 6, ("fold", 1)),
    (N_FOLD, False, (STAGE,) * 6, ("fold", 2)),
)
N_W_IN = sum(c[0] for c in IN_CHUNKS)
LOG2E = 1.4426950408889634
LN2 = 0.6931471805599453
MASK_BIAS = -(2.0 ** 100)
SEL_BLOCKS_MAX = 32


def _dot(a, b, precision=None):
    return jnp.dot(a, b, preferred_element_type=F32, precision=precision)


def _dot_nt(a, b, precision=None):
    return lax.dot_general(a, b, (((1,), (1,)), ((), ())), preferred_element_type=F32,
                           precision=precision)


def _rms(x, g):
    ms = jnp.mean(x * x, axis=-1, keepdims=True)
    return x * lax.rsqrt(ms + EPS) * g


def _silu(x):
    return x * jax.nn.sigmoid(x)


def _params(**kw):
    return pltpu.CompilerParams(vmem_limit_bytes=VMEM_LIMIT, **kw)


def _const_spec(shape):
    nd = len(shape)
    return pl.BlockSpec(shape, lambda *_: (0,) * nd)


def _inproj_body(x_ref, g_ref, wt_ref, c_ref, sa_ref, sb_ref, o_ref, f0_ref, f1_ref, f2_ref, a_ref,
                 st_ref, w_ref, *, per_b):
    tm = x_ref.shape[0]

    @pl.when(pl.program_id(0) == 0)
    def _():
        for j, (src, width, scale) in enumerate(_w_in_plan()):
            rows = wt_ref[0, src:src + LANES, :]
            w_ref[j * LANES:(j + 1) * LANES, :] = (rows * scale if scale != 1.0 else rows).astype(BF16)

    t_seq = (pl.program_id(0) % per_b) * tm + lax.broadcasted_iota(jnp.int32, (tm, LANES), 0)
    lane = lax.broadcasted_iota(jnp.int32, (tm, LANES), 1)
    blk = lax.shift_right_logical(t_seq, 6)
    hot = (lane == blk) | (lane == blk + SEL_BLOCKS_MAX)
    o_ref[:, COL_BLK:COL_BLK + LANES] = jnp.where(hot, 1.0, 0.0).astype(BF16)
    h = _rms(x_ref[...], g_ref[0]).astype(BF16)
    c = c_ref[...]
    sa = sa_ref[...]
    sb = sb_ref[...]
    start = 0
    for size, rope, dests, action in IN_CHUNKS:
        acc = _dot_nt(h, w_ref[start:start + size, :])
        start += size
        for j, dest in enumerate(dests):
            a = acc[:, j * LANES:(j + 1) * LANES]
            if rope:
                a = a * c + pltpu.roll(a, LANES - 8, 1) * sa + pltpu.roll(a, 8, 1) * sb
            if dest is STAGE:
                st_ref[j] = a
            else:
                o_ref[:, dest:dest + LANES] = a.astype(BF16)
        if action is None:
            continue
        kind, piece = action
        if kind == "fold":
            slabs = DIL_WIDTH // LANES
            for gi, f_ref in enumerate((f0_ref, f1_ref, f2_ref)):
                d = DIL_PATTERNS[gi][1]
                for r in range(d):
                    for k in range(slabs):
                        rows = st_ref[gi * slabs + k, pl.ds(r, tm // d, stride=d), :]
                        c0 = piece * DIL_WIDTH + k * LANES
                        f_ref[0, r, :, c0:c0 + LANES] = rows.astype(BF16)
        else:
            slab = dests.index(STAGE)
            nrow = tm // CMP_STRIDE
            toks = [st_ref[slab, pl.ds(j, nrow, stride=CMP_STRIDE), :] for j in range(CMP_STRIDE)]
            for g in range(NSA_KV_HEADS):
                head = slice(g * HEAD_DIM, (g + 1) * HEAD_DIM)
                for m in range(CMP_STRIDE // 2):
                    pair = jnp.concatenate([toks[2 * m][:, head], toks[2 * m + 1][:, head]], axis=1)
                    a_ref[piece, 0, g, :, m * LANES:(m + 1) * LANES] = pair.astype(BF16)


def _layer_spec(arr, layer):
    nd = arr.ndim
    return pl.BlockSpec((1,) + arr.shape[1:], lambda *_: (layer,) + (0,) * (nd - 1))


def _inproj(x2, g, w, layer, rc, rsa, rsb, B, S, tm=512):
    T = x2.shape[0]
    per_b = S // tm
    dils = [d for _, d in DIL_PATTERNS]
    fold_spec = lambda d: pl.BlockSpec((1, d, tm // d, 3 * DIL_WIDTH),
                                       lambda i: (i // per_b, 0, i % per_b, 0))
    fold_shape = lambda d: jax.ShapeDtypeStruct((B, d, S // d, 3 * DIL_WIDTH), BF16)
    cmp_w = CMP_STRIDE * HEAD_DIM
    assert S // SEL_LEN <= SEL_BLOCKS_MAX
    return pl.pallas_call(
        functools.partial(_inproj_body, per_b=per_b),
        grid=(T // tm,),
        in_specs=[
            pl.BlockSpec((tm, D_MODEL), lambda i: (i, 0)),
            _layer_spec(g, layer), _layer_spec(w, layer),
            pl.BlockSpec((tm, LANES), lambda i: (i, 0)),
            pl.BlockSpec((tm, LANES), lambda i: (i, 0)),
            pl.BlockSpec((tm, LANES), lambda i: (i, 0)),
        ],
        out_specs=[pl.BlockSpec((tm, N_PROJ), lambda i: (i, 0)), *[fold_spec(d) for d in dils],
                   pl.BlockSpec((2, 1, NSA_KV_HEADS, tm // CMP_STRIDE, cmp_w),
                                lambda i: (0, i // per_b, 0, i % per_b, 0))],
        out_shape=[jax.ShapeDtypeStruct((T, N_PROJ), BF16), *[fold_shape(d) for d in dils],
                   jax.ShapeDtypeStruct((2, B, NSA_KV_HEADS, S // CMP_STRIDE, cmp_w), BF16)],
        scratch_shapes=[pltpu.VMEM((N_FOLD // LANES, tm, LANES), F32),
                        pltpu.VMEM((N_W_IN, D_MODEL), BF16)],
        compiler_params=_params(),
        name="inproj",
    )(x2, g, w, rc, rsa, rsb)


def _compress_body(a_ref, w1_ref, w2_ref, pos_ref, o_ref):
    nb, ncp = o_ref.shape[1], o_ref.shape[2]
    a = a_ref[0]
    w1 = w1_ref[0, 0]
    half = CMP_STRIDE * HEAD_DIM
    top = _dot(a, w1[:half])
    bot = _dot(a, w1[half:])
    pc = _dot(pos_ref[0, 0], w1)
    rows = a.shape[0]
    hid = top + pltpu.roll(bot, rows - 1, 0) + pc[0:1]
    out = _dot(_silu(hid).astype(BF16), w2_ref[0, 0])
    for b in range(nb):
        heads = [out[(b * NSA_KV_HEADS + g) * ncp:(b * NSA_KV_HEADS + g + 1) * ncp]
                 for g in range(NSA_KV_HEADS)]
        o_ref[0, b] = jnp.concatenate(heads, axis=1).astype(BF16)


def _compress(a, w1, w2, pos, layer, B):
    n, rows, _ = a.shape
    ncp = rows // (B * NSA_KV_HEADS)
    per_kv = lambda arr: pl.BlockSpec((1, 1) + arr.shape[2:], lambda i: (layer, i, 0, 0))
    return pl.pallas_call(
        _compress_body,
        grid=(n,),
        in_specs=[pl.BlockSpec((1, rows, CMP_STRIDE * HEAD_DIM), lambda i: (i, 0, 0)),
                  per_kv(w1), per_kv(w2), per_kv(pos)],
        out_specs=pl.BlockSpec((1, B, ncp, NSA_KV_HEADS * HEAD_DIM), lambda i: (i, 0, 0, 0)),
        out_shape=jax.ShapeDtypeStruct((n, B, ncp, NSA_KV_HEADS * HEAD_DIM), BF16),
        compiler_params=_params(),
        name="compress",
    )(a, w1, w2, pos)


def _softmax2(s):
    m = jnp.max(s, axis=-1, keepdims=True)
    e = jnp.exp2(s - m)
    return e.astype(BF16), jnp.sum(e, axis=-1, keepdims=True)


def _weighted_values(e, l, v):
    nh, tq, nk = e.shape
    return _dot(e.reshape(nh * tq, nk), v) / l.reshape(nh * tq, 1)


SEL_PREFIX = 512


def _nsa_body(q_ref, kc_ref, vc_ref, ksel_ref, vsel_ref, kwin_ref, vwin_ref, gate_ref, mt_ref,
              o_ref, osel_ref, *, tq, seq):
    nblk = seq // SEL_LEN
    ncp = kc_ref.shape[1]
    n_cmp = (seq - CMP_LEN) // CMP_STRIDE + 1
    q0 = pl.program_id(1) * tq
    q = q_ref[0]
    t_col = q0 + lax.broadcasted_iota(jnp.int32, (tq, 1), 0)
    t_row = q0 + lax.broadcasted_iota(jnp.int32, (1, tq), 1)
    gates = jax.nn.sigmoid(gate_ref[0].astype(F32))

    zeros64 = jnp.zeros((tq, HEAD_DIM), BF16)

    def stacked_q(g):
        parts = []
        for hh in range(HEADS_PER_KV):
            h = g * HEADS_PER_KV + hh
            qh = q[:, h * HEAD_DIM:(h + 1) * HEAD_DIM]
            parts.append(jnp.concatenate([qh, zeros64] if g == 0 else [zeros64, qh], axis=1))
        return jnp.concatenate(parts, axis=0)

    qs = [stacked_q(g) for g in range(NSA_KV_HEADS)]

    cidx = lax.broadcasted_iota(jnp.int32, (tq, ncp), 1)
    cmask = ((cidx * CMP_STRIDE + (CMP_LEN - 1)) <= t_col) & (cidx < n_cmp)
    jidx = lax.broadcasted_iota(jnp.int32, (nblk, tq), 0)
    cur = lax.shift_right_logical(t_row, 6)
    forced = (jidx == 0) | (jidx == cur) | (jidx == cur - 1)
    future = jidx > cur
    groups = range(NSA_KV_HEADS)
    span = WIN + tq
    ks = pl.multiple_of(jnp.maximum(q0 - WIN, 0), tq)
    kw = kwin_ref[0, pl.ds(ks, span), :]
    vw = vwin_ref[0, pl.ds(ks, span), :]
    wpos = ks + lax.broadcasted_iota(jnp.int32, (1, span), 1)
    wmask = (wpos <= t_col) & (t_col - wpos <= WIN - 1)

    s_cmp = [jnp.where(cmask[None], _dot_nt(qs[g], kc_ref[0]).reshape(HEADS_PER_KV, tq, ncp), NEG)
             for g in groups]
    s_win = [jnp.where(wmask[None], _dot_nt(qs[g], kw).reshape(HEADS_PER_KV, tq, span), NEG)
             for g in groups]
    p_cmp = []
    for g in groups:
        m = jnp.max(s_cmp[g], axis=-1, keepdims=True)
        e = jnp.where(cmask[None], jnp.exp2(s_cmp[g] - m), 0.0)
        den = jnp.sum(e, axis=-1, keepdims=True)
        p_cmp.append(e / jnp.where(den > 0, den, 1.0))
    e_win = [_softmax2(s_win[g]) for g in groups]
    o_cmp = [_dot(p_cmp[g].astype(BF16).reshape(HEADS_PER_KV * tq, ncp), vc_ref[0]) for g in groups]
    imps = [_dot_nt(mt_ref[...], p_cmp[g][0] + p_cmp[g][1] + p_cmp[g][2] + p_cmp[g][3],
                    precision=lax.Precision.HIGHEST) for g in groups]
    o_win = [_weighted_values(*e_win[g], vw) for g in groups]

    first_blk = lax.shift_right_logical(q0, 6)
    picked_rows, before_rows = [], []
    for g in groups:
        imp = jnp.where(forced, BIG, imps[g])
        imp = jnp.where(future, -BIG, imp)
        rank = jnp.zeros((nblk, tq), jnp.int32)
        for i in range(nblk):
            row = imp[i:i + 1, :]
            beats = (row > imp) | ((row == imp) & (jidx > i))
            rank = rank + beats.astype(jnp.int32)
        picked_rows.append(jnp.where(rank < N_SEL, 0.0, MASK_BIAS))
        before_rows.append(jnp.where((rank < N_SEL) & (jidx < first_blk), 0.0, MASK_BIAS))
    assert 2 * NSA_KV_HEADS * SEL_BLOCKS_MAX == LANES
    bias_t = jnp.concatenate(before_rows + picked_rows, axis=0)
    bias_main = bias_t.T
    bias_diag = pltpu.roll(bias_main, LANES // 2, 1)
    lane_group = lax.shift_right_logical(lax.broadcasted_iota(jnp.int32, (tq, LANES), 1), 5)

    def with_bias(g, bias):
        own = jnp.where(lane_group == g, bias, 0.0).astype(BF16)
        return jnp.concatenate([qs[g], jnp.concatenate([own] * HEADS_PER_KV, axis=0)], axis=1)

    qb = [with_bias(g, bias_main) for g in groups]

    kdiag = ksel_ref[0, pl.ds(pl.multiple_of(q0, tq), tq), :]
    vdiag = vsel_ref[0, pl.ds(pl.multiple_of(q0, tq), tq), :]
    tri = (lax.broadcasted_iota(jnp.int32, (tq, tq), 1) <= lax.broadcasted_iota(jnp.int32, (tq, tq), 0))
    s_diag = [jnp.where(tri[None], _dot_nt(with_bias(g, bias_diag), kdiag)
                        .reshape(HEADS_PER_KV, tq, tq), NEG) for g in groups]
    n_prefix = q0 // SEL_PREFIX + 1
    for n in range(1, seq // SEL_PREFIX + 1):
        klen = n * SEL_PREFIX

        @pl.when(n_prefix == n)
        def _(klen=klen):
            vall = jnp.concatenate([vsel_ref[0, :klen, :], vdiag], axis=0)
            s = [jnp.concatenate(
                [_dot_nt(qb[g], ksel_ref[0, :klen, :]).reshape(HEADS_PER_KV, tq, klen), s_diag[g]],
                axis=-1) for g in groups]
            ew = [_softmax2(s[g]) for g in groups]
            for g in groups:
                osel_ref[g] = _weighted_values(*ew[g], vall)

    o_sel = [osel_ref[g] for g in groups]

    outs = []
    for g in range(NSA_KV_HEADS):
        for hh in range(HEADS_PER_KV):
            h = g * HEADS_PER_KV + hh
            acc = jnp.zeros((tq, HEAD_DIM), F32)
            for br, o in enumerate((o_cmp[g], o_sel[g], o_win[g])):
                oh = o[hh * tq:(hh + 1) * tq, g * HEAD_DIM:(g + 1) * HEAD_DIM]
                acc = acc + gates[:, 3 * h + br:3 * h + br + 1] * oh
            outs.append(acc)
    o_ref[0] = jnp.concatenate(outs, axis=1).astype(BF16)


def _nsa(proj3, kc, vc, mt, tq=128):
    B, S, _ = proj3.shape
    blk = lambda c: c // LANES
    seq_spec = lambda c: pl.BlockSpec((1, S, LANES), lambda b, i: (b, 0, blk(c)))
    ncp = kc.shape[1]
    return pl.pallas_call(
        functools.partial(_nsa_body, tq=tq, seq=S),
        grid=(B, S // tq),
        in_specs=[
            pl.BlockSpec((1, tq, NSA_HEADS * HEAD_DIM),
                         lambda b, i: (b, i, COL_QA // (NSA_HEADS * HEAD_DIM))),
            pl.BlockSpec((1, ncp, LANES), lambda b, i: (b, 0, 0)),
            pl.BlockSpec((1, ncp, LANES), lambda b, i: (b, 0, 0)),
            pl.BlockSpec((1, S, 2 * LANES), lambda b, i: (b, 0, COL_KSEL // (2 * LANES))),
            seq_spec(COL_VSEL), seq_spec(COL_KWIN), seq_spec(COL_VWIN),
            pl.BlockSpec((1, tq, LANES), lambda b, i: (b, i, blk(COL_GATE))),
            _const_spec(mt.shape),
        ],
        out_specs=pl.BlockSpec((1, tq, NSA_HEADS * HEAD_DIM), lambda b, i: (b, i, 0)),
        out_shape=jax.ShapeDtypeStruct((B, S, NSA_HEADS * HEAD_DIM), BF16),
        scratch_shapes=[pltpu.VMEM((NSA_KV_HEADS, HEADS_PER_KV * tq, LANES), F32)],
        compiler_params=_params(),
        name="nsa",
    )(proj3, kc, vc, proj3, proj3, proj3, proj3, proj3, mt)


DIL_SUB = 128
DIL_ROWS = 512


def _dil_body(q_ref, kp_ref, kc_ref, vp_ref, vc_ref, o_ref, lse_ref, *, n_back):
    sub = DIL_SUB
    rb, tq = q_ref.shape[1], q_ref.shape[2]
    t0 = pl.program_id(2) * tq
    head_of = lax.shift_right_logical(lax.broadcasted_iota(jnp.int32, (sub, DIL_WIDTH), 1), 6)
    lane = lax.broadcasted_iota(jnp.int32, (sub, LANES), 1)
    diff = (sub + lax.broadcasted_iota(jnp.int32, (sub, 1), 0)
            - lax.broadcasted_iota(jnp.int32, (1, 2 * sub), 1))
    band = (diff >= 0) & (diff <= n_back)
    band0 = band & (lax.broadcasted_iota(jnp.int32, (1, 2 * sub), 1) + t0 >= sub)
    tiles = [(r, j) for r in range(rb) for j in range(tq // sub)]
    keys = {r: jnp.concatenate([kp_ref[0, r], kc_ref[0, r]], axis=0) for r in range(rb)}
    vals = {r: jnp.concatenate([vp_ref[0, r], vc_ref[0, r]], axis=0) for r in range(rb)}
    scores = []
    for r, j in tiles:
        q = q_ref[0, r, j * sub:(j + 1) * sub, :]
        qs = jnp.concatenate([jnp.where(head_of == h, q, jnp.zeros_like(q))
                              for h in range(DIL_HEADS)], axis=0)
        s = _dot_nt(qs, keys[r][j * sub:(j + 2) * sub]).reshape(DIL_HEADS, sub, 2 * sub)
        scores.append(jnp.where((band0 if j == 0 else band)[None], s, NEG))
    stats = []
    for s in scores:
        m = jnp.max(s, axis=-1, keepdims=True)
        e = jnp.exp2(s - m)
        stats.append((m, e, jnp.sum(e, axis=-1, keepdims=True)))
    for (r, j), (m, e, l) in zip(tiles, stats):
        o = _dot(e.astype(BF16).reshape(DIL_HEADS * sub, 2 * sub), vals[r][j * sub:(j + 2) * sub])
        o = o.reshape(DIL_HEADS, sub, DIL_WIDTH) / l
        lse = m * LN2 + jnp.log(l)
        o_acc = jnp.zeros((sub, DIL_WIDTH), F32)
        lse_out = jnp.zeros((sub, LANES), F32)
        for h in range(DIL_HEADS):
            o_acc = jnp.where(head_of == h, o[h], o_acc)
            lse_out = jnp.where(lane == h, lse[h], lse_out)
        o_ref[0, r, j * sub:(j + 1) * sub, :] = o_acc
        lse_ref[0, r, j * sub:(j + 1) * sub, :] = lse_out


def _dilated(arr, cols, window, dilation, name):
    B, d, L, _ = arr.shape
    n_back = window // dilation
    tq = min(L, DIL_ROWS)
    rb = DIL_ROWS // tq
    assert d == dilation and n_back <= DIL_SUB and L % tq == 0 and d % rb == 0
    per = tq // DIL_SUB
    qc, kc, vc = cols
    cur = lambda c: pl.BlockSpec((1, rb, tq, DIL_WIDTH), lambda b, r, i: (b, r, i, c))
    prev = lambda c: pl.BlockSpec((1, rb, DIL_SUB, DIL_WIDTH),
                                  lambda b, r, i: (b, r, jnp.maximum(i * per - 1, 0), c))
    return pl.pallas_call(
        functools.partial(_dil_body, n_back=n_back),
        grid=(B, dilation // rb, L // tq),
        in_specs=[cur(qc), prev(kc), cur(kc), prev(vc), cur(vc)],
        out_specs=[pl.BlockSpec((1, rb, tq, DIL_WIDTH), lambda b, r, i: (b, r, i, 0)),
                   pl.BlockSpec((1, rb, tq, LANES), lambda b, r, i: (b, r, i, 0))],
        out_shape=[jax.ShapeDtypeStruct((B, dilation, L, DIL_WIDTH), F32),
                   jax.ShapeDtypeStruct((B, dilation, L, LANES), F32)],
        compiler_params=_params(),
        name=name,
    )(arr, arr, arr, arr, arr)


def _mixout_body(oa_ref, ob0_ref, ob1_ref, ob2_ref, l0_ref, l1_ref, l2_ref, mg_ref, x_ref,
                 pa_ref, pb_ref, wo_ref, out_ref, so_ref, sl_ref):
    tm = x_ref.shape[0]

    def interleaved(src_ref, st_ref):
        d = src_ref.shape[1]
        if d == 1:
            return src_ref[0, 0]
        slabs = src_ref.shape[3] // LANES
        for r in range(d):
            for k in range(slabs):
                st_ref[k, pl.ds(r, tm // d, stride=d), :] = src_ref[0, r, :, k * LANES:(k + 1) * LANES]
        return jnp.concatenate([st_ref[k] for k in range(slabs)], axis=1)

    lses = [interleaved(l, sl_ref) for l in (l0_ref, l1_ref, l2_ref)]
    mx = jnp.maximum(jnp.maximum(lses[0], lses[1]), lses[2])
    ws = [jnp.exp(l - mx) for l in lses]
    den = ws[0] + ws[1] + ws[2]
    ob = jnp.zeros((tm, DIL_WIDTH), F32)
    for w, o_ref in zip(ws, (ob0_ref, ob1_ref, ob2_ref)):
        alpha = w / den
        wide = jnp.concatenate(
            [jnp.broadcast_to(alpha[:, h:h + 1], (tm, HEAD_DIM)) for h in range(DIL_HEADS)], axis=1)
        ob = ob + wide * interleaved(o_ref, so_ref)
    ya = _dot(oa_ref[...], pa_ref[0])
    yb = _dot(ob.astype(BF16), pb_ref[0])
    gm = jax.nn.sigmoid(mg_ref[...].astype(F32))
    y = gm[:, :D_MODEL] * ya + gm[:, D_MODEL:] * yb
    out_ref[...] = x_ref[...] + _dot(y.astype(BF16), wo_ref[0])


def _mixout(oa, obs, lses, proj, x2, pa, pb, wo, layer, S, tm=512):
    T = x2.shape[0]
    per_b = S // tm
    row = lambda w: pl.BlockSpec((tm, w), lambda i: (i, 0))
    folded = lambda a: pl.BlockSpec((1, a.shape[1], tm // a.shape[1], a.shape[3]),
                                    lambda i: (i // per_b, 0, i % per_b, 0))
    return pl.pallas_call(
        _mixout_body,
        grid=(T // tm,),
        in_specs=[row(NSA_HEADS * HEAD_DIM), *[folded(a) for a in obs], *[folded(a) for a in lses],
                  pl.BlockSpec((tm, 2 * D_MODEL), lambda i: (i, COL_MERGE // (2 * D_MODEL))),
                  row(D_MODEL),
                  _layer_spec(pa, layer), _layer_spec(pb, layer), _layer_spec(wo, layer)],
        out_specs=row(D_MODEL),
        out_shape=jax.ShapeDtypeStruct((T, D_MODEL), F32),
        scratch_shapes=[pltpu.VMEM((DIL_WIDTH // LANES, tm, LANES), F32),
                        pltpu.VMEM((1, tm, LANES), F32)],
        compiler_params=_params(),
        name="mixout",
    )(oa, *obs, *lses, proj, x2, pa, pb, wo)


FF_CHUNK = 512


def _ffn_body(x_ref, g_ref, w1_ref, w3_ref, w2_ref, *rest):
    n_cast = (len(rest) - 1) // 2
    src_refs, o_ref, dst_refs = rest[:n_cast], rest[n_cast], rest[n_cast + 1:]
    x = x_ref[...]
    h = _rms(x, g_ref[...]).astype(BF16)
    acc = jnp.zeros(x.shape, F32)
    for c0 in range(0, D_FF, FF_CHUNK):
        c1 = min(c0 + FF_CHUNK, D_FF)
        act = _silu(_dot(h, w1_ref[:, c0:c1])) * _dot(h, w3_ref[:, c0:c1])
        acc = acc + _dot(act.astype(BF16), w2_ref[c0:c1, :])
    o_ref[...] = x + acc
    for src, dst in zip(src_refs, dst_refs):
        dst[...] = src[...].astype(BF16)


def _ffn(x2, g, w1, w3, w2, to_bf16=(), tm=512):
    T = x2.shape[0]
    steps = T // tm
    held = lambda a: pl.BlockSpec(a.shape, lambda i: (0, 0), pipeline_mode=pl.Buffered(1))
    sliced = lambda a: pl.BlockSpec((a.shape[0] // steps, a.shape[1]), lambda i: (i, 0))
    assert all(a.shape[0] % (16 * steps) == 0 for a in to_bf16)
    out, *copies = pl.pallas_call(
        _ffn_body,
        grid=(steps,),
        in_specs=[pl.BlockSpec((tm, D_MODEL), lambda i: (i, 0)), _const_spec(g.shape),
                  held(w1), held(w3), held(w2), *[sliced(a) for a in to_bf16]],
        out_specs=[pl.BlockSpec((tm, D_MODEL), lambda i: (i, 0)), *[sliced(a) for a in to_bf16]],
        out_shape=[jax.ShapeDtypeStruct((T, D_MODEL), F32),
                   *[jax.ShapeDtypeStruct(a.shape, BF16) for a in to_bf16]],
        compiler_params=_params(),
        name="ffn",
    )(x2, g, w1, w3, w2, *to_bf16)
    return out, copies


MOE_TM = 1024
MOE_CHUNK = 128
MOE_BIG = 2 * MOE_CHUNK
MOE_MERGED = 4


def _router_body(x_ref, g_ref, r_ref, h_ref, combw_ref, posw_ref, post_ref, cnt_ref):
    tm = x_ref.shape[0]
    h = _rms(x_ref[...], g_ref[...])
    h_ref[...] = h.astype(BF16)
    r = r_ref[...]
    h_hi, r_hi = h.astype(BF16), r.astype(BF16)
    h_lo = (h - h_hi.astype(F32)).astype(BF16)
    r_lo = (r - r_hi.astype(F32)).astype(BF16)
    logits = _dot(h_hi, r_hi) + (_dot(h_hi, r_lo) + _dot(h_lo, r_hi))
    lane = lax.broadcasted_iota(jnp.int32, logits.shape, 1)
    lg = jnp.where(lane < N_EXPERTS, logits, NEG)
    m1 = jnp.max(lg, axis=-1, keepdims=True)
    i1 = jnp.min(jnp.where(lg == m1, lane, LANES), axis=-1, keepdims=True)
    lg2 = jnp.where(lane == i1, NEG, lg)
    m2 = jnp.max(lg2, axis=-1, keepdims=True)
    i2 = jnp.min(jnp.where(lg2 == m2, lane, LANES), axis=-1, keepdims=True)
    e2 = jnp.exp(m2 - m1)
    den = 1.0 + e2
    w_first, w_second = 1.0 / den, e2 / den
    chosen = [jnp.broadcast_to((i1 == ex) | (i2 == ex), (tm, LANES)) for ex in range(N_EXPERTS)]
    ones = jnp.concatenate([jnp.where(c, 1.0, 0.0).astype(BF16) for c in chosen], axis=1)
    row = lax.broadcasted_iota(jnp.int32, (LANES, LANES), 0)
    col = lax.broadcasted_iota(jnp.int32, (LANES, LANES), 1)
    tri = jnp.where(col < row, 1.0, 0.0).astype(BF16)
    running = jnp.zeros((1, N_EXPERTS * LANES), F32)
    parts = []
    for b in range(tm // LANES):
        blk = ones[b * LANES:(b + 1) * LANES]
        parts.append(_dot(tri, blk) + running)
        running = running + jnp.sum(blk.astype(F32), axis=0, keepdims=True)
    before = jnp.concatenate(parts, axis=0)
    by_lane = jnp.zeros((tm, LANES), F32)
    cnt = jnp.zeros((1, LANES), F32)
    for ex in range(N_EXPERTS):
        slab = slice(ex * LANES, (ex + 1) * LANES)
        combw_ref[:, slab] = jnp.broadcast_to(
            jnp.where(i1 == ex, w_first, 0.0) + jnp.where(i2 == ex, w_second, 0.0), (tm, LANES))
        pos = jnp.where(chosen[ex], before[:, slab], -1.0)
        posw_ref[:, slab] = pos
        by_lane = jnp.where(lane == ex, pos, by_lane)
        cnt = jnp.where(lane[:1] == ex, running[:, slab], cnt)
    post_ref[0] = by_lane.T[:N_EXPERTS]
    cnt_ref[0] = jnp.broadcast_to(cnt, (8, LANES)).astype(jnp.int32)


def _router(x2, g, r, tm=MOE_TM):
    T = x2.shape[0]
    nt = T // tm
    row = lambda w: pl.BlockSpec((tm, w), lambda i: (i, 0))
    return pl.pallas_call(
        _router_body,
        grid=(nt,),
        in_specs=[row(D_MODEL), _const_spec(g.shape), _const_spec(r.shape)],
        out_specs=[row(D_MODEL), row(N_EXPERTS * LANES), row(N_EXPERTS * LANES),
                   pl.BlockSpec((1, N_EXPERTS, tm), lambda i: (i, 0, 0)),
                   pl.BlockSpec((1, 8, LANES), lambda i: (i, 0, 0))],
        out_shape=[jax.ShapeDtypeStruct((T, D_MODEL), BF16),
                   jax.ShapeDtypeStruct((T, N_EXPERTS * LANES), F32),
                   jax.ShapeDtypeStruct((T, N_EXPERTS * LANES), F32),
                   jax.ShapeDtypeStruct((nt, N_EXPERTS, tm), F32),
                   jax.ShapeDtypeStruct((nt, 8, LANES), jnp.int32)],
        compiler_params=_params(),
        name="router",
    )(x2, g, r)


def _moe_body(cnt_ref, x_ref, h_ref, comb_ref, pos_ref, post_ref, w1_ref, w3_ref, w2_ref, fn_ref,
              o_ref, y_ref):
    i = pl.program_id(0)
    e = pl.program_id(1)
    merged = MOE_MERGED * LANES
    n = cnt_ref[i * N_EXPERTS + e]
    n_big = (n + (MOE_BIG - MOE_CHUNK - 1)) // MOE_BIG
    rest_row = pl.multiple_of(n_big * MOE_BIG, MOE_BIG)
    has_rest = n > rest_row
    post_e = post_ref[0, pl.ds(e, 1), :]

    @pl.when(e == 0)
    def _():
        o_ref[...] = x_ref[...]

    @pl.when((i == 0) & (e == 0))
    def _():
        y_ref[...] = jnp.zeros_like(y_ref)

    def scatter_add(first_row, first_slot, n_slabs):
        lane_slot = lax.broadcasted_iota(jnp.int32, (1, LANES), 1).astype(F32)
        hot = [jnp.where(pos_ref[...] == first_slot + (k * LANES) + lane_slot, 1.0, 0.0).astype(BF16)
               for k in range(n_slabs)]
        scatter = hot[0] if n_slabs == 1 else jnp.concatenate(hot, axis=1)
        weight = jnp.concatenate([comb_ref[...]] * (D_MODEL // LANES), axis=1)
        ys = y_ref[pl.ds(first_row, n_slabs * LANES), :].astype(BF16)
        o_ref[...] += weight * _dot(scatter, ys)

    def chunk(r0, size):
        base = r0.astype(F32)
        slot_col = base + lax.broadcasted_iota(jnp.int32, (size, 1), 0).astype(F32)
        gather = jnp.where(post_e == slot_col, 1.0, 0.0).astype(BF16)
        xe = _dot(gather, h_ref[...]).astype(BF16)
        act = _silu(_dot(xe, w1_ref[0])) * _dot(xe, w3_ref[0])
        y = _dot(act.astype(BF16), w2_ref[0])

        @pl.when(r0 < merged)
        def _():
            y_ref[pl.ds(r0, size), :] = y

        @pl.when(r0 >= merged)
        def _():
            y_ref[merged:merged + size, :] = y
            for k in range(size // LANES):
                scatter_add(merged + k * LANES, base + float(k * LANES), 1)

    def big_chunk(c, carry):
        chunk(pl.multiple_of(c * MOE_BIG, MOE_BIG), MOE_BIG)
        return carry

    lax.fori_loop(0, n_big, big_chunk, 0)

    @pl.when(has_rest)
    def _():
        chunk(rest_row, MOE_CHUNK)

    @pl.when(n > 0)
    def _():
        scatter_add(0, 0.0, MOE_MERGED)

    @pl.when(e == pl.num_programs(1) - 1)
    def _():
        o_ref[...] = _rms(o_ref[...], fn_ref[...])


def _moe(x2, h, comb, pos, post, counts, w1, w3, w2, fn, tm=MOE_TM):
    T = x2.shape[0]
    once = lambda w: pl.BlockSpec((tm, w), lambda i, e, cnt: (i, 0), pipeline_mode=pl.Buffered(1))
    tile = lambda w: pl.BlockSpec((tm, w), lambda i, e, cnt: (i, 0))
    assert MOE_CHUNK == LANES
    slab = pl.BlockSpec((tm, LANES), lambda i, e, cnt: (i, e))
    expert = lambda a: pl.BlockSpec((1,) + a.shape[1:], lambda i, e, cnt: (e, 0, 0))
    grid_spec = pltpu.PrefetchScalarGridSpec(
        num_scalar_prefetch=1,
        grid=(T // tm, N_EXPERTS),
        in_specs=[once(D_MODEL), once(D_MODEL), slab, slab,
                  pl.BlockSpec((1, N_EXPERTS, tm), lambda i, e, cnt: (i, 0, 0)),
                  expert(w1), expert(w3), expert(w2),
                  pl.BlockSpec(fn.shape, lambda i, e, cnt: (0, 0))],
        out_specs=tile(D_MODEL),
        scratch_shapes=[pltpu.VMEM((MOE_MERGED * LANES + MOE_BIG, D_MODEL), F32)],
    )
    return pl.pallas_call(
        _moe_body,
        grid_spec=grid_spec,
        out_shape=jax.ShapeDtypeStruct((T, D_MODEL), F32),
        compiler_params=_params(),
        name="moe",
    )(counts, x2, h, comb, pos, post, w1, w3, w2, fn)


def _rope_lane_tables(positions):
    half = ROT_DIM // 2
    inv = ROPE_THETA ** (-jnp.arange(0, ROT_DIM, 2, dtype=F32) / ROT_DIM)
    ang = positions.astype(F32).reshape(-1, 1) * inv
    cos, sin = jnp.cos(ang), jnp.sin(ang)
    lane = np.arange(LANES)
    within = lane % HEAD_DIM
    pick = lane % half
    cos_l, sin_l = cos[:, pick], sin[:, pick]
    c = jnp.where(within < ROT_DIM, cos_l, 1.0)
    sa = jnp.where(within < half, -sin_l, 0.0)
    sb = jnp.where((within >= half) & (within < ROT_DIM), sin_l, 0.0)
    return c, sa, sb


def _w_in_plan():
    scale = HEAD_DIM ** -0.5 * LOG2E
    qa, kv, gate, qkv_b, merge = 0, 512, 1280, 1304, 3608
    nb = N_DIL_GROUPS * DIL_WIDTH
    kv_piece = lambda j: [(kv + LANES * j, LANES, 1.0)]
    span = lambda start, width, s=1.0: [(start + c, LANES, s) for c in range(0, width, LANES)]
    plan = (span(qa, 512, scale)
            + kv_piece(2) + kv_piece(4) + kv_piece(0)
            + kv_piece(3) + kv_piece(5)
            + [(gate, 3 * NSA_HEADS, 1.0)]
            + kv_piece(1)
            + span(merge, 2 * D_MODEL)
            + span(qkv_b, nb, scale) + span(qkv_b + nb, nb) + span(qkv_b + 2 * nb, nb))
    assert len(plan) * LANES == N_W_IN
    return plan


def _importance_matrix_t(seq, ncp):
    n_c = (seq - CMP_LEN) // CMP_STRIDE + 1
    starts = np.arange(n_c) * CMP_STRIDE
    bstart = np.arange(seq // SEL_LEN) * SEL_LEN
    overlap = np.clip(np.minimum(starts[:, None] + CMP_LEN, bstart[None, :] + SEL_LEN)
                      - np.maximum(starts[:, None], bstart[None, :]), 0, None)
    m = np.zeros((ncp, seq // SEL_LEN), np.float32)
    m[:n_c] = overlap.astype(np.float32) / CMP_LEN
    return jnp.asarray(m.T)


def _mixer(x2, layer, B, S, tables, mt, prm):
    proj, *folds, a = _inproj(x2, prm["norm_mix"], prm["w_in"], layer, *tables, B, S)
    proj3 = proj.reshape(B, S, N_PROJ)
    ncp = S // CMP_STRIDE
    cmp = _compress(a.reshape(2, B * NSA_KV_HEADS * ncp, CMP_STRIDE * HEAD_DIM),
                    prm["cmp_w1"], prm["cmp_w2"], prm["cmp_pos"], layer, B)
    oa = _nsa(proj3, cmp[0], cmp[1], mt).reshape(B * S, NSA_HEADS * HEAD_DIM)
    obs, lses = [], []
    for g, ((w, d), arr) in enumerate(zip(DIL_PATTERNS, folds)):
        o, lse = _dilated(arr, (0, 1, 2), w, d, f"dilated{g}")
        obs.append(o)
        lses.append(lse)
    return _mixout(oa, obs, lses, proj, x2, prm["p_a"], prm["p_b"], prm["w_o"], layer, S)


def kernel(x, positions, norm_mix, w_in, cmp_pos_k, cmp_pos_v, cmp_k_w1, cmp_k_w2, cmp_v_w1,
           cmp_v_w2, w_branch_a, w_branch_b, w_out, norm_ffn, ffn_w1, ffn_w3, ffn_w2, router,
           moe_w1, moe_w3, moe_w2, final_norm):
    B, S, D = x.shape
    depth = norm_mix.shape[0]
    assert depth == 2 and D == D_MODEL
    tables = _rope_lane_tables(positions)
    mt = _importance_matrix_t(S, S // CMP_STRIDE)
    cmp_pos = jnp.stack([cmp_pos_k, cmp_pos_v], axis=1).reshape(depth, 2, 1, CMP_LEN * HEAD_DIM)
    prm = {
        "norm_mix": norm_mix.reshape(depth, 1, D),
        "w_in": jnp.swapaxes(w_in, 1, 2),
        "cmp_w1": jnp.stack([cmp_k_w1, cmp_v_w1], axis=1).astype(BF16),
        "cmp_w2": jnp.stack([cmp_k_w2, cmp_v_w2], axis=1).astype(BF16),
        "cmp_pos": jnp.broadcast_to(cmp_pos, (depth, 2, 8, CMP_LEN * HEAD_DIM)).astype(BF16),
        "p_a": w_branch_a.astype(BF16), "p_b": w_branch_b.astype(BF16), "w_o": w_out.astype(BF16),
    }
    x2 = x.reshape(B * S, D)
    x2 = _mixer(x2, 0, B, S, tables, mt, prm)
    experts = [w[0].reshape(-1, w.shape[-1]) for w in (moe_w1, moe_w3, moe_w2)]
    x2, experts = _ffn(x2, norm_ffn[0].reshape(1, -1), ffn_w1[0].astype(BF16),
                       ffn_w3[0].astype(BF16), ffn_w2[0].astype(BF16), to_bf16=experts)
    ew1, ew3, ew2 = [w.reshape(m.shape[1:]) for w, m in zip(experts, (moe_w1, moe_w3, moe_w2))]
    x2 = _mixer(x2, 1, B, S, tables, mt, prm)
    g1 = norm_ffn[1].reshape(1, -1)
    r = jnp.pad(router[0], ((0, 0), (0, LANES - N_EXPERTS)))
    h, comb, pos, post, cnt = _router(x2, g1, r)
    counts = cnt[:, 0, :N_EXPERTS].reshape(-1)
    out = _moe(x2, h, comb, pos, post, counts, ew1, ew3, ew2, final_norm.reshape(1, -1))
    return out.reshape(B, S, D)
```

```python
import functools

import numpy as np
import jax
import jax.numpy as jnp
from jax import lax
from jax.experimental import pallas as pl
from jax.experimental.pallas import tpu as pltpu

F32 = jnp.float32
BF16 = jnp.bfloat16

D_MODEL = 1024
HEAD_DIM = 64
ROT_DIM = HEAD_DIM // 4
ROPE_THETA = 500000.0
EPS = 1e-6
NSA_HEADS = 8
NSA_KV_HEADS = 2
HEADS_PER_KV = NSA_HEADS // NSA_KV_HEADS
CMP_LEN = 32
CMP_STRIDE = 16
CMP_HIDDEN = 128
SEL_LEN = 64
N_SEL = 16
WIN = 512
DIL_PATTERNS = ((128, 1), (512, 4), (2048, 16))
N_DIL_GROUPS = 3
DIL_HEADS = 4
D_FF = 2816
N_EXPERTS = 8

LANES = 128
VMEM_LIMIT = 56 * 1024 * 1024
NEG = -1e30
BIG = 1e30

COL_MERGE = 0
COL_QA = 2048
COL_KSEL = 2560
COL_BLK = COL_KSEL + 128
COL_KWIN = 2816
COL_VSEL = 2944
COL_VWIN = 3072
COL_GATE = 3200
N_PROJ = 3328
DIL_WIDTH = DIL_HEADS * HEAD_DIM
N_FOLD = N_DIL_GROUPS * DIL_WIDTH
STAGE = None
IN_CHUNKS = (
    (512, True, tuple(COL_QA + 128 * j for j in range(4)), None),
    (384, True, (COL_KSEL, COL_KWIN, STAGE), ("cmp", 0)),
    (512, False, (COL_VSEL, COL_VWIN, COL_GATE, STAGE), ("cmp", 1)),
    (512, False, tuple(COL_MERGE + 128 * j for j in range(0, 4)), None),
    (512, False, tuple(COL_MERGE + 128 * j for j in range(4, 8)), None),
    (512, False, tuple(COL_MERGE + 128 * j for j in range(8, 12)), None),
    (512, False, tuple(COL_MERGE + 128 * j for j in range(12, 16)), None),
    (N_FOLD, True, (STAGE,) * 6, ("fold", 0)),
    (N_FOLD, True, (STAGE,) * 6, ("fold", 1)),
    (N_FOLD, False, (STAGE,) * 6, ("fold", 2)),
)
N_W_IN = sum(c[0] for c in IN_CHUNKS)
LOG2E = 1.4426950408889634
LN2 = 0.6931471805599453
MASK_BIAS = -(2.0 ** 100)
SEL_BLOCKS_MAX = 32


def _dot(a, b, precision=None):
    return jnp.dot(a, b, preferred_element_type=F32, precision=precision)


def _dot_nt(a, b, precision=None):
    return lax.dot_general(a, b, (((1,), (1,)), ((), ())), preferred_element_type=F32,
                           precision=precision)


def _rms(x, g):
    ms = jnp.mean(x * x, axis=-1, keepdims=True)
    return x * lax.rsqrt(ms + EPS) * g


def _silu(x):
    return x * jax.nn.sigmoid(x)


def _params(**kw):
    return pltpu.CompilerParams(vmem_limit_bytes=VMEM_LIMIT, **kw)


def _const_spec(shape):
    nd = len(shape)
    return pl.BlockSpec(shape, lambda *_: (0,) * nd)


def _inproj_body(x_ref, g_ref, wt_ref, c_ref, sa_ref, sb_ref, o_ref, f0_ref, f1_ref, f2_ref, a_ref,
                 st_ref, w_ref, *, per_b):
    tm = x_ref.shape[0]

    @pl.when(pl.program_id(0) == 0)
    def _():
        for j, (src, width, scale) in enumerate(_w_in_plan()):
            rows = wt_ref[0, src:src + LANES, :]
            w_ref[j * LANES:(j + 1) * LANES, :] = (rows * scale if scale != 1.0 else rows).astype(BF16)

    t_seq = (pl.program_id(0) % per_b) * tm + lax.broadcasted_iota(jnp.int32, (tm, LANES), 0)
    lane = lax.broadcasted_iota(jnp.int32, (tm, LANES), 1)
    blk = lax.shift_right_logical(t_seq, 6)
    hot = (lane == blk) | (lane == blk + SEL_BLOCKS_MAX)
    o_ref[:, COL_BLK:COL_BLK + LANES] = jnp.where(hot, 1.0, 0.0).astype(BF16)
    h = _rms(x_ref[...], g_ref[0]).astype(BF16)
    c = c_ref[...]
    sa = sa_ref[...]
    sb = sb_ref[...]
    start = 0
    for size, rope, dests, action in IN_CHUNKS:
        acc = _dot_nt(h, w_ref[start:start + size, :])
        start += size
        for j, dest in enumerate(dests):
            a = acc[:, j * LANES:(j + 1) * LANES]
            if rope:
                a = a * c + pltpu.roll(a, LANES - 8, 1) * sa + pltpu.roll(a, 8, 1) * sb
            if dest is STAGE:
                st_ref[j] = a
            else:
                o_ref[:, dest:dest + LANES] = a.astype(BF16)
        if action is None:
            continue
        kind, piece = action
        if kind == "fold":
            slabs = DIL_WIDTH // LANES
            for gi, f_ref in enumerate((f0_ref, f1_ref, f2_ref)):
                d = DIL_PATTERNS[gi][1]
                for r in range(d):
                    for k in range(slabs):
                        rows = st_ref[gi * slabs + k, pl.ds(r, tm // d, stride=d), :]
                        c0 = piece * DIL_WIDTH + k * LANES
                        f_ref[0, r, :, c0:c0 + LANES] = rows.astype(BF16)
        else:
            slab = dests.index(STAGE)
            nrow = tm // CMP_STRIDE
            toks = [st_ref[slab, pl.ds(j, nrow, stride=CMP_STRIDE), :] for j in range(CMP_STRIDE)]
            for g in range(NSA_KV_HEADS):
                head = slice(g * HEAD_DIM, (g + 1) * HEAD_DIM)
                for m in range(CMP_STRIDE // 2):
                    pair = jnp.concatenate([toks[2 * m][:, head], toks[2 * m + 1][:, head]], axis=1)
                    a_ref[piece, 0, g, :, m * LANES:(m + 1) * LANES] = pair.astype(BF16)


def _layer_spec(arr, layer):
    nd = arr.ndim
    return pl.BlockSpec((1,) + arr.shape[1:], lambda *_: (layer,) + (0,) * (nd - 1))


def _inproj(x2, g, w, layer, rc, rsa, rsb, B, S, tm=512):
    T = x2.shape[0]
    per_b = S // tm
    dils = [d for _, d in DIL_PATTERNS]
    fold_spec = lambda d: pl.BlockSpec((1, d, tm // d, 3 * DIL_WIDTH),
                                       lambda i: (i // per_b, 0, i % per_b, 0))
    fold_shape = lambda d: jax.ShapeDtypeStruct((B, d, S // d, 3 * DIL_WIDTH), BF16)
    cmp_w = CMP_STRIDE * HEAD_DIM
    assert S // SEL_LEN <= SEL_BLOCKS_MAX
    return pl.pallas_call(
        functools.partial(_inproj_body, per_b=per_b),
        grid=(T // tm,),
        in_specs=[
            pl.BlockSpec((tm, D_MODEL), lambda i: (i, 0)),
            _layer_spec(g, layer), _layer_spec(w, layer),
            pl.BlockSpec((tm, LANES), lambda i: (i, 0)),
            pl.BlockSpec((tm, LANES), lambda i: (i, 0)),
            pl.BlockSpec((tm, LANES), lambda i: (i, 0)),
        ],
        out_specs=[pl.BlockSpec((tm, N_PROJ), lambda i: (i, 0)), *[fold_spec(d) for d in dils],
                   pl.BlockSpec((2, 1, NSA_KV_HEADS, tm // CMP_STRIDE, cmp_w),
                                lambda i: (0, i // per_b, 0, i % per_b, 0))],
        out_shape=[jax.ShapeDtypeStruct((T, N_PROJ), BF16), *[fold_shape(d) for d in dils],
                   jax.ShapeDtypeStruct((2, B, NSA_KV_HEADS, S // CMP_STRIDE, cmp_w), BF16)],
        scratch_shapes=[pltpu.VMEM((N_FOLD // LANES, tm, LANES), F32),
                        pltpu.VMEM((N_W_IN, D_MODEL), BF16)],
        compiler_params=_params(),
        name="inproj",
    )(x2, g, w, rc, rsa, rsb)


def _compress_body(a_ref, w1_ref, w2_ref, pos_ref, o_ref):
    nb, ncp = o_ref.shape[1], o_ref.shape[2]
    a = a_ref[0]
    w1 = w1_ref[0, 0]
    half = CMP_STRIDE * HEAD_DIM
    top = _dot(a, w1[:half])
    bot = _dot(a, w1[half:])
    pc = _dot(pos_ref[0, 0], w1)
    rows = a.shape[0]
    hid = top + pltpu.roll(bot, rows - 1, 0) + pc[0:1]
    out = _dot(_silu(hid).astype(BF16), w2_ref[0, 0])
    for b in range(nb):
        heads = [out[(b * NSA_KV_HEADS + g) * ncp:(b * NSA_KV_HEADS + g + 1) * ncp]
                 for g in range(NSA_KV_HEADS)]
        o_ref[0, b] = jnp.concatenate(heads, axis=1).astype(BF16)


def _compress(a, w1, w2, pos, layer, B):
    n, rows, _ = a.shape
    ncp = rows // (B * NSA_KV_HEADS)
    per_kv = lambda arr: pl.BlockSpec((1, 1) + arr.shape[2:], lambda i: (layer, i, 0, 0))
    return pl.pallas_call(
        _compress_body,
        grid=(n,),
        in_specs=[pl.BlockSpec((1, rows, CMP_STRIDE * HEAD_DIM), lambda i: (i, 0, 0)),
                  per_kv(w1), per_kv(w2), per_kv(pos)],
        out_specs=pl.BlockSpec((1, B, ncp, NSA_KV_HEADS * HEAD_DIM), lambda i: (i, 0, 0, 0)),
        out_shape=jax.ShapeDtypeStruct((n, B, ncp, NSA_KV_HEADS * HEAD_DIM), BF16),
        compiler_params=_params(),
        name="compress",
    )(a, w1, w2, pos)


def _softmax2(s):
    m = jnp.max(s, axis=-1, keepdims=True)
    e = jnp.exp2(s - m)
    return e.astype(BF16), jnp.sum(e, axis=-1, keepdims=True)


def _weighted_values(e, l, v):
    nh, tq, nk = e.shape
    return _dot(e.reshape(nh * tq, nk), v) / l.reshape(nh * tq, 1)


SEL_PREFIX = 512


def _nsa_body(q_ref, kc_ref, vc_ref, ksel_ref, vsel_ref, kwin_ref, vwin_ref, gate_ref, mt_ref,
              o_ref, osel_ref, *, tq, seq):
    nblk = seq // SEL_LEN
    ncp = kc_ref.shape[1]
    n_cmp = (seq - CMP_LEN) // CMP_STRIDE + 1
    q0 = pl.program_id(1) * tq
    q = q_ref[0]
    t_col = q0 + lax.broadcasted_iota(jnp.int32, (tq, 1), 0)
    t_row = q0 + lax.broadcasted_iota(jnp.int32, (1, tq), 1)
    gates = jax.nn.sigmoid(gate_ref[0].astype(F32))

    zeros64 = jnp.zeros((tq, HEAD_DIM), BF16)

    def stacked_q(g):
        parts = []
        for hh in range(HEADS_PER_KV):
            h = g * HEADS_PER_KV + hh
            qh = q[:, h * HEAD_DIM:(h + 1) * HEAD_DIM]
            parts.append(jnp.concatenate([qh, zeros64] if g == 0 else [zeros64, qh], axis=1))
        return jnp.concatenate(parts, axis=0)

    qs = [stacked_q(g) for g in range(NSA_KV_HEADS)]

    cidx = lax.broadcasted_iota(jnp.int32, (tq, ncp), 1)
    cmask = ((cidx * CMP_STRIDE + (CMP_LEN - 1)) <= t_col) & (cidx < n_cmp)
    jidx = lax.broadcasted_iota(jnp.int32, (nblk, tq), 0)
    cur = lax.shift_right_logical(t_row, 6)
    forced = (jidx == 0) | (jidx == cur) | (jidx == cur - 1)
    future = jidx > cur
    groups = range(NSA_KV_HEADS)
    span = WIN + tq
    ks = pl.multiple_of(jnp.maximum(q0 - WIN, 0), tq)
    kw = kwin_ref[0, pl.ds(ks, span), :]
    vw = vwin_ref[0, pl.ds(ks, span), :]
    wpos = ks + lax.broadcasted_iota(jnp.int32, (1, span), 1)
    wmask = (wpos <= t_col) & (t_col - wpos <= WIN - 1)

    s_cmp = [jnp.where(cmask[None], _dot_nt(qs[g], kc_ref[0]).reshape(HEADS_PER_KV, tq, ncp), NEG)
             for g in groups]
    s_win = [jnp.where(wmask[None], _dot_nt(qs[g], kw).reshape(HEADS_PER_KV, tq, span), NEG)
             for g in groups]
    p_cmp = []
    for g in groups:
        m = jnp.max(s_cmp[g], axis=-1, keepdims=True)
        e = jnp.where(cmask[None], jnp.exp2(s_cmp[g] - m), 0.0)
        den = jnp.sum(e, axis=-1, keepdims=True)
        p_cmp.append(e / jnp.where(den > 0, den, 1.0))
    e_win = [_softmax2(s_win[g]) for g in groups]
    o_cmp = [_dot(p_cmp[g].astype(BF16).reshape(HEADS_PER_KV * tq, ncp), vc_ref[0]) for g in groups]
    imps = [_dot_nt(mt_ref[...], p_cmp[g][0] + p_cmp[g][1] + p_cmp[g][2] + p_cmp[g][3],
                    precision=lax.Precision.HIGHEST) for g in groups]
    o_win = [_weighted_values(*e_win[g], vw) for g in groups]

    first_blk = lax.shift_right_logical(q0, 6)
    picked_rows, before_rows = [], []
    for g in groups:
        imp = jnp.where(forced, BIG, imps[g])
        imp = jnp.where(future, -BIG, imp)
        rank = jnp.zeros((nblk, tq), jnp.int32)
        for i in range(nblk):
            row = imp[i:i + 1, :]
            beats = (row > imp) | ((row == imp) & (jidx > i))
            rank = rank + beats.astype(jnp.int32)
        picked_rows.append(jnp.where(rank < N_SEL, 0.0, MASK_BIAS))
        before_rows.append(jnp.where((rank < N_SEL) & (jidx < first_blk), 0.0, MASK_BIAS))
    assert 2 * NSA_KV_HEADS * SEL_BLOCKS_MAX == LANES
    bias_t = jnp.concatenate(before_rows + picked_rows, axis=0)
    bias_main = bias_t.T
    bias_diag = pltpu.roll(bias_main, LANES // 2, 1)
    lane_group = lax.shift_right_logical(lax.broadcasted_iota(jnp.int32, (tq, LANES), 1), 5)

    def with_bias(g, bias):
        own = jnp.where(lane_group == g, bias, 0.0).astype(BF16)
        return jnp.concatenate([qs[g], jnp.concatenate([own] * HEADS_PER_KV, axis=0)], axis=1)

    qb = [with_bias(g, bias_main) for g in groups]

    kdiag = ksel_ref[0, pl.ds(pl.multiple_of(q0, tq), tq), :]
    vdiag = vsel_ref[0, pl.ds(pl.multiple_of(q0, tq), tq), :]
    tri = (lax.broadcasted_iota(jnp.int32, (tq, tq), 1) <= lax.broadcasted_iota(jnp.int32, (tq, tq), 0))
    s_diag = [jnp.where(tri[None], _dot_nt(with_bias(g, bias_diag), kdiag)
                        .reshape(HEADS_PER_KV, tq, tq), NEG) for g in groups]
    n_prefix = q0 // SEL_PREFIX + 1
    for n in range(1, seq // SEL_PREFIX + 1):
        klen = n * SEL_PREFIX

        @pl.when(n_prefix == n)
        def _(klen=klen):
            vall = jnp.concatenate([vsel_ref[0, :klen, :], vdiag], axis=0)
            s = [jnp.concatenate(
                [_dot_nt(qb[g], ksel_ref[0, :klen, :]).reshape(HEADS_PER_KV, tq, klen), s_diag[g]],
                axis=-1) for g in groups]
            ew = [_softmax2(s[g]) for g in groups]
            for g in groups:
                osel_ref[g] = _weighted_values(*ew[g], vall)

    o_sel = [osel_ref[g] for g in groups]

    outs = []
    for g in range(NSA_KV_HEADS):
        for hh in range(HEADS_PER_KV):
            h = g * HEADS_PER_KV + hh
            acc = jnp.zeros((tq, HEAD_DIM), F32)
            for br, o in enumerate((o_cmp[g], o_sel[g], o_win[g])):
                oh = o[hh * tq:(hh + 1) * tq, g * HEAD_DIM:(g + 1) * HEAD_DIM]
                acc = acc + gates[:, 3 * h + br:3 * h + br + 1] * oh
            outs.append(acc)
    o_ref[0] = jnp.concatenate(outs, axis=1).astype(BF16)


def _nsa(proj3, kc, vc, mt, tq=128):
    B, S, _ = proj3.shape
    blk = lambda c: c // LANES
    seq_spec = lambda c: pl.BlockSpec((1, S, LANES), lambda b, i: (b, 0, blk(c)))
    ncp = kc.shape[1]
    return pl.pallas_call(
        functools.partial(_nsa_body, tq=tq, seq=S),
        grid=(B, S // tq),
        in_specs=[
            pl.BlockSpec((1, tq, NSA_HEADS * HEAD_DIM),
                         lambda b, i: (b, i, COL_QA // (NSA_HEADS * HEAD_DIM))),
            pl.BlockSpec((1, ncp, LANES), lambda b, i: (b, 0, 0)),
            pl.BlockSpec((1, ncp, LANES), lambda b, i: (b, 0, 0)),
            pl.BlockSpec((1, S, 2 * LANES), lambda b, i: (b, 0, COL_KSEL // (2 * LANES))),
            seq_spec(COL_VSEL), seq_spec(COL_KWIN), seq_spec(COL_VWIN),
            pl.BlockSpec((1, tq, LANES), lambda b, i: (b, i, blk(COL_GATE))),
            _const_spec(mt.shape),
        ],
        out_specs=pl.BlockSpec((1, tq, NSA_HEADS * HEAD_DIM), lambda b, i: (b, i, 0)),
        out_shape=jax.ShapeDtypeStruct((B, S, NSA_HEADS * HEAD_DIM), BF16),
        scratch_shapes=[pltpu.VMEM((NSA_KV_HEADS, HEADS_PER_KV * tq, LANES), F32)],
        compiler_params=_params(),
        name="nsa",
    )(proj3, kc, vc, proj3, proj3, proj3, proj3, proj3, mt)


DIL_SUB = 128
DIL_ROWS = 1024


def _dil_body(q_ref, kp_ref, kc_ref, vp_ref, vc_ref, o_ref, lse_ref, *, n_back):
    sub = DIL_SUB
    rb, tq = q_ref.shape[1], q_ref.shape[2]
    t0 = pl.program_id(2) * tq
    head_of = lax.shift_right_logical(lax.broadcasted_iota(jnp.int32, (sub, DIL_WIDTH), 1), 6)
    lane = lax.broadcasted_iota(jnp.int32, (sub, LANES), 1)
    diff = (sub + lax.broadcasted_iota(jnp.int32, (sub, 1), 0)
            - lax.broadcasted_iota(jnp.int32, (1, 2 * sub), 1))
    band = (diff >= 0) & (diff <= n_back)
    band0 = band & (lax.broadcasted_iota(jnp.int32, (1, 2 * sub), 1) + t0 >= sub)
    tiles = [(r, j) for r in range(rb) for j in range(tq // sub)]
    keys = {r: jnp.concatenate([kp_ref[0, r], kc_ref[0, r]], axis=0) for r in range(rb)}
    vals = {r: jnp.concatenate([vp_ref[0, r], vc_ref[0, r]], axis=0) for r in range(rb)}
    scores = []
    for r, j in tiles:
        q = q_ref[0, r, j * sub:(j + 1) * sub, :]
        qs = jnp.concatenate([jnp.where(head_of == h, q, jnp.zeros_like(q))
                              for h in range(DIL_HEADS)], axis=0)
        s = _dot_nt(qs, keys[r][j * sub:(j + 2) * sub]).reshape(DIL_HEADS, sub, 2 * sub)
        scores.append(jnp.where((band0 if j == 0 else band)[None], s, NEG))
    stats = []
    for s in scores:
        m = jnp.max(s, axis=-1, keepdims=True)
        e = jnp.exp2(s - m)
        stats.append((m, e, jnp.sum(e, axis=-1, keepdims=True)))
    for (r, j), (m, e, l) in zip(tiles, stats):
        o = _dot(e.astype(BF16).reshape(DIL_HEADS * sub, 2 * sub), vals[r][j * sub:(j + 2) * sub])
        o = o.reshape(DIL_HEADS, sub, DIL_WIDTH) / l
        lse = m * LN2 + jnp.log(l)
        o_acc = jnp.zeros((sub, DIL_WIDTH), F32)
        lse_out = jnp.zeros((sub, LANES), F32)
        for h in range(DIL_HEADS):
            o_acc = jnp.where(head_of == h, o[h], o_acc)
            lse_out = jnp.where(lane == h, lse[h], lse_out)
        o_ref[0, r, j * sub:(j + 1) * sub, :] = o_acc
        lse_ref[0, r, j * sub:(j + 1) * sub, :] = lse_out


def _dilated(arr, cols, window, dilation, name):
    B, d, L, _ = arr.shape
    n_back = window // dilation
    tq = min(L, DIL_ROWS)
    rb = DIL_ROWS // tq
    assert d == dilation and n_back <= DIL_SUB and L % tq == 0 and d % rb == 0
    per = tq // DIL_SUB
    qc, kc, vc = cols
    cur = lambda c: pl.BlockSpec((1, rb, tq, DIL_WIDTH), lambda b, r, i: (b, r, i, c))
    prev = lambda c: pl.BlockSpec((1, rb, DIL_SUB, DIL_WIDTH),
                                  lambda b, r, i: (b, r, jnp.maximum(i * per - 1, 0), c))
    return pl.pallas_call(
        functools.partial(_dil_body, n_back=n_back),
        grid=(B, dilation // rb, L // tq),
        in_specs=[cur(qc), prev(kc), cur(kc), prev(vc), cur(vc)],
        out_specs=[pl.BlockSpec((1, rb, tq, DIL_WIDTH), lambda b, r, i: (b, r, i, 0)),
                   pl.BlockSpec((1, rb, tq, LANES), lambda b, r, i: (b, r, i, 0))],
        out_shape=[jax.ShapeDtypeStruct((B, dilation, L, DIL_WIDTH), F32),
                   jax.ShapeDtypeStruct((B, dilation, L, LANES), F32)],
        compiler_params=_params(),
        name=name,
    )(arr, arr, arr, arr, arr)


def _mixout_body(oa_ref, ob0_ref, ob1_ref, ob2_ref, l0_ref, l1_ref, l2_ref, mg_ref, x_ref,
                 pa_ref, pb_ref, wo_ref, out_ref, so_ref, sl_ref):
    tm = x_ref.shape[0]

    def interleaved(src_ref, st_ref):
        d = src_ref.shape[1]
        if d == 1:
            return src_ref[0, 0]
        slabs = src_ref.shape[3] // LANES
        for r in range(d):
            for k in range(slabs):
                st_ref[k, pl.ds(r, tm // d, stride=d), :] = src_ref[0, r, :, k * LANES:(k + 1) * LANES]
        return jnp.concatenate([st_ref[k] for k in range(slabs)], axis=1)

    lses = [interleaved(l, sl_ref) for l in (l0_ref, l1_ref, l2_ref)]
    mx = jnp.maximum(jnp.maximum(lses[0], lses[1]), lses[2])
    ws = [jnp.exp(l - mx) for l in lses]
    den = ws[0] + ws[1] + ws[2]
    ob = jnp.zeros((tm, DIL_WIDTH), F32)
    for w, o_ref in zip(ws, (ob0_ref, ob1_ref, ob2_ref)):
        alpha = w / den
        wide = jnp.concatenate(
            [jnp.broadcast_to(alpha[:, h:h + 1], (tm, HEAD_DIM)) for h in range(DIL_HEADS)], axis=1)
        ob = ob + wide * interleaved(o_ref, so_ref)
    ya = _dot(oa_ref[...], pa_ref[0])
    yb = _dot(ob.astype(BF16), pb_ref[0])
    gm = jax.nn.sigmoid(mg_ref[...].astype(F32))
    y = gm[:, :D_MODEL] * ya + gm[:, D_MODEL:] * yb
    out_ref[...] = x_ref[...] + _dot(y.astype(BF16), wo_ref[0])


def _mixout(oa, obs, lses, proj, x2, pa, pb, wo, layer, S, tm=512):
    T = x2.shape[0]
    per_b = S // tm
    row = lambda w: pl.BlockSpec((tm, w), lambda i: (i, 0))
    folded = lambda a: pl.BlockSpec((1, a.shape[1], tm // a.shape[1], a.shape[3]),
                                    lambda i: (i // per_b, 0, i % per_b, 0))
    return pl.pallas_call(
        _mixout_body,
        grid=(T // tm,),
        in_specs=[row(NSA_HEADS * HEAD_DIM), *[folded(a) for a in obs], *[folded(a) for a in lses],
                  pl.BlockSpec((tm, 2 * D_MODEL), lambda i: (i, COL_MERGE // (2 * D_MODEL))),
                  row(D_MODEL),
                  _layer_spec(pa, layer), _layer_spec(pb, layer), _layer_spec(wo, layer)],
        out_specs=row(D_MODEL),
        out_shape=jax.ShapeDtypeStruct((T, D_MODEL), F32),
        scratch_shapes=[pltpu.VMEM((DIL_WIDTH // LANES, tm, LANES), F32),
                        pltpu.VMEM((1, tm, LANES), F32)],
        compiler_params=_params(),
        name="mixout",
    )(oa, *obs, *lses, proj, x2, pa, pb, wo)


FF_CHUNK = 512


def _ffn_body(x_ref, g_ref, w1_ref, w3_ref, w2_ref, *rest):
    n_cast = (len(rest) - 1) // 2
    src_refs, o_ref, dst_refs = rest[:n_cast], rest[n_cast], rest[n_cast + 1:]
    x = x_ref[...]
    h = _rms(x, g_ref[...]).astype(BF16)
    acc = jnp.zeros(x.shape, F32)
    for c0 in range(0, D_FF, FF_CHUNK):
        c1 = min(c0 + FF_CHUNK, D_FF)
        act = _silu(_dot(h, w1_ref[:, c0:c1])) * _dot(h, w3_ref[:, c0:c1])
        acc = acc + _dot(act.astype(BF16), w2_ref[c0:c1, :])
    o_ref[...] = x + acc
    for src, dst in zip(src_refs, dst_refs):
        dst[...] = src[...].astype(BF16)


def _ffn(x2, g, w1, w3, w2, to_bf16=(), tm=512):
    T = x2.shape[0]
    steps = T // tm
    held = lambda a: pl.BlockSpec(a.shape, lambda i: (0, 0), pipeline_mode=pl.Buffered(1))
    sliced = lambda a: pl.BlockSpec((a.shape[0] // steps, a.shape[1]), lambda i: (i, 0))
    assert all(a.shape[0] % (16 * steps) == 0 for a in to_bf16)
    out, *copies = pl.pallas_call(
        _ffn_body,
        grid=(steps,),
        in_specs=[pl.BlockSpec((tm, D_MODEL), lambda i: (i, 0)), _const_spec(g.shape),
                  held(w1), held(w3), held(w2), *[sliced(a) for a in to_bf16]],
        out_specs=[pl.BlockSpec((tm, D_MODEL), lambda i: (i, 0)), *[sliced(a) for a in to_bf16]],
        out_shape=[jax.ShapeDtypeStruct((T, D_MODEL), F32),
                   *[jax.ShapeDtypeStruct(a.shape, BF16) for a in to_bf16]],
        compiler_params=_params(),
        name="ffn",
    )(x2, g, w1, w3, w2, *to_bf16)
    return out, copies


MOE_TM = 1024
MOE_CHUNK = 128
MOE_BIG = 2 * MOE_CHUNK
MOE_MERGED = 4


def _router_body(x_ref, g_ref, r_ref, h_ref, combw_ref, posw_ref, post_ref, cnt_ref):
    tm = x_ref.shape[0]
    h = _rms(x_ref[...], g_ref[...])
    h_ref[...] = h.astype(BF16)
    r = r_ref[...]
    h_hi, r_hi = h.astype(BF16), r.astype(BF16)
    h_lo = (h - h_hi.astype(F32)).astype(BF16)
    r_lo = (r - r_hi.astype(F32)).astype(BF16)
    logits = _dot(h_hi, r_hi) + (_dot(h_hi, r_lo) + _dot(h_lo, r_hi))
    lane = lax.broadcasted_iota(jnp.int32, logits.shape, 1)
    lg = jnp.where(lane < N_EXPERTS, logits, NEG)
    m1 = jnp.max(lg, axis=-1, keepdims=True)
    i1 = jnp.min(jnp.where(lg == m1, lane, LANES), axis=-1, keepdims=True)
    lg2 = jnp.where(lane == i1, NEG, lg)
    m2 = jnp.max(lg2, axis=-1, keepdims=True)
    i2 = jnp.min(jnp.where(lg2 == m2, lane, LANES), axis=-1, keepdims=True)
    e2 = jnp.exp(m2 - m1)
    den = 1.0 + e2
    w_first, w_second = 1.0 / den, e2 / den
    chosen = [jnp.broadcast_to((i1 == ex) | (i2 == ex), (tm, LANES)) for ex in range(N_EXPERTS)]
    ones = jnp.concatenate([jnp.where(c, 1.0, 0.0).astype(BF16) for c in chosen], axis=1)
    row = lax.broadcasted_iota(jnp.int32, (LANES, LANES), 0)
    col = lax.broadcasted_iota(jnp.int32, (LANES, LANES), 1)
    tri = jnp.where(col < row, 1.0, 0.0).astype(BF16)
    running = jnp.zeros((1, N_EXPERTS * LANES), F32)
    parts = []
    for b in range(tm // LANES):
        blk = ones[b * LANES:(b + 1) * LANES]
        parts.append(_dot(tri, blk) + running)
        running = running + jnp.sum(blk.astype(F32), axis=0, keepdims=True)
    before = jnp.concatenate(parts, axis=0)
    by_lane = jnp.zeros((tm, LANES), F32)
    cnt = jnp.zeros((1, LANES), F32)
    for ex in range(N_EXPERTS):
        slab = slice(ex * LANES, (ex + 1) * LANES)
        combw_ref[:, slab] = jnp.broadcast_to(
            jnp.where(i1 == ex, w_first, 0.0) + jnp.where(i2 == ex, w_second, 0.0), (tm, LANES))
        pos = jnp.where(chosen[ex], before[:, slab], -1.0)
        posw_ref[:, slab] = pos
        by_lane = jnp.where(lane == ex, pos, by_lane)
        cnt = jnp.where(lane[:1] == ex, running[:, slab], cnt)
    post_ref[0] = by_lane.T[:N_EXPERTS]
    cnt_ref[0] = jnp.broadcast_to(cnt, (8, LANES)).astype(jnp.int32)


def _router(x2, g, r, tm=MOE_TM):
    T = x2.shape[0]
    nt = T // tm
    row = lambda w: pl.BlockSpec((tm, w), lambda i: (i, 0))
    return pl.pallas_call(
        _router_body,
        grid=(nt,),
        in_specs=[row(D_MODEL), _const_spec(g.shape), _const_spec(r.shape)],
        out_specs=[row(D_MODEL), row(N_EXPERTS * LANES), row(N_EXPERTS * LANES),
                   pl.BlockSpec((1, N_EXPERTS, tm), lambda i: (i, 0, 0)),
                   pl.BlockSpec((1, 8, LANES), lambda i: (i, 0, 0))],
        out_shape=[jax.ShapeDtypeStruct((T, D_MODEL), BF16),
                   jax.ShapeDtypeStruct((T, N_EXPERTS * LANES), F32),
                   jax.ShapeDtypeStruct((T, N_EXPERTS * LANES), F32),
                   jax.ShapeDtypeStruct((nt, N_EXPERTS, tm), F32),
                   jax.ShapeDtypeStruct((nt, 8, LANES), jnp.int32)],
        compiler_params=_params(),
        name="router",
    )(x2, g, r)


def _moe_body(cnt_ref, x_ref, h_ref, comb_ref, pos_ref, post_ref, w1_ref, w3_ref, w2_ref, fn_ref,
              o_ref, y_ref):
    i = pl.program_id(0)
    e = pl.program_id(1)
    merged = MOE_MERGED * LANES
    n = cnt_ref[i * N_EXPERTS + e]
    n_big = (n + (MOE_BIG - MOE_CHUNK - 1)) // MOE_BIG
    rest_row = pl.multiple_of(n_big * MOE_BIG, MOE_BIG)
    has_rest = n > rest_row
    post_e = post_ref[0, pl.ds(e, 1), :]

    @pl.when(e == 0)
    def _():
        o_ref[...] = x_ref[...]

    @pl.when((i == 0) & (e == 0))
    def _():
        y_ref[...] = jnp.zeros_like(y_ref)

    def scatter_add(first_row, first_slot, n_slabs):
        lane_slot = lax.broadcasted_iota(jnp.int32, (1, LANES), 1).astype(F32)
        hot = [jnp.where(pos_ref[...] == first_slot + (k * LANES) + lane_slot, 1.0, 0.0).astype(BF16)
               for k in range(n_slabs)]
        scatter = hot[0] if n_slabs == 1 else jnp.concatenate(hot, axis=1)
        weight = jnp.concatenate([comb_ref[...]] * (D_MODEL // LANES), axis=1)
        ys = y_ref[pl.ds(first_row, n_slabs * LANES), :].astype(BF16)
        o_ref[...] += weight * _dot(scatter, ys)

    def chunk(r0, size):
        base = r0.astype(F32)
        slot_col = base + lax.broadcasted_iota(jnp.int32, (size, 1), 0).astype(F32)
        gather = jnp.where(post_e == slot_col, 1.0, 0.0).astype(BF16)
        xe = _dot(gather, h_ref[...]).astype(BF16)
        act = _silu(_dot(xe, w1_ref[0])) * _dot(xe, w3_ref[0])
        y = _dot(act.astype(BF16), w2_ref[0])

        @pl.when(r0 < merged)
        def _():
            y_ref[pl.ds(r0, size), :] = y

        @pl.when(r0 >= merged)
        def _():
            y_ref[merged:merged + size, :] = y
            for k in range(size // LANES):
                scatter_add(merged + k * LANES, base + float(k * LANES), 1)

    def big_chunk(c, carry):
        chunk(pl.multiple_of(c * MOE_BIG, MOE_BIG), MOE_BIG)
        return carry

    lax.fori_loop(0, n_big, big_chunk, 0)

    @pl.when(has_rest)
    def _():
        chunk(rest_row, MOE_CHUNK)

    @pl.when(n > 0)
    def _():
        scatter_add(0, 0.0, MOE_MERGED)

    @pl.when(e == pl.num_programs(1) - 1)
    def _():
        o_ref[...] = _rms(o_ref[...], fn_ref[...])


def _moe(x2, h, comb, pos, post, counts, w1, w3, w2, fn, tm=MOE_TM):
    T = x2.shape[0]
    once = lambda w: pl.BlockSpec((tm, w), lambda i, e, cnt: (i, 0), pipeline_mode=pl.Buffered(1))
    tile = lambda w: pl.BlockSpec((tm, w), lambda i, e, cnt: (i, 0))
    assert MOE_CHUNK == LANES
    slab = pl.BlockSpec((tm, LANES), lambda i, e, cnt: (i, e))
    expert = lambda a: pl.BlockSpec((1,) + a.shape[1:], lambda i, e, cnt: (e, 0, 0))
    grid_spec = pltpu.PrefetchScalarGridSpec(
        num_scalar_prefetch=1,
        grid=(T // tm, N_EXPERTS),
        in_specs=[once(D_MODEL), once(D_MODEL), slab, slab,
                  pl.BlockSpec((1, N_EXPERTS, tm), lambda i, e, cnt: (i, 0, 0)),
                  expert(w1), expert(w3), expert(w2),
                  pl.BlockSpec(fn.shape, lambda i, e, cnt: (0, 0))],
        out_specs=tile(D_MODEL),
        scratch_shapes=[pltpu.VMEM((MOE_MERGED * LANES + MOE_BIG, D_MODEL), F32)],
    )
    return pl.pallas_call(
        _moe_body,
        grid_spec=grid_spec,
        out_shape=jax.ShapeDtypeStruct((T, D_MODEL), F32),
        compiler_params=_params(),
        name="moe",
    )(counts, x2, h, comb, pos, post, w1, w3, w2, fn)


def _rope_lane_tables(positions):
    half = ROT_DIM // 2
    inv = ROPE_THETA ** (-jnp.arange(0, ROT_DIM, 2, dtype=F32) / ROT_DIM)
    ang = positions.astype(F32).reshape(-1, 1) * inv
    cos, sin = jnp.cos(ang), jnp.sin(ang)
    lane = np.arange(LANES)
    within = lane % HEAD_DIM
    pick = lane % half
    cos_l, sin_l = cos[:, pick], sin[:, pick]
    c = jnp.where(within < ROT_DIM, cos_l, 1.0)
    sa = jnp.where(within < half, -sin_l, 0.0)
    sb = jnp.where((within >= half) & (within < ROT_DIM), sin_l, 0.0)
    return c, sa, sb


def _w_in_plan():
    scale = HEAD_DIM ** -0.5 * LOG2E
    qa, kv, gate, qkv_b, merge = 0, 512, 1280, 1304, 3608
    nb = N_DIL_GROUPS * DIL_WIDTH
    kv_piece = lambda j: [(kv + LANES * j, LANES, 1.0)]
    span = lambda start, width, s=1.0: [(start + c, LANES, s) for c in range(0, width, LANES)]
    plan = (span(qa, 512, scale)
            + kv_piece(2) + kv_piece(4) + kv_piece(0)
            + kv_piece(3) + kv_piece(5)
            + [(gate, 3 * NSA_HEADS, 1.0)]
            + kv_piece(1)
            + span(merge, 2 * D_MODEL)
            + span(qkv_b, nb, scale) + span(qkv_b + nb, nb) + span(qkv_b + 2 * nb, nb))
    assert len(plan) * LANES == N_W_IN
    return plan


def _importance_matrix_t(seq, ncp):
    n_c = (seq - CMP_LEN) // CMP_STRIDE + 1
    starts = np.arange(n_c) * CMP_STRIDE
    bstart = np.arange(seq // SEL_LEN) * SEL_LEN
    overlap = np.clip(np.minimum(starts[:, None] + CMP_LEN, bstart[None, :] + SEL_LEN)
                      - np.maximum(starts[:, None], bstart[None, :]), 0, None)
    m = np.zeros((ncp, seq // SEL_LEN), np.float32)
    m[:n_c] = overlap.astype(np.float32) / CMP_LEN
    return jnp.asarray(m.T)


def _mixer(x2, layer, B, S, tables, mt, prm):
    proj, *folds, a = _inproj(x2, prm["norm_mix"], prm["w_in"], layer, *tables, B, S)
    proj3 = proj.reshape(B, S, N_PROJ)
    ncp = S // CMP_STRIDE
    cmp = _compress(a.reshape(2, B * NSA_KV_HEADS * ncp, CMP_STRIDE * HEAD_DIM),
                    prm["cmp_w1"], prm["cmp_w2"], prm["cmp_pos"], layer, B)
    oa = _nsa(proj3, cmp[0], cmp[1], mt).reshape(B * S, NSA_HEADS * HEAD_DIM)
    obs, lses = [], []
    for g, ((w, d), arr) in enumerate(zip(DIL_PATTERNS, folds)):
        o, lse = _dilated(arr, (0, 1, 2), w, d, f"dilated{g}")
        obs.append(o)
        lses.append(lse)
    return _mixout(oa, obs, lses, proj, x2, prm["p_a"], prm["p_b"], prm["w_o"], layer, S)


def kernel(x, positions, norm_mix, w_in, cmp_pos_k, cmp_pos_v, cmp_k_w1, cmp_k_w2, cmp_v_w1,
           cmp_v_w2, w_branch_a, w_branch_b, w_out, norm_ffn, ffn_w1, ffn_w3, ffn_w2, router,
           moe_w1, moe_w3, moe_w2, final_norm):
    B, S, D = x.shape
    depth = norm_mix.shape[0]
    assert depth == 2 and D == D_MODEL
    tables = _rope_lane_tables(positions)
    mt = _importance_matrix_t(S, S // CMP_STRIDE)
    cmp_pos = jnp.stack([cmp_pos_k, cmp_pos_v], axis=1).reshape(depth, 2, 1, CMP_LEN * HEAD_DIM)
    prm = {
        "norm_mix": norm_mix.reshape(depth, 1, D),
        "w_in": jnp.swapaxes(w_in, 1, 2),
        "cmp_w1": jnp.stack([cmp_k_w1, cmp_v_w1], axis=1).astype(BF16),
        "cmp_w2": jnp.stack([cmp_k_w2, cmp_v_w2], axis=1).astype(BF16),
        "cmp_pos": jnp.broadcast_to(cmp_pos, (depth, 2, 8, CMP_LEN * HEAD_DIM)).astype(BF16),
        "p_a": w_branch_a.astype(BF16), "p_b": w_branch_b.astype(BF16), "w_o": w_out.astype(BF16),
    }
    x2 = x.reshape(B * S, D)
    x2 = _mixer(x2, 0, B, S, tables, mt, prm)
    experts = [w[0].reshape(-1, w.shape[-1]) for w in (moe_w1, moe_w3, moe_w2)]
    x2, experts = _ffn(x2, norm_ffn[0].reshape(1, -1), ffn_w1[0].astype(BF16),
                       ffn_w3[0].astype(BF16), ffn_w2[0].astype(BF16), to_bf16=experts)
    ew1, ew3, ew2 = [w.reshape(m.shape[1:]) for w, m in zip(experts, (moe_w1, moe_w3, moe_w2))]
    x2 = _mixer(x2, 1, B, S, tables, mt, prm)
    g1 = norm_ffn[1].reshape(1, -1)
    r = jnp.pad(router[0], ((0, 0), (0, LANES - N_EXPERTS)))
    h, comb, pos, post, cnt = _router(x2, g1, r)
    counts = cnt[:, 0, :N_EXPERTS].reshape(-1)
    out = _moe(x2, h, comb, pos, post, counts, ew1, ew3, ew2, final_norm.reshape(1, -1))
    return out.reshape(B, S, D)
```

```python
import functools

import numpy as np
import jax
import jax.numpy as jnp
from jax import lax
from jax.experimental import pallas as pl
from jax.experimental.pallas import tpu as pltpu

F32 = jnp.float32
BF16 = jnp.bfloat16

D_MODEL = 1024
HEAD_DIM = 64
ROT_DIM = HEAD_DIM // 4
ROPE_THETA = 500000.0
EPS = 1e-6
NSA_HEADS = 8
NSA_KV_HEADS = 2
HEADS_PER_KV = NSA_HEADS // NSA_KV_HEADS
CMP_LEN = 32
CMP_STRIDE = 16
CMP_HIDDEN = 128
SEL_LEN = 64
N_SEL = 16
WIN = 512
DIL_PATTERNS = ((128, 1), (512, 4), (2048, 16))
N_DIL_GROUPS = 3
DIL_HEADS = 4
D_FF = 2816
N_EXPERTS = 8

LANES = 128
VMEM_LIMIT = 56 * 1024 * 1024
NEG = -1e30
BIG = 1e30

COL_MERGE = 0
COL_QA = 2048
COL_KSEL = 2560
COL_BLK = COL_KSEL + 128
COL_KWIN = 2816
COL_VSEL = 2944
COL_VWIN = 3072
COL_GATE = 3200
N_PROJ = 3328
DIL_WIDTH = DIL_HEADS * HEAD_DIM
N_FOLD = N_DIL_GROUPS * DIL_WIDTH
STAGE = None
IN_CHUNKS = (
    (512, True, tuple(COL_QA + 128 * j for j in range(4)), None),
    (384, True, (COL_KSEL, COL_KWIN, STAGE), ("cmp", 0)),
    (512, False, (COL_VSEL, COL_VWIN, COL_GATE, STAGE), ("cmp", 1)),
    (512, False, tuple(COL_MERGE + 128 * j for j in range(0, 4)), None),
    (512, False, tuple(COL_MERGE + 128 * j for j in range(4, 8)), None),
    (512, False, tuple(COL_MERGE + 128 * j for j in range(8, 12)), None),
    (512, False, tuple(COL_MERGE + 128 * j for j in range(12, 16)), None),
    (N_FOLD, True, (STAGE,) * 6, ("fold", 0)),
    (N_FOLD, True, (STAGE,) * 6, ("fold", 1)),
    (N_FOLD, False, (STAGE,) * 6, ("fold", 2)),
)
N_W_IN = sum(c[0] for c in IN_CHUNKS)
LOG2E = 1.4426950408889634
LN2 = 0.6931471805599453
MASK_BIAS = -(2.0 ** 100)
SEL_BLOCKS_MAX = 32


def _dot(a, b, precision=None):
    return jnp.dot(a, b, preferred_element_type=F32, precision=precision)


def _dot_nt(a, b, precision=None):
    return lax.dot_general(a, b, (((1,), (1,)), ((), ())), preferred_element_type=F32,
                           precision=precision)


def _rms(x, g):
    ms = jnp.mean(x * x, axis=-1, keepdims=True)
    return x * lax.rsqrt(ms + EPS) * g


def _silu(x):
    return x * jax.nn.sigmoid(x)


def _params(**kw):
    return pltpu.CompilerParams(vmem_limit_bytes=VMEM_LIMIT, **kw)


def _const_spec(shape):
    nd = len(shape)
    return pl.BlockSpec(shape, lambda *_: (0,) * nd)


def _inproj_body(x_ref, g_ref, wt_ref, c_ref, sa_ref, sb_ref, o_ref, f0_ref, f1_ref, f2_ref, a_ref,
                 st_ref, w_ref, *, per_b):
    tm = x_ref.shape[0]

    @pl.when(pl.program_id(0) == 0)
    def _():
        for j, (src, width, scale) in enumerate(_w_in_plan()):
            rows = wt_ref[0, src:src + LANES, :]
            w_ref[j * LANES:(j + 1) * LANES, :] = (rows * scale if scale != 1.0 else rows).astype(BF16)

    t_seq = (pl.program_id(0) % per_b) * tm + lax.broadcasted_iota(jnp.int32, (tm, LANES), 0)
    lane = lax.broadcasted_iota(jnp.int32, (tm, LANES), 1)
    blk = lax.shift_right_logical(t_seq, 6)
    hot = (lane == blk) | (lane == blk + SEL_BLOCKS_MAX)
    o_ref[:, COL_BLK:COL_BLK + LANES] = jnp.where(hot, 1.0, 0.0).astype(BF16)
    h = _rms(x_ref[...], g_ref[0]).astype(BF16)
    c = c_ref[...]
    sa = sa_ref[...]
    sb = sb_ref[...]
    start = 0
    for size, rope, dests, action in IN_CHUNKS:
        acc = _dot_nt(h, w_ref[start:start + size, :])
        start += size
        for j, dest in enumerate(dests):
            a = acc[:, j * LANES:(j + 1) * LANES]
            if rope:
                a = a * c + pltpu.roll(a, LANES - 8, 1) * sa + pltpu.roll(a, 8, 1) * sb
            if dest is STAGE:
                st_ref[j] = a
            else:
                o_ref[:, dest:dest + LANES] = a.astype(BF16)
        if action is None:
            continue
        kind, piece = action
        if kind == "fold":
            slabs = DIL_WIDTH // LANES
            for gi, f_ref in enumerate((f0_ref, f1_ref, f2_ref)):
                d = DIL_PATTERNS[gi][1]
                for r in range(d):
                    for k in range(slabs):
                        rows = st_ref[gi * slabs + k, pl.ds(r, tm // d, stride=d), :]
                        c0 = piece * DIL_WIDTH + k * LANES
                        f_ref[0, r, :, c0:c0 + LANES] = rows.astype(BF16)
        else:
            slab = dests.index(STAGE)
            nrow = tm // CMP_STRIDE
            toks = [st_ref[slab, pl.ds(j, nrow, stride=CMP_STRIDE), :] for j in range(CMP_STRIDE)]
            for g in range(NSA_KV_HEADS):
                head = slice(g * HEAD_DIM, (g + 1) * HEAD_DIM)
                for m in range(CMP_STRIDE // 2):
                    pair = jnp.concatenate([toks[2 * m][:, head], toks[2 * m + 1][:, head]], axis=1)
                    a_ref[piece, 0, g, :, m * LANES:(m + 1) * LANES] = pair.astype(BF16)


def _layer_spec(arr, layer):
    nd = arr.ndim
    return pl.BlockSpec((1,) + arr.shape[1:], lambda *_: (layer,) + (0,) * (nd - 1))


def _inproj(x2, g, w, layer, rc, rsa, rsb, B, S, tm=512):
    T = x2.shape[0]
    per_b = S // tm
    dils = [d for _, d in DIL_PATTERNS]
    fold_spec = lambda d: pl.BlockSpec((1, d, tm // d, 3 * DIL_WIDTH),
                                       lambda i: (i // per_b, 0, i % per_b, 0))
    fold_shape = lambda d: jax.ShapeDtypeStruct((B, d, S // d, 3 * DIL_WIDTH), BF16)
    cmp_w = CMP_STRIDE * HEAD_DIM
    assert S // SEL_LEN <= SEL_BLOCKS_MAX
    return pl.pallas_call(
        functools.partial(_inproj_body, per_b=per_b),
        grid=(T // tm,),
        in_specs=[
            pl.BlockSpec((tm, D_MODEL), lambda i: (i, 0)),
            _layer_spec(g, layer), _layer_spec(w, layer),
            pl.BlockSpec((tm, LANES), lambda i: (i, 0)),
            pl.BlockSpec((tm, LANES), lambda i: (i, 0)),
            pl.BlockSpec((tm, LANES), lambda i: (i, 0)),
        ],
        out_specs=[pl.BlockSpec((tm, N_PROJ), lambda i: (i, 0)), *[fold_spec(d) for d in dils],
                   pl.BlockSpec((2, 1, NSA_KV_HEADS, tm // CMP_STRIDE, cmp_w),
                                lambda i: (0, i // per_b, 0, i % per_b, 0))],
        out_shape=[jax.ShapeDtypeStruct((T, N_PROJ), BF16), *[fold_shape(d) for d in dils],
                   jax.ShapeDtypeStruct((2, B, NSA_KV_HEADS, S // CMP_STRIDE, cmp_w), BF16)],
        scratch_shapes=[pltpu.VMEM((N_FOLD // LANES, tm, LANES), F32),
                        pltpu.VMEM((N_W_IN, D_MODEL), BF16)],
        compiler_params=_params(),
        name="inproj",
    )(x2, g, w, rc, rsa, rsb)


def _compress_body(a_ref, w1_ref, w2_ref, pos_ref, o_ref):
    nb, ncp = o_ref.shape[1], o_ref.shape[2]
    a = a_ref[0]
    w1 = w1_ref[0, 0]
    half = CMP_STRIDE * HEAD_DIM
    top = _dot(a, w1[:half])
    bot = _dot(a, w1[half:])
    pc = _dot(pos_ref[0, 0], w1)
    rows = a.shape[0]
    hid = top + pltpu.roll(bot, rows - 1, 0) + pc[0:1]
    out = _dot(_silu(hid).astype(BF16), w2_ref[0, 0])
    for b in range(nb):
        heads = [out[(b * NSA_KV_HEADS + g) * ncp:(b * NSA_KV_HEADS + g + 1) * ncp]
                 for g in range(NSA_KV_HEADS)]
        o_ref[0, b] = jnp.concatenate(heads, axis=1).astype(BF16)


def _compress(a, w1, w2, pos, layer, B):
    n, rows, _ = a.shape
    ncp = rows // (B * NSA_KV_HEADS)
    per_kv = lambda arr: pl.BlockSpec((1, 1) + arr.shape[2:], lambda i: (layer, i, 0, 0))
    return pl.pallas_call(
        _compress_body,
        grid=(n,),
        in_specs=[pl.BlockSpec((1, rows, CMP_STRIDE * HEAD_DIM), lambda i: (i, 0, 0)),
                  per_kv(w1), per_kv(w2), per_kv(pos)],
        out_specs=pl.BlockSpec((1, B, ncp, NSA_KV_HEADS * HEAD_DIM), lambda i: (i, 0, 0, 0)),
        out_shape=jax.ShapeDtypeStruct((n, B, ncp, NSA_KV_HEADS * HEAD_DIM), BF16),
        compiler_params=_params(),
        name="compress",
    )(a, w1, w2, pos)


def _softmax2(s):
    m = jnp.max(s, axis=-1, keepdims=True)
    e = jnp.exp2(s - m)
    return e.astype(BF16), jnp.sum(e, axis=-1, keepdims=True)


def _weighted_values(e, l, v):
    nh, tq, nk = e.shape
    return _dot(e.reshape(nh * tq, nk), v) / l.reshape(nh * tq, 1)


SEL_PREFIX = 512


def _nsa_body(q_ref, kc_ref, vc_ref, ksel_ref, vsel_ref, kwin_ref, vwin_ref, gate_ref, mt_ref,
              o_ref, osel_ref, *, tq, seq):
    nblk = seq // SEL_LEN
    ncp = kc_ref.shape[1]
    n_cmp = (seq - CMP_LEN) // CMP_STRIDE + 1
    q0 = pl.program_id(1) * tq
    q = q_ref[0]
    t_col = q0 + lax.broadcasted_iota(jnp.int32, (tq, 1), 0)
    t_row = q0 + lax.broadcasted_iota(jnp.int32, (1, tq), 1)
    gates = jax.nn.sigmoid(gate_ref[0].astype(F32))

    zeros64 = jnp.zeros((tq, HEAD_DIM), BF16)

    def stacked_q(g):
        parts = []
        for hh in range(HEADS_PER_KV):
            h = g * HEADS_PER_KV + hh
            qh = q[:, h * HEAD_DIM:(h + 1) * HEAD_DIM]
            parts.append(jnp.concatenate([qh, zeros64] if g == 0 else [zeros64, qh], axis=1))
        return jnp.concatenate(parts, axis=0)

    qs = [stacked_q(g) for g in range(NSA_KV_HEADS)]

    cidx = lax.broadcasted_iota(jnp.int32, (tq, ncp), 1)
    cmask = ((cidx * CMP_STRIDE + (CMP_LEN - 1)) <= t_col) & (cidx < n_cmp)
    jidx = lax.broadcasted_iota(jnp.int32, (nblk, tq), 0)
    cur = lax.shift_right_logical(t_row, 6)
    forced = (jidx == 0) | (jidx == cur) | (jidx == cur - 1)
    future = jidx > cur
    groups = range(NSA_KV_HEADS)
    span = WIN + tq
    ks = pl.multiple_of(jnp.maximum(q0 - WIN, 0), tq)
    kw = kwin_ref[0, pl.ds(ks, span), :]
    vw = vwin_ref[0, pl.ds(ks, span), :]
    wpos = ks + lax.broadcasted_iota(jnp.int32, (1, span), 1)
    wmask = (wpos <= t_col) & (t_col - wpos <= WIN - 1)

    s_cmp = [jnp.where(cmask[None], _dot_nt(qs[g], kc_ref[0]).reshape(HEADS_PER_KV, tq, ncp), NEG)
             for g in groups]
    s_win = [jnp.where(wmask[None], _dot_nt(qs[g], kw).reshape(HEADS_PER_KV, tq, span), NEG)
             for g in groups]
    p_cmp = []
    for g in groups:
        m = jnp.max(s_cmp[g], axis=-1, keepdims=True)
        e = jnp.where(cmask[None], jnp.exp2(s_cmp[g] - m), 0.0)
        den = jnp.sum(e, axis=-1, keepdims=True)
        p_cmp.append(e / jnp.where(den > 0, den, 1.0))
    e_win = [_softmax2(s_win[g]) for g in groups]
    o_cmp = [_dot(p_cmp[g].astype(BF16).reshape(HEADS_PER_KV * tq, ncp), vc_ref[0]) for g in groups]
    imps = [_dot_nt(mt_ref[...], p_cmp[g][0] + p_cmp[g][1] + p_cmp[g][2] + p_cmp[g][3],
                    precision=lax.Precision.HIGHEST) for g in groups]
    o_win = [_weighted_values(*e_win[g], vw) for g in groups]

    def head_part(o, g, hh):
        return o[hh * tq:(hh + 1) * tq, g * HEAD_DIM:(g + 1) * HEAD_DIM]

    def gate_col(h, br):
        return jnp.broadcast_to(gates[:, 3 * h + br:3 * h + br + 1], (tq, HEAD_DIM))

    heads = [(g, hh, g * HEADS_PER_KV + hh) for g in groups for hh in range(HEADS_PER_KV)]
    partial = [gate_col(h, 0) * head_part(o_cmp[g], g, hh) + gate_col(h, 2) * head_part(o_win[g], g, hh)
               for g, hh, h in heads]
    sel_gate = [gate_col(h, 1) for _, _, h in heads]

    first_blk = lax.shift_right_logical(q0, 6)
    picked_rows, before_rows = [], []
    for g in groups:
        imp = jnp.where(forced, BIG, imps[g])
        imp = jnp.where(future, -BIG, imp)
        rank = jnp.zeros((nblk, tq), jnp.int32)
        for i in range(nblk):
            row = imp[i:i + 1, :]
            beats = (row > imp) | ((row == imp) & (jidx > i))
            rank = rank + beats.astype(jnp.int32)
        picked_rows.append(jnp.where(rank < N_SEL, 0.0, MASK_BIAS))
        before_rows.append(jnp.where((rank < N_SEL) & (jidx < first_blk), 0.0, MASK_BIAS))
    assert 2 * NSA_KV_HEADS * SEL_BLOCKS_MAX == LANES
    bias_t = jnp.concatenate(before_rows + picked_rows, axis=0)
    bias_main = bias_t.T
    bias_diag = pltpu.roll(bias_main, LANES // 2, 1)
    lane_group = lax.shift_right_logical(lax.broadcasted_iota(jnp.int32, (tq, LANES), 1), 5)

    def with_bias(g, bias):
        own = jnp.where(lane_group == g, bias, 0.0).astype(BF16)
        return jnp.concatenate([qs[g], jnp.concatenate([own] * HEADS_PER_KV, axis=0)], axis=1)

    qb = [with_bias(g, bias_main) for g in groups]

    kdiag = ksel_ref[0, pl.ds(pl.multiple_of(q0, tq), tq), :]
    vdiag = vsel_ref[0, pl.ds(pl.multiple_of(q0, tq), tq), :]
    tri = (lax.broadcasted_iota(jnp.int32, (tq, tq), 1) <= lax.broadcasted_iota(jnp.int32, (tq, tq), 0))
    s_diag = [jnp.where(tri[None], _dot_nt(with_bias(g, bias_diag), kdiag)
                        .reshape(HEADS_PER_KV, tq, tq), NEG) for g in groups]
    n_prefix = q0 // SEL_PREFIX + 1
    for n in range(1, seq // SEL_PREFIX + 1):
        klen = n * SEL_PREFIX

        @pl.when(n_prefix == n)
        def _(klen=klen):
            vall = jnp.concatenate([vsel_ref[0, :klen, :], vdiag], axis=0)
            s = [jnp.concatenate(
                [_dot_nt(qb[g], ksel_ref[0, :klen, :]).reshape(HEADS_PER_KV, tq, klen), s_diag[g]],
                axis=-1) for g in groups]
            ew = [_softmax2(s[g]) for g in groups]
            for g in groups:
                osel_ref[g] = _weighted_values(*ew[g], vall)

    o_sel = [osel_ref[g] for g in groups]

    outs = [partial[h] + sel_gate[h] * head_part(o_sel[g], g, hh) for g, hh, h in heads]
    o_ref[0] = jnp.concatenate(outs, axis=1).astype(BF16)


def _nsa(proj3, kc, vc, mt, tq=128):
    B, S, _ = proj3.shape
    blk = lambda c: c // LANES
    seq_spec = lambda c: pl.BlockSpec((1, S, LANES), lambda b, i: (b, 0, blk(c)))
    ncp = kc.shape[1]
    return pl.pallas_call(
        functools.partial(_nsa_body, tq=tq, seq=S),
        grid=(B, S // tq),
        in_specs=[
            pl.BlockSpec((1, tq, NSA_HEADS * HEAD_DIM),
                         lambda b, i: (b, i, COL_QA // (NSA_HEADS * HEAD_DIM))),
            pl.BlockSpec((1, ncp, LANES), lambda b, i: (b, 0, 0)),
            pl.BlockSpec((1, ncp, LANES), lambda b, i: (b, 0, 0)),
            pl.BlockSpec((1, S, 2 * LANES), lambda b, i: (b, 0, COL_KSEL // (2 * LANES))),
            seq_spec(COL_VSEL), seq_spec(COL_KWIN), seq_spec(COL_VWIN),
            pl.BlockSpec((1, tq, LANES), lambda b, i: (b, i, blk(COL_GATE))),
            _const_spec(mt.shape),
        ],
        out_specs=pl.BlockSpec((1, tq, NSA_HEADS * HEAD_DIM), lambda b, i: (b, i, 0)),
        out_shape=jax.ShapeDtypeStruct((B, S, NSA_HEADS * HEAD_DIM), BF16),
        scratch_shapes=[pltpu.VMEM((NSA_KV_HEADS, HEADS_PER_KV * tq, LANES), F32)],
        compiler_params=_params(),
        name="nsa",
    )(proj3, kc, vc, proj3, proj3, proj3, proj3, proj3, mt)


DIL_SUB = 128
DIL_ROWS = 1024


def _dil_body(q_ref, kp_ref, kc_ref, vp_ref, vc_ref, o_ref, lse_ref, *, n_back):
    sub = DIL_SUB
    rb, tq = q_ref.shape[1], q_ref.shape[2]
    t0 = pl.program_id(2) * tq
    head_of = lax.shift_right_logical(lax.broadcasted_iota(jnp.int32, (sub, DIL_WIDTH), 1), 6)
    lane = lax.broadcasted_iota(jnp.int32, (sub, LANES), 1)
    diff = (sub + lax.broadcasted_iota(jnp.int32, (sub, 1), 0)
            - lax.broadcasted_iota(jnp.int32, (1, 2 * sub), 1))
    band = (diff >= 0) & (diff <= n_back)
    band0 = band & (lax.broadcasted_iota(jnp.int32, (1, 2 * sub), 1) + t0 >= sub)
    tiles = [(r, j) for r in range(rb) for j in range(tq // sub)]
    keys = {r: jnp.concatenate([kp_ref[0, r], kc_ref[0, r]], axis=0) for r in range(rb)}
    vals = {r: jnp.concatenate([vp_ref[0, r], vc_ref[0, r]], axis=0) for r in range(rb)}
    scores = []
    for r, j in tiles:
        q = q_ref[0, r, j * sub:(j + 1) * sub, :]
        qs = jnp.concatenate([jnp.where(head_of == h, q, jnp.zeros_like(q))
                              for h in range(DIL_HEADS)], axis=0)
        s = _dot_nt(qs, keys[r][j * sub:(j + 2) * sub]).reshape(DIL_HEADS, sub, 2 * sub)
        scores.append(jnp.where((band0 if j == 0 else band)[None], s, NEG))
    stats = []
    for s in scores:
        m = jnp.max(s, axis=-1, keepdims=True)
        e = jnp.exp2(s - m)
        stats.append((m, e, jnp.sum(e, axis=-1, keepdims=True)))
    for (r, j), (m, e, l) in zip(tiles, stats):
        o = _dot(e.astype(BF16).reshape(DIL_HEADS * sub, 2 * sub), vals[r][j * sub:(j + 2) * sub])
        o = o.reshape(DIL_HEADS, sub, DIL_WIDTH) / l
        lse = m * LN2 + jnp.log(l)
        o_acc = jnp.zeros((sub, DIL_WIDTH), F32)
        lse_out = jnp.zeros((sub, LANES), F32)
        for h in range(DIL_HEADS):
            o_acc = jnp.where(head_of == h, o[h], o_acc)
            lse_out = jnp.where(lane == h, lse[h], lse_out)
        o_ref[0, r, j * sub:(j + 1) * sub, :] = o_acc
        lse_ref[0, r, j * sub:(j + 1) * sub, :] = lse_out


def _dilated(arr, cols, window, dilation, name):
    B, d, L, _ = arr.shape
    n_back = window // dilation
    tq = min(L, DIL_ROWS)
    rb = DIL_ROWS // tq
    assert d == dilation and n_back <= DIL_SUB and L % tq == 0 and d % rb == 0
    per = tq // DIL_SUB
    qc, kc, vc = cols
    cur = lambda c: pl.BlockSpec((1, rb, tq, DIL_WIDTH), lambda b, r, i: (b, r, i, c))
    prev = lambda c: pl.BlockSpec((1, rb, DIL_SUB, DIL_WIDTH),
                                  lambda b, r, i: (b, r, jnp.maximum(i * per - 1, 0), c))
    return pl.pallas_call(
        functools.partial(_dil_body, n_back=n_back),
        grid=(B, dilation // rb, L // tq),
        in_specs=[cur(qc), prev(kc), cur(kc), prev(vc), cur(vc)],
        out_specs=[pl.BlockSpec((1, rb, tq, DIL_WIDTH), lambda b, r, i: (b, r, i, 0)),
                   pl.BlockSpec((1, rb, tq, LANES), lambda b, r, i: (b, r, i, 0))],
        out_shape=[jax.ShapeDtypeStruct((B, dilation, L, DIL_WIDTH), F32),
                   jax.ShapeDtypeStruct((B, dilation, L, LANES), F32)],
        compiler_params=_params(),
        name=name,
    )(arr, arr, arr, arr, arr)


def _mixout_body(oa_ref, ob0_ref, ob1_ref, ob2_ref, l0_ref, l1_ref, l2_ref, mg_ref, x_ref,
                 pa_ref, pb_ref, wo_ref, out_ref, so_ref, sl_ref):
    tm = x_ref.shape[0]

    def interleaved(src_ref, st_ref):
        d = src_ref.shape[1]
        if d == 1:
            return src_ref[0, 0]
        slabs = src_ref.shape[3] // LANES
        for r in range(d):
            for k in range(slabs):
                st_ref[k, pl.ds(r, tm // d, stride=d), :] = src_ref[0, r, :, k * LANES:(k + 1) * LANES]
        return jnp.concatenate([st_ref[k] for k in range(slabs)], axis=1)

    lses = [interleaved(l, sl_ref) for l in (l0_ref, l1_ref, l2_ref)]
    mx = jnp.maximum(jnp.maximum(lses[0], lses[1]), lses[2])
    ws = [jnp.exp(l - mx) for l in lses]
    den = ws[0] + ws[1] + ws[2]
    ob = jnp.zeros((tm, DIL_WIDTH), F32)
    for w, o_ref in zip(ws, (ob0_ref, ob1_ref, ob2_ref)):
        alpha = w / den
        wide = jnp.concatenate(
            [jnp.broadcast_to(alpha[:, h:h + 1], (tm, HEAD_DIM)) for h in range(DIL_HEADS)], axis=1)
        ob = ob + wide * interleaved(o_ref, so_ref)
    ya = _dot(oa_ref[...], pa_ref[0])
    yb = _dot(ob.astype(BF16), pb_ref[0])
    gm = jax.nn.sigmoid(mg_ref[...].astype(F32))
    y = gm[:, :D_MODEL] * ya + gm[:, D_MODEL:] * yb
    out_ref[...] = x_ref[...] + _dot(y.astype(BF16), wo_ref[0])


def _mixout(oa, obs, lses, proj, x2, pa, pb, wo, layer, S, tm=1024):
    T = x2.shape[0]
    per_b = S // tm
    row = lambda w: pl.BlockSpec((tm, w), lambda i: (i, 0))
    folded = lambda a: pl.BlockSpec((1, a.shape[1], tm // a.shape[1], a.shape[3]),
                                    lambda i: (i // per_b, 0, i % per_b, 0))
    return pl.pallas_call(
        _mixout_body,
        grid=(T // tm,),
        in_specs=[row(NSA_HEADS * HEAD_DIM), *[folded(a) for a in obs], *[folded(a) for a in lses],
                  pl.BlockSpec((tm, 2 * D_MODEL), lambda i: (i, COL_MERGE // (2 * D_MODEL))),
                  row(D_MODEL),
                  _layer_spec(pa, layer), _layer_spec(pb, layer), _layer_spec(wo, layer)],
        out_specs=row(D_MODEL),
        out_shape=jax.ShapeDtypeStruct((T, D_MODEL), F32),
        scratch_shapes=[pltpu.VMEM((DIL_WIDTH // LANES, tm, LANES), F32),
                        pltpu.VMEM((1, tm, LANES), F32)],
        compiler_params=_params(),
        name="mixout",
    )(oa, *obs, *lses, proj, x2, pa, pb, wo)


FF_CHUNK = 512


def _ffn_body(x_ref, g_ref, w1_ref, w3_ref, w2_ref, *rest):
    n_cast = (len(rest) - 1) // 2
    src_refs, o_ref, dst_refs = rest[:n_cast], rest[n_cast], rest[n_cast + 1:]
    x = x_ref[...]
    h = _rms(x, g_ref[...]).astype(BF16)
    acc = jnp.zeros(x.shape, F32)
    for c0 in range(0, D_FF, FF_CHUNK):
        c1 = min(c0 + FF_CHUNK, D_FF)
        act = _silu(_dot(h, w1_ref[:, c0:c1])) * _dot(h, w3_ref[:, c0:c1])
        acc = acc + _dot(act.astype(BF16), w2_ref[c0:c1, :])
    o_ref[...] = x + acc
    for src, dst in zip(src_refs, dst_refs):
        dst[...] = src[...].astype(BF16)


def _ffn(x2, g, w1, w3, w2, to_bf16=(), tm=512):
    T = x2.shape[0]
    steps = T // tm
    held = lambda a: pl.BlockSpec(a.shape, lambda i: (0, 0), pipeline_mode=pl.Buffered(1))
    sliced = lambda a: pl.BlockSpec((a.shape[0] // steps, a.shape[1]), lambda i: (i, 0))
    assert all(a.shape[0] % (16 * steps) == 0 for a in to_bf16)
    out, *copies = pl.pallas_call(
        _ffn_body,
        grid=(steps,),
        in_specs=[pl.BlockSpec((tm, D_MODEL), lambda i: (i, 0)), _const_spec(g.shape),
                  held(w1), held(w3), held(w2), *[sliced(a) for a in to_bf16]],
        out_specs=[pl.BlockSpec((tm, D_MODEL), lambda i: (i, 0)), *[sliced(a) for a in to_bf16]],
        out_shape=[jax.ShapeDtypeStruct((T, D_MODEL), F32),
                   *[jax.ShapeDtypeStruct(a.shape, BF16) for a in to_bf16]],
        compiler_params=_params(),
        name="ffn",
    )(x2, g, w1, w3, w2, *to_bf16)
    return out, copies


MOE_TM = 1024
MOE_CHUNK = 128
MOE_BIG = 2 * MOE_CHUNK
MOE_MERGED = 4


def _router_body(x_ref, g_ref, r_ref, h_ref, combw_ref, posw_ref, post_ref, cnt_ref):
    tm = x_ref.shape[0]
    h = _rms(x_ref[...], g_ref[...])
    h_ref[...] = h.astype(BF16)
    r = r_ref[...]
    h_hi, r_hi = h.astype(BF16), r.astype(BF16)
    h_lo = (h - h_hi.astype(F32)).astype(BF16)
    r_lo = (r - r_hi.astype(F32)).astype(BF16)
    logits = _dot(h_hi, r_hi) + (_dot(h_hi, r_lo) + _dot(h_lo, r_hi))
    lane = lax.broadcasted_iota(jnp.int32, logits.shape, 1)
    lg = jnp.where(lane < N_EXPERTS, logits, NEG)
    m1 = jnp.max(lg, axis=-1, keepdims=True)
    i1 = jnp.min(jnp.where(lg == m1, lane, LANES), axis=-1, keepdims=True)
    lg2 = jnp.where(lane == i1, NEG, lg)
    m2 = jnp.max(lg2, axis=-1, keepdims=True)
    i2 = jnp.min(jnp.where(lg2 == m2, lane, LANES), axis=-1, keepdims=True)
    e2 = jnp.exp(m2 - m1)
    den = 1.0 + e2
    w_first, w_second = 1.0 / den, e2 / den
    chosen = [jnp.broadcast_to((i1 == ex) | (i2 == ex), (tm, LANES)) for ex in range(N_EXPERTS)]
    ones = jnp.concatenate([jnp.where(c, 1.0, 0.0).astype(BF16) for c in chosen], axis=1)
    row = lax.broadcasted_iota(jnp.int32, (LANES, LANES), 0)
    col = lax.broadcasted_iota(jnp.int32, (LANES, LANES), 1)
    tri = jnp.where(col < row, 1.0, 0.0).astype(BF16)
    running = jnp.zeros((1, N_EXPERTS * LANES), F32)
    parts = []
    for b in range(tm // LANES):
        blk = ones[b * LANES:(b + 1) * LANES]
        parts.append(_dot(tri, blk) + running)
        running = running + jnp.sum(blk.astype(F32), axis=0, keepdims=True)
    before = jnp.concatenate(parts, axis=0)
    by_lane = jnp.zeros((tm, LANES), F32)
    cnt = jnp.zeros((1, LANES), F32)
    for ex in range(N_EXPERTS):
        slab = slice(ex * LANES, (ex + 1) * LANES)
        combw_ref[:, slab] = jnp.broadcast_to(
            jnp.where(i1 == ex, w_first, 0.0) + jnp.where(i2 == ex, w_second, 0.0), (tm, LANES))
        pos = jnp.where(chosen[ex], before[:, slab], -1.0)
        posw_ref[:, slab] = pos
        by_lane = jnp.where(lane == ex, pos, by_lane)
        cnt = jnp.where(lane[:1] == ex, running[:, slab], cnt)
    post_ref[0] = by_lane.T[:N_EXPERTS]
    cnt_ref[0] = jnp.broadcast_to(cnt, (8, LANES)).astype(jnp.int32)


def _router(x2, g, r, tm=MOE_TM):
    T = x2.shape[0]
    nt = T // tm
    row = lambda w: pl.BlockSpec((tm, w), lambda i: (i, 0))
    return pl.pallas_call(
        _router_body,
        grid=(nt,),
        in_specs=[row(D_MODEL), _const_spec(g.shape), _const_spec(r.shape)],
        out_specs=[row(D_MODEL), row(N_EXPERTS * LANES), row(N_EXPERTS * LANES),
                   pl.BlockSpec((1, N_EXPERTS, tm), lambda i: (i, 0, 0)),
                   pl.BlockSpec((1, 8, LANES), lambda i: (i, 0, 0))],
        out_shape=[jax.ShapeDtypeStruct((T, D_MODEL), BF16),
                   jax.ShapeDtypeStruct((T, N_EXPERTS * LANES), F32),
                   jax.ShapeDtypeStruct((T, N_EXPERTS * LANES), F32),
                   jax.ShapeDtypeStruct((nt, N_EXPERTS, tm), F32),
                   jax.ShapeDtypeStruct((nt, 8, LANES), jnp.int32)],
        compiler_params=_params(),
        name="router",
    )(x2, g, r)


def _moe_body(cnt_ref, x_ref, h_ref, comb_ref, pos_ref, post_ref, w1_ref, w3_ref, w2_ref, fn_ref,
              o_ref, y_ref):
    i = pl.program_id(0)
    e = pl.program_id(1)
    merged = MOE_MERGED * LANES
    n = cnt_ref[i * N_EXPERTS + e]
    n_big = (n + (MOE_BIG - MOE_CHUNK - 1)) // MOE_BIG
    rest_row = pl.multiple_of(n_big * MOE_BIG, MOE_BIG)
    has_rest = n > rest_row
    post_e = post_ref[0, pl.ds(e, 1), :]

    @pl.when(e == 0)
    def _():
        o_ref[...] = x_ref[...]

    @pl.when((i == 0) & (e == 0))
    def _():
        y_ref[...] = jnp.zeros_like(y_ref)

    def scatter_add(first_row, first_slot, n_slabs):
        lane_slot = lax.broadcasted_iota(jnp.int32, (1, LANES), 1).astype(F32)
        hot = [jnp.where(pos_ref[...] == first_slot + (k * LANES) + lane_slot, 1.0, 0.0).astype(BF16)
               for k in range(n_slabs)]
        scatter = hot[0] if n_slabs == 1 else jnp.concatenate(hot, axis=1)
        weight = jnp.concatenate([comb_ref[...]] * (D_MODEL // LANES), axis=1)
        ys = y_ref[pl.ds(first_row, n_slabs * LANES), :].astype(BF16)
        o_ref[...] += weight * _dot(scatter, ys)

    def chunk(r0, size):
        base = r0.astype(F32)
        slot_col = base + lax.broadcasted_iota(jnp.int32, (size, 1), 0).astype(F32)
        gather = jnp.where(post_e == slot_col, 1.0, 0.0).astype(BF16)
        xe = _dot(gather, h_ref[...]).astype(BF16)
        act = _silu(_dot(xe, w1_ref[0])) * _dot(xe, w3_ref[0])
        y = _dot(act.astype(BF16), w2_ref[0])

        @pl.when(r0 < merged)
        def _():
            y_ref[pl.ds(r0, size), :] = y

        @pl.when(r0 >= merged)
        def _():
            y_ref[merged:merged + size, :] = y
            for k in range(size // LANES):
                scatter_add(merged + k * LANES, base + float(k * LANES), 1)

    def big_chunk(c, carry):
        chunk(pl.multiple_of(c * MOE_BIG, MOE_BIG), MOE_BIG)
        return carry

    lax.fori_loop(0, n_big, big_chunk, 0)

    @pl.when(has_rest)
    def _():
        chunk(rest_row, MOE_CHUNK)

    @pl.when(n > 0)
    def _():
        scatter_add(0, 0.0, MOE_MERGED)

    @pl.when(e == pl.num_programs(1) - 1)
    def _():
        o_ref[...] = _rms(o_ref[...], fn_ref[...])


def _moe(x2, h, comb, pos, post, counts, w1, w3, w2, fn, tm=MOE_TM):
    T = x2.shape[0]
    once = lambda w: pl.BlockSpec((tm, w), lambda i, e, cnt: (i, 0), pipeline_mode=pl.Buffered(1))
    tile = lambda w: pl.BlockSpec((tm, w), lambda i, e, cnt: (i, 0))
    assert MOE_CHUNK == LANES
    slab = pl.BlockSpec((tm, LANES), lambda i, e, cnt: (i, e))
    expert = lambda a: pl.BlockSpec((1,) + a.shape[1:], lambda i, e, cnt: (e, 0, 0))
    grid_spec = pltpu.PrefetchScalarGridSpec(
        num_scalar_prefetch=1,
        grid=(T // tm, N_EXPERTS),
        in_specs=[once(D_MODEL), once(D_MODEL), slab, slab,
                  pl.BlockSpec((1, N_EXPERTS, tm), lambda i, e, cnt: (i, 0, 0)),
                  expert(w1), expert(w3), expert(w2),
                  pl.BlockSpec(fn.shape, lambda i, e, cnt: (0, 0))],
        out_specs=tile(D_MODEL),
        scratch_shapes=[pltpu.VMEM((MOE_MERGED * LANES + MOE_BIG, D_MODEL), F32)],
    )
    return pl.pallas_call(
        _moe_body,
        grid_spec=grid_spec,
        out_shape=jax.ShapeDtypeStruct((T, D_MODEL), F32),
        compiler_params=_params(),
        name="moe",
    )(counts, x2, h, comb, pos, post, w1, w3, w2, fn)


def _rope_lane_tables(positions):
    half = ROT_DIM // 2
    inv = ROPE_THETA ** (-jnp.arange(0, ROT_DIM, 2, dtype=F32) / ROT_DIM)
    ang = positions.astype(F32).reshape(-1, 1) * inv
    cos, sin = jnp.cos(ang), jnp.sin(ang)
    lane = np.arange(LANES)
    within = lane % HEAD_DIM
    pick = lane % half
    cos_l, sin_l = cos[:, pick], sin[:, pick]
    c = jnp.where(within < ROT_DIM, cos_l, 1.0)
    sa = jnp.where(within < half, -sin_l, 0.0)
    sb = jnp.where((within >= half) & (within < ROT_DIM), sin_l, 0.0)
    return c, sa, sb


def _w_in_plan():
    scale = HEAD_DIM ** -0.5 * LOG2E
    qa, kv, gate, qkv_b, merge = 0, 512, 1280, 1304, 3608
    nb = N_DIL_GROUPS * DIL_WIDTH
    kv_piece = lambda j: [(kv + LANES * j, LANES, 1.0)]
    span = lambda start, width, s=1.0: [(start + c, LANES, s) for c in range(0, width, LANES)]
    plan = (span(qa, 512, scale)
            + kv_piece(2) + kv_piece(4) + kv_piece(0)
            + kv_piece(3) + kv_piece(5)
            + [(gate, 3 * NSA_HEADS, 1.0)]
            + kv_piece(1)
            + span(merge, 2 * D_MODEL)
            + span(qkv_b, nb, scale) + span(qkv_b + nb, nb) + span(qkv_b + 2 * nb, nb))
    assert len(plan) * LANES == N_W_IN
    return plan


def _importance_matrix_t(seq, ncp):
    n_c = (seq - CMP_LEN) // CMP_STRIDE + 1
    starts = np.arange(n_c) * CMP_STRIDE
    bstart = np.arange(seq // SEL_LEN) * SEL_LEN
    overlap = np.clip(np.minimum(starts[:, None] + CMP_LEN, bstart[None, :] + SEL_LEN)
                      - np.maximum(starts[:, None], bstart[None, :]), 0, None)
    m = np.zeros((ncp, seq // SEL_LEN), np.float32)
    m[:n_c] = overlap.astype(np.float32) / CMP_LEN
    return jnp.asarray(m.T)


def _mixer(x2, layer, B, S, tables, mt, prm):
    proj, *folds, a = _inproj(x2, prm["norm_mix"], prm["w_in"], layer, *tables, B, S)
    proj3 = proj.reshape(B, S, N_PROJ)
    ncp = S // CMP_STRIDE
    cmp = _compress(a.reshape(2, B * NSA_KV_HEADS * ncp, CMP_STRIDE * HEAD_DIM),
                    prm["cmp_w1"], prm["cmp_w2"], prm["cmp_pos"], layer, B)
    oa = _nsa(proj3, cmp[0], cmp[1], mt).reshape(B * S, NSA_HEADS * HEAD_DIM)
    obs, lses = [], []
    for g, ((w, d), arr) in enumerate(zip(DIL_PATTERNS, folds)):
        o, lse = _dilated(arr, (0, 1, 2), w, d, f"dilated{g}")
        obs.append(o)
        lses.append(lse)
    return _mixout(oa, obs, lses, proj, x2, prm["p_a"], prm["p_b"], prm["w_o"], layer, S)


def kernel(x, positions, norm_mix, w_in, cmp_pos_k, cmp_pos_v, cmp_k_w1, cmp_k_w2, cmp_v_w1,
           cmp_v_w2, w_branch_a, w_branch_b, w_out, norm_ffn, ffn_w1, ffn_w3, ffn_w2, router,
           moe_w1, moe_w3, moe_w2, final_norm):
    B, S, D = x.shape
    depth = norm_mix.shape[0]
    assert depth == 2 and D == D_MODEL
    tables = _rope_lane_tables(positions)
    mt = _importance_matrix_t(S, S // CMP_STRIDE)
    cmp_pos = jnp.stack([cmp_pos_k, cmp_pos_v], axis=1).reshape(depth, 2, 1, CMP_LEN * HEAD_DIM)
    prm = {
        "norm_mix": norm_mix.reshape(depth, 1, D),
        "w_in": jnp.swapaxes(w_in, 1, 2),
        "cmp_w1": jnp.stack([cmp_k_w1, cmp_v_w1], axis=1).astype(BF16),
        "cmp_w2": jnp.stack([cmp_k_w2, cmp_v_w2], axis=1).astype(BF16),
        "cmp_pos": jnp.broadcast_to(cmp_pos, (depth, 2, 8, CMP_LEN * HEAD_DIM)).astype(BF16),
        "p_a": w_branch_a.astype(BF16), "p_b": w_branch_b.astype(BF16), "w_o": w_out.astype(BF16),
    }
    x2 = x.reshape(B * S, D)
    x2 = _mixer(x2, 0, B, S, tables, mt, prm)
    experts = [w[0].reshape(-1, w.shape[-1]) for w in (moe_w1, moe_w3, moe_w2)]
    x2, experts = _ffn(x2, norm_ffn[0].reshape(1, -1), ffn_w1[0].astype(BF16),
                       ffn_w3[0].astype(BF16), ffn_w2[0].astype(BF16), to_bf16=experts)
    ew1, ew3, ew2 = [w.reshape(m.shape[1:]) for w, m in zip(experts, (moe_w1, moe_w3, moe_w2))]
    x2 = _mixer(x2, 1, B, S, tables, mt, prm)
    g1 = norm_ffn[1].reshape(1, -1)
    r = jnp.pad(router[0], ((0, 0), (0, LANES - N_EXPERTS)))
    h, comb, pos, post, cnt = _router(x2, g1, r)
    counts = cnt[:, 0, :N_EXPERTS].reshape(-1)
    out = _moe(x2, h, comb, pos, post, counts, ew1, ew3, ew2, final_norm.reshape(1, -1))
    return out.reshape(B, S, D)
```

```python
import functools

import numpy as np
import jax
import jax.numpy as jnp
from jax import lax
from jax.experimental import pallas as pl
from jax.experimental.pallas import tpu as pltpu

F32 = jnp.float32
BF16 = jnp.bfloat16

D_MODEL = 1024
HEAD_DIM = 64
ROT_DIM = HEAD_DIM // 4
ROPE_THETA = 500000.0
EPS = 1e-6
NSA_HEADS = 8
NSA_KV_HEADS = 2
HEADS_PER_KV = NSA_HEADS // NSA_KV_HEADS
CMP_LEN = 32
CMP_STRIDE = 16
CMP_HIDDEN = 128
SEL_LEN = 64
N_SEL = 16
WIN = 512
DIL_PATTERNS = ((128, 1), (512, 4), (2048, 16))
N_DIL_GROUPS = 3
DIL_HEADS = 4
D_FF = 2816
N_EXPERTS = 8

LANES = 128
VMEM_LIMIT = 56 * 1024 * 1024
NEG = -1e30
BIG = 1e30

COL_MERGE = 0
COL_QA = 2048
COL_KSEL = 2560
COL_BLK = COL_KSEL + 128
COL_KWIN = 2816
COL_VSEL = 2944
COL_VWIN = 3072
COL_GATE = 3200
N_PROJ = 3328
DIL_WIDTH = DIL_HEADS * HEAD_DIM
N_FOLD = N_DIL_GROUPS * DIL_WIDTH
STAGE = None
IN_CHUNKS = (
    (512, True, tuple(COL_QA + 128 * j for j in range(4)), None),
    (384, True, (COL_KSEL, COL_KWIN, STAGE), ("cmp", 0)),
    (512, False, (COL_VSEL, COL_VWIN, COL_GATE, STAGE), ("cmp", 1)),
    (512, False, tuple(COL_MERGE + 128 * j for j in range(0, 4)), None),
    (512, False, tuple(COL_MERGE + 128 * j for j in range(4, 8)), None),
    (512, False, tuple(COL_MERGE + 128 * j for j in range(8, 12)), None),
    (512, False, tuple(COL_MERGE + 128 * j for j in range(12, 16)), None),
    (N_FOLD, True, (STAGE,) * 6, ("fold", 0)),
    (N_FOLD, True, (STAGE,) * 6, ("fold", 1)),
    (N_FOLD, False, (STAGE,) * 6, ("fold", 2)),
)
N_W_IN = sum(c[0] for c in IN_CHUNKS)
LOG2E = 1.4426950408889634
LN2 = 0.6931471805599453
MASK_BIAS = -(2.0 ** 100)
SEL_BLOCKS_MAX = 32


def _dot(a, b, precision=None):
    return jnp.dot(a, b, preferred_element_type=F32, precision=precision)


def _dot_nt(a, b, precision=None):
    return lax.dot_general(a, b, (((1,), (1,)), ((), ())), preferred_element_type=F32,
                           precision=precision)


def _rms(x, g):
    ms = jnp.mean(x * x, axis=-1, keepdims=True)
    return x * lax.rsqrt(ms + EPS) * g


def _silu(x):
    return x * jax.nn.sigmoid(x)


def _params(**kw):
    return pltpu.CompilerParams(vmem_limit_bytes=VMEM_LIMIT, **kw)


def _const_spec(shape):
    nd = len(shape)
    return pl.BlockSpec(shape, lambda *_: (0,) * nd)


def _inproj_body(x_ref, g_ref, wt_ref, c_ref, sa_ref, sb_ref, o_ref, f0_ref, f1_ref, f2_ref, a_ref,
                 st_ref, w_ref, *, per_b):
    tm = x_ref.shape[0]

    @pl.when(pl.program_id(0) == 0)
    def _():
        for j, (src, width, scale) in enumerate(_w_in_plan()):
            rows = wt_ref[0, src:src + LANES, :]
            w_ref[j * LANES:(j + 1) * LANES, :] = (rows * scale if scale != 1.0 else rows).astype(BF16)

    t_seq = (pl.program_id(0) % per_b) * tm + lax.broadcasted_iota(jnp.int32, (tm, LANES), 0)
    lane = lax.broadcasted_iota(jnp.int32, (tm, LANES), 1)
    blk = lax.shift_right_logical(t_seq, 6)
    hot = (lane == blk) | (lane == blk + SEL_BLOCKS_MAX)
    o_ref[:, COL_BLK:COL_BLK + LANES] = jnp.where(hot, 1.0, 0.0).astype(BF16)
    h = _rms(x_ref[...], g_ref[0]).astype(BF16)
    c = c_ref[...]
    sa = sa_ref[...]
    sb = sb_ref[...]
    start = 0
    for size, rope, dests, action in IN_CHUNKS:
        acc = _dot_nt(h, w_ref[start:start + size, :])
        start += size
        for j, dest in enumerate(dests):
            a = acc[:, j * LANES:(j + 1) * LANES]
            if rope:
                a = a * c + pltpu.roll(a, LANES - 8, 1) * sa + pltpu.roll(a, 8, 1) * sb
            if dest is STAGE:
                st_ref[j] = a
            else:
                o_ref[:, dest:dest + LANES] = a.astype(BF16)
        if action is None:
            continue
        kind, piece = action
        if kind == "fold":
            slabs = DIL_WIDTH // LANES
            for gi, f_ref in enumerate((f0_ref, f1_ref, f2_ref)):
                d = DIL_PATTERNS[gi][1]
                for r in range(d):
                    for k in range(slabs):
                        rows = st_ref[gi * slabs + k, pl.ds(r, tm // d, stride=d), :]
                        c0 = piece * DIL_WIDTH + k * LANES
                        f_ref[0, r, :, c0:c0 + LANES] = rows.astype(BF16)
        else:
            slab = dests.index(STAGE)
            nrow = tm // CMP_STRIDE
            toks = [st_ref[slab, pl.ds(j, nrow, stride=CMP_STRIDE), :] for j in range(CMP_STRIDE)]
            for g in range(NSA_KV_HEADS):
                head = slice(g * HEAD_DIM, (g + 1) * HEAD_DIM)
                for m in range(CMP_STRIDE // 2):
                    pair = jnp.concatenate([toks[2 * m][:, head], toks[2 * m + 1][:, head]], axis=1)
                    a_ref[piece, 0, g, :, m * LANES:(m + 1) * LANES] = pair.astype(BF16)


def _layer_spec(arr, layer):
    nd = arr.ndim
    return pl.BlockSpec((1,) + arr.shape[1:], lambda *_: (layer,) + (0,) * (nd - 1))


def _inproj(x2, g, w, layer, rc, rsa, rsb, B, S, tm=512):
    T = x2.shape[0]
    per_b = S // tm
    dils = [d for _, d in DIL_PATTERNS]
    fold_spec = lambda d: pl.BlockSpec((1, d, tm // d, 3 * DIL_WIDTH),
                                       lambda i: (i // per_b, 0, i % per_b, 0))
    fold_shape = lambda d: jax.ShapeDtypeStruct((B, d, S // d, 3 * DIL_WIDTH), BF16)
    cmp_w = CMP_STRIDE * HEAD_DIM
    assert S // SEL_LEN <= SEL_BLOCKS_MAX
    return pl.pallas_call(
        functools.partial(_inproj_body, per_b=per_b),
        grid=(T // tm,),
        in_specs=[
            pl.BlockSpec((tm, D_MODEL), lambda i: (i, 0)),
            _layer_spec(g, layer), _layer_spec(w, layer),
            pl.BlockSpec((tm, LANES), lambda i: (i, 0)),
            pl.BlockSpec((tm, LANES), lambda i: (i, 0)),
            pl.BlockSpec((tm, LANES), lambda i: (i, 0)),
        ],
        out_specs=[pl.BlockSpec((tm, N_PROJ), lambda i: (i, 0)), *[fold_spec(d) for d in dils],
                   pl.BlockSpec((2, 1, NSA_KV_HEADS, tm // CMP_STRIDE, cmp_w),
                                lambda i: (0, i // per_b, 0, i % per_b, 0))],
        out_shape=[jax.ShapeDtypeStruct((T, N_PROJ), BF16), *[fold_shape(d) for d in dils],
                   jax.ShapeDtypeStruct((2, B, NSA_KV_HEADS, S // CMP_STRIDE, cmp_w), BF16)],
        scratch_shapes=[pltpu.VMEM((N_FOLD // LANES, tm, LANES), F32),
                        pltpu.VMEM((N_W_IN, D_MODEL), BF16)],
        compiler_params=_params(),
        name="inproj",
    )(x2, g, w, rc, rsa, rsb)


def _compress_body(a_ref, w1_ref, w2_ref, pos_ref, o_ref):
    nb, ncp = o_ref.shape[1], o_ref.shape[2]
    a = a_ref[0]
    w1 = w1_ref[0, 0]
    half = CMP_STRIDE * HEAD_DIM
    top = _dot(a, w1[:half])
    bot = _dot(a, w1[half:])
    pc = _dot(pos_ref[0, 0], w1)
    rows = a.shape[0]
    hid = top + pltpu.roll(bot, rows - 1, 0) + pc[0:1]
    out = _dot(_silu(hid).astype(BF16), w2_ref[0, 0])
    for b in range(nb):
        heads = [out[(b * NSA_KV_HEADS + g) * ncp:(b * NSA_KV_HEADS + g + 1) * ncp]
                 for g in range(NSA_KV_HEADS)]
        o_ref[0, b] = jnp.concatenate(heads, axis=1).astype(BF16)


def _compress(a, w1, w2, pos, layer, B):
    n, rows, _ = a.shape
    ncp = rows // (B * NSA_KV_HEADS)
    per_kv = lambda arr: pl.BlockSpec((1, 1) + arr.shape[2:], lambda i: (layer, i, 0, 0))
    return pl.pallas_call(
        _compress_body,
        grid=(n,),
        in_specs=[pl.BlockSpec((1, rows, CMP_STRIDE * HEAD_DIM), lambda i: (i, 0, 0)),
                  per_kv(w1), per_kv(w2), per_kv(pos)],
        out_specs=pl.BlockSpec((1, B, ncp, NSA_KV_HEADS * HEAD_DIM), lambda i: (i, 0, 0, 0)),
        out_shape=jax.ShapeDtypeStruct((n, B, ncp, NSA_KV_HEADS * HEAD_DIM), BF16),
        compiler_params=_params(),
        name="compress",
    )(a, w1, w2, pos)


def _softmax2(s):
    m = jnp.max(s, axis=-1, keepdims=True)
    e = jnp.exp2(s - m)
    return e.astype(BF16), jnp.sum(e, axis=-1, keepdims=True)


def _weighted_values(e, l, v):
    nh, tq, nk = e.shape
    return _dot(e.reshape(nh * tq, nk), v) / l.reshape(nh * tq, 1)


SEL_PREFIX = 256


def _nsa_body(q_ref, kc_ref, vc_ref, ksel_ref, vsel_ref, kwin_ref, vwin_ref, gate_ref, mt_ref,
              o_ref, osel_ref, *, tq, seq):
    nblk = seq // SEL_LEN
    ncp = kc_ref.shape[1]
    n_cmp = (seq - CMP_LEN) // CMP_STRIDE + 1
    q0 = pl.program_id(1) * tq
    q = q_ref[0]
    t_col = q0 + lax.broadcasted_iota(jnp.int32, (tq, 1), 0)
    t_row = q0 + lax.broadcasted_iota(jnp.int32, (1, tq), 1)
    gates = jax.nn.sigmoid(gate_ref[0].astype(F32))

    zeros64 = jnp.zeros((tq, HEAD_DIM), BF16)

    def stacked_q(g):
        parts = []
        for hh in range(HEADS_PER_KV):
            h = g * HEADS_PER_KV + hh
            qh = q[:, h * HEAD_DIM:(h + 1) * HEAD_DIM]
            parts.append(jnp.concatenate([qh, zeros64] if g == 0 else [zeros64, qh], axis=1))
        return jnp.concatenate(parts, axis=0)

    qs = [stacked_q(g) for g in range(NSA_KV_HEADS)]

    cidx = lax.broadcasted_iota(jnp.int32, (tq, ncp), 1)
    cmask = ((cidx * CMP_STRIDE + (CMP_LEN - 1)) <= t_col) & (cidx < n_cmp)
    jidx = lax.broadcasted_iota(jnp.int32, (nblk, tq), 0)
    cur = lax.shift_right_logical(t_row, 6)
    forced = (jidx == 0) | (jidx == cur) | (jidx == cur - 1)
    future = jidx > cur
    groups = range(NSA_KV_HEADS)
    span = WIN + tq
    ks = pl.multiple_of(jnp.maximum(q0 - WIN, 0), tq)
    kw = kwin_ref[0, pl.ds(ks, span), :]
    vw = vwin_ref[0, pl.ds(ks, span), :]
    wpos = ks + lax.broadcasted_iota(jnp.int32, (1, span), 1)
    wmask = (wpos <= t_col) & (t_col - wpos <= WIN - 1)

    s_cmp = [jnp.where(cmask[None], _dot_nt(qs[g], kc_ref[0]).reshape(HEADS_PER_KV, tq, ncp), NEG)
             for g in groups]
    s_win = [jnp.where(wmask[None], _dot_nt(qs[g], kw).reshape(HEADS_PER_KV, tq, span), NEG)
             for g in groups]
    p_cmp = []
    for g in groups:
        m = jnp.max(s_cmp[g], axis=-1, keepdims=True)
        e = jnp.where(cmask[None], jnp.exp2(s_cmp[g] - m), 0.0)
        den = jnp.sum(e, axis=-1, keepdims=True)
        p_cmp.append(e / jnp.where(den > 0, den, 1.0))
    e_win = [_softmax2(s_win[g]) for g in groups]
    o_cmp = [_dot(p_cmp[g].astype(BF16).reshape(HEADS_PER_KV * tq, ncp), vc_ref[0]) for g in groups]
    imps = [_dot_nt(mt_ref[...], p_cmp[g][0] + p_cmp[g][1] + p_cmp[g][2] + p_cmp[g][3],
                    precision=lax.Precision.HIGHEST) for g in groups]
    o_win = [_weighted_values(*e_win[g], vw) for g in groups]

    def head_part(o, g, hh):
        return o[hh * tq:(hh + 1) * tq, g * HEAD_DIM:(g + 1) * HEAD_DIM]

    def gate_col(h, br):
        return jnp.broadcast_to(gates[:, 3 * h + br:3 * h + br + 1], (tq, HEAD_DIM))

    heads = [(g, hh, g * HEADS_PER_KV + hh) for g in groups for hh in range(HEADS_PER_KV)]
    partial = [gate_col(h, 0) * head_part(o_cmp[g], g, hh) + gate_col(h, 2) * head_part(o_win[g], g, hh)
               for g, hh, h in heads]
    sel_gate = [gate_col(h, 1) for _, _, h in heads]

    first_blk = lax.shift_right_logical(q0, 6)
    picked_rows, before_rows = [], []
    for g in groups:
        imp = jnp.where(forced, BIG, imps[g])
        imp = jnp.where(future, -BIG, imp)
        rank = jnp.zeros((nblk, tq), jnp.int32)
        for i in range(nblk):
            row = imp[i:i + 1, :]
            beats = (row > imp) | ((row == imp) & (jidx > i))
            rank = rank + beats.astype(jnp.int32)
        picked_rows.append(jnp.where(rank < N_SEL, 0.0, MASK_BIAS))
        before_rows.append(jnp.where((rank < N_SEL) & (jidx < first_blk), 0.0, MASK_BIAS))
    assert 2 * NSA_KV_HEADS * SEL_BLOCKS_MAX == LANES
    bias_t = jnp.concatenate(before_rows + picked_rows, axis=0)
    bias_main = bias_t.T
    bias_diag = pltpu.roll(bias_main, LANES // 2, 1)
    lane_group = lax.shift_right_logical(lax.broadcasted_iota(jnp.int32, (tq, LANES), 1), 5)

    def with_bias(g, bias):
        own = jnp.where(lane_group == g, bias, 0.0).astype(BF16)
        return jnp.concatenate([qs[g], jnp.concatenate([own] * HEADS_PER_KV, axis=0)], axis=1)

    qb = [with_bias(g, bias_main) for g in groups]

    kdiag = ksel_ref[0, pl.ds(pl.multiple_of(q0, tq), tq), :]
    vdiag = vsel_ref[0, pl.ds(pl.multiple_of(q0, tq), tq), :]
    tri = (lax.broadcasted_iota(jnp.int32, (tq, tq), 1) <= lax.broadcasted_iota(jnp.int32, (tq, tq), 0))
    s_diag = [jnp.where(tri[None], _dot_nt(with_bias(g, bias_diag), kdiag)
                        .reshape(HEADS_PER_KV, tq, tq), NEG) for g in groups]
    n_prefix = q0 // SEL_PREFIX + 1
    for n in range(1, seq // SEL_PREFIX + 1):
        klen = n * SEL_PREFIX

        @pl.when(n_prefix == n)
        def _(klen=klen):
            vall = jnp.concatenate([vsel_ref[0, :klen, :], vdiag], axis=0)
            s = [jnp.concatenate(
                [_dot_nt(qb[g], ksel_ref[0, :klen, :]).reshape(HEADS_PER_KV, tq, klen), s_diag[g]],
                axis=-1) for g in groups]
            ew = [_softmax2(s[g]) for g in groups]
            for g in groups:
                osel_ref[g] = _weighted_values(*ew[g], vall)

    o_sel = [osel_ref[g] for g in groups]

    outs = [partial[h] + sel_gate[h] * head_part(o_sel[g], g, hh) for g, hh, h in heads]
    o_ref[0] = jnp.concatenate(outs, axis=1).astype(BF16)


def _nsa(proj3, kc, vc, mt, tq=128):
    B, S, _ = proj3.shape
    blk = lambda c: c // LANES
    seq_spec = lambda c: pl.BlockSpec((1, S, LANES), lambda b, i: (b, 0, blk(c)))
    ncp = kc.shape[1]
    return pl.pallas_call(
        functools.partial(_nsa_body, tq=tq, seq=S),
        grid=(B, S // tq),
        in_specs=[
            pl.BlockSpec((1, tq, NSA_HEADS * HEAD_DIM),
                         lambda b, i: (b, i, COL_QA // (NSA_HEADS * HEAD_DIM))),
            pl.BlockSpec((1, ncp, LANES), lambda b, i: (b, 0, 0)),
            pl.BlockSpec((1, ncp, LANES), lambda b, i: (b, 0, 0)),
            pl.BlockSpec((1, S, 2 * LANES), lambda b, i: (b, 0, COL_KSEL // (2 * LANES))),
            seq_spec(COL_VSEL), seq_spec(COL_KWIN), seq_spec(COL_VWIN),
            pl.BlockSpec((1, tq, LANES), lambda b, i: (b, i, blk(COL_GATE))),
            _const_spec(mt.shape),
        ],
        out_specs=pl.BlockSpec((1, tq, NSA_HEADS * HEAD_DIM), lambda b, i: (b, i, 0)),
        out_shape=jax.ShapeDtypeStruct((B, S, NSA_HEADS * HEAD_DIM), BF16),
        scratch_shapes=[pltpu.VMEM((NSA_KV_HEADS, HEADS_PER_KV * tq, LANES), F32)],
        compiler_params=_params(),
        name="nsa",
    )(proj3, kc, vc, proj3, proj3, proj3, proj3, proj3, mt)


DIL_SUB = 128
DIL_ROWS = 1024


def _dil_body(q_ref, kp_ref, kc_ref, vp_ref, vc_ref, o_ref, lse_ref, *, n_back):
    sub = DIL_SUB
    rb, tq = q_ref.shape[1], q_ref.shape[2]
    t0 = pl.program_id(2) * tq
    head_of = lax.shift_right_logical(lax.broadcasted_iota(jnp.int32, (sub, DIL_WIDTH), 1), 6)
    lane = lax.broadcasted_iota(jnp.int32, (sub, LANES), 1)
    diff = (sub + lax.broadcasted_iota(jnp.int32, (sub, 1), 0)
            - lax.broadcasted_iota(jnp.int32, (1, 2 * sub), 1))
    band = (diff >= 0) & (diff <= n_back)
    band0 = band & (lax.broadcasted_iota(jnp.int32, (1, 2 * sub), 1) + t0 >= sub)
    tiles = [(r, j) for r in range(rb) for j in range(tq // sub)]
    keys = {r: jnp.concatenate([kp_ref[0, r], kc_ref[0, r]], axis=0) for r in range(rb)}
    vals = {r: jnp.concatenate([vp_ref[0, r], vc_ref[0, r]], axis=0) for r in range(rb)}
    scores = []
    for r, j in tiles:
        q = q_ref[0, r, j * sub:(j + 1) * sub, :]
        qs = jnp.concatenate([jnp.where(head_of == h, q, jnp.zeros_like(q))
                              for h in range(DIL_HEADS)], axis=0)
        s = _dot_nt(qs, keys[r][j * sub:(j + 2) * sub]).reshape(DIL_HEADS, sub, 2 * sub)
        scores.append(jnp.where((band0 if j == 0 else band)[None], s, NEG))
    stats = []
    for s in scores:
        m = jnp.max(s, axis=-1, keepdims=True)
        e = jnp.exp2(s - m)
        stats.append((m, e, jnp.sum(e, axis=-1, keepdims=True)))
    for (r, j), (m, e, l) in zip(tiles, stats):
        o = _dot(e.astype(BF16).reshape(DIL_HEADS * sub, 2 * sub), vals[r][j * sub:(j + 2) * sub])
        o = o.reshape(DIL_HEADS, sub, DIL_WIDTH) / l
        lse = m * LN2 + jnp.log(l)
        o_acc = jnp.zeros((sub, DIL_WIDTH), F32)
        lse_out = jnp.zeros((sub, LANES), F32)
        for h in range(DIL_HEADS):
            o_acc = jnp.where(head_of == h, o[h], o_acc)
            lse_out = jnp.where(lane == h, lse[h], lse_out)
        o_ref[0, r, j * sub:(j + 1) * sub, :] = o_acc
        lse_ref[0, r, j * sub:(j + 1) * sub, :] = lse_out


def _dilated(arr, cols, window, dilation, name):
    B, d, L, _ = arr.shape
    n_back = window // dilation
    tq = min(L, DIL_ROWS)
    rb = DIL_ROWS // tq
    assert d == dilation and n_back <= DIL_SUB and L % tq == 0 and d % rb == 0
    per = tq // DIL_SUB
    qc, kc, vc = cols
    cur = lambda c: pl.BlockSpec((1, rb, tq, DIL_WIDTH), lambda b, r, i: (b, r, i, c))
    prev = lambda c: pl.BlockSpec((1, rb, DIL_SUB, DIL_WIDTH),
                                  lambda b, r, i: (b, r, jnp.maximum(i * per - 1, 0), c))
    return pl.pallas_call(
        functools.partial(_dil_body, n_back=n_back),
        grid=(B, dilation // rb, L // tq),
        in_specs=[cur(qc), prev(kc), cur(kc), prev(vc), cur(vc)],
        out_specs=[pl.BlockSpec((1, rb, tq, DIL_WIDTH), lambda b, r, i: (b, r, i, 0)),
                   pl.BlockSpec((1, rb, tq, LANES), lambda b, r, i: (b, r, i, 0))],
        out_shape=[jax.ShapeDtypeStruct((B, dilation, L, DIL_WIDTH), F32),
                   jax.ShapeDtypeStruct((B, dilation, L, LANES), F32)],
        compiler_params=_params(),
        name=name,
    )(arr, arr, arr, arr, arr)


def _mixout_body(oa_ref, ob0_ref, ob1_ref, ob2_ref, l0_ref, l1_ref, l2_ref, mg_ref, x_ref,
                 pa_ref, pb_ref, wo_ref, out_ref, so_ref, sl_ref):
    tm = x_ref.shape[0]

    def interleaved(src_ref, st_ref):
        d = src_ref.shape[1]
        if d == 1:
            return src_ref[0, 0]
        slabs = src_ref.shape[3] // LANES
        for r in range(d):
            for k in range(slabs):
                st_ref[k, pl.ds(r, tm // d, stride=d), :] = src_ref[0, r, :, k * LANES:(k + 1) * LANES]
        return jnp.concatenate([st_ref[k] for k in range(slabs)], axis=1)

    lses = [interleaved(l, sl_ref) for l in (l0_ref, l1_ref, l2_ref)]
    mx = jnp.maximum(jnp.maximum(lses[0], lses[1]), lses[2])
    ws = [jnp.exp(l - mx) for l in lses]
    den = ws[0] + ws[1] + ws[2]
    ob = jnp.zeros((tm, DIL_WIDTH), F32)
    for w, o_ref in zip(ws, (ob0_ref, ob1_ref, ob2_ref)):
        alpha = w / den
        wide = jnp.concatenate(
            [jnp.broadcast_to(alpha[:, h:h + 1], (tm, HEAD_DIM)) for h in range(DIL_HEADS)], axis=1)
        ob = ob + wide * interleaved(o_ref, so_ref)
    ya = _dot(oa_ref[...], pa_ref[0])
    yb = _dot(ob.astype(BF16), pb_ref[0])
    gm = jax.nn.sigmoid(mg_ref[...].astype(F32))
    y = gm[:, :D_MODEL] * ya + gm[:, D_MODEL:] * yb
    out_ref[...] = x_ref[...] + _dot(y.astype(BF16), wo_ref[0])


def _mixout(oa, obs, lses, proj, x2, pa, pb, wo, layer, S, tm=1024):
    T = x2.shape[0]
    per_b = S // tm
    row = lambda w: pl.BlockSpec((tm, w), lambda i: (i, 0))
    folded = lambda a: pl.BlockSpec((1, a.shape[1], tm // a.shape[1], a.shape[3]),
                                    lambda i: (i // per_b, 0, i % per_b, 0))
    return pl.pallas_call(
        _mixout_body,
        grid=(T // tm,),
        in_specs=[row(NSA_HEADS * HEAD_DIM), *[folded(a) for a in obs], *[folded(a) for a in lses],
                  pl.BlockSpec((tm, 2 * D_MODEL), lambda i: (i, COL_MERGE // (2 * D_MODEL))),
                  row(D_MODEL),
                  _layer_spec(pa, layer), _layer_spec(pb, layer), _layer_spec(wo, layer)],
        out_specs=row(D_MODEL),
        out_shape=jax.ShapeDtypeStruct((T, D_MODEL), F32),
        scratch_shapes=[pltpu.VMEM((DIL_WIDTH // LANES, tm, LANES), F32),
                        pltpu.VMEM((1, tm, LANES), F32)],
        compiler_params=_params(),
        name="mixout",
    )(oa, *obs, *lses, proj, x2, pa, pb, wo)


FF_CHUNK = 512


def _ffn_body(x_ref, g_ref, w1_ref, w3_ref, w2_ref, *rest):
    n_cast = (len(rest) - 1) // 2
    src_refs, o_ref, dst_refs = rest[:n_cast], rest[n_cast], rest[n_cast + 1:]
    x = x_ref[...]
    h = _rms(x, g_ref[...]).astype(BF16)
    acc = jnp.zeros(x.shape, F32)
    for c0 in range(0, D_FF, FF_CHUNK):
        c1 = min(c0 + FF_CHUNK, D_FF)
        act = _silu(_dot(h, w1_ref[:, c0:c1])) * _dot(h, w3_ref[:, c0:c1])
        acc = acc + _dot(act.astype(BF16), w2_ref[c0:c1, :])
    o_ref[...] = x + acc
    for src, dst in zip(src_refs, dst_refs):
        dst[...] = src[...].astype(BF16)


def _ffn(x2, g, w1, w3, w2, to_bf16=(), tm=512):
    T = x2.shape[0]
    steps = T // tm
    held = lambda a: pl.BlockSpec(a.shape, lambda i: (0, 0), pipeline_mode=pl.Buffered(1))
    sliced = lambda a: pl.BlockSpec((a.shape[0] // steps, a.shape[1]), lambda i: (i, 0))
    assert all(a.shape[0] % (16 * steps) == 0 for a in to_bf16)
    out, *copies = pl.pallas_call(
        _ffn_body,
        grid=(steps,),
        in_specs=[pl.BlockSpec((tm, D_MODEL), lambda i: (i, 0)), _const_spec(g.shape),
                  held(w1), held(w3), held(w2), *[sliced(a) for a in to_bf16]],
        out_specs=[pl.BlockSpec((tm, D_MODEL), lambda i: (i, 0)), *[sliced(a) for a in to_bf16]],
        out_shape=[jax.ShapeDtypeStruct((T, D_MODEL), F32),
                   *[jax.ShapeDtypeStruct(a.shape, BF16) for a in to_bf16]],
        compiler_params=_params(),
        name="ffn",
    )(x2, g, w1, w3, w2, *to_bf16)
    return out, copies


MOE_TM = 1024
MOE_CHUNK = 128
MOE_BIG = 2 * MOE_CHUNK
MOE_MERGED = 4


def _router_body(x_ref, g_ref, r_ref, h_ref, combw_ref, posw_ref, post_ref, cnt_ref):
    tm = x_ref.shape[0]
    h = _rms(x_ref[...], g_ref[...])
    h_ref[...] = h.astype(BF16)
    r = r_ref[...]
    h_hi, r_hi = h.astype(BF16), r.astype(BF16)
    h_lo = (h - h_hi.astype(F32)).astype(BF16)
    r_lo = (r - r_hi.astype(F32)).astype(BF16)
    logits = _dot(h_hi, r_hi) + (_dot(h_hi, r_lo) + _dot(h_lo, r_hi))
    lane = lax.broadcasted_iota(jnp.int32, logits.shape, 1)
    lg = jnp.where(lane < N_EXPERTS, logits, NEG)
    m1 = jnp.max(lg, axis=-1, keepdims=True)
    i1 = jnp.min(jnp.where(lg == m1, lane, LANES), axis=-1, keepdims=True)
    lg2 = jnp.where(lane == i1, NEG, lg)
    m2 = jnp.max(lg2, axis=-1, keepdims=True)
    i2 = jnp.min(jnp.where(lg2 == m2, lane, LANES), axis=-1, keepdims=True)
    e2 = jnp.exp(m2 - m1)
    den = 1.0 + e2
    w_first, w_second = 1.0 / den, e2 / den
    chosen = [jnp.broadcast_to((i1 == ex) | (i2 == ex), (tm, LANES)) for ex in range(N_EXPERTS)]
    ones = jnp.concatenate([jnp.where(c, 1.0, 0.0).astype(BF16) for c in chosen], axis=1)
    row = lax.broadcasted_iota(jnp.int32, (LANES, LANES), 0)
    col = lax.broadcasted_iota(jnp.int32, (LANES, LANES), 1)
    tri = jnp.where(col < row, 1.0, 0.0).astype(BF16)
    running = jnp.zeros((1, N_EXPERTS * LANES), F32)
    parts = []
    for b in range(tm // LANES):
        blk = ones[b * LANES:(b + 1) * LANES]
        parts.append(_dot(tri, blk) + running)
        running = running + jnp.sum(blk.astype(F32), axis=0, keepdims=True)
    before = jnp.concatenate(parts, axis=0)
    by_lane = jnp.zeros((tm, LANES), F32)
    cnt = jnp.zeros((1, LANES), F32)
    for ex in range(N_EXPERTS):
        slab = slice(ex * LANES, (ex + 1) * LANES)
        combw_ref[:, slab] = jnp.broadcast_to(
            jnp.where(i1 == ex, w_first, 0.0) + jnp.where(i2 == ex, w_second, 0.0), (tm, LANES))
        pos = jnp.where(chosen[ex], before[:, slab], -1.0)
        posw_ref[:, slab] = pos
        by_lane = jnp.where(lane == ex, pos, by_lane)
        cnt = jnp.where(lane[:1] == ex, running[:, slab], cnt)
    post_ref[0] = by_lane.T[:N_EXPERTS]
    cnt_ref[0] = jnp.broadcast_to(cnt, (8, LANES)).astype(jnp.int32)


def _router(x2, g, r, tm=MOE_TM):
    T = x2.shape[0]
    nt = T // tm
    row = lambda w: pl.BlockSpec((tm, w), lambda i: (i, 0))
    return pl.pallas_call(
        _router_body,
        grid=(nt,),
        in_specs=[row(D_MODEL), _const_spec(g.shape), _const_spec(r.shape)],
        out_specs=[row(D_MODEL), row(N_EXPERTS * LANES), row(N_EXPERTS * LANES),
                   pl.BlockSpec((1, N_EXPERTS, tm), lambda i: (i, 0, 0)),
                   pl.BlockSpec((1, 8, LANES), lambda i: (i, 0, 0))],
        out_shape=[jax.ShapeDtypeStruct((T, D_MODEL), BF16),
                   jax.ShapeDtypeStruct((T, N_EXPERTS * LANES), F32),
                   jax.ShapeDtypeStruct((T, N_EXPERTS * LANES), F32),
                   jax.ShapeDtypeStruct((nt, N_EXPERTS, tm), F32),
                   jax.ShapeDtypeStruct((nt, 8, LANES), jnp.int32)],
        compiler_params=_params(),
        name="router",
    )(x2, g, r)


def _moe_body(cnt_ref, x_ref, h_ref, comb_ref, pos_ref, post_ref, w1_ref, w3_ref, w2_ref, fn_ref,
              o_ref, y_ref):
    i = pl.program_id(0)
    e = pl.program_id(1)
    merged = MOE_MERGED * LANES
    n = cnt_ref[i * N_EXPERTS + e]
    n_big = (n + (MOE_BIG - MOE_CHUNK - 1)) // MOE_BIG
    rest_row = pl.multiple_of(n_big * MOE_BIG, MOE_BIG)
    has_rest = n > rest_row
    post_e = post_ref[0, pl.ds(e, 1), :]

    @pl.when(e == 0)
    def _():
        o_ref[...] = x_ref[...]

    @pl.when((i == 0) & (e == 0))
    def _():
        y_ref[...] = jnp.zeros_like(y_ref)

    def scatter_add(first_row, first_slot, n_slabs):
        lane_slot = lax.broadcasted_iota(jnp.int32, (1, LANES), 1).astype(F32)
        hot = [jnp.where(pos_ref[...] == first_slot + (k * LANES) + lane_slot, 1.0, 0.0).astype(BF16)
               for k in range(n_slabs)]
        scatter = hot[0] if n_slabs == 1 else jnp.concatenate(hot, axis=1)
        weight = jnp.concatenate([comb_ref[...]] * (D_MODEL // LANES), axis=1)
        ys = y_ref[pl.ds(first_row, n_slabs * LANES), :].astype(BF16)
        o_ref[...] += weight * _dot(scatter, ys)

    def chunk(r0, size):
        base = r0.astype(F32)
        slot_col = base + lax.broadcasted_iota(jnp.int32, (size, 1), 0).astype(F32)
        gather = jnp.where(post_e == slot_col, 1.0, 0.0).astype(BF16)
        xe = _dot(gather, h_ref[...]).astype(BF16)
        act = _silu(_dot(xe, w1_ref[0])) * _dot(xe, w3_ref[0])
        y = _dot(act.astype(BF16), w2_ref[0])

        @pl.when(r0 < merged)
        def _():
            y_ref[pl.ds(r0, size), :] = y

        @pl.when(r0 >= merged)
        def _():
            y_ref[merged:merged + size, :] = y
            for k in range(size // LANES):
                scatter_add(merged + k * LANES, base + float(k * LANES), 1)

    def big_chunk(c, carry):
        chunk(pl.multiple_of(c * MOE_BIG, MOE_BIG), MOE_BIG)
        return carry

    lax.fori_loop(0, n_big, big_chunk, 0)

    @pl.when(has_rest)
    def _():
        chunk(rest_row, MOE_CHUNK)

    @pl.when(n > 0)
    def _():
        scatter_add(0, 0.0, MOE_MERGED)

    @pl.when(e == pl.num_programs(1) - 1)
    def _():
        o_ref[...] = _rms(o_ref[...], fn_ref[...])


def _moe(x2, h, comb, pos, post, counts, w1, w3, w2, fn, tm=MOE_TM):
    T = x2.shape[0]
    once = lambda w: pl.BlockSpec((tm, w), lambda i, e, cnt: (i, 0), pipeline_mode=pl.Buffered(1))
    tile = lambda w: pl.BlockSpec((tm, w), lambda i, e, cnt: (i, 0))
    assert MOE_CHUNK == LANES
    slab = pl.BlockSpec((tm, LANES), lambda i, e, cnt: (i, e))
    expert = lambda a: pl.BlockSpec((1,) + a.shape[1:], lambda i, e, cnt: (e, 0, 0))
    grid_spec = pltpu.PrefetchScalarGridSpec(
        num_scalar_prefetch=1,
        grid=(T // tm, N_EXPERTS),
        in_specs=[once(D_MODEL), once(D_MODEL), slab, slab,
                  pl.BlockSpec((1, N_EXPERTS, tm), lambda i, e, cnt: (i, 0, 0)),
                  expert(w1), expert(w3), expert(w2),
                  pl.BlockSpec(fn.shape, lambda i, e, cnt: (0, 0))],
        out_specs=tile(D_MODEL),
        scratch_shapes=[pltpu.VMEM((MOE_MERGED * LANES + MOE_BIG, D_MODEL), F32)],
    )
    return pl.pallas_call(
        _moe_body,
        grid_spec=grid_spec,
        out_shape=jax.ShapeDtypeStruct((T, D_MODEL), F32),
        compiler_params=_params(),
        name="moe",
    )(counts, x2, h, comb, pos, post, w1, w3, w2, fn)


def _rope_lane_tables(positions):
    half = ROT_DIM // 2
    inv = ROPE_THETA ** (-jnp.arange(0, ROT_DIM, 2, dtype=F32) / ROT_DIM)
    ang = positions.astype(F32).reshape(-1, 1) * inv
    cos, sin = jnp.cos(ang), jnp.sin(ang)
    lane = np.arange(LANES)
    within = lane % HEAD_DIM
    pick = lane % half
    cos_l, sin_l = cos[:, pick], sin[:, pick]
    c = jnp.where(within < ROT_DIM, cos_l, 1.0)
    sa = jnp.where(within < half, -sin_l, 0.0)
    sb = jnp.where((within >= half) & (within < ROT_DIM), sin_l, 0.0)
    return c, sa, sb


def _w_in_plan():
    scale = HEAD_DIM ** -0.5 * LOG2E
    qa, kv, gate, qkv_b, merge = 0, 512, 1280, 1304, 3608
    nb = N_DIL_GROUPS * DIL_WIDTH
    kv_piece = lambda j: [(kv + LANES * j, LANES, 1.0)]
    span = lambda start, width, s=1.0: [(start + c, LANES, s) for c in range(0, width, LANES)]
    plan = (span(qa, 512, scale)
            + kv_piece(2) + kv_piece(4) + kv_piece(0)
            + kv_piece(3) + kv_piece(5)
            + [(gate, 3 * NSA_HEADS, 1.0)]
            + kv_piece(1)
            + span(merge, 2 * D_MODEL)
            + span(qkv_b, nb, scale) + span(qkv_b + nb, nb) + span(qkv_b + 2 * nb, nb))
    assert len(plan) * LANES == N_W_IN
    return plan


def _importance_matrix_t(seq, ncp):
    n_c = (seq - CMP_LEN) // CMP_STRIDE + 1
    starts = np.arange(n_c) * CMP_STRIDE
    bstart = np.arange(seq // SEL_LEN) * SEL_LEN
    overlap = np.clip(np.minimum(starts[:, None] + CMP_LEN, bstart[None, :] + SEL_LEN)
                      - np.maximum(starts[:, None], bstart[None, :]), 0, None)
    m = np.zeros((ncp, seq // SEL_LEN), np.float32)
    m[:n_c] = overlap.astype(np.float32) / CMP_LEN
    return jnp.asarray(m.T)


def _mixer(x2, layer, B, S, tables, mt, prm):
    proj, *folds, a = _inproj(x2, prm["norm_mix"], prm["w_in"], layer, *tables, B, S)
    proj3 = proj.reshape(B, S, N_PROJ)
    ncp = S // CMP_STRIDE
    cmp = _compress(a.reshape(2, B * NSA_KV_HEADS * ncp, CMP_STRIDE * HEAD_DIM),
                    prm["cmp_w1"], prm["cmp_w2"], prm["cmp_pos"], layer, B)
    oa = _nsa(proj3, cmp[0], cmp[1], mt).reshape(B * S, NSA_HEADS * HEAD_DIM)
    obs, lses = [], []
    for g, ((w, d), arr) in enumerate(zip(DIL_PATTERNS, folds)):
        o, lse = _dilated(arr, (0, 1, 2), w, d, f"dilated{g}")
        obs.append(o)
        lses.append(lse)
    return _mixout(oa, obs, lses, proj, x2, prm["p_a"], prm["p_b"], prm["w_o"], layer, S)


def kernel(x, positions, norm_mix, w_in, cmp_pos_k, cmp_pos_v, cmp_k_w1, cmp_k_w2, cmp_v_w1,
           cmp_v_w2, w_branch_a, w_branch_b, w_out, norm_ffn, ffn_w1, ffn_w3, ffn_w2, router,
           moe_w1, moe_w3, moe_w2, final_norm):
    B, S, D = x.shape
    depth = norm_mix.shape[0]
    assert depth == 2 and D == D_MODEL
    tables = _rope_lane_tables(positions)
    mt = _importance_matrix_t(S, S // CMP_STRIDE)
    cmp_pos = jnp.stack([cmp_pos_k, cmp_pos_v], axis=1).reshape(depth, 2, 1, CMP_LEN * HEAD_DIM)
    prm = {
        "norm_mix": norm_mix.reshape(depth, 1, D),
        "w_in": jnp.swapaxes(w_in, 1, 2),
        "cmp_w1": jnp.stack([cmp_k_w1, cmp_v_w1], axis=1).astype(BF16),
        "cmp_w2": jnp.stack([cmp_k_w2, cmp_v_w2], axis=1).astype(BF16),
        "cmp_pos": jnp.broadcast_to(cmp_pos, (depth, 2, 8, CMP_LEN * HEAD_DIM)).astype(BF16),
        "p_a": w_branch_a.astype(BF16), "p_b": w_branch_b.astype(BF16), "w_o": w_out.astype(BF16),
    }
    x2 = x.reshape(B * S, D)
    x2 = _mixer(x2, 0, B, S, tables, mt, prm)
    experts = [w[0].reshape(-1, w.shape[-1]) for w in (moe_w1, moe_w3, moe_w2)]
    x2, experts = _ffn(x2, norm_ffn[0].reshape(1, -1), ffn_w1[0].astype(BF16),
                       ffn_w3[0].astype(BF16), ffn_w2[0].astype(BF16), to_bf16=experts)
    ew1, ew3, ew2 = [w.reshape(m.shape[1:]) for w, m in zip(experts, (moe_w1, moe_w3, moe_w2))]
    x2 = _mixer(x2, 1, B, S, tables, mt, prm)
    g1 = norm_ffn[1].reshape(1, -1)
    r = jnp.pad(router[0], ((0, 0), (0, LANES - N_EXPERTS)))
    h, comb, pos, post, cnt = _router(x2, g1, r)
    counts = cnt[:, 0, :N_EXPERTS].reshape(-1)
    out = _moe(x2, h, comb, pos, post, counts, ew1, ew3, ew2, final_norm.reshape(1, -1))
    return out.reshape(B, S, D)
```

```python
import functools

import numpy as np
import jax
import jax.numpy as jnp
from jax import lax
from jax.experimental import pallas as pl
from jax.experimental.pallas import tpu as pltpu

F32 = jnp.float32
BF16 = jnp.bfloat16

D_MODEL = 1024
HEAD_DIM = 64
ROT_DIM = HEAD_DIM // 4
ROPE_THETA = 500000.0
EPS = 1e-6
NSA_HEADS = 8
NSA_KV_HEADS = 2
HEADS_PER_KV = NSA_HEADS // NSA_KV_HEADS
CMP_LEN = 32
CMP_STRIDE = 16
SEL_LEN = 64
N_SEL = 16
WIN = 512
DIL_PATTERNS = ((128, 1), (512, 4), (2048, 16))
N_DIL_GROUPS = 3
DIL_HEADS = 4
D_FF = 2816
N_EXPERTS = 8

LANES = 128
VMEM_LIMIT = 56 * 1024 * 1024
NEG = -1e30
BIG = 1e30

COL_MERGE = 0
COL_QA = 2048
COL_KSEL = 2560
COL_BLK = COL_KSEL + 128
COL_KWIN = 2816
COL_VSEL = 2944
COL_VWIN = 3072
COL_GATE = 3200
N_PROJ = 3328
DIL_WIDTH = DIL_HEADS * HEAD_DIM
N_FOLD = N_DIL_GROUPS * DIL_WIDTH
STAGE = None
IN_CHUNKS = (
    (512, True, tuple(COL_QA + 128 * j for j in range(4)), None),
    (384, True, (COL_KSEL, COL_KWIN, STAGE), ("cmp", 0)),
    (512, False, (COL_VSEL, COL_VWIN, COL_GATE, STAGE), ("cmp", 1)),
    (512, False, tuple(COL_MERGE + 128 * j for j in range(0, 4)), None),
    (512, False, tuple(COL_MERGE + 128 * j for j in range(4, 8)), None),
    (512, False, tuple(COL_MERGE + 128 * j for j in range(8, 12)), None),
    (512, False, tuple(COL_MERGE + 128 * j for j in range(12, 16)), None),
    (N_FOLD, True, (STAGE,) * 6, ("fold", 0)),
    (N_FOLD, True, (STAGE,) * 6, ("fold", 1)),
    (N_FOLD, False, (STAGE,) * 6, ("fold", 2)),
)
N_W_IN = sum(c[0] for c in IN_CHUNKS)
LOG2E = 1.4426950408889634
LN2 = 0.6931471805599453
MASK_BIAS = -(2.0 ** 100)
SEL_BLOCKS_MAX = 32


def _dot(a, b, precision=None):
    return jnp.dot(a, b, preferred_element_type=F32, precision=precision)


def _dot_nt(a, b, precision=None):
    return lax.dot_general(a, b, (((1,), (1,)), ((), ())), preferred_element_type=F32,
                           precision=precision)


def _rms(x, g):
    ms = jnp.mean(x * x, axis=-1, keepdims=True)
    return x * lax.rsqrt(ms + EPS) * g


def _silu(x):
    return x * jax.nn.sigmoid(x)


def _params(**kw):
    return pltpu.CompilerParams(vmem_limit_bytes=VMEM_LIMIT, **kw)


def _const_spec(shape):
    nd = len(shape)
    return pl.BlockSpec(shape, lambda *_: (0,) * nd)


def _inproj_body(x_ref, g_ref, wt_ref, c_ref, sa_ref, sb_ref, o_ref, f0_ref, f1_ref, f2_ref, a_ref,
                 st_ref, w_ref, *, per_b):
    tm = x_ref.shape[0]

    @pl.when(pl.program_id(0) == 0)
    def _():
        for j, (src, width, scale) in enumerate(_w_in_plan()):
            rows = wt_ref[0, src:src + LANES, :]
            w_ref[j * LANES:(j + 1) * LANES, :] = (rows * scale if scale != 1.0 else rows).astype(BF16)

    t_seq = (pl.program_id(0) % per_b) * tm + lax.broadcasted_iota(jnp.int32, (tm, LANES), 0)
    lane = lax.broadcasted_iota(jnp.int32, (tm, LANES), 1)
    blk = lax.shift_right_logical(t_seq, 6)
    hot = (lane == blk) | (lane == blk + SEL_BLOCKS_MAX)
    o_ref[:, COL_BLK:COL_BLK + LANES] = jnp.where(hot, 1.0, 0.0).astype(BF16)
    h = _rms(x_ref[...], g_ref[0]).astype(BF16)
    c = c_ref[...]
    sa = sa_ref[...]
    sb = sb_ref[...]
    start = 0
    for size, rope, dests, action in IN_CHUNKS:
        acc = _dot_nt(h, w_ref[start:start + size, :])
        start += size
        for j, dest in enumerate(dests):
            a = acc[:, j * LANES:(j + 1) * LANES]
            if rope:
                a = a * c + pltpu.roll(a, LANES - 8, 1) * sa + pltpu.roll(a, 8, 1) * sb
            if dest is STAGE:
                st_ref[j] = a
            else:
                o_ref[:, dest:dest + LANES] = a.astype(BF16)
        if action is None:
            continue
        kind, piece = action
        if kind == "fold":
            slabs = DIL_WIDTH // LANES
            for gi, f_ref in enumerate((f0_ref, f1_ref, f2_ref)):
                d = DIL_PATTERNS[gi][1]
                for r in range(d):
                    for k in range(slabs):
                        rows = st_ref[gi * slabs + k, pl.ds(r, tm // d, stride=d), :]
                        c0 = piece * DIL_WIDTH + k * LANES
                        f_ref[0, r, :, c0:c0 + LANES] = rows.astype(BF16)
        else:
            slab = dests.index(STAGE)
            nrow = tm // CMP_STRIDE
            toks = [st_ref[slab, pl.ds(j, nrow, stride=CMP_STRIDE), :] for j in range(CMP_STRIDE)]
            for g in range(NSA_KV_HEADS):
                head = slice(g * HEAD_DIM, (g + 1) * HEAD_DIM)
                for m in range(CMP_STRIDE // 2):
                    pair = jnp.concatenate([toks[2 * m][:, head], toks[2 * m + 1][:, head]], axis=1)
                    a_ref[piece, 0, g, :, m * LANES:(m + 1) * LANES] = pair.astype(BF16)


def _layer_spec(arr, layer):
    nd = arr.ndim
    return pl.BlockSpec((1,) + arr.shape[1:], lambda *_: (layer,) + (0,) * (nd - 1))


def _inproj(x2, g, w, layer, rc, rsa, rsb, B, S, tm=512):
    T = x2.shape[0]
    per_b = S // tm
    dils = [d for _, d in DIL_PATTERNS]
    fold_spec = lambda d: pl.BlockSpec((1, d, tm // d, 3 * DIL_WIDTH),
                                       lambda i: (i // per_b, 0, i % per_b, 0))
    fold_shape = lambda d: jax.ShapeDtypeStruct((B, d, S // d, 3 * DIL_WIDTH), BF16)
    cmp_w = CMP_STRIDE * HEAD_DIM
    assert S // SEL_LEN <= SEL_BLOCKS_MAX
    return pl.pallas_call(
        functools.partial(_inproj_body, per_b=per_b),
        grid=(T // tm,),
        in_specs=[
            pl.BlockSpec((tm, D_MODEL), lambda i: (i, 0)),
            _layer_spec(g, layer), _layer_spec(w, layer),
            pl.BlockSpec((tm, LANES), lambda i: (i, 0)),
            pl.BlockSpec((tm, LANES), lambda i: (i, 0)),
            pl.BlockSpec((tm, LANES), lambda i: (i, 0)),
        ],
        out_specs=[pl.BlockSpec((tm, N_PROJ), lambda i: (i, 0)), *[fold_spec(d) for d in dils],
                   pl.BlockSpec((2, 1, NSA_KV_HEADS, tm // CMP_STRIDE, cmp_w),
                                lambda i: (0, i // per_b, 0, i % per_b, 0))],
        out_shape=[jax.ShapeDtypeStruct((T, N_PROJ), BF16), *[fold_shape(d) for d in dils],
                   jax.ShapeDtypeStruct((2, B, NSA_KV_HEADS, S // CMP_STRIDE, cmp_w), BF16)],
        scratch_shapes=[pltpu.VMEM((N_FOLD // LANES, tm, LANES), F32),
                        pltpu.VMEM((N_W_IN, D_MODEL), BF16)],
        compiler_params=_params(),
        name="inproj",
    )(x2, g, w, rc, rsa, rsb)


def _compress_body(a_ref, w1_ref, w2_ref, pos_ref, o_ref):
    nb, ncp = o_ref.shape[1], o_ref.shape[2]
    a = a_ref[0]
    w1 = w1_ref[0, 0]
    half = CMP_STRIDE * HEAD_DIM
    top = _dot(a, w1[:half])
    bot = _dot(a, w1[half:])
    pc = _dot(pos_ref[0, 0], w1)
    rows = a.shape[0]
    hid = top + pltpu.roll(bot, rows - 1, 0) + pc[0:1]
    out = _dot(_silu(hid).astype(BF16), w2_ref[0, 0])
    for b in range(nb):
        heads = [out[(b * NSA_KV_HEADS + g) * ncp:(b * NSA_KV_HEADS + g + 1) * ncp]
                 for g in range(NSA_KV_HEADS)]
        o_ref[0, b] = jnp.concatenate(heads, axis=1).astype(BF16)


def _compress(a, w1, w2, pos, layer, B):
    n, rows, _ = a.shape
    ncp = rows // (B * NSA_KV_HEADS)
    per_kv = lambda arr: pl.BlockSpec((1, 1) + arr.shape[2:], lambda i: (layer, i, 0, 0))
    return pl.pallas_call(
        _compress_body,
        grid=(n,),
        in_specs=[pl.BlockSpec((1, rows, CMP_STRIDE * HEAD_DIM), lambda i: (i, 0, 0)),
                  per_kv(w1), per_kv(w2), per_kv(pos)],
        out_specs=pl.BlockSpec((1, B, ncp, NSA_KV_HEADS * HEAD_DIM), lambda i: (i, 0, 0, 0)),
        out_shape=jax.ShapeDtypeStruct((n, B, ncp, NSA_KV_HEADS * HEAD_DIM), BF16),
        compiler_params=_params(),
        name="compress",
    )(a, w1, w2, pos)


def _softmax2(s):
    m = jnp.max(s, axis=-1, keepdims=True)
    e = jnp.exp2(s - m)
    return e.astype(BF16), jnp.sum(e, axis=-1, keepdims=True)


def _weighted_values(e, l, v):
    nh, tq, nk = e.shape
    return _dot(e.reshape(nh * tq, nk), v) / l.reshape(nh * tq, 1)


SEL_PREFIX = 256


def _nsa_body(q_ref, kc_ref, vc_ref, ksel_ref, vsel_ref, kwin_ref, vwin_ref, gate_ref, mt_ref,
              o_ref, osel_ref, *, tq, seq):
    nblk = seq // SEL_LEN
    ncp = kc_ref.shape[1]
    n_cmp = (seq - CMP_LEN) // CMP_STRIDE + 1
    q0 = pl.program_id(1) * tq
    q = q_ref[0]
    t_col = q0 + lax.broadcasted_iota(jnp.int32, (tq, 1), 0)
    t_row = q0 + lax.broadcasted_iota(jnp.int32, (1, tq), 1)
    gates = jax.nn.sigmoid(gate_ref[0].astype(F32))

    zeros64 = jnp.zeros((tq, HEAD_DIM), BF16)

    def stacked_q(g):
        parts = []
        for hh in range(HEADS_PER_KV):
            h = g * HEADS_PER_KV + hh
            qh = q[:, h * HEAD_DIM:(h + 1) * HEAD_DIM]
            parts.append(jnp.concatenate([qh, zeros64] if g == 0 else [zeros64, qh], axis=1))
        return jnp.concatenate(parts, axis=0)

    qs = [stacked_q(g) for g in range(NSA_KV_HEADS)]

    cidx = lax.broadcasted_iota(jnp.int32, (tq, ncp), 1)
    cmask = ((cidx * CMP_STRIDE + (CMP_LEN - 1)) <= t_col) & (cidx < n_cmp)
    jidx = lax.broadcasted_iota(jnp.int32, (nblk, tq), 0)
    cur = lax.shift_right_logical(t_row, 6)
    forced = (jidx == 0) | (jidx == cur) | (jidx == cur - 1)
    future = jidx > cur
    groups = range(NSA_KV_HEADS)
    span = WIN + tq
    ks = pl.multiple_of(jnp.maximum(q0 - WIN, 0), tq)
    kw = kwin_ref[0, pl.ds(ks, span), :]
    vw = vwin_ref[0, pl.ds(ks, span), :]
    wpos = ks + lax.broadcasted_iota(jnp.int32, (1, span), 1)
    wmask = (wpos <= t_col) & (t_col - wpos <= WIN - 1)

    s_cmp = [jnp.where(cmask[None], _dot_nt(qs[g], kc_ref[0]).reshape(HEADS_PER_KV, tq, ncp), NEG)
             for g in groups]
    s_win = [jnp.where(wmask[None], _dot_nt(qs[g], kw).reshape(HEADS_PER_KV, tq, span), NEG)
             for g in groups]
    p_cmp = []
    for g in groups:
        m = jnp.max(s_cmp[g], axis=-1, keepdims=True)
        e = jnp.where(cmask[None], jnp.exp2(s_cmp[g] - m), 0.0)
        den = jnp.sum(e, axis=-1, keepdims=True)
        p_cmp.append(e / jnp.where(den > 0, den, 1.0))
    e_win = [_softmax2(s_win[g]) for g in groups]
    o_cmp = [_dot(p_cmp[g].astype(BF16).reshape(HEADS_PER_KV * tq, ncp), vc_ref[0]) for g in groups]
    imps = [_dot_nt(mt_ref[...], p_cmp[g][0] + p_cmp[g][1] + p_cmp[g][2] + p_cmp[g][3],
                    precision=lax.Precision.HIGHEST) for g in groups]
    o_win = [_weighted_values(*e_win[g], vw) for g in groups]

    def head_part(o, g, hh):
        return o[hh * tq:(hh + 1) * tq, g * HEAD_DIM:(g + 1) * HEAD_DIM]

    def gate_col(h, br):
        return jnp.broadcast_to(gates[:, 3 * h + br:3 * h + br + 1], (tq, HEAD_DIM))

    heads = [(g, hh, g * HEADS_PER_KV + hh) for g in groups for hh in range(HEADS_PER_KV)]
    partial = [gate_col(h, 0) * head_part(o_cmp[g], g, hh) + gate_col(h, 2) * head_part(o_win[g], g, hh)
               for g, hh, h in heads]
    sel_gate = [gate_col(h, 1) for _, _, h in heads]

    first_blk = lax.shift_right_logical(q0, 6)
    picked_rows, before_rows = [], []
    for g in groups:
        imp = jnp.where(forced, BIG, imps[g])
        imp = jnp.where(future, -BIG, imp)
        rank = jnp.zeros((nblk, tq), jnp.int32)
        for i in range(nblk):
            row = imp[i:i + 1, :]
            beats = (row > imp) | ((row == imp) & (jidx > i))
            rank = rank + beats.astype(jnp.int32)
        picked_rows.append(jnp.where(rank < N_SEL, 0.0, MASK_BIAS))
        before_rows.append(jnp.where((rank < N_SEL) & (jidx < first_blk), 0.0, MASK_BIAS))
    assert 2 * NSA_KV_HEADS * SEL_BLOCKS_MAX == LANES
    bias_t = jnp.concatenate(before_rows + picked_rows, axis=0)
    bias_main = bias_t.T
    bias_diag = pltpu.roll(bias_main, LANES // 2, 1)
    lane_group = lax.shift_right_logical(lax.broadcasted_iota(jnp.int32, (tq, LANES), 1), 5)

    def with_bias(g, bias):
        own = jnp.where(lane_group == g, bias, 0.0).astype(BF16)
        return jnp.concatenate([qs[g], jnp.concatenate([own] * HEADS_PER_KV, axis=0)], axis=1)

    qb = [with_bias(g, bias_main) for g in groups]

    kdiag = ksel_ref[0, pl.ds(pl.multiple_of(q0, tq), tq), :]
    vdiag = vsel_ref[0, pl.ds(pl.multiple_of(q0, tq), tq), :]
    tri = (lax.broadcasted_iota(jnp.int32, (tq, tq), 1) <= lax.broadcasted_iota(jnp.int32, (tq, tq), 0))
    s_diag = [jnp.where(tri[None], _dot_nt(with_bias(g, bias_diag), kdiag)
                        .reshape(HEADS_PER_KV, tq, tq), NEG) for g in groups]
    n_prefix = q0 // SEL_PREFIX + 1
    for n in range(1, seq // SEL_PREFIX + 1):
        klen = n * SEL_PREFIX

        @pl.when(n_prefix == n)
        def _(klen=klen):
            vall = jnp.concatenate([vsel_ref[0, :klen, :], vdiag], axis=0)
            s = [jnp.concatenate(
                [_dot_nt(qb[g], ksel_ref[0, :klen, :]).reshape(HEADS_PER_KV, tq, klen), s_diag[g]],
                axis=-1) for g in groups]
            ew = [_softmax2(s[g]) for g in groups]
            for g in groups:
                osel_ref[g] = _weighted_values(*ew[g], vall)

    o_sel = [osel_ref[g] for g in groups]

    outs = [partial[h] + sel_gate[h] * head_part(o_sel[g], g, hh) for g, hh, h in heads]
    o_ref[0] = jnp.concatenate(outs, axis=1).astype(BF16)


def _nsa(proj3, kc, vc, mt, tq=128):
    B, S, _ = proj3.shape
    blk = lambda c: c // LANES
    seq_spec = lambda c: pl.BlockSpec((1, S, LANES), lambda b, i: (b, 0, blk(c)))
    ncp = kc.shape[1]
    return pl.pallas_call(
        functools.partial(_nsa_body, tq=tq, seq=S),
        grid=(B, S // tq),
        in_specs=[
            pl.BlockSpec((1, tq, NSA_HEADS * HEAD_DIM),
                         lambda b, i: (b, i, COL_QA // (NSA_HEADS * HEAD_DIM))),
            pl.BlockSpec((1, ncp, LANES), lambda b, i: (b, 0, 0)),
            pl.BlockSpec((1, ncp, LANES), lambda b, i: (b, 0, 0)),
            pl.BlockSpec((1, S, 2 * LANES), lambda b, i: (b, 0, COL_KSEL // (2 * LANES))),
            seq_spec(COL_VSEL), seq_spec(COL_KWIN), seq_spec(COL_VWIN),
            pl.BlockSpec((1, tq, LANES), lambda b, i: (b, i, blk(COL_GATE))),
            _const_spec(mt.shape),
        ],
        out_specs=pl.BlockSpec((1, tq, NSA_HEADS * HEAD_DIM), lambda b, i: (b, i, 0)),
        out_shape=jax.ShapeDtypeStruct((B, S, NSA_HEADS * HEAD_DIM), BF16),
        scratch_shapes=[pltpu.VMEM((NSA_KV_HEADS, HEADS_PER_KV * tq, LANES), F32)],
        compiler_params=_params(),
        name="nsa",
    )(proj3, kc, vc, proj3, proj3, proj3, proj3, proj3, mt)


DIL_SUB = 128
DIL_ROWS = 1024


def _dil_body(q_ref, kp_ref, kc_ref, vp_ref, vc_ref, o_ref, lse_ref, *, n_back):
    sub = DIL_SUB
    rb, tq = q_ref.shape[1], q_ref.shape[2]
    t0 = pl.program_id(2) * tq
    head_of = lax.shift_right_logical(lax.broadcasted_iota(jnp.int32, (sub, DIL_WIDTH), 1), 6)
    lane = lax.broadcasted_iota(jnp.int32, (sub, LANES), 1)
    diff = (sub + lax.broadcasted_iota(jnp.int32, (sub, 1), 0)
            - lax.broadcasted_iota(jnp.int32, (1, 2 * sub), 1))
    band = (diff >= 0) & (diff <= n_back)
    band0 = band & (lax.broadcasted_iota(jnp.int32, (1, 2 * sub), 1) + t0 >= sub)
    tiles = [(r, j) for r in range(rb) for j in range(tq // sub)]
    keys = {r: jnp.concatenate([kp_ref[0, r], kc_ref[0, r]], axis=0) for r in range(rb)}
    vals = {r: jnp.concatenate([vp_ref[0, r], vc_ref[0, r]], axis=0) for r in range(rb)}
    scores = []
    for r, j in tiles:
        q = q_ref[0, r, j * sub:(j + 1) * sub, :]
        qs = jnp.concatenate([jnp.where(head_of == h, q, jnp.zeros_like(q))
                              for h in range(DIL_HEADS)], axis=0)
        s = _dot_nt(qs, keys[r][j * sub:(j + 2) * sub]).reshape(DIL_HEADS, sub, 2 * sub)
        scores.append(jnp.where((band0 if j == 0 else band)[None], s, NEG))
    stats = []
    for s in scores:
        m = jnp.max(s, axis=-1, keepdims=True)
        e = jnp.exp2(s - m)
        stats.append((m, e, jnp.sum(e, axis=-1, keepdims=True)))
    for (r, j), (m, e, l) in zip(tiles, stats):
        o = _dot(e.astype(BF16).reshape(DIL_HEADS * sub, 2 * sub), vals[r][j * sub:(j + 2) * sub])
        o = o.reshape(DIL_HEADS, sub, DIL_WIDTH) / l
        lse = m * LN2 + jnp.log(l)
        o_acc = jnp.zeros((sub, DIL_WIDTH), F32)
        lse_out = jnp.zeros((sub, LANES), F32)
        for h in range(DIL_HEADS):
            o_acc = jnp.where(head_of == h, o[h], o_acc)
            lse_out = jnp.where(lane == h, lse[h], lse_out)
        o_ref[0, r, j * sub:(j + 1) * sub, :] = o_acc
        lse_ref[0, r, j * sub:(j + 1) * sub, :] = lse_out


def _dilated(arr, cols, window, dilation, name):
    B, d, L, _ = arr.shape
    n_back = window // dilation
    tq = min(L, DIL_ROWS)
    rb = DIL_ROWS // tq
    assert d == dilation and n_back <= DIL_SUB and L % tq == 0 and d % rb == 0
    per = tq // DIL_SUB
    qc, kc, vc = cols
    cur = lambda c: pl.BlockSpec((1, rb, tq, DIL_WIDTH), lambda b, r, i: (b, r, i, c))
    prev = lambda c: pl.BlockSpec((1, rb, DIL_SUB, DIL_WIDTH),
                                  lambda b, r, i: (b, r, jnp.maximum(i * per - 1, 0), c))
    return pl.pallas_call(
        functools.partial(_dil_body, n_back=n_back),
        grid=(B, dilation // rb, L // tq),
        in_specs=[cur(qc), prev(kc), cur(kc), prev(vc), cur(vc)],
        out_specs=[pl.BlockSpec((1, rb, tq, DIL_WIDTH), lambda b, r, i: (b, r, i, 0)),
                   pl.BlockSpec((1, rb, tq, LANES), lambda b, r, i: (b, r, i, 0))],
        out_shape=[jax.ShapeDtypeStruct((B, dilation, L, DIL_WIDTH), F32),
                   jax.ShapeDtypeStruct((B, dilation, L, LANES), F32)],
        compiler_params=_params(),
        name=name,
    )(arr, arr, arr, arr, arr)


def _mixout_body(oa_ref, ob0_ref, ob1_ref, ob2_ref, l0_ref, l1_ref, l2_ref, mg_ref, x_ref,
                 pa_ref, pb_ref, wo_ref, out_ref, so_ref, sl_ref):
    tm = x_ref.shape[0]

    def interleaved(src_ref, st_ref):
        d = src_ref.shape[1]
        if d == 1:
            return src_ref[0, 0]
        slabs = src_ref.shape[3] // LANES
        for r in range(d):
            for k in range(slabs):
                st_ref[k, pl.ds(r, tm // d, stride=d), :] = src_ref[0, r, :, k * LANES:(k + 1) * LANES]
        return jnp.concatenate([st_ref[k] for k in range(slabs)], axis=1)

    lses = [interleaved(l, sl_ref) for l in (l0_ref, l1_ref, l2_ref)]
    mx = jnp.maximum(jnp.maximum(lses[0], lses[1]), lses[2])
    ws = [jnp.exp(l - mx) for l in lses]
    den = ws[0] + ws[1] + ws[2]
    ob = jnp.zeros((tm, DIL_WIDTH), F32)
    for w, o_ref in zip(ws, (ob0_ref, ob1_ref, ob2_ref)):
        alpha = w / den
        wide = jnp.concatenate(
            [jnp.broadcast_to(alpha[:, h:h + 1], (tm, HEAD_DIM)) for h in range(DIL_HEADS)], axis=1)
        ob = ob + wide * interleaved(o_ref, so_ref)
    ya = _dot(oa_ref[...], pa_ref[0])
    yb = _dot(ob.astype(BF16), pb_ref[0])
    gm = jax.nn.sigmoid(mg_ref[...].astype(F32))
    y = gm[:, :D_MODEL] * ya + gm[:, D_MODEL:] * yb
    out_ref[...] = x_ref[...] + _dot(y.astype(BF16), wo_ref[0])


def _mixout(oa, obs, lses, proj, x2, pa, pb, wo, layer, S, tm=1024):
    T = x2.shape[0]
    per_b = S // tm
    row = lambda w: pl.BlockSpec((tm, w), lambda i: (i, 0))
    folded = lambda a: pl.BlockSpec((1, a.shape[1], tm // a.shape[1], a.shape[3]),
                                    lambda i: (i // per_b, 0, i % per_b, 0))
    return pl.pallas_call(
        _mixout_body,
        grid=(T // tm,),
        in_specs=[row(NSA_HEADS * HEAD_DIM), *[folded(a) for a in obs], *[folded(a) for a in lses],
                  pl.BlockSpec((tm, 2 * D_MODEL), lambda i: (i, COL_MERGE // (2 * D_MODEL))),
                  row(D_MODEL),
                  _layer_spec(pa, layer), _layer_spec(pb, layer), _layer_spec(wo, layer)],
        out_specs=row(D_MODEL),
        out_shape=jax.ShapeDtypeStruct((T, D_MODEL), F32),
        scratch_shapes=[pltpu.VMEM((DIL_WIDTH // LANES, tm, LANES), F32),
                        pltpu.VMEM((1, tm, LANES), F32)],
        compiler_params=_params(),
        name="mixout",
    )(oa, *obs, *lses, proj, x2, pa, pb, wo)


FF_CHUNK = 512


def _ffn_body(x_ref, g_ref, w1_ref, w3_ref, w2_ref, *rest):
    n_cast = (len(rest) - 1) // 2
    src_refs, o_ref, dst_refs = rest[:n_cast], rest[n_cast], rest[n_cast + 1:]
    x = x_ref[...]
    h = _rms(x, g_ref[...]).astype(BF16)
    acc = jnp.zeros(x.shape, F32)
    for c0 in range(0, D_FF, FF_CHUNK):
        c1 = min(c0 + FF_CHUNK, D_FF)
        act = _silu(_dot(h, w1_ref[:, c0:c1])) * _dot(h, w3_ref[:, c0:c1])
        acc = acc + _dot(act.astype(BF16), w2_ref[c0:c1, :])
    o_ref[...] = x + acc
    for src, dst in zip(src_refs, dst_refs):
        dst[...] = src[...].astype(BF16)


def _ffn(x2, g, w1, w3, w2, to_bf16=(), tm=512):
    T = x2.shape[0]
    steps = T // tm
    held = lambda a: pl.BlockSpec(a.shape, lambda i: (0, 0), pipeline_mode=pl.Buffered(1))
    sliced = lambda a: pl.BlockSpec((a.shape[0] // steps, a.shape[1]), lambda i: (i, 0))
    assert all(a.shape[0] % (16 * steps) == 0 for a in to_bf16)
    out, *copies = pl.pallas_call(
        _ffn_body,
        grid=(steps,),
        in_specs=[pl.BlockSpec((tm, D_MODEL), lambda i: (i, 0)), _const_spec(g.shape),
                  held(w1), held(w3), held(w2), *[sliced(a) for a in to_bf16]],
        out_specs=[pl.BlockSpec((tm, D_MODEL), lambda i: (i, 0)), *[sliced(a) for a in to_bf16]],
        out_shape=[jax.ShapeDtypeStruct((T, D_MODEL), F32),
                   *[jax.ShapeDtypeStruct(a.shape, BF16) for a in to_bf16]],
        compiler_params=_params(),
        name="ffn",
    )(x2, g, w1, w3, w2, *to_bf16)
    return out, copies


MOE_TM = 1024
MOE_CHUNK = 128
MOE_BIG = 2 * MOE_CHUNK
MOE_MERGED = 4


def _router_body(x_ref, g_ref, r_ref, h_ref, combw_ref, posw_ref, post_ref, cnt_ref):
    tm = x_ref.shape[0]
    h = _rms(x_ref[...], g_ref[...])
    h_ref[...] = h.astype(BF16)
    r = r_ref[...]
    h_hi, r_hi = h.astype(BF16), r.astype(BF16)
    h_lo = (h - h_hi.astype(F32)).astype(BF16)
    r_lo = (r - r_hi.astype(F32)).astype(BF16)
    logits = _dot(h_hi, r_hi) + (_dot(h_hi, r_lo) + _dot(h_lo, r_hi))
    lane = lax.broadcasted_iota(jnp.int32, logits.shape, 1)
    lg = jnp.where(lane < N_EXPERTS, logits, NEG)
    m1 = jnp.max(lg, axis=-1, keepdims=True)
    i1 = jnp.min(jnp.where(lg == m1, lane, LANES), axis=-1, keepdims=True)
    lg2 = jnp.where(lane == i1, NEG, lg)
    m2 = jnp.max(lg2, axis=-1, keepdims=True)
    i2 = jnp.min(jnp.where(lg2 == m2, lane, LANES), axis=-1, keepdims=True)
    e2 = jnp.exp(m2 - m1)
    den = 1.0 + e2
    w_first, w_second = 1.0 / den, e2 / den
    chosen = [jnp.broadcast_to((i1 == ex) | (i2 == ex), (tm, LANES)) for ex in range(N_EXPERTS)]
    ones = jnp.concatenate([jnp.where(c, 1.0, 0.0).astype(BF16) for c in chosen], axis=1)
    row = lax.broadcasted_iota(jnp.int32, (LANES, LANES), 0)
    col = lax.broadcasted_iota(jnp.int32, (LANES, LANES), 1)
    tri = jnp.where(col < row, 1.0, 0.0).astype(BF16)
    running = jnp.zeros((1, N_EXPERTS * LANES), F32)
    parts = []
    for b in range(tm // LANES):
        blk = ones[b * LANES:(b + 1) * LANES]
        parts.append(_dot(tri, blk) + running)
        running = running + jnp.sum(blk.astype(F32), axis=0, keepdims=True)
    before = jnp.concatenate(parts, axis=0)
    by_lane = jnp.zeros((tm, LANES), F32)
    cnt = jnp.zeros((1, LANES), F32)
    for ex in range(N_EXPERTS):
        slab = slice(ex * LANES, (ex + 1) * LANES)
        combw_ref[:, slab] = jnp.broadcast_to(
            jnp.where(i1 == ex, w_first, 0.0) + jnp.where(i2 == ex, w_second, 0.0), (tm, LANES))
        pos = jnp.where(chosen[ex], before[:, slab], -1.0)
        posw_ref[:, slab] = pos
        by_lane = jnp.where(lane == ex, pos, by_lane)
        cnt = jnp.where(lane[:1] == ex, running[:, slab], cnt)
    post_ref[0] = by_lane.T[:N_EXPERTS]
    cnt_ref[0] = jnp.broadcast_to(cnt, (8, LANES)).astype(jnp.int32)


def _router(x2, g, r, tm=MOE_TM):
    T = x2.shape[0]
    nt = T // tm
    row = lambda w: pl.BlockSpec((tm, w), lambda i: (i, 0))
    return pl.pallas_call(
        _router_body,
        grid=(nt,),
        in_specs=[row(D_MODEL), _const_spec(g.shape), _const_spec(r.shape)],
        out_specs=[row(D_MODEL), row(N_EXPERTS * LANES), row(N_EXPERTS * LANES),
                   pl.BlockSpec((1, N_EXPERTS, tm), lambda i: (i, 0, 0)),
                   pl.BlockSpec((1, 8, LANES), lambda i: (i, 0, 0))],
        out_shape=[jax.ShapeDtypeStruct((T, D_MODEL), BF16),
                   jax.ShapeDtypeStruct((T, N_EXPERTS * LANES), F32),
                   jax.ShapeDtypeStruct((T, N_EXPERTS * LANES), F32),
                   jax.ShapeDtypeStruct((nt, N_EXPERTS, tm), F32),
                   jax.ShapeDtypeStruct((nt, 8, LANES), jnp.int32)],
        compiler_params=_params(),
        name="router",
    )(x2, g, r)


def _moe_body(cnt_ref, x_ref, h_ref, comb_ref, pos_ref, post_ref, w1_ref, w3_ref, w2_ref, fn_ref,
              o_ref, y_ref):
    i = pl.program_id(0)
    e = pl.program_id(1)
    merged = MOE_MERGED * LANES
    n = cnt_ref[i * N_EXPERTS + e]
    n_big = (n + (MOE_BIG - MOE_CHUNK - 1)) // MOE_BIG
    rest_row = pl.multiple_of(n_big * MOE_BIG, MOE_BIG)
    has_rest = n > rest_row
    post_e = post_ref[0, pl.ds(e, 1), :]

    @pl.when(e == 0)
    def _():
        o_ref[...] = x_ref[...]

    @pl.when((i == 0) & (e == 0))
    def _():
        y_ref[...] = jnp.zeros_like(y_ref)

    def scatter_add(first_row, first_slot, n_slabs):
        lane_slot = lax.broadcasted_iota(jnp.int32, (1, LANES), 1).astype(F32)
        hot = [jnp.where(pos_ref[...] == first_slot + (k * LANES) + lane_slot, 1.0, 0.0).astype(BF16)
               for k in range(n_slabs)]
        scatter = hot[0] if n_slabs == 1 else jnp.concatenate(hot, axis=1)
        weight = jnp.concatenate([comb_ref[...]] * (D_MODEL // LANES), axis=1)
        ys = y_ref[pl.ds(first_row, n_slabs * LANES), :].astype(BF16)
        o_ref[...] += weight * _dot(scatter, ys)

    def chunk(r0, size):
        base = r0.astype(F32)
        slot_col = base + lax.broadcasted_iota(jnp.int32, (size, 1), 0).astype(F32)
        gather = jnp.where(post_e == slot_col, 1.0, 0.0).astype(BF16)
        xe = _dot(gather, h_ref[...]).astype(BF16)
        act = _silu(_dot(xe, w1_ref[0])) * _dot(xe, w3_ref[0])
        y = _dot(act.astype(BF16), w2_ref[0])

        @pl.when(r0 < merged)
        def _():
            y_ref[pl.ds(r0, size), :] = y

        @pl.when(r0 >= merged)
        def _():
            y_ref[merged:merged + size, :] = y
            for k in range(size // LANES):
                scatter_add(merged + k * LANES, base + float(k * LANES), 1)

    def big_chunk(c, carry):
        chunk(pl.multiple_of(c * MOE_BIG, MOE_BIG), MOE_BIG)
        return carry

    lax.fori_loop(0, n_big, big_chunk, 0)

    @pl.when(has_rest)
    def _():
        chunk(rest_row, MOE_CHUNK)

    @pl.when(n > 0)
    def _():
        scatter_add(0, 0.0, MOE_MERGED)

    @pl.when(e == pl.num_programs(1) - 1)
    def _():
        o_ref[...] = _rms(o_ref[...], fn_ref[...])


def _moe(x2, h, comb, pos, post, counts, w1, w3, w2, fn, tm=MOE_TM):
    T = x2.shape[0]
    once = lambda w: pl.BlockSpec((tm, w), lambda i, e, cnt: (i, 0), pipeline_mode=pl.Buffered(1))
    tile = lambda w: pl.BlockSpec((tm, w), lambda i, e, cnt: (i, 0))
    assert MOE_CHUNK == LANES
    slab = pl.BlockSpec((tm, LANES), lambda i, e, cnt: (i, e))
    expert = lambda a: pl.BlockSpec((1,) + a.shape[1:], lambda i, e, cnt: (e, 0, 0))
    grid_spec = pltpu.PrefetchScalarGridSpec(
        num_scalar_prefetch=1,
        grid=(T // tm, N_EXPERTS),
        in_specs=[once(D_MODEL), once(D_MODEL), slab, slab,
                  pl.BlockSpec((1, N_EXPERTS, tm), lambda i, e, cnt: (i, 0, 0)),
                  expert(w1), expert(w3), expert(w2),
                  pl.BlockSpec(fn.shape, lambda i, e, cnt: (0, 0))],
        out_specs=tile(D_MODEL),
        scratch_shapes=[pltpu.VMEM((MOE_MERGED * LANES + MOE_BIG, D_MODEL), F32)],
    )
    return pl.pallas_call(
        _moe_body,
        grid_spec=grid_spec,
        out_shape=jax.ShapeDtypeStruct((T, D_MODEL), F32),
        compiler_params=_params(),
        name="moe",
    )(counts, x2, h, comb, pos, post, w1, w3, w2, fn)


def _rope_lane_tables(positions):
    half = ROT_DIM // 2
    inv = ROPE_THETA ** (-jnp.arange(0, ROT_DIM, 2, dtype=F32) / ROT_DIM)
    ang = positions.astype(F32).reshape(-1, 1) * inv
    cos, sin = jnp.cos(ang), jnp.sin(ang)
    within = np.arange(LANES) % HEAD_DIM
    cos_l, sin_l = jnp.tile(cos, (1, LANES // half)), jnp.tile(sin, (1, LANES // half))
    c = jnp.where(within < ROT_DIM, cos_l, 1.0)
    sa = jnp.where(within < half, -sin_l, 0.0)
    sb = jnp.where((within >= half) & (within < ROT_DIM), sin_l, 0.0)
    return c, sa, sb


def _w_in_plan():
    scale = HEAD_DIM ** -0.5 * LOG2E
    qa, kv, gate, qkv_b, merge = 0, 512, 1280, 1304, 3608
    nb = N_DIL_GROUPS * DIL_WIDTH
    kv_piece = lambda j: [(kv + LANES * j, LANES, 1.0)]
    span = lambda start, width, s=1.0: [(start + c, LANES, s) for c in range(0, width, LANES)]
    plan = (span(qa, 512, scale)
            + kv_piece(2) + kv_piece(4) + kv_piece(0)
            + kv_piece(3) + kv_piece(5)
            + [(gate, 3 * NSA_HEADS, 1.0)]
            + kv_piece(1)
            + span(merge, 2 * D_MODEL)
            + span(qkv_b, nb, scale) + span(qkv_b + nb, nb) + span(qkv_b + 2 * nb, nb))
    assert len(plan) * LANES == N_W_IN
    return plan


def _importance_matrix_t(seq, ncp):
    n_c = (seq - CMP_LEN) // CMP_STRIDE + 1
    starts = np.arange(n_c) * CMP_STRIDE
    bstart = np.arange(seq // SEL_LEN) * SEL_LEN
    overlap = np.clip(np.minimum(starts[:, None] + CMP_LEN, bstart[None, :] + SEL_LEN)
                      - np.maximum(starts[:, None], bstart[None, :]), 0, None)
    m = np.zeros((ncp, seq // SEL_LEN), np.float32)
    m[:n_c] = overlap.astype(np.float32) / CMP_LEN
    return jnp.asarray(m.T)


def _mixer(x2, layer, B, S, tables, mt, prm):
    proj, *folds, a = _inproj(x2, prm["norm_mix"], prm["w_in"], layer, *tables, B, S)
    proj3 = proj.reshape(B, S, N_PROJ)
    ncp = S // CMP_STRIDE
    cmp = _compress(a.reshape(2, B * NSA_KV_HEADS * ncp, CMP_STRIDE * HEAD_DIM),
                    prm["cmp_w1"], prm["cmp_w2"], prm["cmp_pos"], layer, B)
    oa = _nsa(proj3, cmp[0], cmp[1], mt).reshape(B * S, NSA_HEADS * HEAD_DIM)
    obs, lses = [], []
    for g, ((w, d), arr) in enumerate(zip(DIL_PATTERNS, folds)):
        o, lse = _dilated(arr, (0, 1, 2), w, d, f"dilated{g}")
        obs.append(o)
        lses.append(lse)
    return _mixout(oa, obs, lses, proj, x2, prm["p_a"], prm["p_b"], prm["w_o"], layer, S)


def kernel(x, positions, norm_mix, w_in, cmp_pos_k, cmp_pos_v, cmp_k_w1, cmp_k_w2, cmp_v_w1,
           cmp_v_w2, w_branch_a, w_branch_b, w_out, norm_ffn, ffn_w1, ffn_w3, ffn_w2, router,
           moe_w1, moe_w3, moe_w2, final_norm):
    B, S, D = x.shape
    depth = norm_mix.shape[0]
    assert depth == 2 and D == D_MODEL
    tables = _rope_lane_tables(positions)
    mt = _importance_matrix_t(S, S // CMP_STRIDE)
    cmp_pos = jnp.stack([cmp_pos_k, cmp_pos_v], axis=1).reshape(depth, 2, 1, CMP_LEN * HEAD_DIM)
    prm = {
        "norm_mix": norm_mix.reshape(depth, 1, D),
        "w_in": jnp.swapaxes(w_in, 1, 2),
        "cmp_w1": jnp.stack([cmp_k_w1, cmp_v_w1], axis=1).astype(BF16),
        "cmp_w2": jnp.stack([cmp_k_w2, cmp_v_w2], axis=1).astype(BF16),
        "cmp_pos": jnp.broadcast_to(cmp_pos, (depth, 2, 8, CMP_LEN * HEAD_DIM)).astype(BF16),
        "p_a": w_branch_a.astype(BF16), "p_b": w_branch_b.astype(BF16), "w_o": w_out.astype(BF16),
    }
    x2 = x.reshape(B * S, D)
    x2 = _mixer(x2, 0, B, S, tables, mt, prm)
    experts = [w[0].reshape(-1, w.shape[-1]) for w in (moe_w1, moe_w3, moe_w2)]
    x2, experts = _ffn(x2, norm_ffn[0].reshape(1, -1), ffn_w1[0].astype(BF16),
                       ffn_w3[0].astype(BF16), ffn_w2[0].astype(BF16), to_bf16=experts)
    ew1, ew3, ew2 = [w.reshape(m.shape[1:]) for w, m in zip(experts, (moe_w1, moe_w3, moe_w2))]
    x2 = _mixer(x2, 1, B, S, tables, mt, prm)
    g1 = norm_ffn[1].reshape(1, -1)
    r = jnp.pad(router[0], ((0, 0), (0, LANES - N_EXPERTS)))
    h, comb, pos, post, cnt = _router(x2, g1, r)
    counts = cnt[:, 0, :N_EXPERTS].reshape(-1)
    out = _moe(x2, h, comb, pos, post, counts, ew1, ew3, ew2, final_norm.reshape(1, -1))
    return out.reshape(B, S, D)
```

```python
import functools

import numpy as np
import jax
import jax.numpy as jnp
from jax import lax
from jax.experimental import pallas as pl
from jax.experimental.pallas import tpu as pltpu

F32 = jnp.float32
BF16 = jnp.bfloat16

D_MODEL = 1024
HEAD_DIM = 64
ROT_DIM = HEAD_DIM // 4
ROPE_THETA = 500000.0
EPS = 1e-6
NSA_HEADS = 8
NSA_KV_HEADS = 2
HEADS_PER_KV = NSA_HEADS // NSA_KV_HEADS
CMP_LEN = 32
CMP_STRIDE = 16
SEL_LEN = 64
N_SEL = 16
WIN = 512
DIL_PATTERNS = ((128, 1), (512, 4), (2048, 16))
N_DIL_GROUPS = 3
DIL_HEADS = 4
D_FF = 2816
N_EXPERTS = 8

LANES = 128
VMEM_LIMIT = 56 * 1024 * 1024
NEG = -1e30
BIG = 1e30

COL_MERGE = 0
COL_QA = 2048
COL_KSEL = 2560
COL_BLK = COL_KSEL + 128
COL_KWIN = 2816
COL_VSEL = 2944
COL_VWIN = 3072
COL_GATE = 3200
N_PROJ = 3328
DIL_WIDTH = DIL_HEADS * HEAD_DIM
N_FOLD = N_DIL_GROUPS * DIL_WIDTH
STAGE = None
IN_CHUNKS = (
    (512, True, tuple(COL_QA + 128 * j for j in range(4)), None),
    (384, True, (COL_KSEL, COL_KWIN, STAGE), ("cmp", 0)),
    (512, False, (COL_VSEL, COL_VWIN, COL_GATE, STAGE), ("cmp", 1)),
    (512, False, tuple(COL_MERGE + 128 * j for j in range(0, 4)), None),
    (512, False, tuple(COL_MERGE + 128 * j for j in range(4, 8)), None),
    (512, False, tuple(COL_MERGE + 128 * j for j in range(8, 12)), None),
    (512, False, tuple(COL_MERGE + 128 * j for j in range(12, 16)), None),
    (N_FOLD, True, (STAGE,) * 6, ("fold", 0)),
    (N_FOLD, True, (STAGE,) * 6, ("fold", 1)),
    (N_FOLD, False, (STAGE,) * 6, ("fold", 2)),
)
N_W_IN = sum(c[0] for c in IN_CHUNKS)
LOG2E = 1.4426950408889634
LN2 = 0.6931471805599453
MASK_BIAS = -(2.0 ** 100)
SEL_BLOCKS_MAX = 32


def _dot(a, b, precision=None):
    return jnp.dot(a, b, preferred_element_type=F32, precision=precision)


def _dot_nt(a, b, precision=None):
    return lax.dot_general(a, b, (((1,), (1,)), ((), ())), preferred_element_type=F32,
                           precision=precision)


def _rms(x, g):
    ms = jnp.mean(x * x, axis=-1, keepdims=True)
    return x * lax.rsqrt(ms + EPS) * g


def _silu(x):
    return x * jax.nn.sigmoid(x)


def _params(**kw):
    return pltpu.CompilerParams(vmem_limit_bytes=VMEM_LIMIT, **kw)


def _const_spec(shape):
    nd = len(shape)
    return pl.BlockSpec(shape, lambda *_: (0,) * nd)


def _inproj_body(x_ref, g_ref, wt_ref, c_ref, sa_ref, sb_ref, o_ref, f0_ref, f1_ref, f2_ref, a_ref,
                 st_ref, w_ref, *, per_b):
    tm = x_ref.shape[0]

    @pl.when(pl.program_id(0) == 0)
    def _():
        for j, (src, width, scale) in enumerate(_w_in_plan()):
            rows = wt_ref[0, src:src + LANES, :]
            w_ref[j * LANES:(j + 1) * LANES, :] = (rows * scale if scale != 1.0 else rows).astype(BF16)

    t_seq = (pl.program_id(0) % per_b) * tm + lax.broadcasted_iota(jnp.int32, (tm, LANES), 0)
    lane = lax.broadcasted_iota(jnp.int32, (tm, LANES), 1)
    blk = lax.shift_right_logical(t_seq, 6)
    hot = (lane == blk) | (lane == blk + SEL_BLOCKS_MAX)
    o_ref[:, COL_BLK:COL_BLK + LANES] = jnp.where(hot, 1.0, 0.0).astype(BF16)
    h = _rms(x_ref[...], g_ref[0]).astype(BF16)
    c = c_ref[...]
    sa = sa_ref[...]
    sb = sb_ref[...]
    start = 0
    for size, rope, dests, action in IN_CHUNKS:
        acc = _dot_nt(h, w_ref[start:start + size, :])
        start += size
        for j, dest in enumerate(dests):
            a = acc[:, j * LANES:(j + 1) * LANES]
            if rope:
                a = a * c + pltpu.roll(a, LANES - 8, 1) * sa + pltpu.roll(a, 8, 1) * sb
            if dest is STAGE:
                st_ref[j] = a
            else:
                o_ref[:, dest:dest + LANES] = a.astype(BF16)
        if action is None:
            continue
        kind, piece = action
        if kind == "fold":
            slabs = DIL_WIDTH // LANES
            for gi, f_ref in enumerate((f0_ref, f1_ref, f2_ref)):
                d = DIL_PATTERNS[gi][1]
                for r in range(d):
                    for k in range(slabs):
                        rows = st_ref[gi * slabs + k, pl.ds(r, tm // d, stride=d), :]
                        c0 = piece * DIL_WIDTH + k * LANES
                        f_ref[0, r, :, c0:c0 + LANES] = rows.astype(BF16)
        else:
            slab = dests.index(STAGE)
            nrow = tm // CMP_STRIDE
            toks = [st_ref[slab, pl.ds(j, nrow, stride=CMP_STRIDE), :] for j in range(CMP_STRIDE)]
            for g in range(NSA_KV_HEADS):
                head = slice(g * HEAD_DIM, (g + 1) * HEAD_DIM)
                for m in range(CMP_STRIDE // 2):
                    pair = jnp.concatenate([toks[2 * m][:, head], toks[2 * m + 1][:, head]], axis=1)
                    a_ref[piece, 0, g, :, m * LANES:(m + 1) * LANES] = pair.astype(BF16)


def _layer_spec(arr, layer):
    nd = arr.ndim
    return pl.BlockSpec((1,) + arr.shape[1:], lambda *_: (layer,) + (0,) * (nd - 1))


def _inproj(x2, g, w, layer, rc, rsa, rsb, B, S, tm=512):
    T = x2.shape[0]
    per_b = S // tm
    dils = [d for _, d in DIL_PATTERNS]
    fold_spec = lambda d: pl.BlockSpec((1, d, tm // d, 3 * DIL_WIDTH),
                                       lambda i: (i // per_b, 0, i % per_b, 0))
    fold_shape = lambda d: jax.ShapeDtypeStruct((B, d, S // d, 3 * DIL_WIDTH), BF16)
    cmp_w = CMP_STRIDE * HEAD_DIM
    assert S // SEL_LEN <= SEL_BLOCKS_MAX
    return pl.pallas_call(
        functools.partial(_inproj_body, per_b=per_b),
        grid=(T // tm,),
        in_specs=[
            pl.BlockSpec((tm, D_MODEL), lambda i: (i, 0)),
            _layer_spec(g, layer), _layer_spec(w, layer),
            pl.BlockSpec((tm, LANES), lambda i: (i, 0)),
            pl.BlockSpec((tm, LANES), lambda i: (i, 0)),
            pl.BlockSpec((tm, LANES), lambda i: (i, 0)),
        ],
        out_specs=[pl.BlockSpec((tm, N_PROJ), lambda i: (i, 0)), *[fold_spec(d) for d in dils],
                   pl.BlockSpec((2, 1, NSA_KV_HEADS, tm // CMP_STRIDE, cmp_w),
                                lambda i: (0, i // per_b, 0, i % per_b, 0))],
        out_shape=[jax.ShapeDtypeStruct((T, N_PROJ), BF16), *[fold_shape(d) for d in dils],
                   jax.ShapeDtypeStruct((2, B, NSA_KV_HEADS, S // CMP_STRIDE, cmp_w), BF16)],
        scratch_shapes=[pltpu.VMEM((N_FOLD // LANES, tm, LANES), F32),
                        pltpu.VMEM((N_W_IN, D_MODEL), BF16)],
        compiler_params=_params(),
        name="inproj",
    )(x2, g, w, rc, rsa, rsb)


def _compress_body(a_ref, w1_ref, w2_ref, pos_ref, o_ref):
    nb, ncp = o_ref.shape[1], o_ref.shape[2]
    a = a_ref[0]
    w1 = w1_ref[0, 0]
    half = CMP_STRIDE * HEAD_DIM
    top = _dot(a, w1[:half])
    bot = _dot(a, w1[half:])
    pc = _dot(pos_ref[0, 0], w1)
    rows = a.shape[0]
    hid = top + pltpu.roll(bot, rows - 1, 0) + pc[0:1]
    out = _dot(_silu(hid).astype(BF16), w2_ref[0, 0])
    for b in range(nb):
        heads = [out[(b * NSA_KV_HEADS + g) * ncp:(b * NSA_KV_HEADS + g + 1) * ncp]
                 for g in range(NSA_KV_HEADS)]
        o_ref[0, b] = jnp.concatenate(heads, axis=1).astype(BF16)


def _compress(a, w1, w2, pos, layer, B):
    n, rows, _ = a.shape
    ncp = rows // (B * NSA_KV_HEADS)
    per_kv = lambda arr: pl.BlockSpec((1, 1) + arr.shape[2:], lambda i: (layer, i, 0, 0))
    return pl.pallas_call(
        _compress_body,
        grid=(n,),
        in_specs=[pl.BlockSpec((1, rows, CMP_STRIDE * HEAD_DIM), lambda i: (i, 0, 0)),
                  per_kv(w1), per_kv(w2), per_kv(pos)],
        out_specs=pl.BlockSpec((1, B, ncp, NSA_KV_HEADS * HEAD_DIM), lambda i: (i, 0, 0, 0)),
        out_shape=jax.ShapeDtypeStruct((n, B, ncp, NSA_KV_HEADS * HEAD_DIM), BF16),
        compiler_params=_params(),
        name="compress",
    )(a, w1, w2, pos)


def _softmax2(s):
    m = jnp.max(s, axis=-1, keepdims=True)
    e = jnp.exp2(s - m)
    return e.astype(BF16), jnp.sum(e, axis=-1, keepdims=True)


def _weighted_values(e, l, v):
    nh, tq, nk = e.shape
    return _dot(e.reshape(nh * tq, nk), v) / l.reshape(nh * tq, 1)


SEL_PREFIX = 256


def _nsa_body(q_ref, kc_ref, vc_ref, ksel_ref, vsel_ref, kwin_ref, vwin_ref, gate_ref, mt_ref,
              o_ref, osel_ref, *, tq, seq):
    nblk = seq // SEL_LEN
    ncp = kc_ref.shape[1]
    n_cmp = (seq - CMP_LEN) // CMP_STRIDE + 1
    q0 = pl.program_id(1) * tq
    q = q_ref[0]
    t_col = q0 + lax.broadcasted_iota(jnp.int32, (tq, 1), 0)
    t_row = q0 + lax.broadcasted_iota(jnp.int32, (1, tq), 1)
    gates = jax.nn.sigmoid(gate_ref[0].astype(F32))

    zeros64 = jnp.zeros((tq, HEAD_DIM), BF16)

    def stacked_q(g):
        parts = []
        for hh in range(HEADS_PER_KV):
            h = g * HEADS_PER_KV + hh
            qh = q[:, h * HEAD_DIM:(h + 1) * HEAD_DIM]
            parts.append(jnp.concatenate([qh, zeros64] if g == 0 else [zeros64, qh], axis=1))
        return jnp.concatenate(parts, axis=0)

    qs = [stacked_q(g) for g in range(NSA_KV_HEADS)]

    cidx = lax.broadcasted_iota(jnp.int32, (tq, ncp), 1)
    cmask = ((cidx * CMP_STRIDE + (CMP_LEN - 1)) <= t_col) & (cidx < n_cmp)
    jidx = lax.broadcasted_iota(jnp.int32, (nblk, tq), 0)
    cur = lax.shift_right_logical(t_row, 6)
    forced = (jidx == 0) | (jidx == cur) | (jidx == cur - 1)
    future = jidx > cur
    groups = range(NSA_KV_HEADS)
    span = WIN + tq
    ks = pl.multiple_of(jnp.maximum(q0 - WIN, 0), tq)
    kw = kwin_ref[0, pl.ds(ks, span), :]
    vw = vwin_ref[0, pl.ds(ks, span), :]
    wpos = ks + lax.broadcasted_iota(jnp.int32, (1, span), 1)
    wmask = (wpos <= t_col) & (t_col - wpos <= WIN - 1)

    s_cmp = [jnp.where(cmask[None], _dot_nt(qs[g], kc_ref[0]).reshape(HEADS_PER_KV, tq, ncp), NEG)
             for g in groups]
    s_win = [jnp.where(wmask[None], _dot_nt(qs[g], kw).reshape(HEADS_PER_KV, tq, span), NEG)
             for g in groups]
    p_cmp = []
    for g in groups:
        m = jnp.max(s_cmp[g], axis=-1, keepdims=True)
        e = jnp.where(cmask[None], jnp.exp2(s_cmp[g] - m), 0.0)
        den = jnp.sum(e, axis=-1, keepdims=True)
        p_cmp.append(e / jnp.where(den > 0, den, 1.0))
    e_win = [_softmax2(s_win[g]) for g in groups]
    o_cmp = [_dot(p_cmp[g].astype(BF16).reshape(HEADS_PER_KV * tq, ncp), vc_ref[0]) for g in groups]
    imps = [_dot_nt(mt_ref[...], p_cmp[g][0] + p_cmp[g][1] + p_cmp[g][2] + p_cmp[g][3],
                    precision=lax.Precision.HIGHEST) for g in groups]
    o_win = [_weighted_values(*e_win[g], vw) for g in groups]

    def head_part(o, g, hh):
        return o[hh * tq:(hh + 1) * tq, g * HEAD_DIM:(g + 1) * HEAD_DIM]

    def gate_col(h, br):
        return jnp.broadcast_to(gates[:, 3 * h + br:3 * h + br + 1], (tq, HEAD_DIM))

    heads = [(g, hh, g * HEADS_PER_KV + hh) for g in groups for hh in range(HEADS_PER_KV)]
    partial = [gate_col(h, 0) * head_part(o_cmp[g], g, hh) + gate_col(h, 2) * head_part(o_win[g], g, hh)
               for g, hh, h in heads]
    sel_gate = [gate_col(h, 1) for _, _, h in heads]

    first_blk = lax.shift_right_logical(q0, 6)
    picked_rows, before_rows = [], []
    for g in groups:
        imp = jnp.where(forced, BIG, imps[g])
        imp = jnp.where(future, -BIG, imp)
        rank = jnp.zeros((nblk, tq), jnp.int32)
        for i in range(nblk):
            row = imp[i:i + 1, :]
            beats = (row > imp) | ((row == imp) & (jidx > i))
            rank = rank + beats.astype(jnp.int32)
        picked_rows.append(jnp.where(rank < N_SEL, 0.0, MASK_BIAS))
        before_rows.append(jnp.where((rank < N_SEL) & (jidx < first_blk), 0.0, MASK_BIAS))
    assert 2 * NSA_KV_HEADS * SEL_BLOCKS_MAX == LANES
    bias_t = jnp.concatenate(before_rows + picked_rows, axis=0)
    bias_main = bias_t.T
    bias_diag = pltpu.roll(bias_main, LANES // 2, 1)
    lane_group = lax.shift_right_logical(lax.broadcasted_iota(jnp.int32, (tq, LANES), 1), 5)

    def with_bias(g, bias):
        own = jnp.where(lane_group == g, bias, 0.0).astype(BF16)
        return jnp.concatenate([qs[g], jnp.concatenate([own] * HEADS_PER_KV, axis=0)], axis=1)

    qb = [with_bias(g, bias_main) for g in groups]

    kdiag = ksel_ref[0, pl.ds(pl.multiple_of(q0, tq), tq), :]
    vdiag = vsel_ref[0, pl.ds(pl.multiple_of(q0, tq), tq), :]
    tri = (lax.broadcasted_iota(jnp.int32, (tq, tq), 1) <= lax.broadcasted_iota(jnp.int32, (tq, tq), 0))
    s_diag = [jnp.where(tri[None], _dot_nt(with_bias(g, bias_diag), kdiag)
                        .reshape(HEADS_PER_KV, tq, tq), NEG) for g in groups]
    n_prefix = q0 // SEL_PREFIX + 1
    for n in range(1, seq // SEL_PREFIX + 1):
        klen = n * SEL_PREFIX

        @pl.when(n_prefix == n)
        def _(klen=klen):
            vall = jnp.concatenate([vsel_ref[0, :klen, :], vdiag], axis=0)
            s = [jnp.concatenate(
                [_dot_nt(qb[g], ksel_ref[0, :klen, :]).reshape(HEADS_PER_KV, tq, klen), s_diag[g]],
                axis=-1) for g in groups]
            ew = [_softmax2(s[g]) for g in groups]
            for g in groups:
                osel_ref[g] = _weighted_values(*ew[g], vall)

    o_sel = [osel_ref[g] for g in groups]

    outs = [partial[h] + sel_gate[h] * head_part(o_sel[g], g, hh) for g, hh, h in heads]
    o_ref[0] = jnp.concatenate(outs, axis=1).astype(BF16)


def _nsa(proj3, kc, vc, mt, tq=128):
    B, S, _ = proj3.shape
    blk = lambda c: c // LANES
    seq_spec = lambda c: pl.BlockSpec((1, S, LANES), lambda b, i: (b, 0, blk(c)))
    ncp = kc.shape[1]
    return pl.pallas_call(
        functools.partial(_nsa_body, tq=tq, seq=S),
        grid=(B, S // tq),
        in_specs=[
            pl.BlockSpec((1, tq, NSA_HEADS * HEAD_DIM),
                         lambda b, i: (b, i, COL_QA // (NSA_HEADS * HEAD_DIM))),
            pl.BlockSpec((1, ncp, LANES), lambda b, i: (b, 0, 0)),
            pl.BlockSpec((1, ncp, LANES), lambda b, i: (b, 0, 0)),
            pl.BlockSpec((1, S, 2 * LANES), lambda b, i: (b, 0, COL_KSEL // (2 * LANES))),
            seq_spec(COL_VSEL), seq_spec(COL_KWIN), seq_spec(COL_VWIN),
            pl.BlockSpec((1, tq, LANES), lambda b, i: (b, i, blk(COL_GATE))),
            _const_spec(mt.shape),
        ],
        out_specs=pl.BlockSpec((1, tq, NSA_HEADS * HEAD_DIM), lambda b, i: (b, i, 0)),
        out_shape=jax.ShapeDtypeStruct((B, S, NSA_HEADS * HEAD_DIM), BF16),
        scratch_shapes=[pltpu.VMEM((NSA_KV_HEADS, HEADS_PER_KV * tq, LANES), F32)],
        compiler_params=_params(),
        name="nsa",
    )(proj3, kc, vc, proj3, proj3, proj3, proj3, proj3, mt)


DIL_SUB = 128
DIL_ROWS = 1024


def _dil_body(q_ref, kp_ref, kc_ref, vp_ref, vc_ref, o_ref, lse_ref, *, n_back):
    sub = DIL_SUB
    rb, tq = q_ref.shape[1], q_ref.shape[2]
    t0 = pl.program_id(2) * tq
    head_of = lax.shift_right_logical(lax.broadcasted_iota(jnp.int32, (sub, DIL_WIDTH), 1), 6)
    lane = lax.broadcasted_iota(jnp.int32, (sub, LANES), 1)
    diff = (sub + lax.broadcasted_iota(jnp.int32, (sub, 1), 0)
            - lax.broadcasted_iota(jnp.int32, (1, 2 * sub), 1))
    band = (diff >= 0) & (diff <= n_back)
    band0 = band & (lax.broadcasted_iota(jnp.int32, (1, 2 * sub), 1) + t0 >= sub)
    tiles = [(r, j) for r in range(rb) for j in range(tq // sub)]
    keys = {r: jnp.concatenate([kp_ref[0, r], kc_ref[0, r]], axis=0) for r in range(rb)}
    vals = {r: jnp.concatenate([vp_ref[0, r], vc_ref[0, r]], axis=0) for r in range(rb)}
    scores = []
    for r, j in tiles:
        q = q_ref[0, r, j * sub:(j + 1) * sub, :]
        qs = jnp.concatenate([jnp.where(head_of == h, q, jnp.zeros_like(q))
                              for h in range(DIL_HEADS)], axis=0)
        s = _dot_nt(qs, keys[r][j * sub:(j + 2) * sub]).reshape(DIL_HEADS, sub, 2 * sub)
        scores.append(jnp.where((band0 if j == 0 else band)[None], s, NEG))
    stats = []
    for s in scores:
        m = jnp.max(s, axis=-1, keepdims=True)
        e = jnp.exp2(s - m)
        stats.append((m, e, jnp.sum(e, axis=-1, keepdims=True)))
    for (r, j), (m, e, l) in zip(tiles, stats):
        o = _dot(e.astype(BF16).reshape(DIL_HEADS * sub, 2 * sub), vals[r][j * sub:(j + 2) * sub])
        o = o.reshape(DIL_HEADS, sub, DIL_WIDTH) / l
        lse = m * LN2 + jnp.log(l)
        o_acc = jnp.zeros((sub, DIL_WIDTH), F32)
        lse_out = jnp.zeros((sub, LANES), F32)
        for h in range(DIL_HEADS):
            o_acc = jnp.where(head_of == h, o[h], o_acc)
            lse_out = jnp.where(lane == h, lse[h], lse_out)
        o_ref[0, r, j * sub:(j + 1) * sub, :] = o_acc.astype(BF16)
        lse_ref[0, r, j * sub:(j + 1) * sub, :] = lse_out


def _dilated(arr, cols, window, dilation, name):
    B, d, L, _ = arr.shape
    n_back = window // dilation
    tq = min(L, DIL_ROWS)
    rb = DIL_ROWS // tq
    assert d == dilation and n_back <= DIL_SUB and L % tq == 0 and d % rb == 0
    per = tq // DIL_SUB
    qc, kc, vc = cols
    cur = lambda c: pl.BlockSpec((1, rb, tq, DIL_WIDTH), lambda b, r, i: (b, r, i, c))
    prev = lambda c: pl.BlockSpec((1, rb, DIL_SUB, DIL_WIDTH),
                                  lambda b, r, i: (b, r, jnp.maximum(i * per - 1, 0), c))
    return pl.pallas_call(
        functools.partial(_dil_body, n_back=n_back),
        grid=(B, dilation // rb, L // tq),
        in_specs=[cur(qc), prev(kc), cur(kc), prev(vc), cur(vc)],
        out_specs=[pl.BlockSpec((1, rb, tq, DIL_WIDTH), lambda b, r, i: (b, r, i, 0)),
                   pl.BlockSpec((1, rb, tq, LANES), lambda b, r, i: (b, r, i, 0))],
        out_shape=[jax.ShapeDtypeStruct((B, dilation, L, DIL_WIDTH), BF16),
                   jax.ShapeDtypeStruct((B, dilation, L, LANES), F32)],
        compiler_params=_params(),
        name=name,
    )(arr, arr, arr, arr, arr)


def _mixout_body(oa_ref, ob0_ref, ob1_ref, ob2_ref, l0_ref, l1_ref, l2_ref, mg_ref, x_ref,
                 pa_ref, pb_ref, wo_ref, out_ref, so_ref, sl_ref):
    tm = x_ref.shape[0]

    def interleaved(src_ref, st_ref):
        d = src_ref.shape[1]
        if d == 1:
            return src_ref[0, 0].astype(F32)
        slabs = src_ref.shape[3] // LANES
        for r in range(d):
            for k in range(slabs):
                st_ref[k, pl.ds(r, tm // d, stride=d), :] = (
                    src_ref[0, r, :, k * LANES:(k + 1) * LANES].astype(F32))
        return jnp.concatenate([st_ref[k] for k in range(slabs)], axis=1)

    lses = [interleaved(l, sl_ref) for l in (l0_ref, l1_ref, l2_ref)]
    mx = jnp.maximum(jnp.maximum(lses[0], lses[1]), lses[2])
    ws = [jnp.exp(l - mx) for l in lses]
    den = ws[0] + ws[1] + ws[2]
    ob = jnp.zeros((tm, DIL_WIDTH), F32)
    for w, o_ref in zip(ws, (ob0_ref, ob1_ref, ob2_ref)):
        alpha = w / den
        wide = jnp.concatenate(
            [jnp.broadcast_to(alpha[:, h:h + 1], (tm, HEAD_DIM)) for h in range(DIL_HEADS)], axis=1)
        ob = ob + wide * interleaved(o_ref, so_ref)
    ya = _dot(oa_ref[...], pa_ref[0])
    yb = _dot(ob.astype(BF16), pb_ref[0])
    gm = jax.nn.sigmoid(mg_ref[...].astype(F32))
    y = gm[:, :D_MODEL] * ya + gm[:, D_MODEL:] * yb
    out_ref[...] = x_ref[...] + _dot(y.astype(BF16), wo_ref[0])


def _mixout(oa, obs, lses, proj, x2, pa, pb, wo, layer, S, tm=1024):
    T = x2.shape[0]
    per_b = S // tm
    row = lambda w: pl.BlockSpec((tm, w), lambda i: (i, 0))
    folded = lambda a: pl.BlockSpec((1, a.shape[1], tm // a.shape[1], a.shape[3]),
                                    lambda i: (i // per_b, 0, i % per_b, 0))
    return pl.pallas_call(
        _mixout_body,
        grid=(T // tm,),
        in_specs=[row(NSA_HEADS * HEAD_DIM), *[folded(a) for a in obs], *[folded(a) for a in lses],
                  pl.BlockSpec((tm, 2 * D_MODEL), lambda i: (i, COL_MERGE // (2 * D_MODEL))),
                  row(D_MODEL),
                  _layer_spec(pa, layer), _layer_spec(pb, layer), _layer_spec(wo, layer)],
        out_specs=row(D_MODEL),
        out_shape=jax.ShapeDtypeStruct((T, D_MODEL), F32),
        scratch_shapes=[pltpu.VMEM((DIL_WIDTH // LANES, tm, LANES), F32),
                        pltpu.VMEM((1, tm, LANES), F32)],
        compiler_params=_params(),
        name="mixout",
    )(oa, *obs, *lses, proj, x2, pa, pb, wo)


FF_CHUNK = 512


def _ffn_body(x_ref, g_ref, w1_ref, w3_ref, w2_ref, *rest):
    n_cast = (len(rest) - 1) // 2
    src_refs, o_ref, dst_refs = rest[:n_cast], rest[n_cast], rest[n_cast + 1:]
    x = x_ref[...]
    h = _rms(x, g_ref[...]).astype(BF16)
    acc = jnp.zeros(x.shape, F32)
    for c0 in range(0, D_FF, FF_CHUNK):
        c1 = min(c0 + FF_CHUNK, D_FF)
        act = _silu(_dot(h, w1_ref[:, c0:c1])) * _dot(h, w3_ref[:, c0:c1])
        acc = acc + _dot(act.astype(BF16), w2_ref[c0:c1, :])
    o_ref[...] = x + acc
    for src, dst in zip(src_refs, dst_refs):
        dst[...] = src[...].astype(BF16)


def _ffn(x2, g, w1, w3, w2, to_bf16=(), tm=512):
    T = x2.shape[0]
    steps = T // tm
    held = lambda a: pl.BlockSpec(a.shape, lambda i: (0, 0), pipeline_mode=pl.Buffered(1))
    sliced = lambda a: pl.BlockSpec((a.shape[0] // steps, a.shape[1]), lambda i: (i, 0))
    assert all(a.shape[0] % (16 * steps) == 0 for a in to_bf16)
    out, *copies = pl.pallas_call(
        _ffn_body,
        grid=(steps,),
        in_specs=[pl.BlockSpec((tm, D_MODEL), lambda i: (i, 0)), _const_spec(g.shape),
                  held(w1), held(w3), held(w2), *[sliced(a) for a in to_bf16]],
        out_specs=[pl.BlockSpec((tm, D_MODEL), lambda i: (i, 0)), *[sliced(a) for a in to_bf16]],
        out_shape=[jax.ShapeDtypeStruct((T, D_MODEL), F32),
                   *[jax.ShapeDtypeStruct(a.shape, BF16) for a in to_bf16]],
        compiler_params=_params(),
        name="ffn",
    )(x2, g, w1, w3, w2, *to_bf16)
    return out, copies


MOE_TM = 1024
MOE_CHUNK = 128
MOE_BIG = 2 * MOE_CHUNK
MOE_MERGED = 4


def _router_body(x_ref, g_ref, r_ref, h_ref, combw_ref, posw_ref, post_ref, cnt_ref):
    tm = x_ref.shape[0]
    h = _rms(x_ref[...], g_ref[...])
    h_ref[...] = h.astype(BF16)
    r = r_ref[...]
    h_hi, r_hi = h.astype(BF16), r.astype(BF16)
    h_lo = (h - h_hi.astype(F32)).astype(BF16)
    r_lo = (r - r_hi.astype(F32)).astype(BF16)
    logits = _dot(h_hi, r_hi) + (_dot(h_hi, r_lo) + _dot(h_lo, r_hi))
    lane = lax.broadcasted_iota(jnp.int32, logits.shape, 1)
    lg = jnp.where(lane < N_EXPERTS, logits, NEG)
    m1 = jnp.max(lg, axis=-1, keepdims=True)
    i1 = jnp.min(jnp.where(lg == m1, lane, LANES), axis=-1, keepdims=True)
    lg2 = jnp.where(lane == i1, NEG, lg)
    m2 = jnp.max(lg2, axis=-1, keepdims=True)
    i2 = jnp.min(jnp.where(lg2 == m2, lane, LANES), axis=-1, keepdims=True)
    e2 = jnp.exp(m2 - m1)
    den = 1.0 + e2
    w_first, w_second = 1.0 / den, e2 / den
    chosen = [jnp.broadcast_to((i1 == ex) | (i2 == ex), (tm, LANES)) for ex in range(N_EXPERTS)]
    ones = jnp.concatenate([jnp.where(c, 1.0, 0.0).astype(BF16) for c in chosen], axis=1)
    row = lax.broadcasted_iota(jnp.int32, (LANES, LANES), 0)
    col = lax.broadcasted_iota(jnp.int32, (LANES, LANES), 1)
    tri = jnp.where(col < row, 1.0, 0.0).astype(BF16)
    running = jnp.zeros((1, N_EXPERTS * LANES), F32)
    parts = []
    for b in range(tm // LANES):
        blk = ones[b * LANES:(b + 1) * LANES]
        parts.append(_dot(tri, blk) + running)
        running = running + jnp.sum(blk.astype(F32), axis=0, keepdims=True)
    before = jnp.concatenate(parts, axis=0)
    by_lane = jnp.zeros((tm, LANES), F32)
    cnt = jnp.zeros((1, LANES), F32)
    for ex in range(N_EXPERTS):
        slab = slice(ex * LANES, (ex + 1) * LANES)
        combw_ref[:, slab] = jnp.broadcast_to(
            jnp.where(i1 == ex, w_first, 0.0) + jnp.where(i2 == ex, w_second, 0.0), (tm, LANES))
        pos = jnp.where(chosen[ex], before[:, slab], -1.0)
        posw_ref[:, slab] = pos
        by_lane = jnp.where(lane == ex, pos, by_lane)
        cnt = jnp.where(lane[:1] == ex, running[:, slab], cnt)
    post_ref[0] = by_lane.T[:N_EXPERTS]
    cnt_ref[0] = jnp.broadcast_to(cnt, (8, LANES)).astype(jnp.int32)


def _router(x2, g, r, tm=MOE_TM):
    T = x2.shape[0]
    nt = T // tm
    row = lambda w: pl.BlockSpec((tm, w), lambda i: (i, 0))
    return pl.pallas_call(
        _router_body,
        grid=(nt,),
        in_specs=[row(D_MODEL), _const_spec(g.shape), _const_spec(r.shape)],
        out_specs=[row(D_MODEL), row(N_EXPERTS * LANES), row(N_EXPERTS * LANES),
                   pl.BlockSpec((1, N_EXPERTS, tm), lambda i: (i, 0, 0)),
                   pl.BlockSpec((1, 8, LANES), lambda i: (i, 0, 0))],
        out_shape=[jax.ShapeDtypeStruct((T, D_MODEL), BF16),
                   jax.ShapeDtypeStruct((T, N_EXPERTS * LANES), F32),
                   jax.ShapeDtypeStruct((T, N_EXPERTS * LANES), F32),
                   jax.ShapeDtypeStruct((nt, N_EXPERTS, tm), F32),
                   jax.ShapeDtypeStruct((nt, 8, LANES), jnp.int32)],
        compiler_params=_params(),
        name="router",
    )(x2, g, r)


def _moe_body(cnt_ref, x_ref, h_ref, comb_ref, pos_ref, post_ref, w1_ref, w3_ref, w2_ref, fn_ref,
              o_ref, y_ref):
    i = pl.program_id(0)
    e = pl.program_id(1)
    merged = MOE_MERGED * LANES
    n = cnt_ref[i * N_EXPERTS + e]
    n_big = (n + (MOE_BIG - MOE_CHUNK - 1)) // MOE_BIG
    rest_row = pl.multiple_of(n_big * MOE_BIG, MOE_BIG)
    has_rest = n > rest_row
    post_e = post_ref[0, pl.ds(e, 1), :]

    @pl.when(e == 0)
    def _():
        o_ref[...] = x_ref[...]

    @pl.when((i == 0) & (e == 0))
    def _():
        y_ref[...] = jnp.zeros_like(y_ref)

    def scatter_add(first_row, first_slot, n_slabs):
        lane_slot = lax.broadcasted_iota(jnp.int32, (1, LANES), 1).astype(F32)
        hot = [jnp.where(pos_ref[...] == first_slot + (k * LANES) + lane_slot, 1.0, 0.0).astype(BF16)
               for k in range(n_slabs)]
        scatter = hot[0] if n_slabs == 1 else jnp.concatenate(hot, axis=1)
        weight = jnp.concatenate([comb_ref[...]] * (D_MODEL // LANES), axis=1)
        ys = y_ref[pl.ds(first_row, n_slabs * LANES), :].astype(BF16)
        o_ref[...] += weight * _dot(scatter, ys)

    def chunk(r0, size):
        base = r0.astype(F32)
        slot_col = base + lax.broadcasted_iota(jnp.int32, (size, 1), 0).astype(F32)
        gather = jnp.where(post_e == slot_col, 1.0, 0.0).astype(BF16)
        xe = _dot(gather, h_ref[...]).astype(BF16)
        act = _silu(_dot(xe, w1_ref[0])) * _dot(xe, w3_ref[0])
        y = _dot(act.astype(BF16), w2_ref[0])

        @pl.when(r0 < merged)
        def _():
            y_ref[pl.ds(r0, size), :] = y

        @pl.when(r0 >= merged)
        def _():
            y_ref[merged:merged + size, :] = y
            for k in range(size // LANES):
                scatter_add(merged + k * LANES, base + float(k * LANES), 1)

    def big_chunk(c, carry):
        chunk(pl.multiple_of(c * MOE_BIG, MOE_BIG), MOE_BIG)
        return carry

    lax.fori_loop(0, n_big, big_chunk, 0)

    @pl.when(has_rest)
    def _():
        chunk(rest_row, MOE_CHUNK)

    @pl.when(n > 0)
    def _():
        scatter_add(0, 0.0, MOE_MERGED)

    @pl.when(e == pl.num_programs(1) - 1)
    def _():
        o_ref[...] = _rms(o_ref[...], fn_ref[...])


def _moe(x2, h, comb, pos, post, counts, w1, w3, w2, fn, tm=MOE_TM):
    T = x2.shape[0]
    once = lambda w: pl.BlockSpec((tm, w), lambda i, e, cnt: (i, 0), pipeline_mode=pl.Buffered(1))
    tile = lambda w: pl.BlockSpec((tm, w), lambda i, e, cnt: (i, 0))
    assert MOE_CHUNK == LANES
    slab = pl.BlockSpec((tm, LANES), lambda i, e, cnt: (i, e))
    expert = lambda a: pl.BlockSpec((1,) + a.shape[1:], lambda i, e, cnt: (e, 0, 0))
    grid_spec = pltpu.PrefetchScalarGridSpec(
        num_scalar_prefetch=1,
        grid=(T // tm, N_EXPERTS),
        in_specs=[once(D_MODEL), once(D_MODEL), slab, slab,
                  pl.BlockSpec((1, N_EXPERTS, tm), lambda i, e, cnt: (i, 0, 0)),
                  expert(w1), expert(w3), expert(w2),
                  pl.BlockSpec(fn.shape, lambda i, e, cnt: (0, 0))],
        out_specs=tile(D_MODEL),
        scratch_shapes=[pltpu.VMEM((MOE_MERGED * LANES + MOE_BIG, D_MODEL), F32)],
    )
    return pl.pallas_call(
        _moe_body,
        grid_spec=grid_spec,
        out_shape=jax.ShapeDtypeStruct((T, D_MODEL), F32),
        compiler_params=_params(),
        name="moe",
    )(counts, x2, h, comb, pos, post, w1, w3, w2, fn)


def _rope_lane_tables(positions):
    half = ROT_DIM // 2
    inv = ROPE_THETA ** (-jnp.arange(0, ROT_DIM, 2, dtype=F32) / ROT_DIM)
    ang = positions.astype(F32).reshape(-1, 1) * inv
    cos, sin = jnp.cos(ang), jnp.sin(ang)
    within = np.arange(LANES) % HEAD_DIM
    cos_l, sin_l = jnp.tile(cos, (1, LANES // half)), jnp.tile(sin, (1, LANES // half))
    c = jnp.where(within < ROT_DIM, cos_l, 1.0)
    sa = jnp.where(within < half, -sin_l, 0.0)
    sb = jnp.where((within >= half) & (within < ROT_DIM), sin_l, 0.0)
    return c, sa, sb


def _w_in_plan():
    scale = HEAD_DIM ** -0.5 * LOG2E
    qa, kv, gate, qkv_b, merge = 0, 512, 1280, 1304, 3608
    nb = N_DIL_GROUPS * DIL_WIDTH
    kv_piece = lambda j: [(kv + LANES * j, LANES, 1.0)]
    span = lambda start, width, s=1.0: [(start + c, LANES, s) for c in range(0, width, LANES)]
    plan = (span(qa, 512, scale)
            + kv_piece(2) + kv_piece(4) + kv_piece(0)
            + kv_piece(3) + kv_piece(5)
            + [(gate, 3 * NSA_HEADS, 1.0)]
            + kv_piece(1)
            + span(merge, 2 * D_MODEL)
            + span(qkv_b, nb, scale) + span(qkv_b + nb, nb) + span(qkv_b + 2 * nb, nb))
    assert len(plan) * LANES == N_W_IN
    return plan


def _importance_matrix_t(seq, ncp):
    n_c = (seq - CMP_LEN) // CMP_STRIDE + 1
    starts = np.arange(n_c) * CMP_STRIDE
    bstart = np.arange(seq // SEL_LEN) * SEL_LEN
    overlap = np.clip(np.minimum(starts[:, None] + CMP_LEN, bstart[None, :] + SEL_LEN)
                      - np.maximum(starts[:, None], bstart[None, :]), 0, None)
    m = np.zeros((ncp, seq // SEL_LEN), np.float32)
    m[:n_c] = overlap.astype(np.float32) / CMP_LEN
    return jnp.asarray(m.T)


def _mixer(x2, layer, B, S, tables, mt, prm):
    proj, *folds, a = _inproj(x2, prm["norm_mix"], prm["w_in"], layer, *tables, B, S)
    proj3 = proj.reshape(B, S, N_PROJ)
    ncp = S // CMP_STRIDE
    cmp = _compress(a.reshape(2, B * NSA_KV_HEADS * ncp, CMP_STRIDE * HEAD_DIM),
                    prm["cmp_w1"], prm["cmp_w2"], prm["cmp_pos"], layer, B)
    oa = _nsa(proj3, cmp[0], cmp[1], mt).reshape(B * S, NSA_HEADS * HEAD_DIM)
    obs, lses = [], []
    for g, ((w, d), arr) in enumerate(zip(DIL_PATTERNS, folds)):
        o, lse = _dilated(arr, (0, 1, 2), w, d, f"dilated{g}")
        obs.append(o)
        lses.append(lse)
    return _mixout(oa, obs, lses, proj, x2, prm["p_a"], prm["p_b"], prm["w_o"], layer, S)


def kernel(x, positions, norm_mix, w_in, cmp_pos_k, cmp_pos_v, cmp_k_w1, cmp_k_w2, cmp_v_w1,
           cmp_v_w2, w_branch_a, w_branch_b, w_out, norm_ffn, ffn_w1, ffn_w3, ffn_w2, router,
           moe_w1, moe_w3, moe_w2, final_norm):
    B, S, D = x.shape
    depth = norm_mix.shape[0]
    assert depth == 2 and D == D_MODEL
    tables = _rope_lane_tables(positions)
    mt = _importance_matrix_t(S, S // CMP_STRIDE)
    cmp_pos = jnp.stack([cmp_pos_k, cmp_pos_v], axis=1).reshape(depth, 2, 1, CMP_LEN * HEAD_DIM)
    prm = {
        "norm_mix": norm_mix.reshape(depth, 1, D),
        "w_in": jnp.swapaxes(w_in, 1, 2),
        "cmp_w1": jnp.stack([cmp_k_w1, cmp_v_w1], axis=1).astype(BF16),
        "cmp_w2": jnp.stack([cmp_k_w2, cmp_v_w2], axis=1).astype(BF16),
        "cmp_pos": jnp.broadcast_to(cmp_pos, (depth, 2, 8, CMP_LEN * HEAD_DIM)).astype(BF16),
        "p_a": w_branch_a.astype(BF16), "p_b": w_branch_b.astype(BF16), "w_o": w_out.astype(BF16),
    }
    x2 = x.reshape(B * S, D)
    x2 = _mixer(x2, 0, B, S, tables, mt, prm)
    experts = [w[0].reshape(-1, w.shape[-1]) for w in (moe_w1, moe_w3, moe_w2)]
    x2, experts = _ffn(x2, norm_ffn[0].reshape(1, -1), ffn_w1[0].astype(BF16),
                       ffn_w3[0].astype(BF16), ffn_w2[0].astype(BF16), to_bf16=experts)
    ew1, ew3, ew2 = [w.reshape(m.shape[1:]) for w, m in zip(experts, (moe_w1, moe_w3, moe_w2))]
    x2 = _mixer(x2, 1, B, S, tables, mt, prm)
    g1 = norm_ffn[1].reshape(1, -1)
    r = jnp.pad(router[0], ((0, 0), (0, LANES - N_EXPERTS)))
    h, comb, pos, post, cnt = _router(x2, g1, r)
    counts = cnt[:, 0, :N_EXPERTS].reshape(-1)
    out = _moe(x2, h, comb, pos, post, counts, ew1, ew3, ew2, final_norm.reshape(1, -1))
    return out.reshape(B, S, D)
```

```python
import functools

import numpy as np
import jax
import jax.numpy as jnp
from jax import lax
from jax.experimental import pallas as pl
from jax.experimental.pallas import tpu as pltpu

F32 = jnp.float32
BF16 = jnp.bfloat16

D_MODEL = 1024
HEAD_DIM = 64
ROT_DIM = HEAD_DIM // 4
ROPE_THETA = 500000.0
EPS = 1e-6
NSA_HEADS = 8
NSA_KV_HEADS = 2
HEADS_PER_KV = NSA_HEADS // NSA_KV_HEADS
CMP_LEN = 32
CMP_STRIDE = 16
SEL_LEN = 64
N_SEL = 16
WIN = 512
DIL_PATTERNS = ((128, 1), (512, 4), (2048, 16))
N_DIL_GROUPS = 3
DIL_HEADS = 4
D_FF = 2816
N_EXPERTS = 8

LANES = 128
VMEM_LIMIT = 56 * 1024 * 1024
NEG = -1e30
BIG = 1e30

COL_MERGE = 0
COL_QA = 2048
COL_KSEL = 2560
COL_BLK = COL_KSEL + 128
COL_KWIN = 2816
COL_VSEL = 2944
COL_VWIN = 3072
COL_GATE = 3200
N_PROJ = 3328
DIL_WIDTH = DIL_HEADS * HEAD_DIM
N_FOLD = N_DIL_GROUPS * DIL_WIDTH
STAGE = None
IN_CHUNKS = (
    (512, True, tuple(COL_QA + 128 * j for j in range(4)), None),
    (384, True, (COL_KSEL, COL_KWIN, STAGE), ("cmp", 0)),
    (512, False, (COL_VSEL, COL_VWIN, COL_GATE, STAGE), ("cmp", 1)),
    (512, False, tuple(COL_MERGE + 128 * j for j in range(0, 4)), None),
    (512, False, tuple(COL_MERGE + 128 * j for j in range(4, 8)), None),
    (512, False, tuple(COL_MERGE + 128 * j for j in range(8, 12)), None),
    (512, False, tuple(COL_MERGE + 128 * j for j in range(12, 16)), None),
    (N_FOLD, True, (STAGE,) * 6, ("fold", 0)),
    (N_FOLD, True, (STAGE,) * 6, ("fold", 1)),
    (N_FOLD, False, (STAGE,) * 6, ("fold", 2)),
)
N_W_IN = sum(c[0] for c in IN_CHUNKS)
LOG2E = 1.4426950408889634
LN2 = 0.6931471805599453
MASK_BIAS = -(2.0 ** 100)
SEL_BLOCKS_MAX = 32


def _dot(a, b, precision=None):
    return jnp.dot(a, b, preferred_element_type=F32, precision=precision)


def _dot_nt(a, b, precision=None):
    return lax.dot_general(a, b, (((1,), (1,)), ((), ())), preferred_element_type=F32,
                           precision=precision)


def _rms(x, g):
    ms = jnp.mean(x * x, axis=-1, keepdims=True)
    return x * lax.rsqrt(ms + EPS) * g


def _silu(x):
    return x * jax.nn.sigmoid(x)


def _params(**kw):
    return pltpu.CompilerParams(vmem_limit_bytes=VMEM_LIMIT, **kw)


def _const_spec(shape):
    nd = len(shape)
    return pl.BlockSpec(shape, lambda *_: (0,) * nd)


def _inproj_body(x_ref, g_ref, wt_ref, c_ref, sa_ref, sb_ref, o_ref, f0_ref, f1_ref, f2_ref, a_ref,
                 st_ref, w_ref, *, per_b):
    tm = x_ref.shape[0]

    @pl.when(pl.program_id(0) == 0)
    def _():
        for j, (src, width, scale) in enumerate(_w_in_plan()):
            rows = wt_ref[0, src:src + LANES, :]
            w_ref[j * LANES:(j + 1) * LANES, :] = (rows * scale if scale != 1.0 else rows).astype(BF16)

    t_seq = (pl.program_id(0) % per_b) * tm + lax.broadcasted_iota(jnp.int32, (tm, LANES), 0)
    lane = lax.broadcasted_iota(jnp.int32, (tm, LANES), 1)
    blk = lax.shift_right_logical(t_seq, 6)
    hot = (lane == blk) | (lane == blk + SEL_BLOCKS_MAX)
    o_ref[:, COL_BLK:COL_BLK + LANES] = jnp.where(hot, 1.0, 0.0).astype(BF16)
    h = _rms(x_ref[...], g_ref[0]).astype(BF16)
    c = c_ref[...]
    sa = sa_ref[...]
    sb = sb_ref[...]
    start = 0
    for size, rope, dests, action in IN_CHUNKS:
        acc = _dot_nt(h, w_ref[start:start + size, :])
        start += size
        for j, dest in enumerate(dests):
            a = acc[:, j * LANES:(j + 1) * LANES]
            if rope:
                a = a * c + pltpu.roll(a, LANES - 8, 1) * sa + pltpu.roll(a, 8, 1) * sb
            if dest is STAGE:
                st_ref[j] = a
            else:
                o_ref[:, dest:dest + LANES] = a.astype(BF16)
        if action is None:
            continue
        kind, piece = action
        if kind == "fold":
            slabs = DIL_WIDTH // LANES
            for gi, f_ref in enumerate((f0_ref, f1_ref, f2_ref)):
                d = DIL_PATTERNS[gi][1]
                for r in range(d):
                    for k in range(slabs):
                        rows = st_ref[gi * slabs + k, pl.ds(r, tm // d, stride=d), :]
                        c0 = piece * DIL_WIDTH + k * LANES
                        f_ref[0, r, :, c0:c0 + LANES] = rows.astype(BF16)
        else:
            slab = dests.index(STAGE)
            nrow = tm // CMP_STRIDE
            toks = [st_ref[slab, pl.ds(j, nrow, stride=CMP_STRIDE), :] for j in range(CMP_STRIDE)]
            for g in range(NSA_KV_HEADS):
                head = slice(g * HEAD_DIM, (g + 1) * HEAD_DIM)
                for m in range(CMP_STRIDE // 2):
                    pair = jnp.concatenate([toks[2 * m][:, head], toks[2 * m + 1][:, head]], axis=1)
                    a_ref[piece, 0, g, :, m * LANES:(m + 1) * LANES] = pair.astype(BF16)


def _layer_spec(arr, layer):
    nd = arr.ndim
    return pl.BlockSpec((1,) + arr.shape[1:], lambda *_: (layer,) + (0,) * (nd - 1))


def _inproj(x2, g, w, layer, rc, rsa, rsb, B, S, tm=512):
    T = x2.shape[0]
    per_b = S // tm
    dils = [d for _, d in DIL_PATTERNS]
    fold_spec = lambda d: pl.BlockSpec((1, d, tm // d, 3 * DIL_WIDTH),
                                       lambda i: (i // per_b, 0, i % per_b, 0))
    fold_shape = lambda d: jax.ShapeDtypeStruct((B, d, S // d, 3 * DIL_WIDTH), BF16)
    cmp_w = CMP_STRIDE * HEAD_DIM
    assert S // SEL_LEN <= SEL_BLOCKS_MAX
    return pl.pallas_call(
        functools.partial(_inproj_body, per_b=per_b),
        grid=(T // tm,),
        in_specs=[
            pl.BlockSpec((tm, D_MODEL), lambda i: (i, 0)),
            _layer_spec(g, layer), _layer_spec(w, layer),
            pl.BlockSpec((tm, LANES), lambda i: (i, 0)),
            pl.BlockSpec((tm, LANES), lambda i: (i, 0)),
            pl.BlockSpec((tm, LANES), lambda i: (i, 0)),
        ],
        out_specs=[pl.BlockSpec((tm, N_PROJ), lambda i: (i, 0)), *[fold_spec(d) for d in dils],
                   pl.BlockSpec((2, 1, NSA_KV_HEADS, tm // CMP_STRIDE, cmp_w),
                                lambda i: (0, i // per_b, 0, i % per_b, 0))],
        out_shape=[jax.ShapeDtypeStruct((T, N_PROJ), BF16), *[fold_shape(d) for d in dils],
                   jax.ShapeDtypeStruct((2, B, NSA_KV_HEADS, S // CMP_STRIDE, cmp_w), BF16)],
        scratch_shapes=[pltpu.VMEM((N_FOLD // LANES, tm, LANES), F32),
                        pltpu.VMEM((N_W_IN, D_MODEL), BF16)],
        compiler_params=_params(),
        name="inproj",
    )(x2, g, w, rc, rsa, rsb)


def _compress_body(a_ref, w1_ref, w2_ref, pos_ref, o_ref):
    nb, ncp = o_ref.shape[1], o_ref.shape[2]
    a = a_ref[0]
    w1 = w1_ref[0, 0]
    half = CMP_STRIDE * HEAD_DIM
    top = _dot(a, w1[:half])
    bot = _dot(a, w1[half:])
    pc = _dot(pos_ref[0, 0], w1)
    rows = a.shape[0]
    hid = top + pltpu.roll(bot, rows - 1, 0) + pc[0:1]
    out = _dot(_silu(hid).astype(BF16), w2_ref[0, 0])
    for b in range(nb):
        heads = [out[(b * NSA_KV_HEADS + g) * ncp:(b * NSA_KV_HEADS + g + 1) * ncp]
                 for g in range(NSA_KV_HEADS)]
        o_ref[0, b] = jnp.concatenate(heads, axis=1).astype(BF16)


def _compress(a, w1, w2, pos, layer, B):
    n, rows, _ = a.shape
    ncp = rows // (B * NSA_KV_HEADS)
    per_kv = lambda arr: pl.BlockSpec((1, 1) + arr.shape[2:], lambda i: (layer, i, 0, 0))
    return pl.pallas_call(
        _compress_body,
        grid=(n,),
        in_specs=[pl.BlockSpec((1, rows, CMP_STRIDE * HEAD_DIM), lambda i: (i, 0, 0)),
                  per_kv(w1), per_kv(w2), per_kv(pos)],
        out_specs=pl.BlockSpec((1, B, ncp, NSA_KV_HEADS * HEAD_DIM), lambda i: (i, 0, 0, 0)),
        out_shape=jax.ShapeDtypeStruct((n, B, ncp, NSA_KV_HEADS * HEAD_DIM), BF16),
        compiler_params=_params(),
        name="compress",
    )(a, w1, w2, pos)


def _softmax2(s):
    m = jnp.max(s, axis=-1, keepdims=True)
    e = jnp.exp2(s - m)
    return e.astype(BF16), jnp.sum(e, axis=-1, keepdims=True)


def _weighted_values(e, l, v):
    nh, tq, nk = e.shape
    return _dot(e.reshape(nh * tq, nk), v) / l.reshape(nh * tq, 1)


SEL_PREFIX = 256


def _nsa_body(q_ref, kc_ref, vc_ref, ksel_ref, vsel_ref, kwin_ref, vwin_ref, gate_ref, mt_ref,
              o_ref, osel_ref, *, tq, seq):
    nblk = seq // SEL_LEN
    ncp = kc_ref.shape[1]
    n_cmp = (seq - CMP_LEN) // CMP_STRIDE + 1
    q0 = pl.program_id(1) * tq
    q = q_ref[0]
    t_col = q0 + lax.broadcasted_iota(jnp.int32, (tq, 1), 0)
    t_row = q0 + lax.broadcasted_iota(jnp.int32, (1, tq), 1)
    gates = jax.nn.sigmoid(gate_ref[0].astype(F32))

    zeros64 = jnp.zeros((tq, HEAD_DIM), BF16)

    def stacked_q(g):
        parts = []
        for hh in range(HEADS_PER_KV):
            h = g * HEADS_PER_KV + hh
            qh = q[:, h * HEAD_DIM:(h + 1) * HEAD_DIM]
            parts.append(jnp.concatenate([qh, zeros64] if g == 0 else [zeros64, qh], axis=1))
        return jnp.concatenate(parts, axis=0)

    qs = [stacked_q(g) for g in range(NSA_KV_HEADS)]

    cidx = lax.broadcasted_iota(jnp.int32, (tq, ncp), 1)
    cmask = ((cidx * CMP_STRIDE + (CMP_LEN - 1)) <= t_col) & (cidx < n_cmp)
    jidx = lax.broadcasted_iota(jnp.int32, (nblk, tq), 0)
    cur = lax.shift_right_logical(t_row, 6)
    forced = (jidx == 0) | (jidx == cur) | (jidx == cur - 1)
    future = jidx > cur
    groups = range(NSA_KV_HEADS)
    span = WIN + tq
    ks = pl.multiple_of(jnp.maximum(q0 - WIN, 0), tq)
    kw = kwin_ref[0, pl.ds(ks, span), :]
    vw = vwin_ref[0, pl.ds(ks, span), :]
    wpos = ks + lax.broadcasted_iota(jnp.int32, (1, span), 1)
    wmask = (wpos <= t_col) & (t_col - wpos <= WIN - 1)

    s_cmp = [jnp.where(cmask[None], _dot_nt(qs[g], kc_ref[0]).reshape(HEADS_PER_KV, tq, ncp), NEG)
             for g in groups]
    s_win = [jnp.where(wmask[None], _dot_nt(qs[g], kw).reshape(HEADS_PER_KV, tq, span), NEG)
             for g in groups]
    p_cmp = []
    for g in groups:
        m = jnp.max(s_cmp[g], axis=-1, keepdims=True)
        e = jnp.where(cmask[None], jnp.exp2(s_cmp[g] - m), 0.0)
        den = jnp.sum(e, axis=-1, keepdims=True)
        p_cmp.append(e / jnp.where(den > 0, den, 1.0))
    e_win = [_softmax2(s_win[g]) for g in groups]
    o_cmp = [_dot(p_cmp[g].astype(BF16).reshape(HEADS_PER_KV * tq, ncp), vc_ref[0]) for g in groups]
    imps = [_dot_nt(mt_ref[...], p_cmp[g][0] + p_cmp[g][1] + p_cmp[g][2] + p_cmp[g][3],
                    precision=lax.Precision.HIGHEST) for g in groups]
    o_win = [_weighted_values(*e_win[g], vw) for g in groups]

    def head_part(o, g, hh):
        return o[hh * tq:(hh + 1) * tq, g * HEAD_DIM:(g + 1) * HEAD_DIM]

    def gate_col(h, br):
        return jnp.broadcast_to(gates[:, 3 * h + br:3 * h + br + 1], (tq, HEAD_DIM))

    heads = [(g, hh, g * HEADS_PER_KV + hh) for g in groups for hh in range(HEADS_PER_KV)]
    partial = [gate_col(h, 0) * head_part(o_cmp[g], g, hh) + gate_col(h, 2) * head_part(o_win[g], g, hh)
               for g, hh, h in heads]
    sel_gate = [gate_col(h, 1) for _, _, h in heads]

    first_blk = lax.shift_right_logical(q0, 6)
    picked_rows, before_rows = [], []
    for g in groups:
        imp = jnp.where(forced, BIG, imps[g])
        imp = jnp.where(future, -BIG, imp)
        rank = jnp.zeros((nblk, tq), jnp.int32)
        for i in range(nblk):
            row = imp[i:i + 1, :]
            beats = (row > imp) | ((row == imp) & (jidx > i))
            rank = rank + beats.astype(jnp.int32)
        picked_rows.append(jnp.where(rank < N_SEL, 0.0, MASK_BIAS))
        before_rows.append(jnp.where((rank < N_SEL) & (jidx < first_blk), 0.0, MASK_BIAS))
    assert 2 * NSA_KV_HEADS * SEL_BLOCKS_MAX == LANES
    bias_t = jnp.concatenate(before_rows + picked_rows, axis=0)
    bias_main = bias_t.T
    bias_diag = pltpu.roll(bias_main, LANES // 2, 1)
    lane_group = lax.shift_right_logical(lax.broadcasted_iota(jnp.int32, (tq, LANES), 1), 5)

    def with_bias(g, bias):
        own = jnp.where(lane_group == g, bias, 0.0).astype(BF16)
        return jnp.concatenate([qs[g], jnp.concatenate([own] * HEADS_PER_KV, axis=0)], axis=1)

    qb = [with_bias(g, bias_main) for g in groups]

    kdiag = ksel_ref[0, pl.ds(pl.multiple_of(q0, tq), tq), :]
    vdiag = vsel_ref[0, pl.ds(pl.multiple_of(q0, tq), tq), :]
    tri = (lax.broadcasted_iota(jnp.int32, (tq, tq), 1) <= lax.broadcasted_iota(jnp.int32, (tq, tq), 0))
    s_diag = [jnp.where(tri[None], _dot_nt(with_bias(g, bias_diag), kdiag)
                        .reshape(HEADS_PER_KV, tq, tq), NEG) for g in groups]
    n_prefix = q0 // SEL_PREFIX + 1
    for n in range(1, seq // SEL_PREFIX + 1):
        klen = n * SEL_PREFIX

        @pl.when(n_prefix == n)
        def _(klen=klen):
            vall = jnp.concatenate([vsel_ref[0, :klen, :], vdiag], axis=0)
            s = [jnp.concatenate(
                [_dot_nt(qb[g], ksel_ref[0, :klen, :]).reshape(HEADS_PER_KV, tq, klen), s_diag[g]],
                axis=-1) for g in groups]
            vlane = lax.shift_right_logical(
                lax.broadcasted_iota(jnp.int32, (klen + tq, LANES), 1), 6)
            for g in groups:
                m = jnp.max(s[g], axis=-1, keepdims=True)
                e = jnp.exp2(s[g] - m).astype(BF16).reshape(HEADS_PER_KV * tq, klen + tq)
                o = _dot(e, jnp.where(vlane == g, vall, jnp.ones_like(vall)))
                other = (1 - g) * HEAD_DIM
                osel_ref[g] = o / o[:, other:other + 1]

    o_sel = [osel_ref[g] for g in groups]

    outs = [partial[h] + sel_gate[h] * head_part(o_sel[g], g, hh) for g, hh, h in heads]
    o_ref[0] = jnp.concatenate(outs, axis=1).astype(BF16)


def _nsa(proj3, kc, vc, mt, tq=128):
    B, S, _ = proj3.shape
    blk = lambda c: c // LANES
    seq_spec = lambda c: pl.BlockSpec((1, S, LANES), lambda b, i: (b, 0, blk(c)))
    ncp = kc.shape[1]
    return pl.pallas_call(
        functools.partial(_nsa_body, tq=tq, seq=S),
        grid=(B, S // tq),
        in_specs=[
            pl.BlockSpec((1, tq, NSA_HEADS * HEAD_DIM),
                         lambda b, i: (b, i, COL_QA // (NSA_HEADS * HEAD_DIM))),
            pl.BlockSpec((1, ncp, LANES), lambda b, i: (b, 0, 0)),
            pl.BlockSpec((1, ncp, LANES), lambda b, i: (b, 0, 0)),
            pl.BlockSpec((1, S, 2 * LANES), lambda b, i: (b, 0, COL_KSEL // (2 * LANES))),
            seq_spec(COL_VSEL), seq_spec(COL_KWIN), seq_spec(COL_VWIN),
            pl.BlockSpec((1, tq, LANES), lambda b, i: (b, i, blk(COL_GATE))),
            _const_spec(mt.shape),
        ],
        out_specs=pl.BlockSpec((1, tq, NSA_HEADS * HEAD_DIM), lambda b, i: (b, i, 0)),
        out_shape=jax.ShapeDtypeStruct((B, S, NSA_HEADS * HEAD_DIM), BF16),
        scratch_shapes=[pltpu.VMEM((NSA_KV_HEADS, HEADS_PER_KV * tq, LANES), F32)],
        compiler_params=_params(),
        name="nsa",
    )(proj3, kc, vc, proj3, proj3, proj3, proj3, proj3, mt)


DIL_SUB = 128
DIL_ROWS = 1024


def _dil_body(q_ref, kp_ref, kc_ref, vp_ref, vc_ref, o_ref, lse_ref, *, n_back):
    sub = DIL_SUB
    rb, tq = q_ref.shape[1], q_ref.shape[2]
    t0 = pl.program_id(2) * tq
    head_of = lax.shift_right_logical(lax.broadcasted_iota(jnp.int32, (sub, DIL_WIDTH), 1), 6)
    lane = lax.broadcasted_iota(jnp.int32, (sub, LANES), 1)
    diff = (sub + lax.broadcasted_iota(jnp.int32, (sub, 1), 0)
            - lax.broadcasted_iota(jnp.int32, (1, 2 * sub), 1))
    band = (diff >= 0) & (diff <= n_back)
    band0 = band & (lax.broadcasted_iota(jnp.int32, (1, 2 * sub), 1) + t0 >= sub)
    tiles = [(r, j) for r in range(rb) for j in range(tq // sub)]
    keys = {r: jnp.concatenate([kp_ref[0, r], kc_ref[0, r]], axis=0) for r in range(rb)}
    vals = {r: jnp.concatenate([vp_ref[0, r], vc_ref[0, r]], axis=0) for r in range(rb)}
    scores = []
    for r, j in tiles:
        q = q_ref[0, r, j * sub:(j + 1) * sub, :]
        qs = jnp.concatenate([jnp.where(head_of == h, q, jnp.zeros_like(q))
                              for h in range(DIL_HEADS)], axis=0)
        s = _dot_nt(qs, keys[r][j * sub:(j + 2) * sub]).reshape(DIL_HEADS, sub, 2 * sub)
        scores.append(jnp.where((band0 if j == 0 else band)[None], s, NEG))
    stats = []
    for s in scores:
        m = jnp.max(s, axis=-1, keepdims=True)
        e = jnp.exp2(s - m)
        stats.append((m, e, jnp.sum(e, axis=-1, keepdims=True)))
    for (r, j), (m, e, l) in zip(tiles, stats):
        o = _dot(e.astype(BF16).reshape(DIL_HEADS * sub, 2 * sub), vals[r][j * sub:(j + 2) * sub])
        o = o.reshape(DIL_HEADS, sub, DIL_WIDTH) / l
        lse = m * LN2 + jnp.log(l)
        o_acc = jnp.zeros((sub, DIL_WIDTH), F32)
        lse_out = jnp.zeros((sub, LANES), F32)
        for h in range(DIL_HEADS):
            o_acc = jnp.where(head_of == h, o[h], o_acc)
            lse_out = jnp.where(lane == h, lse[h], lse_out)
        o_ref[0, r, j * sub:(j + 1) * sub, :] = o_acc
        lse_ref[0, r, j * sub:(j + 1) * sub, :] = lse_out


def _dilated(arr, cols, window, dilation, name):
    B, d, L, _ = arr.shape
    n_back = window // dilation
    tq = min(L, DIL_ROWS)
    rb = DIL_ROWS // tq
    assert d == dilation and n_back <= DIL_SUB and L % tq == 0 and d % rb == 0
    per = tq // DIL_SUB
    qc, kc, vc = cols
    cur = lambda c: pl.BlockSpec((1, rb, tq, DIL_WIDTH), lambda b, r, i: (b, r, i, c))
    prev = lambda c: pl.BlockSpec((1, rb, DIL_SUB, DIL_WIDTH),
                                  lambda b, r, i: (b, r, jnp.maximum(i * per - 1, 0), c))
    return pl.pallas_call(
        functools.partial(_dil_body, n_back=n_back),
        grid=(B, dilation // rb, L // tq),
        in_specs=[cur(qc), prev(kc), cur(kc), prev(vc), cur(vc)],
        out_specs=[pl.BlockSpec((1, rb, tq, DIL_WIDTH), lambda b, r, i: (b, r, i, 0)),
                   pl.BlockSpec((1, rb, tq, LANES), lambda b, r, i: (b, r, i, 0))],
        out_shape=[jax.ShapeDtypeStruct((B, dilation, L, DIL_WIDTH), F32),
                   jax.ShapeDtypeStruct((B, dilation, L, LANES), F32)],
        compiler_params=_params(),
        name=name,
    )(arr, arr, arr, arr, arr)


def _mixout_body(oa_ref, ob0_ref, ob1_ref, ob2_ref, l0_ref, l1_ref, l2_ref, mg_ref, x_ref,
                 pa_ref, pb_ref, wo_ref, out_ref, so_ref, sl_ref):
    tm = x_ref.shape[0]

    def interleaved(src_ref, st_ref):
        d = src_ref.shape[1]
        if d == 1:
            return src_ref[0, 0]
        slabs = src_ref.shape[3] // LANES
        for r in range(d):
            for k in range(slabs):
                st_ref[k, pl.ds(r, tm // d, stride=d), :] = src_ref[0, r, :, k * LANES:(k + 1) * LANES]
        return jnp.concatenate([st_ref[k] for k in range(slabs)], axis=1)

    lses = [interleaved(l, sl_ref) for l in (l0_ref, l1_ref, l2_ref)]
    mx = jnp.maximum(jnp.maximum(lses[0], lses[1]), lses[2])
    ws = [jnp.exp(l - mx) for l in lses]
    den = ws[0] + ws[1] + ws[2]
    ob = jnp.zeros((tm, DIL_WIDTH), F32)
    for w, o_ref in zip(ws, (ob0_ref, ob1_ref, ob2_ref)):
        alpha = w / den
        wide = jnp.concatenate(
            [jnp.broadcast_to(alpha[:, h:h + 1], (tm, HEAD_DIM)) for h in range(DIL_HEADS)], axis=1)
        ob = ob + wide * interleaved(o_ref, so_ref)
    ya = _dot(oa_ref[...], pa_ref[0])
    yb = _dot(ob.astype(BF16), pb_ref[0])
    gm = jax.nn.sigmoid(mg_ref[...].astype(F32))
    y = gm[:, :D_MODEL] * ya + gm[:, D_MODEL:] * yb
    out_ref[...] = x_ref[...] + _dot(y.astype(BF16), wo_ref[0])


def _mixout(oa, obs, lses, proj, x2, pa, pb, wo, layer, S, tm=1024):
    T = x2.shape[0]
    per_b = S // tm
    row = lambda w: pl.BlockSpec((tm, w), lambda i: (i, 0))
    folded = lambda a: pl.BlockSpec((1, a.shape[1], tm // a.shape[1], a.shape[3]),
                                    lambda i: (i // per_b, 0, i % per_b, 0))
    return pl.pallas_call(
        _mixout_body,
        grid=(T // tm,),
        in_specs=[row(NSA_HEADS * HEAD_DIM), *[folded(a) for a in obs], *[folded(a) for a in lses],
                  pl.BlockSpec((tm, 2 * D_MODEL), lambda i: (i, COL_MERGE // (2 * D_MODEL))),
                  row(D_MODEL),
                  _layer_spec(pa, layer), _layer_spec(pb, layer), _layer_spec(wo, layer)],
        out_specs=row(D_MODEL),
        out_shape=jax.ShapeDtypeStruct((T, D_MODEL), F32),
        scratch_shapes=[pltpu.VMEM((DIL_WIDTH // LANES, tm, LANES), F32),
                        pltpu.VMEM((1, tm, LANES), F32)],
        compiler_params=_params(),
        name="mixout",
    )(oa, *obs, *lses, proj, x2, pa, pb, wo)


FF_CHUNK = 512


def _ffn_body(x_ref, g_ref, w1_ref, w3_ref, w2_ref, *rest):
    n_cast = (len(rest) - 1) // 2
    src_refs, o_ref, dst_refs = rest[:n_cast], rest[n_cast], rest[n_cast + 1:]
    x = x_ref[...]
    h = _rms(x, g_ref[...]).astype(BF16)
    acc = jnp.zeros(x.shape, F32)
    for c0 in range(0, D_FF, FF_CHUNK):
        c1 = min(c0 + FF_CHUNK, D_FF)
        act = _silu(_dot(h, w1_ref[:, c0:c1])) * _dot(h, w3_ref[:, c0:c1])
        acc = acc + _dot(act.astype(BF16), w2_ref[c0:c1, :])
    o_ref[...] = x + acc
    for src, dst in zip(src_refs, dst_refs):
        dst[...] = src[...].astype(BF16)


def _ffn(x2, g, w1, w3, w2, to_bf16=(), tm=512):
    T = x2.shape[0]
    steps = T // tm
    held = lambda a: pl.BlockSpec(a.shape, lambda i: (0, 0), pipeline_mode=pl.Buffered(1))
    sliced = lambda a: pl.BlockSpec((a.shape[0] // steps, a.shape[1]), lambda i: (i, 0))
    assert all(a.shape[0] % (16 * steps) == 0 for a in to_bf16)
    out, *copies = pl.pallas_call(
        _ffn_body,
        grid=(steps,),
        in_specs=[pl.BlockSpec((tm, D_MODEL), lambda i: (i, 0)), _const_spec(g.shape),
                  held(w1), held(w3), held(w2), *[sliced(a) for a in to_bf16]],
        out_specs=[pl.BlockSpec((tm, D_MODEL), lambda i: (i, 0)), *[sliced(a) for a in to_bf16]],
        out_shape=[jax.ShapeDtypeStruct((T, D_MODEL), F32),
                   *[jax.ShapeDtypeStruct(a.shape, BF16) for a in to_bf16]],
        compiler_params=_params(),
        name="ffn",
    )(x2, g, w1, w3, w2, *to_bf16)
    return out, copies


MOE_TM = 1024
MOE_CHUNK = 128
MOE_BIG = 2 * MOE_CHUNK
MOE_MERGED = 4


def _router_body(x_ref, g_ref, r_ref, h_ref, combw_ref, posw_ref, post_ref, cnt_ref):
    tm = x_ref.shape[0]
    h = _rms(x_ref[...], g_ref[...])
    h_ref[...] = h.astype(BF16)
    r = r_ref[...]
    h_hi, r_hi = h.astype(BF16), r.astype(BF16)
    h_lo = (h - h_hi.astype(F32)).astype(BF16)
    r_lo = (r - r_hi.astype(F32)).astype(BF16)
    logits = _dot(h_hi, r_hi) + (_dot(h_hi, r_lo) + _dot(h_lo, r_hi))
    lane = lax.broadcasted_iota(jnp.int32, logits.shape, 1)
    lg = jnp.where(lane < N_EXPERTS, logits, NEG)
    m1 = jnp.max(lg, axis=-1, keepdims=True)
    i1 = jnp.min(jnp.where(lg == m1, lane, LANES), axis=-1, keepdims=True)
    lg2 = jnp.where(lane == i1, NEG, lg)
    m2 = jnp.max(lg2, axis=-1, keepdims=True)
    i2 = jnp.min(jnp.where(lg2 == m2, lane, LANES), axis=-1, keepdims=True)
    e2 = jnp.exp(m2 - m1)
    den = 1.0 + e2
    w_first, w_second = 1.0 / den, e2 / den
    chosen = [jnp.broadcast_to((i1 == ex) | (i2 == ex), (tm, LANES)) for ex in range(N_EXPERTS)]
    ones = jnp.concatenate([jnp.where(c, 1.0, 0.0).astype(BF16) for c in chosen], axis=1)
    row = lax.broadcasted_iota(jnp.int32, (LANES, LANES), 0)
    col = lax.broadcasted_iota(jnp.int32, (LANES, LANES), 1)
    tri = jnp.where(col < row, 1.0, 0.0).astype(BF16)
    running = jnp.zeros((1, N_EXPERTS * LANES), F32)
    parts = []
    for b in range(tm // LANES):
        blk = ones[b * LANES:(b + 1) * LANES]
        parts.append(_dot(tri, blk) + running)
        running = running + jnp.sum(blk.astype(F32), axis=0, keepdims=True)
    before = jnp.concatenate(parts, axis=0)
    by_lane = jnp.zeros((tm, LANES), F32)
    cnt = jnp.zeros((1, LANES), F32)
    for ex in range(N_EXPERTS):
        slab = slice(ex * LANES, (ex + 1) * LANES)
        combw_ref[:, slab] = jnp.broadcast_to(
            jnp.where(i1 == ex, w_first, 0.0) + jnp.where(i2 == ex, w_second, 0.0), (tm, LANES))
        pos = jnp.where(chosen[ex], before[:, slab], -1.0)
        posw_ref[:, slab] = pos
        by_lane = jnp.where(lane == ex, pos, by_lane)
        cnt = jnp.where(lane[:1] == ex, running[:, slab], cnt)
    post_ref[0] = by_lane.T[:N_EXPERTS]
    cnt_ref[0] = jnp.broadcast_to(cnt, (8, LANES)).astype(jnp.int32)


def _router(x2, g, r, tm=MOE_TM):
    T = x2.shape[0]
    nt = T // tm
    row = lambda w: pl.BlockSpec((tm, w), lambda i: (i, 0))
    return pl.pallas_call(
        _router_body,
        grid=(nt,),
        in_specs=[row(D_MODEL), _const_spec(g.shape), _const_spec(r.shape)],
        out_specs=[row(D_MODEL), row(N_EXPERTS * LANES), row(N_EXPERTS * LANES),
                   pl.BlockSpec((1, N_EXPERTS, tm), lambda i: (i, 0, 0)),
                   pl.BlockSpec((1, 8, LANES), lambda i: (i, 0, 0))],
        out_shape=[jax.ShapeDtypeStruct((T, D_MODEL), BF16),
                   jax.ShapeDtypeStruct((T, N_EXPERTS * LANES), F32),
                   jax.ShapeDtypeStruct((T, N_EXPERTS * LANES), F32),
                   jax.ShapeDtypeStruct((nt, N_EXPERTS, tm), F32),
                   jax.ShapeDtypeStruct((nt, 8, LANES), jnp.int32)],
        compiler_params=_params(),
        name="router",
    )(x2, g, r)


def _moe_body(cnt_ref, x_ref, h_ref, comb_ref, pos_ref, post_ref, w1_ref, w3_ref, w2_ref, fn_ref,
              o_ref, y_ref):
    i = pl.program_id(0)
    e = pl.program_id(1)
    merged = MOE_MERGED * LANES
    n = cnt_ref[i * N_EXPERTS + e]
    n_big = (n + (MOE_BIG - MOE_CHUNK - 1)) // MOE_BIG
    rest_row = pl.multiple_of(n_big * MOE_BIG, MOE_BIG)
    has_rest = n > rest_row
    post_e = post_ref[0, pl.ds(e, 1), :]

    @pl.when(e == 0)
    def _():
        o_ref[...] = x_ref[...]

    @pl.when((i == 0) & (e == 0))
    def _():
        y_ref[...] = jnp.zeros_like(y_ref)

    def scatter_add(first_row, first_slot, n_slabs):
        lane_slot = lax.broadcasted_iota(jnp.int32, (1, LANES), 1).astype(F32)
        hot = [jnp.where(pos_ref[...] == first_slot + (k * LANES) + lane_slot, 1.0, 0.0).astype(BF16)
               for k in range(n_slabs)]
        scatter = hot[0] if n_slabs == 1 else jnp.concatenate(hot, axis=1)
        weight = jnp.concatenate([comb_ref[...]] * (D_MODEL // LANES), axis=1)
        ys = y_ref[pl.ds(first_row, n_slabs * LANES), :].astype(BF16)
        o_ref[...] += weight * _dot(scatter, ys)

    def chunk(r0, size):
        base = r0.astype(F32)
        slot_col = base + lax.broadcasted_iota(jnp.int32, (size, 1), 0).astype(F32)
        gather = jnp.where(post_e == slot_col, 1.0, 0.0).astype(BF16)
        xe = _dot(gather, h_ref[...]).astype(BF16)
        act = _silu(_dot(xe, w1_ref[0])) * _dot(xe, w3_ref[0])
        y = _dot(act.astype(BF16), w2_ref[0])

        @pl.when(r0 < merged)
        def _():
            y_ref[pl.ds(r0, size), :] = y

        @pl.when(r0 >= merged)
        def _():
            y_ref[merged:merged + size, :] = y
            for k in range(size // LANES):
                scatter_add(merged + k * LANES, base + float(k * LANES), 1)

    def big_chunk(c, carry):
        chunk(pl.multiple_of(c * MOE_BIG, MOE_BIG), MOE_BIG)
        return carry

    lax.fori_loop(0, n_big, big_chunk, 0)

    @pl.when(has_rest)
    def _():
        chunk(rest_row, MOE_CHUNK)

    @pl.when(n > 0)
    def _():
        scatter_add(0, 0.0, MOE_MERGED)

    @pl.when(e == pl.num_programs(1) - 1)
    def _():
        o_ref[...] = _rms(o_ref[...], fn_ref[...])


def _moe(x2, h, comb, pos, post, counts, w1, w3, w2, fn, tm=MOE_TM):
    T = x2.shape[0]
    once = lambda w: pl.BlockSpec((tm, w), lambda i, e, cnt: (i, 0), pipeline_mode=pl.Buffered(1))
    tile = lambda w: pl.BlockSpec((tm, w), lambda i, e, cnt: (i, 0))
    assert MOE_CHUNK == LANES
    slab = pl.BlockSpec((tm, LANES), lambda i, e, cnt: (i, e))
    expert = lambda a: pl.BlockSpec((1,) + a.shape[1:], lambda i, e, cnt: (e, 0, 0))
    grid_spec = pltpu.PrefetchScalarGridSpec(
        num_scalar_prefetch=1,
        grid=(T // tm, N_EXPERTS),
        in_specs=[once(D_MODEL), once(D_MODEL), slab, slab,
                  pl.BlockSpec((1, N_EXPERTS, tm), lambda i, e, cnt: (i, 0, 0)),
                  expert(w1), expert(w3), expert(w2),
                  pl.BlockSpec(fn.shape, lambda i, e, cnt: (0, 0))],
        out_specs=tile(D_MODEL),
        scratch_shapes=[pltpu.VMEM((MOE_MERGED * LANES + MOE_BIG, D_MODEL), F32)],
    )
    return pl.pallas_call(
        _moe_body,
        grid_spec=grid_spec,
        out_shape=jax.ShapeDtypeStruct((T, D_MODEL), F32),
        compiler_params=_params(),
        name="moe",
    )(counts, x2, h, comb, pos, post, w1, w3, w2, fn)


def _rope_lane_tables(positions):
    half = ROT_DIM // 2
    inv = ROPE_THETA ** (-jnp.arange(0, ROT_DIM, 2, dtype=F32) / ROT_DIM)
    ang = positions.astype(F32).reshape(-1, 1) * inv
    cos, sin = jnp.cos(ang), jnp.sin(ang)
    within = np.arange(LANES) % HEAD_DIM
    cos_l, sin_l = jnp.tile(cos, (1, LANES // half)), jnp.tile(sin, (1, LANES // half))
    c = jnp.where(within < ROT_DIM, cos_l, 1.0)
    sa = jnp.where(within < half, -sin_l, 0.0)
    sb = jnp.where((within >= half) & (within < ROT_DIM), sin_l, 0.0)
    return c, sa, sb


def _w_in_plan():
    scale = HEAD_DIM ** -0.5 * LOG2E
    qa, kv, gate, qkv_b, merge = 0, 512, 1280, 1304, 3608
    nb = N_DIL_GROUPS * DIL_WIDTH
    kv_piece = lambda j: [(kv + LANES * j, LANES, 1.0)]
    span = lambda start, width, s=1.0: [(start + c, LANES, s) for c in range(0, width, LANES)]
    plan = (span(qa, 512, scale)
            + kv_piece(2) + kv_piece(4) + kv_piece(0)
            + kv_piece(3) + kv_piece(5)
            + [(gate, 3 * NSA_HEADS, 1.0)]
            + kv_piece(1)
            + span(merge, 2 * D_MODEL)
            + span(qkv_b, nb, scale) + span(qkv_b + nb, nb) + span(qkv_b + 2 * nb, nb))
    assert len(plan) * LANES == N_W_IN
    return plan


def _importance_matrix_t(seq, ncp):
    n_c = (seq - CMP_LEN) // CMP_STRIDE + 1
    starts = np.arange(n_c) * CMP_STRIDE
    bstart = np.arange(seq // SEL_LEN) * SEL_LEN
    overlap = np.clip(np.minimum(starts[:, None] + CMP_LEN, bstart[None, :] + SEL_LEN)
                      - np.maximum(starts[:, None], bstart[None, :]), 0, None)
    m = np.zeros((ncp, seq // SEL_LEN), np.float32)
    m[:n_c] = overlap.astype(np.float32) / CMP_LEN
    return jnp.asarray(m.T)


def _mixer(x2, layer, B, S, tables, mt, prm):
    proj, *folds, a = _inproj(x2, prm["norm_mix"], prm["w_in"], layer, *tables, B, S)
    proj3 = proj.reshape(B, S, N_PROJ)
    ncp = S // CMP_STRIDE
    cmp = _compress(a.reshape(2, B * NSA_KV_HEADS * ncp, CMP_STRIDE * HEAD_DIM),
                    prm["cmp_w1"], prm["cmp_w2"], prm["cmp_pos"], layer, B)
    oa = _nsa(proj3, cmp[0], cmp[1], mt).reshape(B * S, NSA_HEADS * HEAD_DIM)
    obs, lses = [], []
    for g, ((w, d), arr) in enumerate(zip(DIL_PATTERNS, folds)):
        o, lse = _dilated(arr, (0, 1, 2), w, d, f"dilated{g}")
        obs.append(o)
        lses.append(lse)
    return _mixout(oa, obs, lses, proj, x2, prm["p_a"], prm["p_b"], prm["w_o"], layer, S)


def kernel(x, positions, norm_mix, w_in, cmp_pos_k, cmp_pos_v, cmp_k_w1, cmp_k_w2, cmp_v_w1,
           cmp_v_w2, w_branch_a, w_branch_b, w_out, norm_ffn, ffn_w1, ffn_w3, ffn_w2, router,
           moe_w1, moe_w3, moe_w2, final_norm):
    B, S, D = x.shape
    depth = norm_mix.shape[0]
    assert depth == 2 and D == D_MODEL
    tables = _rope_lane_tables(positions)
    mt = _importance_matrix_t(S, S // CMP_STRIDE)
    cmp_pos = jnp.stack([cmp_pos_k, cmp_pos_v], axis=1).reshape(depth, 2, 1, CMP_LEN * HEAD_DIM)
    prm = {
        "norm_mix": norm_mix.reshape(depth, 1, D),
        "w_in": jnp.swapaxes(w_in, 1, 2),
        "cmp_w1": jnp.stack([cmp_k_w1, cmp_v_w1], axis=1).astype(BF16),
        "cmp_w2": jnp.stack([cmp_k_w2, cmp_v_w2], axis=1).astype(BF16),
        "cmp_pos": jnp.broadcast_to(cmp_pos, (depth, 2, 8, CMP_LEN * HEAD_DIM)).astype(BF16),
        "p_a": w_branch_a.astype(BF16), "p_b": w_branch_b.astype(BF16), "w_o": w_out.astype(BF16),
    }
    x2 = x.reshape(B * S, D)
    x2 = _mixer(x2, 0, B, S, tables, mt, prm)
    experts = [w[0].reshape(-1, w.shape[-1]) for w in (moe_w1, moe_w3, moe_w2)]
    x2, experts = _ffn(x2, norm_ffn[0].reshape(1, -1), ffn_w1[0].astype(BF16),
                       ffn_w3[0].astype(BF16), ffn_w2[0].astype(BF16), to_bf16=experts)
    ew1, ew3, ew2 = [w.reshape(m.shape[1:]) for w, m in zip(experts, (moe_w1, moe_w3, moe_w2))]
    x2 = _mixer(x2, 1, B, S, tables, mt, prm)
    g1 = norm_ffn[1].reshape(1, -1)
    r = jnp.pad(router[0], ((0, 0), (0, LANES - N_EXPERTS)))
    h, comb, pos, post, cnt = _router(x2, g1, r)
    counts = cnt[:, 0, :N_EXPERTS].reshape(-1)
    out = _moe(x2, h, comb, pos, post, counts, ew1, ew3, ew2, final_norm.reshape(1, -1))
    return out.reshape(B, S, D)
```
